```python
import jax
import jax.numpy as jnp
from jax import lax
import numpy as np

D_MODEL = 1024
BATCH = 16
SEQ = 2048
DEPTH = 2

MEM_LEN = 256
EPS = 1e-6
FOX_HEADS = 8
FOX_HEAD_DIM = 64
FOX_WIDTH = FOX_HEADS * FOX_HEAD_DIM
FOX_BLOCK = 128
FORGET_BIAS_INIT = 3.0
GLA_HEADS = 4
GLA_KEY_DIM = 64
GLA_VAL_DIM = 128
GLA_QK_WIDTH = GLA_HEADS * GLA_KEY_DIM
GLA_V_WIDTH = GLA_HEADS * GLA_VAL_DIM
GLA_GATE_RANK = 16
GLA_GATE_TAU = 16.0
GLA_CHUNK = 64
MIX_WIDTH = FOX_WIDTH + GLA_V_WIDTH
IN_WIDTHS = (FOX_WIDTH, FOX_WIDTH, FOX_WIDTH, FOX_HEADS, GLA_QK_WIDTH, GLA_QK_WIDTH, GLA_V_WIDTH, GLA_V_WIDTH, GLA_GATE_RANK)
IN_PROJ_WIDTH = 3096
X_HEADS = 4
X_HEAD_DIM = 128
X_WIDTH = X_HEADS * X_HEAD_DIM
N_GROUPS = 4
EXPERTS_PER_GROUP = 4
N_EXPERTS = N_GROUPS * EXPERTS_PER_GROUP
TOP_K = 2
D_EXPERT = 512
MOE_BLOCK = 128
RESID_SCALE = 0.5

kernel_name = 'hybrid_fox_gla_hmoe_block'


def rms_norm(x, gain):
    xf = x.astype(jnp.float32)
    y = xf * lax.rsqrt(jnp.mean(xf * xf, axis=-1, keepdims=True) + EPS)
    return (y * gain.astype(jnp.float32)).astype(x.dtype)


def forgetting_attention(q, k, v, f_logit):
    B, S, H, d = q.shape
    q = q.transpose(0, 2, 1, 3)
    k = k.transpose(0, 2, 1, 3)
    v = v.transpose(0, 2, 1, 3)
    log_f = jax.nn.log_sigmoid(f_logit.astype(jnp.float32)).transpose(0, 2, 1)
    c = jnp.cumsum(log_f, axis=-1)
    scale = d ** -0.5
    outs = []
    for blk in range(S // FOX_BLOCK):
        q0 = blk * FOX_BLOCK
        q1 = q0 + FOX_BLOCK
        s = jnp.einsum('bhqd,bhkd->bhqk', q[:, :, q0:q1], k[:, :, :q1], preferred_element_type=jnp.float32) * scale
        s = s + c[:, :, q0:q1, None] - c[:, :, None, :q1]
        causal = jnp.arange(q0, q1)[:, None] >= jnp.arange(q1)[None, :]
        p = jax.nn.softmax(jnp.where(causal, s, -jnp.inf), axis=-1)
        outs.append(jnp.einsum('bhqk,bhkd->bhqd', p.astype(v.dtype), v[:, :, :q1]))
    return jnp.concatenate(outs, axis=2).transpose(0, 2, 1, 3)


def gla_chunked(q, k, v, log_a):
    B, S, H, dk = q.shape
    dv = v.shape[-1]
    C = GLA_CHUNK
    nc = S // C

    def to_chunks(t):
        return t.astype(jnp.float32).reshape(B, nc, C, H, t.shape[-1]).transpose(1, 0, 3, 2, 4)

    qc = to_chunks(q) * (dk ** -0.5)
    kc = to_chunks(k)
    vc = to_chunks(v)
    gc = to_chunks(log_a)
    causal = jnp.tril(jnp.ones((C, C), dtype=bool))[:, :, None]

    def step(state, inp):
        qi, ki, vi, gi = inp
        b = jnp.cumsum(gi, axis=2)
        b_last = b[:, :, -1:, :]
        o_inter = jnp.einsum('bhtd,bhdv->bhtv', qi * jnp.exp(b), state)
        diff = b[:, :, :, None, :] - b[:, :, None, :, :]
        decay = jnp.exp(jnp.where(causal, diff, -jnp.inf))
        attn = jnp.einsum('bhtd,bhsd,bhtsd->bhts', qi, ki, decay)
        o_intra = jnp.einsum('bhts,bhsv->bhtv', attn, vi)
        new_state = jnp.exp(b_last[:, :, 0, :])[..., None] * state + jnp.einsum('bhsd,bhsv->bhdv', ki * jnp.exp(b_last - b), vi)
        return new_state, o_inter + o_intra

    state0 = jnp.zeros((B, H, dk, dv), jnp.float32)
    _, o = lax.scan(step, state0, (qc, kc, vc, gc))
    return o.transpose(1, 0, 3, 2, 4).reshape(B, S, H, dv).astype(v.dtype)


def hybrid_mixer(h, w_in, b_forget, w_alpha_up, b_alpha, fox_out_gain, gla_out_gain, w_out):
    B, S, _ = h.shape
    offsets = [int(o) for o in np.cumsum(IN_WIDTHS)[:-1]]
    fq, fk, fv, ff, gq, gk, gv, gg, ga = jnp.split(h @ w_in, offsets, axis=-1)
    fox = forgetting_attention(
        fq.reshape(B, S, FOX_HEADS, FOX_HEAD_DIM),
        fk.reshape(B, S, FOX_HEADS, FOX_HEAD_DIM),
        fv.reshape(B, S, FOX_HEADS, FOX_HEAD_DIM),
        ff + b_forget)
    fox = rms_norm(fox, fox_out_gain.reshape(FOX_HEADS, FOX_HEAD_DIM))
    log_a = jax.nn.log_sigmoid((ga @ w_alpha_up + b_alpha).astype(jnp.float32)) / GLA_GATE_TAU
    gla = gla_chunked(
        gq.reshape(B, S, GLA_HEADS, GLA_KEY_DIM),
        gk.reshape(B, S, GLA_HEADS, GLA_KEY_DIM),
        gv.reshape(B, S, GLA_HEADS, GLA_VAL_DIM),
        log_a.reshape(B, S, GLA_HEADS, GLA_KEY_DIM))
    gla = rms_norm(gla, gla_out_gain.reshape(GLA_HEADS, GLA_VAL_DIM)) * jax.nn.silu(gg.reshape(B, S, GLA_HEADS, GLA_VAL_DIM))
    y = jnp.concatenate([fox.reshape(B, S, FOX_WIDTH), gla.reshape(B, S, GLA_V_WIDTH)], axis=-1)
    return y @ w_out


def memory_cross_attention(h, mem_n, w_xq, w_xk, w_xv, w_xo):
    B, S, _ = h.shape
    M = mem_n.shape[1]
    q = (h @ w_xq).reshape(B, S, X_HEADS, X_HEAD_DIM)
    k = (mem_n @ w_xk).reshape(B, M, X_HEADS, X_HEAD_DIM)
    v = (mem_n @ w_xv).reshape(B, M, X_HEADS, X_HEAD_DIM)
    s = jnp.einsum('bqhd,bkhd->bhqk', q, k, preferred_element_type=jnp.float32) * (X_HEAD_DIM ** -0.5)
    p = jax.nn.softmax(s, axis=-1)
    o = jnp.einsum('bhqk,bkhd->bqhd', p.astype(v.dtype), v).reshape(B, S, X_WIDTH)
    return o @ w_xo


def hierarchical_moe(h, w_rg, b_rg, w_re, b_re, w_e_gate, w_e_up, w_e_down):
    B, S, D = h.shape
    N = B * S
    hf = h.reshape(N, D)
    g_prob = jax.nn.softmax((hf @ w_rg).astype(jnp.float32) + b_rg.astype(jnp.float32), axis=-1)
    grp = jnp.argmax(g_prob, axis=-1)
    p_grp = jnp.take_along_axis(g_prob, grp[:, None], axis=-1)
    e_logits = ((hf @ w_re).astype(jnp.float32) + b_re.astype(jnp.float32)).reshape(N, N_GROUPS, EXPERTS_PER_GROUP)
    e_logits = jnp.take_along_axis(e_logits, grp[:, None, None], axis=1)[:, 0]
    top_p, top_i = lax.top_k(jax.nn.softmax(e_logits, axis=-1), TOP_K)
    gate = p_grp * top_p / jnp.sum(top_p, axis=-1, keepdims=True)
    expert = grp[:, None].astype(jnp.int32) * EXPERTS_PER_GROUP + top_i.astype(jnp.int32)
    Mrows = N * TOP_K
    flat_e = expert.reshape(Mrows)
    flat_g = gate.reshape(Mrows)
    flat_tok = jnp.repeat(jnp.arange(N, dtype=jnp.int32), TOP_K)
    order = jnp.argsort(flat_e)
    sorted_e = flat_e[order]
    counts = jnp.bincount(flat_e, length=N_EXPERTS)
    padded = (counts + MOE_BLOCK - 1) // MOE_BLOCK * MOE_BLOCK
    starts = jnp.cumsum(counts) - counts
    pad_ends = jnp.cumsum(padded)
    pad_starts = pad_ends - padded
    dest = pad_starts[sorted_e] + jnp.arange(Mrows, dtype=jnp.int32) - starts[sorted_e]
    R = Mrows + N_EXPERTS * MOE_BLOCK
    n_blocks = R // MOE_BLOCK
    row_tok = jnp.full((R,), N, dtype=jnp.int32).at[dest].set(flat_tok[order])
    row_gate = jnp.zeros((R,), jnp.float32).at[dest].set(flat_g[order])
    block_e = jnp.minimum(jnp.searchsorted(pad_ends, jnp.arange(n_blocks) * MOE_BLOCK, side='right'), N_EXPERTS - 1)
    x_rows = jnp.concatenate([hf, jnp.zeros((1, D), hf.dtype)], axis=0)[row_tok].reshape(n_blocks, MOE_BLOCK, D)

    def expert_block(args):
        xb, e = args
        u = jax.nn.silu(xb @ w_e_gate[e]) * (xb @ w_e_up[e])
        return u @ w_e_down[e]

    y_rows = lax.map(expert_block, (x_rows, block_e)).reshape(R, D)
    y = jax.ops.segment_sum(y_rows.astype(jnp.float32) * row_gate[:, None], row_tok, num_segments=N + 1)[:N]
    return y.reshape(B, S, D).astype(h.dtype)


def setup_inputs(seed: int = 0) -> dict:
    key = jax.random.key(seed)
    ks = jax.random.split(key, 26)
    f32 = jnp.float32
    L = DEPTH

    def w(k, shape, fan_in, scale=1.0):
        return jax.random.normal(k, shape, f32) * (scale * fan_in ** -0.5)

    def gain(k, shape):
        return 1.0 + 0.02 * jax.random.normal(k, shape, f32)

    def small(k, shape, scale):
        return scale * jax.random.normal(k, shape, f32)

    return {
        'x': jax.random.normal(ks[0], (BATCH, SEQ, D_MODEL), f32),
        'mem': jax.random.normal(ks[1], (BATCH, MEM_LEN, D_MODEL), f32),
        'mem_norm': gain(ks[2], (D_MODEL,)),
        'mix_norm': gain(ks[3], (L, D_MODEL)),
        'w_in': w(ks[4], (L, D_MODEL, IN_PROJ_WIDTH), D_MODEL),
        'b_forget': FORGET_BIAS_INIT + small(ks[5], (L, FOX_HEADS), 0.1),
        'w_alpha_up': w(ks[6], (L, GLA_GATE_RANK, GLA_QK_WIDTH), GLA_GATE_RANK),
        'b_alpha': small(ks[7], (L, GLA_QK_WIDTH), 0.1),
        'fox_out_gain': gain(ks[8], (L, FOX_WIDTH)),
        'gla_out_gain': gain(ks[9], (L, GLA_V_WIDTH)),
        'w_out': w(ks[10], (L, MIX_WIDTH, D_MODEL), MIX_WIDTH, RESID_SCALE),
        'cross_norm': gain(ks[11], (L, D_MODEL)),
        'w_xq': w(ks[12], (L, D_MODEL, X_WIDTH), D_MODEL),
        'w_xk': w(ks[13], (L, D_MODEL, X_WIDTH), D_MODEL),
        'w_xv': w(ks[14], (L, D_MODEL, X_WIDTH), D_MODEL),
        'w_xo': w(ks[15], (L, X_WIDTH, D_MODEL), X_WIDTH, RESID_SCALE),
        'moe_norm': gain(ks[16], (L, D_MODEL)),
        'w_router_group': w(ks[17], (L, D_MODEL, N_GROUPS), D_MODEL),
        'b_router_group': small(ks[18], (L, N_GROUPS), 0.01),
        'w_router_expert': w(ks[19], (L, D_MODEL, N_EXPERTS), D_MODEL),
        'b_router_expert': small(ks[20], (L, N_EXPERTS), 0.01),
        'w_expert_gate': w(ks[21], (L, N_EXPERTS, D_MODEL, D_EXPERT), D_MODEL),
        'w_expert_up': w(ks[22], (L, N_EXPERTS, D_MODEL, D_EXPERT), D_MODEL),
        'w_expert_down': w(ks[23], (L, N_EXPERTS, D_EXPERT, D_MODEL), D_EXPERT, RESID_SCALE),
        'final_norm': gain(ks[24], (D_MODEL,)),
    }


def reference(x, mem, mem_norm, mix_norm, w_in, b_forget, w_alpha_up, b_alpha, fox_out_gain, gla_out_gain, w_out,
              cross_norm, w_xq, w_xk, w_xv, w_xo, moe_norm, w_router_group, b_router_group, w_router_expert,
              b_router_expert, w_expert_gate, w_expert_up, w_expert_down, final_norm):
    mem_n = rms_norm(mem, mem_norm)
    h = x
    for l in range(DEPTH):
        h = h + hybrid_mixer(rms_norm(h, mix_norm[l]), w_in[l], b_forget[l], w_alpha_up[l], b_alpha[l],
                             fox_out_gain[l], gla_out_gain[l], w_out[l])
        h = h + memory_cross_attention(rms_norm(h, cross_norm[l]), mem_n, w_xq[l], w_xk[l], w_xv[l], w_xo[l])
        h = h + hierarchical_moe(rms_norm(h, moe_norm[l]), w_router_group[l], b_router_group[l],
                                 w_router_expert[l], b_router_expert[l], w_expert_gate[l], w_expert_up[l],
                                 w_expert_down[l])
    return rms_norm(h, final_norm)
```

```python
import functools

import jax
import jax.numpy as jnp
from jax import lax
from jax.experimental import pallas as pl
from jax.experimental.pallas import tpu as pltpu

F32 = jnp.float32
BF16 = jnp.bfloat16
EPS = 1e-6

FOX_HEADS = 8
FOX_DIM = 64
FOX_WIDTH = FOX_HEADS * FOX_DIM
GLA_HEADS = 4
GLA_DK = 64
GLA_DV = 128
GLA_QK = GLA_HEADS * GLA_DK
GLA_V = GLA_HEADS * GLA_DV
GLA_RANK = 16
GLA_TAU = 16.0
GLA_CHUNK = 64
X_HEADS = 4
X_DIM = 128
X_WIDTH = X_HEADS * X_DIM
N_GROUPS = 4
GROUP_SIZE = 4
N_EXPERTS = N_GROUPS * GROUP_SIZE
MAIN_WIDTH = 3 * FOX_WIDTH + 2 * GLA_QK + 2 * GLA_V

LANES = 128
VMEM_LIMIT = 56 * 1024 * 1024

IN_TILE = 512
FOX_TILE = 256
POST_TILE = 256
CUMSUM_TILE = 256
MOE_BLOCK = 256
MOVE_TILE = 256


def _cparams(sem):
    return pltpu.CompilerParams(dimension_semantics=sem, vmem_limit_bytes=VMEM_LIMIT)


def _rms(x, gain):
    return x * lax.rsqrt(jnp.mean(x * x, axis=-1, keepdims=True) + EPS) * gain


def _log_sigmoid(x):
    return jnp.minimum(x, 0.0) - jnp.log1p(jnp.exp(-jnp.abs(x)))


def _dot(a, b):
    return jnp.dot(a, b, preferred_element_type=F32)


def _dot_nt(a, b):
    return lax.dot_general(a, b, (((1,), (1,)), ((), ())), preferred_element_type=F32)


def _split3(x):
    hi = x.astype(BF16)
    r1 = x - hi.astype(F32)
    mid = r1.astype(BF16)
    lo = (r1 - mid.astype(F32)).astype(BF16)
    return hi, mid, lo


def _mem_kv_kernel(mem_ref, gain_ref, wk_ref, wv_ref, k_ref, v_ref):
    mn = _rms(mem_ref[...], gain_ref[...]).astype(BF16)
    for l in range(wk_ref.shape[0]):
        k_ref[l] = _dot(mn, wk_ref[l]).astype(BF16)
        v_ref[l] = _dot(mn, wv_ref[l]).astype(BF16)


def _mem_kv(mem2d, gain, wk, wv, batch, mem_len):
    depth, d_model, width = wk.shape
    out = jax.ShapeDtypeStruct((depth, batch * mem_len, width), BF16)
    return pl.pallas_call(
        _mem_kv_kernel,
        grid=(batch,),
        in_specs=[
            pl.BlockSpec((mem_len, d_model), lambda b: (b, 0)),
            pl.BlockSpec((1, d_model), lambda b: (0, 0)),
            pl.BlockSpec((depth, d_model, width), lambda b: (0, 0, 0)),
            pl.BlockSpec((depth, d_model, width), lambda b: (0, 0, 0)),
        ],
        out_specs=[
            pl.BlockSpec((depth, mem_len, width), lambda b: (0, b, 0)),
            pl.BlockSpec((depth, mem_len, width), lambda b: (0, b, 0)),
        ],
        out_shape=[out, out],
        compiler_params=_cparams(("arbitrary",)),
        name="mem_kv",
    )(mem2d, gain, wk, wv)


def _in_proj_kernel(h_ref, gain_ref, wmain_ref, wsmall_ref, wup_ref, bf_ref, ba_ref,
                    main_ref, logf_ref, loga_ref):
    xn = _rms(h_ref[...], gain_ref[...]).astype(BF16)
    step = 512
    for j in range(MAIN_WIDTH // step):
        main_ref[:, j * step:(j + 1) * step] = _dot(xn, wmain_ref[:, j * step:(j + 1) * step]).astype(BF16)
    small = _dot(xn, wsmall_ref[...])
    lane = lax.broadcasted_iota(jnp.int32, small.shape, 1)
    logf_ref[...] = jnp.where(lane < FOX_HEADS, _log_sigmoid(small + bf_ref[...]), 0.0)
    a = _dot(small.astype(BF16), wup_ref[...]) + ba_ref[...]
    loga_ref[...] = _log_sigmoid(a) * (1.0 / GLA_TAU)


def _in_proj(h, gain, wmain, wsmall, wup, bf, ba):
    n, d_model = h.shape
    tm = IN_TILE
    return pl.pallas_call(
        _in_proj_kernel,
        grid=(n // tm,),
        in_specs=[
            pl.BlockSpec((tm, d_model), lambda i: (i, 0)),
            pl.BlockSpec((1, d_model), lambda i: (0, 0)),
            pl.BlockSpec((d_model, MAIN_WIDTH), lambda i: (0, 0)),
            pl.BlockSpec((d_model, LANES), lambda i: (0, 0)),
            pl.BlockSpec((LANES, GLA_QK), lambda i: (0, 0)),
            pl.BlockSpec((1, LANES), lambda i: (0, 0)),
            pl.BlockSpec((1, GLA_QK), lambda i: (0, 0)),
        ],
        out_specs=[
            pl.BlockSpec((tm, MAIN_WIDTH), lambda i: (i, 0)),
            pl.BlockSpec((tm, LANES), lambda i: (i, 0)),
            pl.BlockSpec((tm, GLA_QK), lambda i: (i, 0)),
        ],
        out_shape=[
            jax.ShapeDtypeStruct((n, MAIN_WIDTH), BF16),
            jax.ShapeDtypeStruct((n, LANES), F32),
            jax.ShapeDtypeStruct((n, GLA_QK), F32),
        ],
        compiler_params=_cparams(("arbitrary",)),
        name="in_proj",
    )(h, gain, wmain, wsmall, wup, bf, ba)


def _cumsum_kernel(x_ref, o_ref):
    t = CUMSUM_TILE
    row = lax.broadcasted_iota(jnp.int32, (t, t), 0)
    col = lax.broadcasted_iota(jnp.int32, (t, t), 1)
    tril = (row >= col).astype(BF16)
    carry = jnp.zeros((1, x_ref.shape[1]), F32)
    for j in range(x_ref.shape[0] // t):
        hi, mid, lo = _split3(x_ref[j * t:(j + 1) * t, :])
        c = _dot(tril, hi) + _dot(tril, mid) + _dot(tril, lo) + carry
        o_ref[j * t:(j + 1) * t, :] = c
        carry = c[t - 1:t, :]


def _seq_cumsum(x, batch, seq):
    return pl.pallas_call(
        _cumsum_kernel,
        grid=(batch,),
        in_specs=[pl.BlockSpec((seq, LANES), lambda b: (b, 0))],
        out_specs=pl.BlockSpec((seq, LANES), lambda b: (b, 0)),
        out_shape=jax.ShapeDtypeStruct(x.shape, F32),
        compiler_params=_cparams(("arbitrary",)),
        name="forget_cumsum",
    )(x)


def _fox_kernel(q_ref, k_ref, v_ref, c_ref, gain_ref, o_ref):
    tq = q_ref.shape[0]
    qi = pl.program_id(2)
    q = q_ref[...]
    lane = lax.broadcasted_iota(jnp.int32, (1, LANES), 1)
    row = lax.broadcasted_iota(jnp.int32, (tq, tq), 0)
    col = lax.broadcasted_iota(jnp.int32, (tq, tq), 1)
    causal = row >= col
    scale = FOX_DIM ** -0.5
    outs = []
    for h in range(2):
        head = (lane >= h * FOX_DIM) & (lane < (h + 1) * FOX_DIM)
        qh = jnp.where(head, q, jnp.zeros_like(q)) * jnp.asarray(scale, BF16)

        def block(j, carry, masked, qh=qh, h=h):
            m, l, acc = carry
            r0 = pl.multiple_of(j * tq, tq)
            ks = k_ref[pl.ds(r0, tq), :]
            vs = v_ref[pl.ds(r0, tq), :]
            s = _dot_nt(qh, ks) - c_ref[j][h:h + 1, :]
            if masked:
                s = jnp.where(causal, s, -jnp.inf)
            m_new = jnp.maximum(m, jnp.max(s, axis=-1, keepdims=True))
            alpha = jnp.exp(m - m_new)
            p = jnp.exp(s - m_new)
            l = alpha * l + jnp.sum(p, axis=-1, keepdims=True)
            acc = alpha * acc + _dot(p.astype(BF16), vs)
            return m_new, l, acc

        init = (jnp.full((tq, 1), -jnp.inf, F32), jnp.zeros((tq, 1), F32), jnp.zeros((tq, LANES), F32))
        carry = lax.fori_loop(0, qi, functools.partial(block, masked=False), init)
        _, l, acc = block(qi, carry, True)
        outs.append(acc / l)
    first = lane < FOX_DIM
    o = jnp.where(first, outs[0], outs[1])
    sq = o * o
    ss0 = jnp.sum(jnp.where(first, sq, 0.0), axis=-1, keepdims=True)
    ss1 = jnp.sum(jnp.where(first, 0.0, sq), axis=-1, keepdims=True)
    ms = jnp.where(first, ss0, ss1) * (1.0 / FOX_DIM)
    o_ref[...] = (o * lax.rsqrt(ms + EPS) * gain_ref[...]).astype(BF16)


def _fox_attention(main, c5, gain, batch, seq):
    n = main.shape[0]
    tq = FOX_TILE
    nq = seq // tq
    pairs = FOX_HEADS // 2
    k_off = FOX_WIDTH // LANES
    v_off = 2 * FOX_WIDTH // LANES
    return pl.pallas_call(
        _fox_kernel,
        grid=(batch, pairs, nq),
        in_specs=[
            pl.BlockSpec((tq, LANES), lambda b, p, i: (b * nq + i, p)),
            pl.BlockSpec((seq, LANES), lambda b, p, i: (b, k_off + p)),
            pl.BlockSpec((seq, LANES), lambda b, p, i: (b, v_off + p)),
            pl.BlockSpec((None, None, nq, 2, tq), lambda b, p, i: (b, p, 0, 0, 0)),
            pl.BlockSpec((1, LANES), lambda b, p, i: (0, p)),
        ],
        out_specs=pl.BlockSpec((tq, LANES), lambda b, p, i: (b * nq + i, p)),
        out_shape=jax.ShapeDtypeStruct((n, FOX_WIDTH), BF16),
        compiler_params=_cparams(("arbitrary", "arbitrary", "arbitrary")),
        name="fox_attention",
    )(main, main, main, c5, gain)


def _gla_kernel(q_ref, k_ref, v_ref, gg_ref, la_ref, gain_ref, o_ref, st_ref):
    seq = q_ref.shape[0]
    cs = GLA_CHUNK
    st_ref[...] = jnp.zeros_like(st_ref)
    row = lax.broadcasted_iota(jnp.int32, (cs, cs), 0)
    col = lax.broadcasted_iota(jnp.int32, (cs, cs), 1)
    tril = row >= col
    tril_b = tril.astype(BF16)
    lane = lax.broadcasted_iota(jnp.int32, (1, 2 * GLA_DK), 1)
    srow = lax.broadcasted_iota(jnp.int32, st_ref.shape, 0)
    scol = lax.broadcasted_iota(jnp.int32, st_ref.shape, 1)
    same_head = (srow >= GLA_DV) == (scol >= GLA_DK)
    gain = gain_ref[...]
    scale = GLA_DK ** -0.5

    def chunk(ci, carry):
        r0 = pl.multiple_of(ci * cs, cs)
        hi, mid, lo = _split3(la_ref[pl.ds(r0, cs), :])
        b = _dot(tril_b, hi) + _dot(tril_b, mid) + _dot(tril_b, lo)
        b_last = b[cs - 1:cs, :]
        q = q_ref[pl.ds(r0, cs), :].astype(F32)
        k = k_ref[pl.ds(r0, cs), :].astype(F32)
        v = v_ref[pl.ds(r0, cs), :]
        qe = q * jnp.exp(b) * scale
        ke = (k * jnp.exp(-b)).astype(BF16)
        kl = (k * jnp.exp(b_last - b)).astype(BF16)
        st = st_ref[...]
        o = _dot_nt(qe.astype(BF16), st.astype(BF16))
        intra = []
        for h in range(2):
            head = (lane >= h * GLA_DK) & (lane < (h + 1) * GLA_DK)
            qh = jnp.where(head, qe, 0.0).astype(BF16)
            att = jnp.where(tril, _dot_nt(qh, ke), 0.0)
            intra.append(_dot(att.astype(BF16), v[:, h * GLA_DV:(h + 1) * GLA_DV]))
        o = o + jnp.concatenate(intra, axis=1)
        upd = lax.dot_general(v, kl, (((0,), (0,)), ((), ())), preferred_element_type=F32)
        st_ref[...] = st * jnp.exp(b_last) + jnp.where(same_head, upd, 0.0)
        normed = []
        for h in range(2):
            oh = o[:, h * GLA_DV:(h + 1) * GLA_DV]
            normed.append(oh * lax.rsqrt(jnp.mean(oh * oh, axis=-1, keepdims=True) + EPS))
        g = gg_ref[pl.ds(r0, cs), :].astype(F32)
        y = jnp.concatenate(normed, axis=1) * gain * (g * jax.nn.sigmoid(g))
        o_ref[pl.ds(r0, cs), :] = y.astype(BF16)
        return carry

    lax.fori_loop(0, seq // cs, chunk, 0)


def _gla(main, loga, gain, batch, seq):
    n = main.shape[0]
    pairs = GLA_HEADS // 2
    q_off = 3 * FOX_WIDTH // LANES
    k_off = q_off + GLA_QK // LANES
    pv = 2 * GLA_DV
    v_off = (3 * FOX_WIDTH + 2 * GLA_QK) // pv
    g_off = v_off + GLA_V // pv
    return pl.pallas_call(
        _gla_kernel,
        grid=(batch, pairs),
        in_specs=[
            pl.BlockSpec((seq, LANES), lambda b, p: (b, q_off + p)),
            pl.BlockSpec((seq, LANES), lambda b, p: (b, k_off + p)),
            pl.BlockSpec((seq, pv), lambda b, p: (b, v_off + p)),
            pl.BlockSpec((seq, pv), lambda b, p: (b, g_off + p)),
            pl.BlockSpec((seq, LANES), lambda b, p: (b, p)),
            pl.BlockSpec((1, pv), lambda b, p: (0, p)),
        ],
        out_specs=pl.BlockSpec((seq, pv), lambda b, p: (b, p)),
        out_shape=jax.ShapeDtypeStruct((n, GLA_V), BF16),
        scratch_shapes=[pltpu.VMEM((pv, 2 * GLA_DK), F32)],
        compiler_params=_cparams(("arbitrary", "arbitrary")),
        name="gla",
    )(main, main, main, main, loga, gain)


def _post_kernel(fox_ref, gla_ref, h_ref, wout_ref, cg_ref, wxq_ref, k_ref, v_ref, wxo_ref, mg_ref,
                 wrh_ref, wrl_ref, br_ref, h2_ref, hn_ref, route_ref, cnt_ref, carry_ref):
    tm = h_ref.shape[0]

    @pl.when(pl.program_id(0) == 0)
    def _():
        carry_ref[...] = jnp.zeros_like(carry_ref)

    y = _dot(fox_ref[...], wout_ref[0:FOX_WIDTH, :]) + _dot(gla_ref[...], wout_ref[FOX_WIDTH:, :])
    h1 = h_ref[...] + y
    hn = _rms(h1, cg_ref[...]).astype(BF16)
    q = _dot(hn, wxq_ref[...]).astype(BF16)
    xscale = X_DIM ** -0.5
    heads = []
    for hh in range(X_HEADS):
        sl = slice(hh * X_DIM, (hh + 1) * X_DIM)
        s = _dot_nt(q[:, sl], k_ref[:, sl]) * xscale
        p = jnp.exp(s - jnp.max(s, axis=-1, keepdims=True))
        heads.append(_dot(p.astype(BF16), v_ref[:, sl]) / jnp.sum(p, axis=-1, keepdims=True))
    o = jnp.concatenate(heads, axis=1).astype(BF16)
    h2 = h1 + _dot(o, wxo_ref[...])
    h2_ref[...] = h2
    hn2 = _rms(h2, mg_ref[...])
    hn_ref[...] = hn2

    xh = hn2.astype(BF16)
    xl = (hn2 - xh.astype(F32)).astype(BF16)
    logits = _dot(xh, wrh_ref[...]) + _dot(xl, wrh_ref[...]) + _dot(xh, wrl_ref[...]) + br_ref[...]
    lane = lax.broadcasted_iota(jnp.int32, (tm, LANES), 1)
    neg = -jnp.inf
    gl = jnp.where(lane < N_GROUPS, logits, neg)
    gmax = jnp.max(gl, axis=-1, keepdims=True)
    ge = jnp.exp(gl - gmax)
    gprob = ge / jnp.sum(ge, axis=-1, keepdims=True)
    pmax = jnp.max(gprob, axis=-1, keepdims=True)
    grp = jnp.min(jnp.where(gprob == pmax, lane, LANES), axis=-1, keepdims=True)
    in_grp = (lane >= N_GROUPS) & (lane < N_GROUPS + N_EXPERTS) & (((lane - N_GROUPS) // GROUP_SIZE) == grp)
    el = jnp.where(in_grp, logits, neg)
    emax = jnp.max(el, axis=-1, keepdims=True)
    ee = jnp.exp(el - emax)
    eprob = ee / jnp.sum(ee, axis=-1, keepdims=True)
    p1 = jnp.max(eprob, axis=-1, keepdims=True)
    lane1 = jnp.min(jnp.where(in_grp & (eprob == p1), lane, LANES), axis=-1, keepdims=True)
    rest = jnp.where(in_grp & (lane != lane1), eprob, -1.0)
    p2 = jnp.max(rest, axis=-1, keepdims=True)
    lane2 = jnp.min(jnp.where(rest == p2, lane, LANES), axis=-1, keepdims=True)
    g1 = pmax * p1 / (p1 + p2)
    g2 = pmax * p2 / (p1 + p2)

    oh1 = lane == lane1
    oh2 = lane == lane2
    both = (oh1 | oh2).astype(BF16)
    row = lax.broadcasted_iota(jnp.int32, (tm, tm), 0)
    col = lax.broadcasted_iota(jnp.int32, (tm, tm), 1)
    before = (row > col).astype(BF16)
    seen = _dot(before, both) + carry_ref[...]
    rank1 = jnp.sum(jnp.where(oh1, seen, 0.0), axis=-1, keepdims=True)
    rank2 = jnp.sum(jnp.where(oh2, seen, 0.0), axis=-1, keepdims=True)
    carry = carry_ref[...] + jnp.sum(both.astype(F32), axis=0, keepdims=True)
    carry_ref[...] = carry
    cnt_ref[...] = carry

    e1 = (lane1 - N_GROUPS).astype(F32)
    e2 = (lane2 - N_GROUPS).astype(F32)
    route = jnp.where(lane == 0, e1, 0.0)
    route = jnp.where(lane == 1, e2, route)
    route = jnp.where(lane == 2, g1, route)
    route = jnp.where(lane == 3, g2, route)
    route = jnp.where(lane == 4, rank1, route)
    route = jnp.where(lane == 5, rank2, route)
    route_ref[...] = route


def _post(fox, gla, h, wout, cg, wxq, kmem, vmem, wxo, mg, wrh, wrl, br, seq, mem_len):
    n, d_model = h.shape
    tm = POST_TILE
    per_seq = seq // tm
    const = lambda i: (0, 0)
    return pl.pallas_call(
        _post_kernel,
        grid=(n // tm,),
        in_specs=[
            pl.BlockSpec((tm, FOX_WIDTH), lambda i: (i, 0)),
            pl.BlockSpec((tm, GLA_V), lambda i: (i, 0)),
            pl.BlockSpec((tm, d_model), lambda i: (i, 0)),
            pl.BlockSpec((FOX_WIDTH + GLA_V, d_model), const),
            pl.BlockSpec((1, d_model), const),
            pl.BlockSpec((d_model, X_WIDTH), const),
            pl.BlockSpec((mem_len, X_WIDTH), lambda i: (i // per_seq, 0)),
            pl.BlockSpec((mem_len, X_WIDTH), lambda i: (i // per_seq, 0)),
            pl.BlockSpec((X_WIDTH, d_model), const),
            pl.BlockSpec((1, d_model), const),
            pl.BlockSpec((d_model, LANES), const),
            pl.BlockSpec((d_model, LANES), const),
            pl.BlockSpec((1, LANES), const),
        ],
        out_specs=[
            pl.BlockSpec((tm, d_model), lambda i: (i, 0)),
            pl.BlockSpec((tm, d_model), lambda i: (i, 0)),
            pl.BlockSpec((tm, LANES), lambda i: (i, 0)),
            pl.BlockSpec((1, LANES), const),
        ],
        out_shape=[
            jax.ShapeDtypeStruct((n, d_model), F32),
            jax.ShapeDtypeStruct((n, d_model), F32),
            jax.ShapeDtypeStruct((n, LANES), F32),
            jax.ShapeDtypeStruct((1, LANES), F32),
        ],
        scratch_shapes=[pltpu.VMEM((1, LANES), F32)],
        compiler_params=_cparams(("arbitrary",)),
        name="post_mixer",
    )(fox, gla, h, wout, cg, wxq, kmem, vmem, wxo, mg, wrh, wrl, br)


def _row_copy(src_ref, src_row, dst_ref, dst_row, sem):
    return pltpu.make_async_copy(src_ref.at[pl.ds(src_row, 1)], dst_ref.at[pl.ds(dst_row, 1)], sem)


def _dispatch_kernel(dest_ref, x_ref, xs_in_ref, xs_ref, sem):
    del xs_in_ref
    td = x_ref.shape[0]
    base = pl.program_id(0) * td

    def issue(t, carry):
        for kk in range(2):
            _row_copy(x_ref, t, xs_ref, dest_ref[(base + t) * 2 + kk], sem).start()
        return carry

    lax.fori_loop(0, td, issue, 0)

    def drain(t, carry):
        for kk in range(2):
            _row_copy(x_ref, t, xs_ref, dest_ref[(base + t) * 2 + kk], sem).wait()
        return carry

    lax.fori_loop(0, td, drain, 0)


def _dispatch(dest, x, xs_init):
    n, d_model = x.shape
    td = MOVE_TILE
    return pl.pallas_call(
        _dispatch_kernel,
        grid_spec=pltpu.PrefetchScalarGridSpec(
            num_scalar_prefetch=1,
            grid=(n // td,),
            in_specs=[
                pl.BlockSpec((td, d_model), lambda i, dest: (i, 0)),
                pl.BlockSpec(memory_space=pl.ANY),
            ],
            out_specs=pl.BlockSpec(memory_space=pl.ANY),
            scratch_shapes=[pltpu.SemaphoreType.DMA],
        ),
        out_shape=jax.ShapeDtypeStruct(xs_init.shape, xs_init.dtype),
        input_output_aliases={2: 0},
        compiler_params=_cparams(("arbitrary",)),
        name="moe_dispatch",
    )(dest, x, xs_init)


def _expert_kernel(be_ref, nu_ref, x_ref, wg_ref, wu_ref, wd_ref, y_ref):
    del be_ref
    i = pl.program_id(0)

    @pl.when(i < nu_ref[0])
    def _():
        x = x_ref[...].astype(BF16)
        g = _dot(x, wg_ref[...])
        u = _dot(x, wu_ref[...])
        a = (g * jax.nn.sigmoid(g) * u).astype(BF16)
        y_ref[...] = _dot(a, wd_ref[...])

    @pl.when(i >= nu_ref[0])
    def _():
        y_ref[...] = jnp.zeros_like(y_ref)


def _experts(block_e, n_used, xs, wg, wu, wd):
    r, d_model = xs.shape
    bm = MOE_BLOCK
    d_exp = wg.shape[-1]
    return pl.pallas_call(
        _expert_kernel,
        grid_spec=pltpu.PrefetchScalarGridSpec(
            num_scalar_prefetch=2,
            grid=(r // bm,),
            in_specs=[
                pl.BlockSpec((bm, d_model), lambda i, be, nu: (i, 0)),
                pl.BlockSpec((None, d_model, d_exp), lambda i, be, nu: (be[i], 0, 0)),
                pl.BlockSpec((None, d_model, d_exp), lambda i, be, nu: (be[i], 0, 0)),
                pl.BlockSpec((None, d_exp, d_model), lambda i, be, nu: (be[i], 0, 0)),
            ],
            out_specs=pl.BlockSpec((bm, d_model), lambda i, be, nu: (i, 0)),
        ),
        out_shape=jax.ShapeDtypeStruct((r, d_model), F32),
        compiler_params=_cparams(("arbitrary",)),
        name="moe_experts",
    )(block_e, n_used, xs, wg, wu, wd)


def _combine_kernel(dest_ref, h_ref, route_ref, gain_ref, y_ref, o_ref, y0_ref, y1_ref, sem, *, final):
    tc = h_ref.shape[0]
    base = pl.program_id(0) * tc
    bufs = (y0_ref, y1_ref)

    def issue(t, carry):
        for kk in range(2):
            _row_copy(y_ref, dest_ref[(base + t) * 2 + kk], bufs[kk], t, sem).start()
        return carry

    lax.fori_loop(0, tc, issue, 0)

    def drain(t, carry):
        for kk in range(2):
            _row_copy(y_ref, dest_ref[(base + t) * 2 + kk], bufs[kk], t, sem).wait()
        return carry

    lax.fori_loop(0, tc, drain, 0)

    route = route_ref[...]
    out = h_ref[...] + route[:, 2:3] * y0_ref[...] + route[:, 3:4] * y1_ref[...]
    if final:
        out = _rms(out, gain_ref[...])
    o_ref[...] = out


def _combine(dest, h, route, gain, y, final):
    n, d_model = h.shape
    tc = MOVE_TILE
    return pl.pallas_call(
        functools.partial(_combine_kernel, final=final),
        grid_spec=pltpu.PrefetchScalarGridSpec(
            num_scalar_prefetch=1,
            grid=(n // tc,),
            in_specs=[
                pl.BlockSpec((tc, d_model), lambda i, dest: (i, 0)),
                pl.BlockSpec((tc, LANES), lambda i, dest: (i, 0)),
                pl.BlockSpec((1, d_model), lambda i, dest: (0, 0)),
                pl.BlockSpec(memory_space=pl.ANY),
            ],
            out_specs=pl.BlockSpec((tc, d_model), lambda i, dest: (i, 0)),
            scratch_shapes=[
                pltpu.VMEM((tc, d_model), F32),
                pltpu.VMEM((tc, d_model), F32),
                pltpu.SemaphoreType.DMA,
            ],
        ),
        out_shape=jax.ShapeDtypeStruct((n, d_model), F32),
        compiler_params=_cparams(("arbitrary",)),
        name="moe_combine",
    )(dest, h, route, gain, y)


def _routing_tables(route, cnt, n_rows):
    bm = MOE_BLOCK
    expert = route[:, 0:2].astype(jnp.int32)
    rank = route[:, 4:6].astype(jnp.int32)
    counts = cnt[0, N_GROUPS:N_GROUPS + N_EXPERTS].astype(jnp.int32)
    padded = (counts + bm - 1) // bm * bm
    pad_ends = jnp.cumsum(padded)
    pad_starts = pad_ends - padded
    dest = (pad_starts[expert] + rank).reshape(-1).astype(jnp.int32)
    n_blocks = n_rows // bm
    block_e = jnp.minimum(jnp.searchsorted(pad_ends, jnp.arange(n_blocks, dtype=jnp.int32) * bm, side='right'),
                          N_EXPERTS - 1).astype(jnp.int32)
    n_used = (pad_ends[-1:] // bm).astype(jnp.int32)
    return dest, block_e, n_used


def kernel(x, mem, mem_norm, mix_norm, w_in, b_forget, w_alpha_up, b_alpha, fox_out_gain, gla_out_gain, w_out,
           cross_norm, w_xq, w_xk, w_xv, w_xo, moe_norm, w_router_group, b_router_group, w_router_expert,
           b_router_expert, w_expert_gate, w_expert_up, w_expert_down, final_norm):
    batch, seq, d_model = x.shape
    mem_len = mem.shape[1]
    depth = w_in.shape[0]
    n = batch * seq
    assert seq % FOX_TILE == 0 and seq % IN_TILE == 0 and seq % POST_TILE == 0 and seq % GLA_CHUNK == 0
    assert n % MOVE_TILE == 0 and d_model % LANES == 0

    c0 = 3 * FOX_WIDTH
    c1 = c0 + FOX_HEADS
    c2 = c1 + 2 * GLA_QK + 2 * GLA_V
    w_main = jnp.concatenate([w_in[:, :, :c0], w_in[:, :, c1:c2]], axis=-1).astype(BF16)
    pad = LANES - FOX_HEADS - GLA_RANK
    w_small = jnp.concatenate([w_in[:, :, c0:c1], w_in[:, :, c2:], jnp.zeros((depth, d_model, pad), F32)],
                              axis=-1).astype(BF16)
    w_up = jnp.concatenate([jnp.zeros((depth, FOX_HEADS, GLA_QK), F32), w_alpha_up,
                            jnp.zeros((depth, pad, GLA_QK), F32)], axis=1).astype(BF16)
    b_f = jnp.pad(b_forget, ((0, 0), (0, LANES - FOX_HEADS)))[:, None, :]
    b_a = b_alpha[:, None, :]
    w_r = jnp.concatenate([w_router_group, w_router_expert,
                           jnp.zeros((depth, d_model, LANES - N_GROUPS - N_EXPERTS), F32)], axis=-1)
    w_rh = w_r.astype(BF16)
    w_rl = (w_r - w_rh.astype(F32)).astype(BF16)
    b_r = jnp.pad(jnp.concatenate([b_router_group, b_router_expert], axis=-1),
                  ((0, 0), (0, LANES - N_GROUPS - N_EXPERTS)))[:, None, :]
    w_out_b = w_out.astype(BF16)
    w_xq_b = w_xq.astype(BF16)
    w_xo_b = w_xo.astype(BF16)
    w_eg = w_expert_gate.astype(BF16)
    w_eu = w_expert_up.astype(BF16)
    w_ed = w_expert_down.astype(BF16)

    kmem, vmem = _mem_kv(mem.reshape(batch * mem_len, d_model), mem_norm[None, :],
                         w_xk.astype(BF16), w_xv.astype(BF16), batch, mem_len)

    n_rows = 2 * n + N_EXPERTS * MOE_BLOCK
    nq = seq // FOX_TILE
    h = x.reshape(n, d_model)
    for l in range(depth):
        main, logf, loga = _in_proj(h, mix_norm[l][None, :], w_main[l], w_small[l], w_up[l], b_f[l], b_a[l])
        c = _seq_cumsum(logf, batch, seq)
        c5 = c[:, :FOX_HEADS].reshape(batch, nq, FOX_TILE, FOX_HEADS // 2, 2).transpose(0, 3, 1, 4, 2)
        fox = _fox_attention(main, c5, fox_out_gain[l][None, :], batch, seq)
        gla = _gla(main, loga, gla_out_gain[l][None, :], batch, seq)
        h2, hn2, route, cnt = _post(fox, gla, h, w_out_b[l], cross_norm[l][None, :], w_xq_b[l], kmem[l], vmem[l],
                                    w_xo_b[l], moe_norm[l][None, :], w_rh[l], w_rl[l], b_r[l], seq, mem_len)
        dest, block_e, n_used = _routing_tables(route, cnt, n_rows)
        xs = _dispatch(dest, hn2, jnp.zeros((n_rows, d_model), F32))
        y = _experts(block_e, n_used, xs, w_eg[l], w_eu[l], w_ed[l])
        h = _combine(dest, h2, route, final_norm[None, :], y, final=(l == depth - 1))
    return h.reshape(batch, seq, d_model)
```

```python
import functools

import jax
import jax.numpy as jnp
from jax import lax
from jax.experimental import pallas as pl
from jax.experimental.pallas import tpu as pltpu

F32 = jnp.float32
BF16 = jnp.bfloat16
EPS = 1e-6

FOX_HEADS = 8
FOX_DIM = 64
FOX_WIDTH = FOX_HEADS * FOX_DIM
GLA_HEADS = 4
GLA_DK = 64
GLA_DV = 128
GLA_QK = GLA_HEADS * GLA_DK
GLA_V = GLA_HEADS * GLA_DV
GLA_RANK = 16
GLA_TAU = 16.0
GLA_CHUNK = 64
X_HEADS = 4
X_DIM = 128
X_WIDTH = X_HEADS * X_DIM
N_GROUPS = 4
GROUP_SIZE = 4
N_EXPERTS = N_GROUPS * GROUP_SIZE
MAIN_WIDTH = 3 * FOX_WIDTH + 2 * GLA_QK + 2 * GLA_V

LANES = 128
VMEM_LIMIT = 56 * 1024 * 1024

IN_TILE = 512
FOX_TILE = 256
POST_TILE = 256
CUMSUM_TILE = 256
MOE_BLOCK = 256
MOVE_TILE = 256
ISSUE_UNROLL = 8


def _cparams(sem):
    return pltpu.CompilerParams(dimension_semantics=sem, vmem_limit_bytes=VMEM_LIMIT)


def _rms(x, gain):
    return x * lax.rsqrt(jnp.mean(x * x, axis=-1, keepdims=True) + EPS) * gain


def _log_sigmoid(x):
    return jnp.minimum(x, 0.0) - jnp.log1p(jnp.exp(-jnp.abs(x)))


def _dot(a, b):
    return jnp.dot(a, b, preferred_element_type=F32)


def _dot_nt(a, b):
    return lax.dot_general(a, b, (((1,), (1,)), ((), ())), preferred_element_type=F32)


def _pack_rows(x):
    half = x.shape[1] // 2
    lo = lax.bitcast_convert_type(x[:, :half].astype(BF16).astype(F32), jnp.uint32)
    hi = lax.bitcast_convert_type(x[:, half:].astype(BF16).astype(F32), jnp.uint32)
    return (lo >> 16) | hi


def _unpack_rows(w):
    lo = lax.bitcast_convert_type(w << 16, F32)
    hi = lax.bitcast_convert_type(w & jnp.uint32(0xFFFF0000), F32)
    return lo, hi


def _split3(x):
    hi = x.astype(BF16)
    r1 = x - hi.astype(F32)
    mid = r1.astype(BF16)
    lo = (r1 - mid.astype(F32)).astype(BF16)
    return hi, mid, lo


def _mem_kv_kernel(mem_ref, gain_ref, wk_ref, wv_ref, k_ref, v_ref):
    mn = _rms(mem_ref[...], gain_ref[...]).astype(BF16)
    for l in range(wk_ref.shape[0]):
        k_ref[l] = _dot(mn, wk_ref[l]).astype(BF16)
        v_ref[l] = _dot(mn, wv_ref[l]).astype(BF16)


def _mem_kv(mem2d, gain, wk, wv, batch, mem_len):
    depth, d_model, width = wk.shape
    out = jax.ShapeDtypeStruct((depth, batch * mem_len, width), BF16)
    return pl.pallas_call(
        _mem_kv_kernel,
        grid=(batch,),
        in_specs=[
            pl.BlockSpec((mem_len, d_model), lambda b: (b, 0)),
            pl.BlockSpec((1, d_model), lambda b: (0, 0)),
            pl.BlockSpec((depth, d_model, width), lambda b: (0, 0, 0)),
            pl.BlockSpec((depth, d_model, width), lambda b: (0, 0, 0)),
        ],
        out_specs=[
            pl.BlockSpec((depth, mem_len, width), lambda b: (0, b, 0)),
            pl.BlockSpec((depth, mem_len, width), lambda b: (0, b, 0)),
        ],
        out_shape=[out, out],
        compiler_params=_cparams(("arbitrary",)),
        name="mem_kv",
    )(mem2d, gain, wk, wv)


def _in_proj_kernel(h_ref, gain_ref, wmain_ref, wsmall_ref, wup_ref, bf_ref, ba_ref,
                    main_ref, logf_ref, loga_ref):
    xn = _rms(h_ref[...], gain_ref[...]).astype(BF16)
    step = 512
    for j in range(MAIN_WIDTH // step):
        main_ref[:, j * step:(j + 1) * step] = _dot(xn, wmain_ref[:, j * step:(j + 1) * step]).astype(BF16)
    small = _dot(xn, wsmall_ref[...])
    lane = lax.broadcasted_iota(jnp.int32, small.shape, 1)
    logf_ref[...] = jnp.where(lane < FOX_HEADS, _log_sigmoid(small + bf_ref[...]), 0.0)
    a = _dot(small.astype(BF16), wup_ref[...]) + ba_ref[...]
    loga_ref[...] = _log_sigmoid(a) * (1.0 / GLA_TAU)


def _in_proj(h, gain, wmain, wsmall, wup, bf, ba):
    n, d_model = h.shape
    tm = IN_TILE
    return pl.pallas_call(
        _in_proj_kernel,
        grid=(n // tm,),
        in_specs=[
            pl.BlockSpec((tm, d_model), lambda i: (i, 0)),
            pl.BlockSpec((1, d_model), lambda i: (0, 0)),
            pl.BlockSpec((d_model, MAIN_WIDTH), lambda i: (0, 0)),
            pl.BlockSpec((d_model, LANES), lambda i: (0, 0)),
            pl.BlockSpec((LANES, GLA_QK), lambda i: (0, 0)),
            pl.BlockSpec((1, LANES), lambda i: (0, 0)),
            pl.BlockSpec((1, GLA_QK), lambda i: (0, 0)),
        ],
        out_specs=[
            pl.BlockSpec((tm, MAIN_WIDTH), lambda i: (i, 0)),
            pl.BlockSpec((tm, LANES), lambda i: (i, 0)),
            pl.BlockSpec((tm, GLA_QK), lambda i: (i, 0)),
        ],
        out_shape=[
            jax.ShapeDtypeStruct((n, MAIN_WIDTH), BF16),
            jax.ShapeDtypeStruct((n, LANES), F32),
            jax.ShapeDtypeStruct((n, GLA_QK), F32),
        ],
        compiler_params=_cparams(("arbitrary",)),
        name="in_proj",
    )(h, gain, wmain, wsmall, wup, bf, ba)


def _cumsum_kernel(x_ref, o_ref):
    t = CUMSUM_TILE
    row = lax.broadcasted_iota(jnp.int32, (t, t), 0)
    col = lax.broadcasted_iota(jnp.int32, (t, t), 1)
    tril = (row >= col).astype(BF16)
    carry = jnp.zeros((1, x_ref.shape[1]), F32)
    for j in range(x_ref.shape[0] // t):
        hi, mid, lo = _split3(x_ref[j * t:(j + 1) * t, :])
        c = _dot(tril, hi) + _dot(tril, mid) + _dot(tril, lo) + carry
        o_ref[j * t:(j + 1) * t, :] = c
        carry = c[t - 1:t, :]


def _seq_cumsum(x, batch, seq):
    return pl.pallas_call(
        _cumsum_kernel,
        grid=(batch,),
        in_specs=[pl.BlockSpec((seq, LANES), lambda b: (b, 0))],
        out_specs=pl.BlockSpec((seq, LANES), lambda b: (b, 0)),
        out_shape=jax.ShapeDtypeStruct(x.shape, F32),
        compiler_params=_cparams(("arbitrary",)),
        name="forget_cumsum",
    )(x)


def _fox_kernel(q_ref, k_ref, v_ref, c_ref, gain_ref, o_ref):
    tq = q_ref.shape[0]
    qi = pl.program_id(2)
    q = q_ref[...]
    lane = lax.broadcasted_iota(jnp.int32, (1, LANES), 1)
    first = lane < FOX_DIM
    zero = jnp.zeros_like(q)
    q2 = jnp.concatenate([jnp.where(first, q, zero), jnp.where(first, zero, q)], axis=0)
    q2 = q2 * jnp.asarray(FOX_DIM ** -0.5, BF16)
    second = lax.broadcasted_iota(jnp.int32, (2 * tq, 1), 0) >= tq
    row = lax.broadcasted_iota(jnp.int32, (2 * tq, tq), 0)
    col = lax.broadcasted_iota(jnp.int32, (2 * tq, tq), 1)
    causal = jnp.where(second, row - tq, row) >= col

    def scores(j):
        r0 = pl.multiple_of(j * tq, tq)
        cj = c_ref[j]
        return _dot_nt(q2, k_ref[pl.ds(r0, tq), :]) - jnp.where(second, cj[1:2, :], cj[0:1, :])

    def softmax_step(s, m, l):
        m_new = jnp.maximum(m, jnp.max(s, axis=-1, keepdims=True))
        alpha = jnp.exp(m - m_new)
        p = jnp.exp(s - m_new)
        return m_new, alpha * l + jnp.sum(p, axis=-1, keepdims=True), alpha, p.astype(BF16)

    def weighted_values(p, j):
        r0 = pl.multiple_of(j * tq, tq)
        return _dot(p, v_ref[pl.ds(r0, tq), :])

    def step(j, carry):
        m, l, acc, s, p_prev, a_prev = carry
        s_next = scores(j + 1)
        acc = a_prev * acc + weighted_values(p_prev, jnp.maximum(j - 1, 0))
        m, l, alpha, p = softmax_step(s, m, l)
        return m, l, acc, s_next, p, alpha

    init = (jnp.full((2 * tq, 1), -jnp.inf, F32), jnp.zeros((2 * tq, 1), F32), jnp.zeros((2 * tq, LANES), F32),
            scores(0), jnp.zeros((2 * tq, tq), BF16), jnp.ones((2 * tq, 1), F32))
    m, l, acc, s, p_prev, a_prev = lax.fori_loop(0, qi, step, init)
    acc = a_prev * acc + weighted_values(p_prev, jnp.maximum(qi - 1, 0))
    m, l, alpha, p = softmax_step(jnp.where(causal, s, -jnp.inf), m, l)
    acc = alpha * acc + weighted_values(p, qi)
    o2 = acc / l
    o = jnp.where(first, o2[:tq], o2[tq:])
    sq = o * o
    ss0 = jnp.sum(jnp.where(first, sq, 0.0), axis=-1, keepdims=True)
    ss1 = jnp.sum(jnp.where(first, 0.0, sq), axis=-1, keepdims=True)
    ms = jnp.where(first, ss0, ss1) * (1.0 / FOX_DIM)
    o_ref[...] = (o * lax.rsqrt(ms + EPS) * gain_ref[...]).astype(BF16)


def _fox_attention(main, c5, gain, batch, seq):
    n = main.shape[0]
    tq = FOX_TILE
    nq = seq // tq
    pairs = FOX_HEADS // 2
    k_off = FOX_WIDTH // LANES
    v_off = 2 * FOX_WIDTH // LANES
    return pl.pallas_call(
        _fox_kernel,
        grid=(batch, pairs, nq),
        in_specs=[
            pl.BlockSpec((tq, LANES), lambda b, p, i: (b * nq + i, p)),
            pl.BlockSpec((seq, LANES), lambda b, p, i: (b, k_off + p)),
            pl.BlockSpec((seq, LANES), lambda b, p, i: (b, v_off + p)),
            pl.BlockSpec((None, None, nq, 2, tq), lambda b, p, i: (b, p, 0, 0, 0)),
            pl.BlockSpec((1, LANES), lambda b, p, i: (0, p)),
        ],
        out_specs=pl.BlockSpec((tq, LANES), lambda b, p, i: (b * nq + i, p)),
        out_shape=jax.ShapeDtypeStruct((n, FOX_WIDTH), BF16),
        compiler_params=_cparams(("arbitrary", "arbitrary", "arbitrary")),
        name="fox_attention",
    )(main, main, main, c5, gain)


def _gla_kernel(q_ref, k_ref, v_ref, gg_ref, la_ref, gain_ref, o_ref, qe_ref, ke_ref, kl_ref, dec_ref, raw_ref):
    seq = q_ref.shape[0]
    cs = GLA_CHUNK
    nc = seq // cs
    width = 2 * GLA_DK

    b = la_ref[...]
    pos = lax.broadcasted_iota(jnp.int32, (seq, width), 0) % cs
    shift = 1
    while shift < cs:
        b = b + jnp.where(pos >= shift, pltpu.roll(b, shift, axis=0), 0.0)
        shift *= 2
    b3 = b.reshape(nc, cs, width)
    b_last = b3[:, cs - 1:cs, :]
    q = q_ref[...].astype(F32)
    k = k_ref[...].astype(F32)
    qe_ref[...] = (q * jnp.exp(b) * (GLA_DK ** -0.5)).astype(BF16)
    ke_ref[...] = (k * jnp.exp(-b)).astype(BF16)
    kl_ref[...] = (k.reshape(nc, cs, width) * jnp.exp(b_last - b3)).reshape(seq, width).astype(BF16)
    dec_ref[...] = jnp.exp(b_last).reshape(nc, width)

    lane = lax.broadcasted_iota(jnp.int32, (1, width), 1)
    first = lane < GLA_DK
    row = lax.broadcasted_iota(jnp.int32, (2 * cs, cs), 0)
    col = lax.broadcasted_iota(jnp.int32, (2 * cs, cs), 1)
    tril2 = jnp.where(row >= cs, row - cs, row) >= col
    srow = lax.broadcasted_iota(jnp.int32, (2 * GLA_DV, width), 0)
    scol = lax.broadcasted_iota(jnp.int32, (2 * GLA_DV, width), 1)
    same_head = (srow >= GLA_DV) == (scol >= GLA_DK)
    unroll = 2

    def chunks(ci, st):
        for u in range(unroll):
            c = ci * unroll + u
            r0 = pl.multiple_of(c * cs, cs)
            qe = qe_ref[pl.ds(r0, cs), :]
            ke = ke_ref[pl.ds(r0, cs), :]
            kl = kl_ref[pl.ds(r0, cs), :]
            v = v_ref[pl.ds(r0, cs), :]
            zero = jnp.zeros_like(qe)
            q2 = jnp.concatenate([jnp.where(first, qe, zero), jnp.where(first, zero, qe)], axis=0)
            att = jnp.where(tril2, _dot_nt(q2, ke), 0.0).astype(BF16)
            oi = _dot(att, v)
            o = _dot_nt(qe, st.astype(BF16))
            o = o + jnp.concatenate([oi[:cs, :GLA_DV], oi[cs:, GLA_DV:]], axis=1)
            raw_ref[pl.ds(r0, cs), :] = o
            upd = lax.dot_general(v, kl, (((0,), (0,)), ((), ())), preferred_element_type=F32)
            st = st * dec_ref[pl.ds(c, 1), :] + jnp.where(same_head, upd, 0.0)
        return st

    lax.fori_loop(0, nc // unroll, chunks, jnp.zeros((2 * GLA_DV, width), F32))

    o = raw_ref[...]
    normed = []
    for h in range(2):
        oh = o[:, h * GLA_DV:(h + 1) * GLA_DV]
        normed.append(oh * lax.rsqrt(jnp.mean(oh * oh, axis=-1, keepdims=True) + EPS))
    g = gg_ref[...].astype(F32)
    o_ref[...] = (jnp.concatenate(normed, axis=1) * gain_ref[...] * (g * jax.nn.sigmoid(g))).astype(BF16)


def _gla(main, loga, gain, batch, seq):
    n = main.shape[0]
    pairs = GLA_HEADS // 2
    q_off = 3 * FOX_WIDTH // LANES
    k_off = q_off + GLA_QK // LANES
    pv = 2 * GLA_DV
    v_off = (3 * FOX_WIDTH + 2 * GLA_QK) // pv
    g_off = v_off + GLA_V // pv
    return pl.pallas_call(
        _gla_kernel,
        grid=(batch, pairs),
        in_specs=[
            pl.BlockSpec((seq, LANES), lambda b, p: (b, q_off + p)),
            pl.BlockSpec((seq, LANES), lambda b, p: (b, k_off + p)),
            pl.BlockSpec((seq, pv), lambda b, p: (b, v_off + p)),
            pl.BlockSpec((seq, pv), lambda b, p: (b, g_off + p)),
            pl.BlockSpec((seq, LANES), lambda b, p: (b, p)),
            pl.BlockSpec((1, pv), lambda b, p: (0, p)),
        ],
        out_specs=pl.BlockSpec((seq, pv), lambda b, p: (b, p)),
        out_shape=jax.ShapeDtypeStruct((n, GLA_V), BF16),
        scratch_shapes=[
            pltpu.VMEM((seq, LANES), BF16),
            pltpu.VMEM((seq, LANES), BF16),
            pltpu.VMEM((seq, LANES), BF16),
            pltpu.VMEM((seq // GLA_CHUNK, LANES), F32),
            pltpu.VMEM((seq, pv), F32),
        ],
        compiler_params=_cparams(("arbitrary", "arbitrary")),
        name="gla",
    )(main, main, main, main, loga, gain)


def _post_kernel(fox_ref, gla_ref, h_ref, wout_ref, cg_ref, wxq_ref, k_ref, v_ref, wxo_ref, mg_ref,
                 wrh_ref, wrl_ref, br_ref, h2_ref, hn_ref, route_ref, cnt_ref, carry_ref):
    tm = h_ref.shape[0]

    @pl.when(pl.program_id(0) == 0)
    def _():
        carry_ref[...] = jnp.zeros_like(carry_ref)

    y = _dot(fox_ref[...], wout_ref[0:FOX_WIDTH, :]) + _dot(gla_ref[...], wout_ref[FOX_WIDTH:, :])
    h1 = h_ref[...] + y
    hn = _rms(h1, cg_ref[...]).astype(BF16)
    q = _dot(hn, wxq_ref[...]).astype(BF16)
    xscale = X_DIM ** -0.5
    heads = []
    for hh in range(X_HEADS):
        sl = slice(hh * X_DIM, (hh + 1) * X_DIM)
        s = _dot_nt(q[:, sl], k_ref[:, sl]) * xscale
        p = jnp.exp(s - jnp.max(s, axis=-1, keepdims=True))
        heads.append(_dot(p.astype(BF16), v_ref[:, sl]) / jnp.sum(p, axis=-1, keepdims=True))
    o = jnp.concatenate(heads, axis=1).astype(BF16)
    h2 = h1 + _dot(o, wxo_ref[...])
    h2_ref[...] = h2
    hn2 = _rms(h2, mg_ref[...])
    hn_ref[...] = _pack_rows(hn2)

    xh = hn2.astype(BF16)
    xl = (hn2 - xh.astype(F32)).astype(BF16)
    logits = _dot(xh, wrh_ref[...]) + _dot(xl, wrh_ref[...]) + _dot(xh, wrl_ref[...]) + br_ref[...]
    lane = lax.broadcasted_iota(jnp.int32, (tm, LANES), 1)
    neg = -jnp.inf
    gl = jnp.where(lane < N_GROUPS, logits, neg)
    gmax = jnp.max(gl, axis=-1, keepdims=True)
    ge = jnp.exp(gl - gmax)
    gprob = ge / jnp.sum(ge, axis=-1, keepdims=True)
    pmax = jnp.max(gprob, axis=-1, keepdims=True)
    grp = jnp.min(jnp.where(gprob == pmax, lane, LANES), axis=-1, keepdims=True)
    in_grp = (lane >= N_GROUPS) & (lane < N_GROUPS + N_EXPERTS) & (((lane - N_GROUPS) // GROUP_SIZE) == grp)
    el = jnp.where(in_grp, logits, neg)
    emax = jnp.max(el, axis=-1, keepdims=True)
    ee = jnp.exp(el - emax)
    eprob = ee / jnp.sum(ee, axis=-1, keepdims=True)
    p1 = jnp.max(eprob, axis=-1, keepdims=True)
    lane1 = jnp.min(jnp.where(in_grp & (eprob == p1), lane, LANES), axis=-1, keepdims=True)
    rest = jnp.where(in_grp & (lane != lane1), eprob, -1.0)
    p2 = jnp.max(rest, axis=-1, keepdims=True)
    lane2 = jnp.min(jnp.where(rest == p2, lane, LANES), axis=-1, keepdims=True)
    g1 = pmax * p1 / (p1 + p2)
    g2 = pmax * p2 / (p1 + p2)

    oh1 = lane == lane1
    oh2 = lane == lane2
    both = (oh1 | oh2).astype(BF16)
    row = lax.broadcasted_iota(jnp.int32, (tm, tm), 0)
    col = lax.broadcasted_iota(jnp.int32, (tm, tm), 1)
    before = (row > col).astype(BF16)
    seen = _dot(before, both) + carry_ref[...]
    rank1 = jnp.sum(jnp.where(oh1, seen, 0.0), axis=-1, keepdims=True)
    rank2 = jnp.sum(jnp.where(oh2, seen, 0.0), axis=-1, keepdims=True)
    carry = carry_ref[...] + jnp.sum(both.astype(F32), axis=0, keepdims=True)
    carry_ref[...] = carry
    cnt_ref[...] = carry

    e1 = (lane1 - N_GROUPS).astype(F32)
    e2 = (lane2 - N_GROUPS).astype(F32)
    route = jnp.where(lane == 0, e1, 0.0)
    route = jnp.where(lane == 1, e2, route)
    route = jnp.where(lane == 2, g1, route)
    route = jnp.where(lane == 3, g2, route)
    route = jnp.where(lane == 4, rank1, route)
    route = jnp.where(lane == 5, rank2, route)
    route_ref[...] = route


def _post(fox, gla, h, wout, cg, wxq, kmem, vmem, wxo, mg, wrh, wrl, br, seq, mem_len):
    n, d_model = h.shape
    tm = POST_TILE
    per_seq = seq // tm
    const = lambda i: (0, 0)
    return pl.pallas_call(
        _post_kernel,
        grid=(n // tm,),
        in_specs=[
            pl.BlockSpec((tm, FOX_WIDTH), lambda i: (i, 0)),
            pl.BlockSpec((tm, GLA_V), lambda i: (i, 0)),
            pl.BlockSpec((tm, d_model), lambda i: (i, 0)),
            pl.BlockSpec((FOX_WIDTH + GLA_V, d_model), const),
            pl.BlockSpec((1, d_model), const),
            pl.BlockSpec((d_model, X_WIDTH), const),
            pl.BlockSpec((mem_len, X_WIDTH), lambda i: (i // per_seq, 0)),
            pl.BlockSpec((mem_len, X_WIDTH), lambda i: (i // per_seq, 0)),
            pl.BlockSpec((X_WIDTH, d_model), const),
            pl.BlockSpec((1, d_model), const),
            pl.BlockSpec((d_model, LANES), const),
            pl.BlockSpec((d_model, LANES), const),
            pl.BlockSpec((1, LANES), const),
        ],
        out_specs=[
            pl.BlockSpec((tm, d_model), lambda i: (i, 0)),
            pl.BlockSpec((tm, d_model // 2), lambda i: (i, 0)),
            pl.BlockSpec((tm, LANES), lambda i: (i, 0)),
            pl.BlockSpec((1, LANES), const),
        ],
        out_shape=[
            jax.ShapeDtypeStruct((n, d_model), F32),
            jax.ShapeDtypeStruct((n, d_model // 2), jnp.uint32),
            jax.ShapeDtypeStruct((n, LANES), F32),
            jax.ShapeDtypeStruct((1, LANES), F32),
        ],
        scratch_shapes=[pltpu.VMEM((1, LANES), F32)],
        compiler_params=_cparams(("arbitrary",)),
        name="post_mixer",
    )(fox, gla, h, wout, cg, wxq, kmem, vmem, wxo, mg, wrh, wrl, br)


def _row_copy(src_ref, src_row, dst_ref, dst_row, sem):
    return pltpu.make_async_copy(src_ref.at[pl.ds(src_row, 1)], dst_ref.at[pl.ds(dst_row, 1)], sem)


def _dispatch_kernel(dest_ref, x_ref, xs_in_ref, xs_ref, sem):
    del xs_in_ref
    td = x_ref.shape[0]
    base = pl.program_id(0) * td

    def issue(t, carry):
        for kk in range(2):
            _row_copy(x_ref, t, xs_ref, dest_ref[(base + t) * 2 + kk], sem).start()
        return carry

    lax.fori_loop(0, td, issue, 0, unroll=ISSUE_UNROLL)
    for kk in range(2):
        pltpu.make_async_copy(x_ref, xs_ref.at[pl.ds(0, td)], sem).wait()


def _dispatch(dest, x, xs_init):
    n, width = x.shape
    td = MOVE_TILE
    return pl.pallas_call(
        _dispatch_kernel,
        grid_spec=pltpu.PrefetchScalarGridSpec(
            num_scalar_prefetch=1,
            grid=(n // td,),
            in_specs=[
                pl.BlockSpec((td, width), lambda i, dest: (i, 0)),
                pl.BlockSpec(memory_space=pl.ANY),
            ],
            out_specs=pl.BlockSpec(memory_space=pl.ANY),
            scratch_shapes=[pltpu.SemaphoreType.DMA],
        ),
        out_shape=jax.ShapeDtypeStruct(xs_init.shape, xs_init.dtype),
        input_output_aliases={2: 0},
        compiler_params=_cparams(("arbitrary",)),
        name="moe_dispatch",
    )(dest, x, xs_init)


def _expert_kernel(be_ref, nu_ref, x_ref, wg_ref, wu_ref, wd_ref, y_ref):
    del be_ref
    i = pl.program_id(0)

    @pl.when(i < nu_ref[0])
    def _():
        lo, hi = _unpack_rows(x_ref[...])
        lo = lo.astype(BF16)
        hi = hi.astype(BF16)
        half = lo.shape[1]
        g = _dot(lo, wg_ref[:half, :]) + _dot(hi, wg_ref[half:, :])
        u = _dot(lo, wu_ref[:half, :]) + _dot(hi, wu_ref[half:, :])
        a = (g * jax.nn.sigmoid(g) * u).astype(BF16)
        y_ref[...] = _pack_rows(_dot(a, wd_ref[...]))

    @pl.when(i >= nu_ref[0])
    def _():
        y_ref[...] = jnp.zeros_like(y_ref)


def _experts(block_e, n_used, xs, wg, wu, wd):
    r, width = xs.shape
    bm = MOE_BLOCK
    d_model, d_exp = wg.shape[-2:]
    return pl.pallas_call(
        _expert_kernel,
        grid_spec=pltpu.PrefetchScalarGridSpec(
            num_scalar_prefetch=2,
            grid=(r // bm,),
            in_specs=[
                pl.BlockSpec((bm, width), lambda i, be, nu: (i, 0)),
                pl.BlockSpec((None, d_model, d_exp), lambda i, be, nu: (be[i], 0, 0)),
                pl.BlockSpec((None, d_model, d_exp), lambda i, be, nu: (be[i], 0, 0)),
                pl.BlockSpec((None, d_exp, d_model), lambda i, be, nu: (be[i], 0, 0)),
            ],
            out_specs=pl.BlockSpec((bm, width), lambda i, be, nu: (i, 0)),
        ),
        out_shape=jax.ShapeDtypeStruct((r, width), jnp.uint32),
        compiler_params=_cparams(("arbitrary",)),
        name="moe_experts",
    )(block_e, n_used, xs, wg, wu, wd)


def _combine_kernel(dest_ref, h_ref, route_ref, gain_ref, y_ref, o_ref, y0_ref, y1_ref, sem, *, final):
    tc = h_ref.shape[0]
    base = pl.program_id(0) * tc
    bufs = (y0_ref, y1_ref)

    def issue(t, carry):
        for kk in range(2):
            _row_copy(y_ref, dest_ref[(base + t) * 2 + kk], bufs[kk], t, sem).start()
        return carry

    lax.fori_loop(0, tc, issue, 0, unroll=ISSUE_UNROLL)
    for kk in range(2):
        pltpu.make_async_copy(y_ref.at[pl.ds(0, tc)], bufs[kk], sem).wait()

    route = route_ref[...]
    y0 = jnp.concatenate(_unpack_rows(y0_ref[...]), axis=1)
    y1 = jnp.concatenate(_unpack_rows(y1_ref[...]), axis=1)
    out = h_ref[...] + route[:, 2:3] * y0 + route[:, 3:4] * y1
    if final:
        out = _rms(out, gain_ref[...])
    o_ref[...] = out


def _combine(dest, h, route, gain, y, final):
    n, d_model = h.shape
    tc = MOVE_TILE
    return pl.pallas_call(
        functools.partial(_combine_kernel, final=final),
        grid_spec=pltpu.PrefetchScalarGridSpec(
            num_scalar_prefetch=1,
            grid=(n // tc,),
            in_specs=[
                pl.BlockSpec((tc, d_model), lambda i, dest: (i, 0)),
                pl.BlockSpec((tc, LANES), lambda i, dest: (i, 0)),
                pl.BlockSpec((1, d_model), lambda i, dest: (0, 0)),
                pl.BlockSpec(memory_space=pl.ANY),
            ],
            out_specs=pl.BlockSpec((tc, d_model), lambda i, dest: (i, 0)),
            scratch_shapes=[
                pltpu.VMEM((tc, d_model // 2), jnp.uint32),
                pltpu.VMEM((tc, d_model // 2), jnp.uint32),
                pltpu.SemaphoreType.DMA,
            ],
        ),
        out_shape=jax.ShapeDtypeStruct((n, d_model), F32),
        compiler_params=_cparams(("arbitrary",)),
        name="moe_combine",
    )(dest, h, route, gain, y)


def _routing_tables(route, cnt, n_rows):
    bm = MOE_BLOCK
    expert = route[:, 0:2].astype(jnp.int32)
    rank = route[:, 4:6].astype(jnp.int32)
    counts = cnt[0, N_GROUPS:N_GROUPS + N_EXPERTS].astype(jnp.int32)
    padded = (counts + bm - 1) // bm * bm
    pad_ends = jnp.cumsum(padded)
    pad_starts = pad_ends - padded
    dest = (pad_starts[expert] + rank).reshape(-1).astype(jnp.int32)
    n_blocks = n_rows // bm
    block_e = jnp.minimum(jnp.searchsorted(pad_ends, jnp.arange(n_blocks, dtype=jnp.int32) * bm, side='right'),
                          N_EXPERTS - 1).astype(jnp.int32)
    n_used = (pad_ends[-1:] // bm).astype(jnp.int32)
    return dest, block_e, n_used


def kernel(x, mem, mem_norm, mix_norm, w_in, b_forget, w_alpha_up, b_alpha, fox_out_gain, gla_out_gain, w_out,
           cross_norm, w_xq, w_xk, w_xv, w_xo, moe_norm, w_router_group, b_router_group, w_router_expert,
           b_router_expert, w_expert_gate, w_expert_up, w_expert_down, final_norm):
    batch, seq, d_model = x.shape
    mem_len = mem.shape[1]
    depth = w_in.shape[0]
    n = batch * seq
    assert seq % FOX_TILE == 0 and seq % IN_TILE == 0 and seq % POST_TILE == 0 and seq % GLA_CHUNK == 0
    assert n % MOVE_TILE == 0 and d_model % LANES == 0

    c0 = 3 * FOX_WIDTH
    c1 = c0 + FOX_HEADS
    c2 = c1 + 2 * GLA_QK + 2 * GLA_V
    w_main = jnp.concatenate([w_in[:, :, :c0], w_in[:, :, c1:c2]], axis=-1).astype(BF16)
    pad = LANES - FOX_HEADS - GLA_RANK
    w_small = jnp.concatenate([w_in[:, :, c0:c1], w_in[:, :, c2:], jnp.zeros((depth, d_model, pad), F32)],
                              axis=-1).astype(BF16)
    w_up = jnp.concatenate([jnp.zeros((depth, FOX_HEADS, GLA_QK), F32), w_alpha_up,
                            jnp.zeros((depth, pad, GLA_QK), F32)], axis=1).astype(BF16)
    b_f = jnp.pad(b_forget, ((0, 0), (0, LANES - FOX_HEADS)))[:, None, :]
    b_a = b_alpha[:, None, :]
    w_r = jnp.concatenate([w_router_group, w_router_expert,
                           jnp.zeros((depth, d_model, LANES - N_GROUPS - N_EXPERTS), F32)], axis=-1)
    w_rh = w_r.astype(BF16)
    w_rl = (w_r - w_rh.astype(F32)).astype(BF16)
    b_r = jnp.pad(jnp.concatenate([b_router_group, b_router_expert], axis=-1),
                  ((0, 0), (0, LANES - N_GROUPS - N_EXPERTS)))[:, None, :]
    w_out_b = w_out.astype(BF16)
    w_xq_b = w_xq.astype(BF16)
    w_xo_b = w_xo.astype(BF16)
    w_eg = w_expert_gate.astype(BF16)
    w_eu = w_expert_up.astype(BF16)
    w_ed = w_expert_down.astype(BF16)

    kmem, vmem = _mem_kv(mem.reshape(batch * mem_len, d_model), mem_norm[None, :],
                         w_xk.astype(BF16), w_xv.astype(BF16), batch, mem_len)

    n_rows = 2 * n + N_EXPERTS * MOE_BLOCK
    nq = seq // FOX_TILE
    h = x.reshape(n, d_model)
    for l in range(depth):
        main, logf, loga = _in_proj(h, mix_norm[l][None, :], w_main[l], w_small[l], w_up[l], b_f[l], b_a[l])
        c = _seq_cumsum(logf, batch, seq)
        c5 = c[:, :FOX_HEADS].reshape(batch, nq, FOX_TILE, FOX_HEADS // 2, 2).transpose(0, 3, 1, 4, 2)
        fox = _fox_attention(main, c5, fox_out_gain[l][None, :], batch, seq)
        gla = _gla(main, loga, gla_out_gain[l][None, :], batch, seq)
        h2, hn2, route, cnt = _post(fox, gla, h, w_out_b[l], cross_norm[l][None, :], w_xq_b[l], kmem[l], vmem[l],
                                    w_xo_b[l], moe_norm[l][None, :], w_rh[l], w_rl[l], b_r[l], seq, mem_len)
        dest, block_e, n_used = _routing_tables(route, cnt, n_rows)
        xs = _dispatch(dest, hn2, jnp.zeros((n_rows, d_model // 2), jnp.uint32))
        y = _experts(block_e, n_used, xs, w_eg[l], w_eu[l], w_ed[l])
        h = _combine(dest, h2, route, final_norm[None, :], y, final=(l == depth - 1))
    return h.reshape(batch, seq, d_model)
```

```python
import functools

import jax
import jax.numpy as jnp
from jax import lax
from jax.experimental import pallas as pl
from jax.experimental.pallas import tpu as pltpu

F32 = jnp.float32
BF16 = jnp.bfloat16
EPS = 1e-6

FOX_HEADS = 8
FOX_DIM = 64
FOX_WIDTH = FOX_HEADS * FOX_DIM
GLA_HEADS = 4
GLA_DK = 64
GLA_DV = 128
GLA_QK = GLA_HEADS * GLA_DK
GLA_V = GLA_HEADS * GLA_DV
GLA_RANK = 16
GLA_TAU = 16.0
GLA_CHUNK = 64
X_HEADS = 4
X_DIM = 128
X_WIDTH = X_HEADS * X_DIM
N_GROUPS = 4
GROUP_SIZE = 4
N_EXPERTS = N_GROUPS * GROUP_SIZE
MAIN_WIDTH = 3 * FOX_WIDTH + 2 * GLA_QK + 2 * GLA_V

LANES = 128
VMEM_LIMIT = 56 * 1024 * 1024

IN_TILE = 512
FOX_TILE = 512
FOX_SLAB = 64
POST_TILE = 512
CUMSUM_TILE = 256
MOE_BLOCK = 512
MOVE_TILE = 256
ISSUE_UNROLL = 8


def _cparams(sem):
    return pltpu.CompilerParams(dimension_semantics=sem, vmem_limit_bytes=VMEM_LIMIT)


def _rms(x, gain):
    return x * lax.rsqrt(jnp.mean(x * x, axis=-1, keepdims=True) + EPS) * gain


def _log_sigmoid(x):
    return jnp.minimum(x, 0.0) - jnp.log1p(jnp.exp(-jnp.abs(x)))


def _dot(a, b):
    return jnp.dot(a, b, preferred_element_type=F32)


def _dot_nt(a, b):
    return lax.dot_general(a, b, (((1,), (1,)), ((), ())), preferred_element_type=F32)


def _pack_rows(x):
    half = x.shape[1] // 2
    lo = lax.bitcast_convert_type(x[:, :half].astype(BF16).astype(F32), jnp.uint32)
    hi = lax.bitcast_convert_type(x[:, half:].astype(BF16).astype(F32), jnp.uint32)
    return (lo >> 16) | hi


def _unpack_rows(w):
    lo = lax.bitcast_convert_type(w << 16, F32)
    hi = lax.bitcast_convert_type(w & jnp.uint32(0xFFFF0000), F32)
    return lo, hi


def _split3(x):
    hi = x.astype(BF16)
    r1 = x - hi.astype(F32)
    mid = r1.astype(BF16)
    lo = (r1 - mid.astype(F32)).astype(BF16)
    return hi, mid, lo


def _mem_kv_kernel(mem_ref, gain_ref, wk_ref, wv_ref, k_ref, v_ref):
    mn = _rms(mem_ref[...], gain_ref[...]).astype(BF16)
    for l in range(wk_ref.shape[0]):
        k_ref[l] = _dot(mn, wk_ref[l]).astype(BF16)
        v_ref[l] = _dot(mn, wv_ref[l]).astype(BF16)


def _mem_kv(mem2d, gain, wk, wv, batch, mem_len):
    depth, d_model, width = wk.shape
    out = jax.ShapeDtypeStruct((depth, batch * mem_len, width), BF16)
    return pl.pallas_call(
        _mem_kv_kernel,
        grid=(batch,),
        in_specs=[
            pl.BlockSpec((mem_len, d_model), lambda b: (b, 0)),
            pl.BlockSpec((1, d_model), lambda b: (0, 0)),
            pl.BlockSpec((depth, d_model, width), lambda b: (0, 0, 0)),
            pl.BlockSpec((depth, d_model, width), lambda b: (0, 0, 0)),
        ],
        out_specs=[
            pl.BlockSpec((depth, mem_len, width), lambda b: (0, b, 0)),
            pl.BlockSpec((depth, mem_len, width), lambda b: (0, b, 0)),
        ],
        out_shape=[out, out],
        compiler_params=_cparams(("arbitrary",)),
        name="mem_kv",
    )(mem2d, gain, wk, wv)


def _in_proj_kernel(h_ref, gain_ref, wmain_ref, wsmall_ref, wup_ref, bf_ref, ba_ref,
                    main_ref, logf_ref, loga_ref):
    xn = _rms(h_ref[...], gain_ref[...]).astype(BF16)
    step = 512
    for j in range(MAIN_WIDTH // step):
        main_ref[:, j * step:(j + 1) * step] = _dot(xn, wmain_ref[:, j * step:(j + 1) * step]).astype(BF16)
    small = _dot(xn, wsmall_ref[...])
    lane = lax.broadcasted_iota(jnp.int32, small.shape, 1)
    logf_ref[...] = jnp.where(lane < FOX_HEADS, _log_sigmoid(small + bf_ref[...]), 0.0)
    a = _dot(small.astype(BF16), wup_ref[...]) + ba_ref[...]
    loga_ref[...] = _log_sigmoid(a) * (1.0 / GLA_TAU)


def _in_proj(h, gain, wmain, wsmall, wup, bf, ba):
    n, d_model = h.shape
    tm = IN_TILE
    return pl.pallas_call(
        _in_proj_kernel,
        grid=(n // tm,),
        in_specs=[
            pl.BlockSpec((tm, d_model), lambda i: (i, 0)),
            pl.BlockSpec((1, d_model), lambda i: (0, 0)),
            pl.BlockSpec((d_model, MAIN_WIDTH), lambda i: (0, 0)),
            pl.BlockSpec((d_model, LANES), lambda i: (0, 0)),
            pl.BlockSpec((LANES, GLA_QK), lambda i: (0, 0)),
            pl.BlockSpec((1, LANES), lambda i: (0, 0)),
            pl.BlockSpec((1, GLA_QK), lambda i: (0, 0)),
        ],
        out_specs=[
            pl.BlockSpec((tm, MAIN_WIDTH), lambda i: (i, 0)),
            pl.BlockSpec((tm, LANES), lambda i: (i, 0)),
            pl.BlockSpec((tm, GLA_QK), lambda i: (i, 0)),
        ],
        out_shape=[
            jax.ShapeDtypeStruct((n, MAIN_WIDTH), BF16),
            jax.ShapeDtypeStruct((n, LANES), F32),
            jax.ShapeDtypeStruct((n, GLA_QK), F32),
        ],
        compiler_params=_cparams(("arbitrary",)),
        name="in_proj",
    )(h, gain, wmain, wsmall, wup, bf, ba)


def _cumsum_kernel(x_ref, o_ref):
    t = CUMSUM_TILE
    row = lax.broadcasted_iota(jnp.int32, (t, t), 0)
    col = lax.broadcasted_iota(jnp.int32, (t, t), 1)
    tril = (row >= col).astype(BF16)
    carry = jnp.zeros((1, x_ref.shape[1]), F32)
    for j in range(x_ref.shape[0] // t):
        hi, mid, lo = _split3(x_ref[j * t:(j + 1) * t, :])
        c = _dot(tril, hi) + _dot(tril, mid) + _dot(tril, lo) + carry
        o_ref[j * t:(j + 1) * t, :] = c
        carry = c[t - 1:t, :]


def _seq_cumsum(x, batch, seq):
    return pl.pallas_call(
        _cumsum_kernel,
        grid=(batch,),
        in_specs=[pl.BlockSpec((seq, LANES), lambda b: (b, 0))],
        out_specs=pl.BlockSpec((seq, LANES), lambda b: (b, 0)),
        out_shape=jax.ShapeDtypeStruct(x.shape, F32),
        compiler_params=_cparams(("arbitrary",)),
        name="forget_cumsum",
    )(x)


def _fox_kernel(q_ref, k_ref, v_ref, c_ref, gain_ref, o_ref, q2_ref, s_ref, p_ref, alpha_ref, m_ref, l_ref, acc_ref):
    tq = q_ref.shape[0]
    rows = 2 * tq
    slab = FOX_SLAB
    qi = pl.program_id(2)
    q = q_ref[...]
    lane = lax.broadcasted_iota(jnp.int32, (1, LANES), 1)
    first = lane < FOX_DIM
    zero = jnp.zeros_like(q)
    scale = jnp.asarray(FOX_DIM ** -0.5, BF16)
    q2_ref[:tq, :] = jnp.where(first, q, zero) * scale
    q2_ref[tq:, :] = jnp.where(first, zero, q) * scale
    m_ref[...] = jnp.full(m_ref.shape, -jnp.inf, F32)
    l_ref[...] = jnp.zeros(l_ref.shape, F32)
    acc_ref[...] = jnp.zeros(acc_ref.shape, F32)
    alpha_ref[...] = jnp.ones(alpha_ref.shape, F32)

    def scores(j):
        r0 = pl.multiple_of(j * tq, tq)
        cj = c_ref[j]
        d = _dot_nt(q2_ref[...], k_ref[pl.ds(r0, tq), :])
        s_ref[:tq, :] = d[:tq] - cj[0:1, :]
        s_ref[tq:, :] = d[tq:] - cj[1:2, :]

    def weighted_values(j):
        r0 = pl.multiple_of(j * tq, tq)
        acc_ref[...] = alpha_ref[...] * acc_ref[...] + _dot(p_ref[...], v_ref[pl.ds(r0, tq), :])

    def softmax(masked):
        for r in range(rows // slab):
            sl = slice(r * slab, (r + 1) * slab)
            s = s_ref[sl, :]
            if masked:
                row = lax.broadcasted_iota(jnp.int32, (slab, tq), 0) + (r * slab) % tq
                col = lax.broadcasted_iota(jnp.int32, (slab, tq), 1)
                s = jnp.where(row >= col, s, -jnp.inf)
            m_old = m_ref[sl, :]
            m_new = jnp.maximum(m_old, jnp.max(s, axis=-1, keepdims=True))
            alpha = jnp.exp(m_old - m_new)
            p = jnp.exp(s - jnp.concatenate([m_new] * (tq // LANES), axis=1))
            l_ref[sl, :] = alpha * l_ref[sl, :] + jnp.sum(p, axis=-1, keepdims=True)
            m_ref[sl, :] = m_new
            alpha_ref[sl, :] = alpha
            p_ref[sl, :] = p.astype(BF16)

    def step(j, carry):
        weighted_values(jnp.maximum(j - 1, 0))
        softmax(False)
        scores(j + 1)
        return carry

    p_ref[...] = jnp.zeros(p_ref.shape, BF16)
    scores(0)
    lax.fori_loop(0, qi, step, 0)
    weighted_values(jnp.maximum(qi - 1, 0))
    softmax(True)
    weighted_values(qi)
    o2 = acc_ref[...] / l_ref[...]
    o = jnp.where(first, o2[:tq], o2[tq:])
    sq = o * o
    ss0 = jnp.sum(jnp.where(first, sq, 0.0), axis=-1, keepdims=True)
    ss1 = jnp.sum(jnp.where(first, 0.0, sq), axis=-1, keepdims=True)
    ms = jnp.where(first, ss0, ss1) * (1.0 / FOX_DIM)
    o_ref[...] = (o * lax.rsqrt(ms + EPS) * gain_ref[...]).astype(BF16)


def _fox_attention(main, c5, gain, batch, seq):
    n = main.shape[0]
    tq = FOX_TILE
    nq = seq // tq
    pairs = FOX_HEADS // 2
    k_off = FOX_WIDTH // LANES
    v_off = 2 * FOX_WIDTH // LANES
    return pl.pallas_call(
        _fox_kernel,
        grid=(batch, pairs, nq),
        in_specs=[
            pl.BlockSpec((tq, LANES), lambda b, p, i: (b * nq + i, p)),
            pl.BlockSpec((seq, LANES), lambda b, p, i: (b, k_off + p)),
            pl.BlockSpec((seq, LANES), lambda b, p, i: (b, v_off + p)),
            pl.BlockSpec((None, None, nq, 2, tq), lambda b, p, i: (b, p, 0, 0, 0)),
            pl.BlockSpec((1, LANES), lambda b, p, i: (0, p)),
        ],
        out_specs=pl.BlockSpec((tq, LANES), lambda b, p, i: (b * nq + i, p)),
        out_shape=jax.ShapeDtypeStruct((n, FOX_WIDTH), BF16),
        scratch_shapes=[
            pltpu.VMEM((2 * tq, LANES), BF16),
            pltpu.VMEM((2 * tq, tq), F32),
            pltpu.VMEM((2 * tq, tq), BF16),
            pltpu.VMEM((2 * tq, LANES), F32),
            pltpu.VMEM((2 * tq, LANES), F32),
            pltpu.VMEM((2 * tq, LANES), F32),
            pltpu.VMEM((2 * tq, LANES), F32),
        ],
        compiler_params=_cparams(("arbitrary", "arbitrary", "arbitrary")),
        name="fox_attention",
    )(main, main, main, c5, gain)


def _gla_kernel(q_ref, k_ref, v_ref, gg_ref, la_ref, gain_ref, o_ref, qe_ref, ke_ref, kl_ref, dec_ref, raw_ref):
    seq = q_ref.shape[0]
    cs = GLA_CHUNK
    nc = seq // cs
    width = 2 * GLA_DK

    b = la_ref[...]
    pos = lax.broadcasted_iota(jnp.int32, (seq, width), 0) % cs
    shift = 1
    while shift < cs:
        b = b + jnp.where(pos >= shift, pltpu.roll(b, shift, axis=0), 0.0)
        shift *= 2
    b3 = b.reshape(nc, cs, width)
    b_last = b3[:, cs - 1:cs, :]
    q = q_ref[...].astype(F32)
    k = k_ref[...].astype(F32)
    qe_ref[...] = (q * jnp.exp(b) * (GLA_DK ** -0.5)).astype(BF16)
    ke_ref[...] = (k * jnp.exp(-b)).astype(BF16)
    kl_ref[...] = (k.reshape(nc, cs, width) * jnp.exp(b_last - b3)).reshape(seq, width).astype(BF16)
    dec_ref[...] = jnp.exp(b_last).reshape(nc, width)

    lane = lax.broadcasted_iota(jnp.int32, (1, width), 1)
    first = lane < GLA_DK
    row = lax.broadcasted_iota(jnp.int32, (2 * cs, cs), 0)
    col = lax.broadcasted_iota(jnp.int32, (2 * cs, cs), 1)
    tril2 = jnp.where(row >= cs, row - cs, row) >= col
    srow = lax.broadcasted_iota(jnp.int32, (2 * GLA_DV, width), 0)
    scol = lax.broadcasted_iota(jnp.int32, (2 * GLA_DV, width), 1)
    same_head = (srow >= GLA_DV) == (scol >= GLA_DK)
    unroll = 2

    def chunks(ci, st):
        for u in range(unroll):
            c = ci * unroll + u
            r0 = pl.multiple_of(c * cs, cs)
            qe = qe_ref[pl.ds(r0, cs), :]
            ke = ke_ref[pl.ds(r0, cs), :]
            kl = kl_ref[pl.ds(r0, cs), :]
            v = v_ref[pl.ds(r0, cs), :]
            zero = jnp.zeros_like(qe)
            q2 = jnp.concatenate([jnp.where(first, qe, zero), jnp.where(first, zero, qe)], axis=0)
            att = jnp.where(tril2, _dot_nt(q2, ke), 0.0).astype(BF16)
            oi = _dot(att, v)
            o = _dot_nt(qe, st.astype(BF16))
            o = o + jnp.concatenate([oi[:cs, :GLA_DV], oi[cs:, GLA_DV:]], axis=1)
            raw_ref[pl.ds(r0, cs), :] = o
            upd = lax.dot_general(v, kl, (((0,), (0,)), ((), ())), preferred_element_type=F32)
            st = st * dec_ref[pl.ds(c, 1), :] + jnp.where(same_head, upd, 0.0)
        return st

    lax.fori_loop(0, nc // unroll, chunks, jnp.zeros((2 * GLA_DV, width), F32))

    o = raw_ref[...]
    normed = []
    for h in range(2):
        oh = o[:, h * GLA_DV:(h + 1) * GLA_DV]
        normed.append(oh * lax.rsqrt(jnp.mean(oh * oh, axis=-1, keepdims=True) + EPS))
    g = gg_ref[...].astype(F32)
    o_ref[...] = (jnp.concatenate(normed, axis=1) * gain_ref[...] * (g * jax.nn.sigmoid(g))).astype(BF16)


def _gla(main, loga, gain, batch, seq):
    n = main.shape[0]
    pairs = GLA_HEADS // 2
    q_off = 3 * FOX_WIDTH // LANES
    k_off = q_off + GLA_QK // LANES
    pv = 2 * GLA_DV
    v_off = (3 * FOX_WIDTH + 2 * GLA_QK) // pv
    g_off = v_off + GLA_V // pv
    return pl.pallas_call(
        _gla_kernel,
        grid=(batch, pairs),
        in_specs=[
            pl.BlockSpec((seq, LANES), lambda b, p: (b, q_off + p)),
            pl.BlockSpec((seq, LANES), lambda b, p: (b, k_off + p)),
            pl.BlockSpec((seq, pv), lambda b, p: (b, v_off + p)),
            pl.BlockSpec((seq, pv), lambda b, p: (b, g_off + p)),
            pl.BlockSpec((seq, LANES), lambda b, p: (b, p)),
            pl.BlockSpec((1, pv), lambda b, p: (0, p)),
        ],
        out_specs=pl.BlockSpec((seq, pv), lambda b, p: (b, p)),
        out_shape=jax.ShapeDtypeStruct((n, GLA_V), BF16),
        scratch_shapes=[
            pltpu.VMEM((seq, LANES), BF16),
            pltpu.VMEM((seq, LANES), BF16),
            pltpu.VMEM((seq, LANES), BF16),
            pltpu.VMEM((seq // GLA_CHUNK, LANES), F32),
            pltpu.VMEM((seq, pv), F32),
        ],
        compiler_params=_cparams(("arbitrary", "arbitrary")),
        name="gla",
    )(main, main, main, main, loga, gain)


def _post_kernel(fox_ref, gla_ref, h_ref, wout_ref, cg_ref, wxq_ref, k_ref, v_ref, wxo_ref, mg_ref,
                 wrh_ref, wrl_ref, br_ref, h2_ref, hn_ref, route_ref, cnt_ref, carry_ref):
    tm = h_ref.shape[0]

    @pl.when(pl.program_id(0) == 0)
    def _():
        carry_ref[...] = jnp.zeros_like(carry_ref)

    y = _dot(fox_ref[...], wout_ref[0:FOX_WIDTH, :]) + _dot(gla_ref[...], wout_ref[FOX_WIDTH:, :])
    h1 = h_ref[...] + y
    hn = _rms(h1, cg_ref[...]).astype(BF16)
    q = _dot(hn, wxq_ref[...]).astype(BF16)
    xscale = X_DIM ** -0.5
    heads = []
    for hh in range(X_HEADS):
        sl = slice(hh * X_DIM, (hh + 1) * X_DIM)
        s = _dot_nt(q[:, sl], k_ref[:, sl]) * xscale
        p = jnp.exp(s - jnp.max(s, axis=-1, keepdims=True))
        heads.append(_dot(p.astype(BF16), v_ref[:, sl]) / jnp.sum(p, axis=-1, keepdims=True))
    o = jnp.concatenate(heads, axis=1).astype(BF16)
    h2 = h1 + _dot(o, wxo_ref[...])
    h2_ref[...] = h2
    hn2 = _rms(h2, mg_ref[...])
    hn_ref[...] = _pack_rows(hn2)

    xh = hn2.astype(BF16)
    xl = (hn2 - xh.astype(F32)).astype(BF16)
    logits = _dot(xh, wrh_ref[...]) + _dot(xl, wrh_ref[...]) + _dot(xh, wrl_ref[...]) + br_ref[...]
    lane = lax.broadcasted_iota(jnp.int32, (tm, LANES), 1)
    neg = -jnp.inf
    gl = jnp.where(lane < N_GROUPS, logits, neg)
    gmax = jnp.max(gl, axis=-1, keepdims=True)
    ge = jnp.exp(gl - gmax)
    gprob = ge / jnp.sum(ge, axis=-1, keepdims=True)
    pmax = jnp.max(gprob, axis=-1, keepdims=True)
    grp = jnp.min(jnp.where(gprob == pmax, lane, LANES), axis=-1, keepdims=True)
    in_grp = (lane >= N_GROUPS) & (lane < N_GROUPS + N_EXPERTS) & (((lane - N_GROUPS) // GROUP_SIZE) == grp)
    el = jnp.where(in_grp, logits, neg)
    emax = jnp.max(el, axis=-1, keepdims=True)
    ee = jnp.exp(el - emax)
    eprob = ee / jnp.sum(ee, axis=-1, keepdims=True)
    p1 = jnp.max(eprob, axis=-1, keepdims=True)
    lane1 = jnp.min(jnp.where(in_grp & (eprob == p1), lane, LANES), axis=-1, keepdims=True)
    rest = jnp.where(in_grp & (lane != lane1), eprob, -1.0)
    p2 = jnp.max(rest, axis=-1, keepdims=True)
    lane2 = jnp.min(jnp.where(rest == p2, lane, LANES), axis=-1, keepdims=True)
    g1 = pmax * p1 / (p1 + p2)
    g2 = pmax * p2 / (p1 + p2)

    oh1 = lane == lane1
    oh2 = lane == lane2
    both = (oh1 | oh2).astype(BF16)
    row = lax.broadcasted_iota(jnp.int32, (tm, tm), 0)
    col = lax.broadcasted_iota(jnp.int32, (tm, tm), 1)
    before = (row > col).astype(BF16)
    seen = _dot(before, both) + carry_ref[...]
    rank1 = jnp.sum(jnp.where(oh1, seen, 0.0), axis=-1, keepdims=True)
    rank2 = jnp.sum(jnp.where(oh2, seen, 0.0), axis=-1, keepdims=True)
    carry = carry_ref[...] + jnp.sum(both.astype(F32), axis=0, keepdims=True)
    carry_ref[...] = carry
    cnt_ref[...] = carry

    e1 = (lane1 - N_GROUPS).astype(F32)
    e2 = (lane2 - N_GROUPS).astype(F32)
    route = jnp.where(lane == 0, e1, 0.0)
    route = jnp.where(lane == 1, e2, route)
    route = jnp.where(lane == 2, g1, route)
    route = jnp.where(lane == 3, g2, route)
    route = jnp.where(lane == 4, rank1, route)
    route = jnp.where(lane == 5, rank2, route)
    route_ref[...] = route


def _post(fox, gla, h, wout, cg, wxq, kmem, vmem, wxo, mg, wrh, wrl, br, seq, mem_len):
    n, d_model = h.shape
    tm = POST_TILE
    per_seq = seq // tm
    const = lambda i: (0, 0)
    return pl.pallas_call(
        _post_kernel,
        grid=(n // tm,),
        in_specs=[
            pl.BlockSpec((tm, FOX_WIDTH), lambda i: (i, 0)),
            pl.BlockSpec((tm, GLA_V), lambda i: (i, 0)),
            pl.BlockSpec((tm, d_model), lambda i: (i, 0)),
            pl.BlockSpec((FOX_WIDTH + GLA_V, d_model), const),
            pl.BlockSpec((1, d_model), const),
            pl.BlockSpec((d_model, X_WIDTH), const),
            pl.BlockSpec((mem_len, X_WIDTH), lambda i: (i // per_seq, 0)),
            pl.BlockSpec((mem_len, X_WIDTH), lambda i: (i // per_seq, 0)),
            pl.BlockSpec((X_WIDTH, d_model), const),
            pl.BlockSpec((1, d_model), const),
            pl.BlockSpec((d_model, LANES), const),
            pl.BlockSpec((d_model, LANES), const),
            pl.BlockSpec((1, LANES), const),
        ],
        out_specs=[
            pl.BlockSpec((tm, d_model), lambda i: (i, 0)),
            pl.BlockSpec((tm, d_model // 2), lambda i: (i, 0)),
            pl.BlockSpec((tm, LANES), lambda i: (i, 0)),
            pl.BlockSpec((1, LANES), const),
        ],
        out_shape=[
            jax.ShapeDtypeStruct((n, d_model), F32),
            jax.ShapeDtypeStruct((n, d_model // 2), jnp.uint32),
            jax.ShapeDtypeStruct((n, LANES), F32),
            jax.ShapeDtypeStruct((1, LANES), F32),
        ],
        scratch_shapes=[pltpu.VMEM((1, LANES), F32)],
        compiler_params=_cparams(("arbitrary",)),
        name="post_mixer",
    )(fox, gla, h, wout, cg, wxq, kmem, vmem, wxo, mg, wrh, wrl, br)


def _row_copy(src_ref, src_row, dst_ref, dst_row, sem):
    return pltpu.make_async_copy(src_ref.at[pl.ds(src_row, 1)], dst_ref.at[pl.ds(dst_row, 1)], sem)


def _dispatch_kernel(dest_ref, x_ref, xs_in_ref, xs_ref, sem):
    del xs_in_ref
    td = x_ref.shape[0]
    base = pl.program_id(0) * td

    def issue(t, carry):
        for kk in range(2):
            _row_copy(x_ref, t, xs_ref, dest_ref[(base + t) * 2 + kk], sem).start()
        return carry

    lax.fori_loop(0, td, issue, 0, unroll=ISSUE_UNROLL)
    for kk in range(2):
        pltpu.make_async_copy(x_ref, xs_ref.at[pl.ds(0, td)], sem).wait()


def _dispatch(dest, x, xs_init):
    n, width = x.shape
    td = MOVE_TILE
    return pl.pallas_call(
        _dispatch_kernel,
        grid_spec=pltpu.PrefetchScalarGridSpec(
            num_scalar_prefetch=1,
            grid=(n // td,),
            in_specs=[
                pl.BlockSpec((td, width), lambda i, dest: (i, 0)),
                pl.BlockSpec(memory_space=pl.ANY),
            ],
            out_specs=pl.BlockSpec(memory_space=pl.ANY),
            scratch_shapes=[pltpu.SemaphoreType.DMA],
        ),
        out_shape=jax.ShapeDtypeStruct(xs_init.shape, xs_init.dtype),
        input_output_aliases={2: 0},
        compiler_params=_cparams(("arbitrary",)),
        name="moe_dispatch",
    )(dest, x, xs_init)


def _expert_kernel(be_ref, nu_ref, x_ref, wg_ref, wu_ref, wd_ref, y_ref):
    del be_ref
    i = pl.program_id(0)

    @pl.when(i < nu_ref[0])
    def _():
        lo, hi = _unpack_rows(x_ref[...])
        lo = lo.astype(BF16)
        hi = hi.astype(BF16)
        half = lo.shape[1]
        g = _dot(lo, wg_ref[:half, :]) + _dot(hi, wg_ref[half:, :])
        u = _dot(lo, wu_ref[:half, :]) + _dot(hi, wu_ref[half:, :])
        a = (g * jax.nn.sigmoid(g) * u).astype(BF16)
        y_ref[...] = _pack_rows(_dot(a, wd_ref[...]))

    @pl.when(i >= nu_ref[0])
    def _():
        y_ref[...] = jnp.zeros_like(y_ref)


def _experts(block_e, n_used, xs, wg, wu, wd):
    r, width = xs.shape
    bm = MOE_BLOCK
    d_model, d_exp = wg.shape[-2:]
    return pl.pallas_call(
        _expert_kernel,
        grid_spec=pltpu.PrefetchScalarGridSpec(
            num_scalar_prefetch=2,
            grid=(r // bm,),
            in_specs=[
                pl.BlockSpec((bm, width), lambda i, be, nu: (i, 0)),
                pl.BlockSpec((None, d_model, d_exp), lambda i, be, nu: (be[i], 0, 0)),
                pl.BlockSpec((None, d_model, d_exp), lambda i, be, nu: (be[i], 0, 0)),
                pl.BlockSpec((None, d_exp, d_model), lambda i, be, nu: (be[i], 0, 0)),
            ],
            out_specs=pl.BlockSpec((bm, width), lambda i, be, nu: (i, 0)),
        ),
        out_shape=jax.ShapeDtypeStruct((r, width), jnp.uint32),
        compiler_params=_cparams(("arbitrary",)),
        name="moe_experts",
    )(block_e, n_used, xs, wg, wu, wd)


def _combine_kernel(dest_ref, h_ref, route_ref, gain_ref, y_ref, o_ref, y0_ref, y1_ref, sem, *, final):
    tc = h_ref.shape[0]
    base = pl.program_id(0) * tc
    bufs = (y0_ref, y1_ref)

    def issue(t, carry):
        for kk in range(2):
            _row_copy(y_ref, dest_ref[(base + t) * 2 + kk], bufs[kk], t, sem).start()
        return carry

    lax.fori_loop(0, tc, issue, 0, unroll=ISSUE_UNROLL)
    for kk in range(2):
        pltpu.make_async_copy(y_ref.at[pl.ds(0, tc)], bufs[kk], sem).wait()

    route = route_ref[...]
    y0 = jnp.concatenate(_unpack_rows(y0_ref[...]), axis=1)
    y1 = jnp.concatenate(_unpack_rows(y1_ref[...]), axis=1)
    out = h_ref[...] + route[:, 2:3] * y0 + route[:, 3:4] * y1
    if final:
        out = _rms(out, gain_ref[...])
    o_ref[...] = out


def _combine(dest, h, route, gain, y, final):
    n, d_model = h.shape
    tc = MOVE_TILE
    return pl.pallas_call(
        functools.partial(_combine_kernel, final=final),
        grid_spec=pltpu.PrefetchScalarGridSpec(
            num_scalar_prefetch=1,
            grid=(n // tc,),
            in_specs=[
                pl.BlockSpec((tc, d_model), lambda i, dest: (i, 0)),
                pl.BlockSpec((tc, LANES), lambda i, dest: (i, 0)),
                pl.BlockSpec((1, d_model), lambda i, dest: (0, 0)),
                pl.BlockSpec(memory_space=pl.ANY),
            ],
            out_specs=pl.BlockSpec((tc, d_model), lambda i, dest: (i, 0)),
            scratch_shapes=[
                pltpu.VMEM((tc, d_model // 2), jnp.uint32),
                pltpu.VMEM((tc, d_model // 2), jnp.uint32),
                pltpu.SemaphoreType.DMA,
            ],
        ),
        out_shape=jax.ShapeDtypeStruct((n, d_model), F32),
        compiler_params=_cparams(("arbitrary",)),
        name="moe_combine",
    )(dest, h, route, gain, y)


def _routing_tables(route, cnt, n_rows):
    bm = MOE_BLOCK
    expert = route[:, 0:2].astype(jnp.int32)
    rank = route[:, 4:6].astype(jnp.int32)
    counts = cnt[0, N_GROUPS:N_GROUPS + N_EXPERTS].astype(jnp.int32)
    padded = (counts + bm - 1) // bm * bm
    pad_ends = jnp.cumsum(padded)
    pad_starts = pad_ends - padded
    dest = (pad_starts[expert] + rank).reshape(-1).astype(jnp.int32)
    n_blocks = n_rows // bm
    block_e = jnp.minimum(jnp.searchsorted(pad_ends, jnp.arange(n_blocks, dtype=jnp.int32) * bm, side='right'),
                          N_EXPERTS - 1).astype(jnp.int32)
    n_used = (pad_ends[-1:] // bm).astype(jnp.int32)
    return dest, block_e, n_used


def kernel(x, mem, mem_norm, mix_norm, w_in, b_forget, w_alpha_up, b_alpha, fox_out_gain, gla_out_gain, w_out,
           cross_norm, w_xq, w_xk, w_xv, w_xo, moe_norm, w_router_group, b_router_group, w_router_expert,
           b_router_expert, w_expert_gate, w_expert_up, w_expert_down, final_norm):
    batch, seq, d_model = x.shape
    mem_len = mem.shape[1]
    depth = w_in.shape[0]
    n = batch * seq
    assert seq % FOX_TILE == 0 and seq % IN_TILE == 0 and seq % POST_TILE == 0 and seq % GLA_CHUNK == 0
    assert n % MOVE_TILE == 0 and d_model % LANES == 0

    c0 = 3 * FOX_WIDTH
    c1 = c0 + FOX_HEADS
    c2 = c1 + 2 * GLA_QK + 2 * GLA_V
    w_main = jnp.concatenate([w_in[:, :, :c0], w_in[:, :, c1:c2]], axis=-1).astype(BF16)
    pad = LANES - FOX_HEADS - GLA_RANK
    w_small = jnp.concatenate([w_in[:, :, c0:c1], w_in[:, :, c2:], jnp.zeros((depth, d_model, pad), F32)],
                              axis=-1).astype(BF16)
    w_up = jnp.concatenate([jnp.zeros((depth, FOX_HEADS, GLA_QK), F32), w_alpha_up,
                            jnp.zeros((depth, pad, GLA_QK), F32)], axis=1).astype(BF16)
    b_f = jnp.pad(b_forget, ((0, 0), (0, LANES - FOX_HEADS)))[:, None, :]
    b_a = b_alpha[:, None, :]
    w_r = jnp.concatenate([w_router_group, w_router_expert,
                           jnp.zeros((depth, d_model, LANES - N_GROUPS - N_EXPERTS), F32)], axis=-1)
    w_rh = w_r.astype(BF16)
    w_rl = (w_r - w_rh.astype(F32)).astype(BF16)
    b_r = jnp.pad(jnp.concatenate([b_router_group, b_router_expert], axis=-1),
                  ((0, 0), (0, LANES - N_GROUPS - N_EXPERTS)))[:, None, :]
    w_out_b = w_out.astype(BF16)
    w_xq_b = w_xq.astype(BF16)
    w_xo_b = w_xo.astype(BF16)
    w_eg = w_expert_gate.astype(BF16)
    w_eu = w_expert_up.astype(BF16)
    w_ed = w_expert_down.astype(BF16)

    kmem, vmem = _mem_kv(mem.reshape(batch * mem_len, d_model), mem_norm[None, :],
                         w_xk.astype(BF16), w_xv.astype(BF16), batch, mem_len)

    n_rows = 2 * n + N_EXPERTS * MOE_BLOCK
    nq = seq // FOX_TILE
    h = x.reshape(n, d_model)
    for l in range(depth):
        main, logf, loga = _in_proj(h, mix_norm[l][None, :], w_main[l], w_small[l], w_up[l], b_f[l], b_a[l])
        c = _seq_cumsum(logf, batch, seq)
        c5 = c[:, :FOX_HEADS].reshape(batch, nq, FOX_TILE, FOX_HEADS // 2, 2).transpose(0, 3, 1, 4, 2)
        fox = _fox_attention(main, c5, fox_out_gain[l][None, :], batch, seq)
        gla = _gla(main, loga, gla_out_gain[l][None, :], batch, seq)
        h2, hn2, route, cnt = _post(fox, gla, h, w_out_b[l], cross_norm[l][None, :], w_xq_b[l], kmem[l], vmem[l],
                                    w_xo_b[l], moe_norm[l][None, :], w_rh[l], w_rl[l], b_r[l], seq, mem_len)
        dest, block_e, n_used = _routing_tables(route, cnt, n_rows)
        xs = _dispatch(dest, hn2, jnp.zeros((n_rows, d_model // 2), jnp.uint32))
        y = _experts(block_e, n_used, xs, w_eg[l], w_eu[l], w_ed[l])
        h = _combine(dest, h2, route, final_norm[None, :], y, final=(l == depth - 1))
    return h.reshape(batch, seq, d_model)
```

```python
import functools

import jax
import jax.numpy as jnp
from jax import lax
from jax.experimental import pallas as pl
from jax.experimental.pallas import tpu as pltpu

F32 = jnp.float32
BF16 = jnp.bfloat16
EPS = 1e-6
LOG2E = 1.4426950408889634

FOX_HEADS = 8
FOX_DIM = 64
FOX_WIDTH = FOX_HEADS * FOX_DIM
GLA_HEADS = 4
GLA_DK = 64
GLA_DV = 128
GLA_QK = GLA_HEADS * GLA_DK
GLA_V = GLA_HEADS * GLA_DV
GLA_RANK = 16
GLA_TAU = 16.0
GLA_CHUNK = 64
X_HEADS = 4
X_DIM = 128
X_WIDTH = X_HEADS * X_DIM
N_GROUPS = 4
GROUP_SIZE = 4
N_EXPERTS = N_GROUPS * GROUP_SIZE
MAIN_WIDTH = 3 * FOX_WIDTH + 2 * GLA_QK + 2 * GLA_V

LANES = 128
ROUTE_WIDTH = 8
VMEM_LIMIT = 56 * 1024 * 1024

IN_TILE = 512
FOX_TILE = 512
FOX_SLAB = 64
POST_TILE = 512
CUMSUM_TILE = 256
MOE_BLOCK = 512
MOVE_TILE = 256
ISSUE_UNROLL = 8


def _cparams(sem):
    return pltpu.CompilerParams(dimension_semantics=sem, vmem_limit_bytes=VMEM_LIMIT)


def _rms(x, gain):
    return x * lax.rsqrt(jnp.mean(x * x, axis=-1, keepdims=True) + EPS) * gain


def _log_sigmoid(x):
    return jnp.minimum(x, 0.0) - jnp.log1p(jnp.exp(-jnp.abs(x)))


def _dot(a, b):
    return jnp.dot(a, b, preferred_element_type=F32)


def _dot_nt(a, b):
    return lax.dot_general(a, b, (((1,), (1,)), ((), ())), preferred_element_type=F32)


def _pack_rows(x):
    half = x.shape[1] // 2
    lo = lax.bitcast_convert_type(x[:, :half].astype(BF16).astype(F32), jnp.uint32)
    hi = lax.bitcast_convert_type(x[:, half:].astype(BF16).astype(F32), jnp.uint32)
    return (lo >> 16) | hi


def _unpack_rows(w):
    lo = lax.bitcast_convert_type(w << 16, F32)
    hi = lax.bitcast_convert_type(w & jnp.uint32(0xFFFF0000), F32)
    return lo, hi


def _split3(x):
    hi = x.astype(BF16)
    r1 = x - hi.astype(F32)
    mid = r1.astype(BF16)
    lo = (r1 - mid.astype(F32)).astype(BF16)
    return hi, mid, lo


def _mem_kv_kernel(mem_ref, gain_ref, wk_ref, wv_ref, k_ref, v_ref):
    mn = _rms(mem_ref[...], gain_ref[...]).astype(BF16)
    for l in range(wk_ref.shape[0]):
        k_ref[l] = _dot(mn, wk_ref[l]).astype(BF16)
        v_ref[l] = _dot(mn, wv_ref[l]).astype(BF16)


def _mem_kv(mem2d, gain, wk, wv, batch, mem_len):
    depth, d_model, width = wk.shape
    out = jax.ShapeDtypeStruct((depth, batch * mem_len, width), BF16)
    return pl.pallas_call(
        _mem_kv_kernel,
        grid=(batch,),
        in_specs=[
            pl.BlockSpec((mem_len, d_model), lambda b: (b, 0)),
            pl.BlockSpec((1, d_model), lambda b: (0, 0)),
            pl.BlockSpec((depth, d_model, width), lambda b: (0, 0, 0)),
            pl.BlockSpec((depth, d_model, width), lambda b: (0, 0, 0)),
        ],
        out_specs=[
            pl.BlockSpec((depth, mem_len, width), lambda b: (0, b, 0)),
            pl.BlockSpec((depth, mem_len, width), lambda b: (0, b, 0)),
        ],
        out_shape=[out, out],
        compiler_params=_cparams(("arbitrary",)),
        name="mem_kv",
    )(mem2d, gain, wk, wv)


def _in_proj_kernel(h_ref, gain_ref, wmain_ref, wsmall_ref, wup_ref, bf_ref, ba_ref,
                    main_ref, logf_ref, loga_ref):
    xn = _rms(h_ref[...], gain_ref[...]).astype(BF16)
    step = 512
    for j in range(MAIN_WIDTH // step):
        main_ref[:, j * step:(j + 1) * step] = _dot(xn, wmain_ref[:, j * step:(j + 1) * step]).astype(BF16)
    small = _dot(xn, wsmall_ref[...])
    lane = lax.broadcasted_iota(jnp.int32, small.shape, 1)
    logf_ref[...] = jnp.where(lane < FOX_HEADS, _log_sigmoid(small + bf_ref[...]), 0.0)
    a = _dot(small.astype(BF16), wup_ref[...]) + ba_ref[...]
    loga_ref[...] = _log_sigmoid(a) * (1.0 / GLA_TAU)


def _in_proj(h, gain, wmain, wsmall, wup, bf, ba):
    n, d_model = h.shape
    tm = IN_TILE
    return pl.pallas_call(
        _in_proj_kernel,
        grid=(n // tm,),
        in_specs=[
            pl.BlockSpec((tm, d_model), lambda i: (i, 0)),
            pl.BlockSpec((1, d_model), lambda i: (0, 0)),
            pl.BlockSpec((d_model, MAIN_WIDTH), lambda i: (0, 0)),
            pl.BlockSpec((d_model, LANES), lambda i: (0, 0)),
            pl.BlockSpec((LANES, GLA_QK), lambda i: (0, 0)),
            pl.BlockSpec((1, LANES), lambda i: (0, 0)),
            pl.BlockSpec((1, GLA_QK), lambda i: (0, 0)),
        ],
        out_specs=[
            pl.BlockSpec((tm, MAIN_WIDTH), lambda i: (i, 0)),
            pl.BlockSpec((tm, LANES), lambda i: (i, 0)),
            pl.BlockSpec((tm, GLA_QK), lambda i: (i, 0)),
        ],
        out_shape=[
            jax.ShapeDtypeStruct((n, MAIN_WIDTH), BF16),
            jax.ShapeDtypeStruct((n, LANES), F32),
            jax.ShapeDtypeStruct((n, GLA_QK), F32),
        ],
        compiler_params=_cparams(("arbitrary",)),
        name="in_proj",
    )(h, gain, wmain, wsmall, wup, bf, ba)


def _cumsum_kernel(x_ref, o_ref):
    t = CUMSUM_TILE
    row = lax.broadcasted_iota(jnp.int32, (t, t), 0)
    col = lax.broadcasted_iota(jnp.int32, (t, t), 1)
    tril = (row >= col).astype(BF16)
    carry = jnp.zeros((1, x_ref.shape[1]), F32)
    for j in range(x_ref.shape[0] // t):
        hi, mid, lo = _split3(x_ref[j * t:(j + 1) * t, :])
        c = _dot(tril, hi) + _dot(tril, mid) + _dot(tril, lo) + carry
        o_ref[j * t:(j + 1) * t, :] = c
        carry = c[t - 1:t, :]


def _seq_cumsum(x, batch, seq):
    return pl.pallas_call(
        _cumsum_kernel,
        grid=(batch,),
        in_specs=[pl.BlockSpec((seq, LANES), lambda b: (b, 0))],
        out_specs=pl.BlockSpec((seq, LANES), lambda b: (b, 0)),
        out_shape=jax.ShapeDtypeStruct(x.shape, F32),
        compiler_params=_cparams(("arbitrary",)),
        name="forget_cumsum",
    )(x)


def _fox_kernel(q_ref, k_ref, v_ref, c_ref, gain_ref, o_ref, q2_ref, s_ref, p_ref, alpha_ref, m_ref, l_ref, acc_ref):
    tq = FOX_TILE
    rows = 2 * tq
    slab = FOX_SLAB
    nq = q_ref.shape[0] // tq
    lane = lax.broadcasted_iota(jnp.int32, (1, LANES), 1)
    first = lane < FOX_DIM
    scale = FOX_DIM ** -0.5 * LOG2E
    for qi in range(nq):
        q = q_ref[qi * tq:(qi + 1) * tq, :].astype(F32) * scale
        q2_ref[qi, :tq, :] = jnp.where(first, q, 0.0).astype(BF16)
        q2_ref[qi, tq:, :] = jnp.where(first, 0.0, q).astype(BF16)

    def scores(qi, j):
        cj = c_ref[j] * LOG2E
        d = _dot_nt(q2_ref[qi], k_ref[j * tq:(j + 1) * tq, :])
        s_ref[:tq, :] = d[:tq] - cj[0:1, :]
        s_ref[tq:, :] = d[tq:] - cj[1:2, :]

    def weighted_values(qi, j):
        par = qi % 2
        acc_ref[par] = alpha_ref[par] * acc_ref[par] + _dot(p_ref[...], v_ref[j * tq:(j + 1) * tq, :])

    def softmax(qi, masked):
        par = qi % 2
        for r in range(rows // slab):
            sl = slice(r * slab, (r + 1) * slab)
            s = s_ref[sl, :]
            if masked:
                row = lax.broadcasted_iota(jnp.int32, (slab, tq), 0) + (r * slab) % tq
                col = lax.broadcasted_iota(jnp.int32, (slab, tq), 1)
                s = jnp.where(row >= col, s, -jnp.inf)
            m_old = m_ref[par, sl, :]
            m_new = jnp.maximum(m_old, jnp.max(s, axis=-1, keepdims=True))
            alpha = jnp.exp2(m_old - m_new)
            p = jnp.exp2(s - jnp.concatenate([m_new] * (tq // LANES), axis=1))
            l_ref[par, sl, :] = alpha * l_ref[par, sl, :] + jnp.sum(p, axis=-1, keepdims=True)
            m_ref[par, sl, :] = m_new
            alpha_ref[par, sl, :] = alpha
            p_ref[sl, :] = p.astype(BF16)

    def finalize(qi):
        par = qi % 2
        o2 = acc_ref[par] / l_ref[par]
        o = jnp.where(first, o2[:tq], o2[tq:])
        sq = o * o
        ss0 = jnp.sum(jnp.where(first, sq, 0.0), axis=-1, keepdims=True)
        ss1 = jnp.sum(jnp.where(first, 0.0, sq), axis=-1, keepdims=True)
        ms = jnp.where(first, ss0, ss1) * (1.0 / FOX_DIM)
        o_ref[qi * tq:(qi + 1) * tq, :] = (o * lax.rsqrt(ms + EPS) * gain_ref[...]).astype(BF16)

    steps = [(qi, j) for qi in range(nq) for j in range(qi + 1)]
    scores(*steps[0])
    for t, (qi, j) in enumerate(steps):
        if t > 0:
            weighted_values(*steps[t - 1])
            if steps[t - 1][0] != qi:
                finalize(steps[t - 1][0])
        if j == 0:
            par = qi % 2
            m_ref[par] = jnp.full(m_ref.shape[1:], -jnp.inf, F32)
            l_ref[par] = jnp.zeros(l_ref.shape[1:], F32)
            acc_ref[par] = jnp.zeros(acc_ref.shape[1:], F32)
        softmax(qi, masked=(j == qi))
        if t + 1 < len(steps):
            scores(*steps[t + 1])
    weighted_values(*steps[-1])
    finalize(steps[-1][0])


def _fox_attention(main, c5, gain, batch, seq):
    n = main.shape[0]
    tq = FOX_TILE
    nq = seq // tq
    pairs = FOX_HEADS // 2
    k_off = FOX_WIDTH // LANES
    v_off = 2 * FOX_WIDTH // LANES
    stat = pltpu.VMEM((2, 2 * tq, LANES), F32)
    return pl.pallas_call(
        _fox_kernel,
        grid=(batch, pairs),
        in_specs=[
            pl.BlockSpec((seq, LANES), lambda b, p: (b, p)),
            pl.BlockSpec((seq, LANES), lambda b, p: (b, k_off + p)),
            pl.BlockSpec((seq, LANES), lambda b, p: (b, v_off + p)),
            pl.BlockSpec((None, None, nq, 2, tq), lambda b, p: (b, p, 0, 0, 0)),
            pl.BlockSpec((1, LANES), lambda b, p: (0, p)),
        ],
        out_specs=pl.BlockSpec((seq, LANES), lambda b, p: (b, p)),
        out_shape=jax.ShapeDtypeStruct((n, FOX_WIDTH), BF16),
        scratch_shapes=[
            pltpu.VMEM((nq, 2 * tq, LANES), BF16),
            pltpu.VMEM((2 * tq, tq), F32),
            pltpu.VMEM((2 * tq, tq), BF16),
            stat, stat, stat, stat,
        ],
        compiler_params=_cparams(("arbitrary", "arbitrary")),
        name="fox_attention",
    )(main, main, main, c5, gain)


def _gla_kernel(q_ref, k_ref, v_ref, gg_ref, la_ref, gain_ref, o_ref, qe_ref, ke_ref, kl_ref, dec_ref, raw_ref):
    seq = q_ref.shape[0]
    cs = GLA_CHUNK
    nc = seq // cs
    width = 2 * GLA_DK

    b = la_ref[...]
    pos = lax.broadcasted_iota(jnp.int32, (seq, width), 0) % cs
    shift = 1
    while shift < cs:
        b = b + jnp.where(pos >= shift, pltpu.roll(b, shift, axis=0), 0.0)
        shift *= 2
    b3 = b.reshape(nc, cs, width)
    b_last = b3[:, cs - 1:cs, :]
    q = q_ref[...].astype(F32)
    k = k_ref[...].astype(F32)
    qe_ref[...] = (q * jnp.exp(b) * (GLA_DK ** -0.5)).astype(BF16)
    ke_ref[...] = (k * jnp.exp(-b)).astype(BF16)
    kl_ref[...] = (k.reshape(nc, cs, width) * jnp.exp(b_last - b3)).reshape(seq, width).astype(BF16)
    dec_ref[...] = jnp.exp(b_last).reshape(nc, width)

    lane = lax.broadcasted_iota(jnp.int32, (1, width), 1)
    first = lane < GLA_DK
    row = lax.broadcasted_iota(jnp.int32, (2 * cs, cs), 0)
    col = lax.broadcasted_iota(jnp.int32, (2 * cs, cs), 1)
    tril2 = jnp.where(row >= cs, row - cs, row) >= col
    srow = lax.broadcasted_iota(jnp.int32, (2 * GLA_DV, width), 0)
    scol = lax.broadcasted_iota(jnp.int32, (2 * GLA_DV, width), 1)
    same_head = (srow >= GLA_DV) == (scol >= GLA_DK)
    unroll = 8

    def chunks(ci, st):
        r0s = [pl.multiple_of((ci * unroll + u) * cs, cs) for u in range(unroll)]
        qes = [qe_ref[pl.ds(r0, cs), :] for r0 in r0s]
        vs = [v_ref[pl.ds(r0, cs), :] for r0 in r0s]
        atts, upds = [], []
        for u in range(unroll):
            zero = jnp.zeros_like(qes[u])
            q2 = jnp.concatenate([jnp.where(first, qes[u], zero), jnp.where(first, zero, qes[u])], axis=0)
            atts.append(jnp.where(tril2, _dot_nt(q2, ke_ref[pl.ds(r0s[u], cs), :]), 0.0).astype(BF16))
        for u in range(unroll):
            upds.append(lax.dot_general(vs[u], kl_ref[pl.ds(r0s[u], cs), :], (((0,), (0,)), ((), ())),
                                        preferred_element_type=F32))
        ois = [_dot(atts[u], vs[u]) for u in range(unroll)]
        for u in range(unroll):
            o = _dot_nt(qes[u], st.astype(BF16))
            o = o + jnp.concatenate([ois[u][:cs, :GLA_DV], ois[u][cs:, GLA_DV:]], axis=1)
            raw_ref[pl.ds(r0s[u], cs), :] = o
            st = st * dec_ref[pl.ds(ci * unroll + u, 1), :] + jnp.where(same_head, upds[u], 0.0)
        return st

    lax.fori_loop(0, nc // unroll, chunks, jnp.zeros((2 * GLA_DV, width), F32))

    o = raw_ref[...]
    normed = []
    for h in range(2):
        oh = o[:, h * GLA_DV:(h + 1) * GLA_DV]
        normed.append(oh * lax.rsqrt(jnp.mean(oh * oh, axis=-1, keepdims=True) + EPS))
    g = gg_ref[...].astype(F32)
    o_ref[...] = (jnp.concatenate(normed, axis=1) * gain_ref[...] * (g * jax.nn.sigmoid(g))).astype(BF16)


def _gla(main, loga, gain, batch, seq):
    n = main.shape[0]
    pairs = GLA_HEADS // 2
    q_off = 3 * FOX_WIDTH // LANES
    k_off = q_off + GLA_QK // LANES
    pv = 2 * GLA_DV
    v_off = (3 * FOX_WIDTH + 2 * GLA_QK) // pv
    g_off = v_off + GLA_V // pv
    return pl.pallas_call(
        _gla_kernel,
        grid=(batch, pairs),
        in_specs=[
            pl.BlockSpec((seq, LANES), lambda b, p: (b, q_off + p)),
            pl.BlockSpec((seq, LANES), lambda b, p: (b, k_off + p)),
            pl.BlockSpec((seq, pv), lambda b, p: (b, v_off + p)),
            pl.BlockSpec((seq, pv), lambda b, p: (b, g_off + p)),
            pl.BlockSpec((seq, LANES), lambda b, p: (b, p)),
            pl.BlockSpec((1, pv), lambda b, p: (0, p)),
        ],
        out_specs=pl.BlockSpec((seq, pv), lambda b, p: (b, p)),
        out_shape=jax.ShapeDtypeStruct((n, GLA_V), BF16),
        scratch_shapes=[
            pltpu.VMEM((seq, LANES), BF16),
            pltpu.VMEM((seq, LANES), BF16),
            pltpu.VMEM((seq, LANES), BF16),
            pltpu.VMEM((seq // GLA_CHUNK, LANES), F32),
            pltpu.VMEM((seq, pv), F32),
        ],
        compiler_params=_cparams(("arbitrary", "arbitrary")),
        name="gla",
    )(main, main, main, main, loga, gain)


def _post_kernel(fox_ref, gla_ref, h_ref, wout_ref, cg_ref, wxq_ref, k_ref, v_ref, wxo_ref, mg_ref,
                 wrh_ref, wrl_ref, br_ref, h2_ref, hn_ref, route_ref, cnt_ref, carry_ref):
    tm = h_ref.shape[0]

    @pl.when(pl.program_id(0) == 0)
    def _():
        carry_ref[...] = jnp.zeros_like(carry_ref)

    y = _dot(fox_ref[...], wout_ref[0:FOX_WIDTH, :]) + _dot(gla_ref[...], wout_ref[FOX_WIDTH:, :])
    h1 = h_ref[...] + y
    hn = _rms(h1, cg_ref[...]).astype(BF16)
    q = _dot(hn, wxq_ref[...]).astype(BF16)
    xscale = X_DIM ** -0.5
    heads = []
    for hh in range(X_HEADS):
        sl = slice(hh * X_DIM, (hh + 1) * X_DIM)
        s = _dot_nt(q[:, sl], k_ref[:, sl]) * xscale
        p = jnp.exp(s - jnp.max(s, axis=-1, keepdims=True))
        heads.append(_dot(p.astype(BF16), v_ref[:, sl]) / jnp.sum(p, axis=-1, keepdims=True))
    o = jnp.concatenate(heads, axis=1).astype(BF16)
    h2 = h1 + _dot(o, wxo_ref[...])
    h2_ref[...] = h2
    hn2 = _rms(h2, mg_ref[...])
    hn_ref[...] = _pack_rows(hn2)

    xh = hn2.astype(BF16)
    xl = (hn2 - xh.astype(F32)).astype(BF16)
    logits = _dot(xh, wrh_ref[...]) + _dot(xl, wrh_ref[...]) + _dot(xh, wrl_ref[...]) + br_ref[...]
    lane = lax.broadcasted_iota(jnp.int32, (tm, LANES), 1)
    neg = -jnp.inf
    gl = jnp.where(lane < N_GROUPS, logits, neg)
    gmax = jnp.max(gl, axis=-1, keepdims=True)
    ge = jnp.exp(gl - gmax)
    gprob = ge / jnp.sum(ge, axis=-1, keepdims=True)
    pmax = jnp.max(gprob, axis=-1, keepdims=True)
    grp = jnp.min(jnp.where(gprob == pmax, lane, LANES), axis=-1, keepdims=True)
    in_grp = (lane >= N_GROUPS) & (lane < N_GROUPS + N_EXPERTS) & (((lane - N_GROUPS) // GROUP_SIZE) == grp)
    el = jnp.where(in_grp, logits, neg)
    emax = jnp.max(el, axis=-1, keepdims=True)
    ee = jnp.exp(el - emax)
    eprob = ee / jnp.sum(ee, axis=-1, keepdims=True)
    p1 = jnp.max(eprob, axis=-1, keepdims=True)
    lane1 = jnp.min(jnp.where(in_grp & (eprob == p1), lane, LANES), axis=-1, keepdims=True)
    rest = jnp.where(in_grp & (lane != lane1), eprob, -1.0)
    p2 = jnp.max(rest, axis=-1, keepdims=True)
    lane2 = jnp.min(jnp.where(rest == p2, lane, LANES), axis=-1, keepdims=True)
    g1 = pmax * p1 / (p1 + p2)
    g2 = pmax * p2 / (p1 + p2)

    oh1 = lane == lane1
    oh2 = lane == lane2
    both = (oh1 | oh2).astype(BF16)
    row = lax.broadcasted_iota(jnp.int32, (tm, tm), 0)
    col = lax.broadcasted_iota(jnp.int32, (tm, tm), 1)
    before = (row > col).astype(BF16)
    seen = _dot(before, both) + carry_ref[...]
    rank1 = jnp.sum(jnp.where(oh1, seen, 0.0), axis=-1, keepdims=True)
    rank2 = jnp.sum(jnp.where(oh2, seen, 0.0), axis=-1, keepdims=True)
    carry = carry_ref[...] + jnp.sum(both.astype(F32), axis=0, keepdims=True)
    carry_ref[...] = carry
    cnt_ref[...] = carry

    e1 = (lane1 - N_GROUPS).astype(F32)
    e2 = (lane2 - N_GROUPS).astype(F32)
    route = jnp.where(lane == 0, e1, 0.0)
    route = jnp.where(lane == 1, e2, route)
    route = jnp.where(lane == 2, g1, route)
    route = jnp.where(lane == 3, g2, route)
    route = jnp.where(lane == 4, rank1, route)
    route = jnp.where(lane == 5, rank2, route)
    route_ref[...] = route[:, :ROUTE_WIDTH]


def _post(fox, gla, h, wout, cg, wxq, kmem, vmem, wxo, mg, wrh, wrl, br, seq, mem_len):
    n, d_model = h.shape
    tm = POST_TILE
    per_seq = seq // tm
    const = lambda i: (0, 0)
    return pl.pallas_call(
        _post_kernel,
        grid=(n // tm,),
        in_specs=[
            pl.BlockSpec((tm, FOX_WIDTH), lambda i: (i, 0)),
            pl.BlockSpec((tm, GLA_V), lambda i: (i, 0)),
            pl.BlockSpec((tm, d_model), lambda i: (i, 0)),
            pl.BlockSpec((FOX_WIDTH + GLA_V, d_model), const),
            pl.BlockSpec((1, d_model), const),
            pl.BlockSpec((d_model, X_WIDTH), const),
            pl.BlockSpec((mem_len, X_WIDTH), lambda i: (i // per_seq, 0)),
            pl.BlockSpec((mem_len, X_WIDTH), lambda i: (i // per_seq, 0)),
            pl.BlockSpec((X_WIDTH, d_model), const),
            pl.BlockSpec((1, d_model), const),
            pl.BlockSpec((d_model, LANES), const),
            pl.BlockSpec((d_model, LANES), const),
            pl.BlockSpec((1, LANES), const),
        ],
        out_specs=[
            pl.BlockSpec((tm, d_model), lambda i: (i, 0)),
            pl.BlockSpec((tm, d_model // 2), lambda i: (i, 0)),
            pl.BlockSpec((tm, ROUTE_WIDTH), lambda i: (i, 0)),
            pl.BlockSpec((1, LANES), const),
        ],
        out_shape=[
            jax.ShapeDtypeStruct((n, d_model), F32),
            jax.ShapeDtypeStruct((n, d_model // 2), jnp.uint32),
            jax.ShapeDtypeStruct((n, ROUTE_WIDTH), F32),
            jax.ShapeDtypeStruct((1, LANES), F32),
        ],
        scratch_shapes=[pltpu.VMEM((1, LANES), F32)],
        compiler_params=_cparams(("arbitrary",)),
        name="post_mixer",
    )(fox, gla, h, wout, cg, wxq, kmem, vmem, wxo, mg, wrh, wrl, br)


def _row_copy(src_ref, src_row, dst_ref, dst_row, sem):
    return pltpu.make_async_copy(src_ref.at[pl.ds(src_row, 1)], dst_ref.at[pl.ds(dst_row, 1)], sem)


def _dispatch_kernel(dest_ref, x_ref, xs_in_ref, xs_ref, sem):
    del xs_in_ref
    td = x_ref.shape[0]
    base = pl.program_id(0) * td

    def issue(t, carry):
        for kk in range(2):
            _row_copy(x_ref, t, xs_ref, dest_ref[(base + t) * 2 + kk], sem).start()
        return carry

    lax.fori_loop(0, td, issue, 0, unroll=ISSUE_UNROLL)
    for kk in range(2):
        pltpu.make_async_copy(x_ref, xs_ref.at[pl.ds(0, td)], sem).wait()


def _dispatch(dest, x, n_rows):
    n, width = x.shape
    td = MOVE_TILE
    return pl.pallas_call(
        _dispatch_kernel,
        grid_spec=pltpu.PrefetchScalarGridSpec(
            num_scalar_prefetch=1,
            grid=(n // td,),
            in_specs=[pl.BlockSpec((td, width), lambda i, dest: (i, 0)), pl.BlockSpec(memory_space=pl.ANY)],
            out_specs=pl.BlockSpec(memory_space=pl.ANY),
            scratch_shapes=[pltpu.SemaphoreType.DMA],
        ),
        out_shape=jax.ShapeDtypeStruct((n_rows, width), x.dtype),
        input_output_aliases={2: 0},
        compiler_params=_cparams(("arbitrary",)),
        name="moe_dispatch",
    )(dest, x, jnp.zeros((n_rows, width), x.dtype))


def _expert_kernel(be_ref, valid_ref, x_ref, wg_ref, wu_ref, wd_ref, y_ref):
    del be_ref
    valid = valid_ref[pl.program_id(0)]

    @pl.when(valid > 0)
    def _():
        lo, hi = _unpack_rows(x_ref[...])
        lo = lo.astype(BF16)
        hi = hi.astype(BF16)
        half = lo.shape[1]
        g = _dot(lo, wg_ref[:half, :]) + _dot(hi, wg_ref[half:, :])
        u = _dot(lo, wu_ref[:half, :]) + _dot(hi, wu_ref[half:, :])
        a = (g * jax.nn.sigmoid(g) * u).astype(BF16)
        y_ref[...] = _pack_rows(_dot(a, wd_ref[...]))

    @pl.when(valid <= 0)
    def _():
        y_ref[...] = jnp.zeros_like(y_ref)


def _experts(block_e, valid, xs, wg, wu, wd):
    r, width = xs.shape
    bm = MOE_BLOCK
    d_model, d_exp = wg.shape[-2:]
    return pl.pallas_call(
        _expert_kernel,
        grid_spec=pltpu.PrefetchScalarGridSpec(
            num_scalar_prefetch=2,
            grid=(r // bm,),
            in_specs=[
                pl.BlockSpec((bm, width), lambda i, be, nu: (i, 0)),
                pl.BlockSpec((None, d_model, d_exp), lambda i, be, nu: (be[i], 0, 0)),
                pl.BlockSpec((None, d_model, d_exp), lambda i, be, nu: (be[i], 0, 0)),
                pl.BlockSpec((None, d_exp, d_model), lambda i, be, nu: (be[i], 0, 0)),
            ],
            out_specs=pl.BlockSpec((bm, width), lambda i, be, nu: (i, 0)),
        ),
        out_shape=jax.ShapeDtypeStruct((r, width), jnp.uint32),
        compiler_params=_cparams(("arbitrary",)),
        name="moe_experts",
    )(block_e, valid, xs, wg, wu, wd)


def _combine_kernel(dest_ref, h_ref, route_ref, gain_ref, y_ref, o_ref, y0_ref, y1_ref, sem, *, final):
    tc = h_ref.shape[0]
    base = pl.program_id(0) * tc
    bufs = (y0_ref, y1_ref)

    def issue(t, carry):
        for kk in range(2):
            _row_copy(y_ref, dest_ref[(base + t) * 2 + kk], bufs[kk], t, sem).start()
        return carry

    lax.fori_loop(0, tc, issue, 0, unroll=ISSUE_UNROLL)
    for kk in range(2):
        pltpu.make_async_copy(y_ref.at[pl.ds(0, tc)], bufs[kk], sem).wait()

    route = route_ref[...]
    y0 = jnp.concatenate(_unpack_rows(y0_ref[...]), axis=1)
    y1 = jnp.concatenate(_unpack_rows(y1_ref[...]), axis=1)
    out = h_ref[...] + route[:, 2:3] * y0 + route[:, 3:4] * y1
    if final:
        out = _rms(out, gain_ref[...])
    o_ref[...] = out


def _combine(dest, h, route, gain, y, final):
    n, d_model = h.shape
    tc = MOVE_TILE
    return pl.pallas_call(
        functools.partial(_combine_kernel, final=final),
        grid_spec=pltpu.PrefetchScalarGridSpec(
            num_scalar_prefetch=1,
            grid=(n // tc,),
            in_specs=[
                pl.BlockSpec((tc, d_model), lambda i, dest: (i, 0)),
                pl.BlockSpec((tc, ROUTE_WIDTH), lambda i, dest: (i, 0)),
                pl.BlockSpec((1, d_model), lambda i, dest: (0, 0)),
                pl.BlockSpec(memory_space=pl.ANY),
            ],
            out_specs=pl.BlockSpec((tc, d_model), lambda i, dest: (i, 0)),
            scratch_shapes=[
                pltpu.VMEM((tc, d_model // 2), jnp.uint32),
                pltpu.VMEM((tc, d_model // 2), jnp.uint32),
                pltpu.SemaphoreType.DMA,
            ],
        ),
        out_shape=jax.ShapeDtypeStruct((n, d_model), F32),
        compiler_params=_cparams(("arbitrary",)),
        name="moe_combine",
    )(dest, h, route, gain, y)


def _routing_tables(route, cnt, n_rows):
    bm = MOE_BLOCK
    expert = route[:, 0:2].astype(jnp.int32)
    rank = route[:, 4:6].astype(jnp.int32)
    counts = cnt[0, N_GROUPS:N_GROUPS + N_EXPERTS].astype(jnp.int32)
    padded = (counts + bm - 1) // bm * bm
    pad_ends = jnp.cumsum(padded)
    pad_starts = pad_ends - padded
    ids = jnp.arange(N_EXPERTS, dtype=jnp.int32)
    start_of = jnp.sum(jnp.where(expert[..., None] == ids, pad_starts, 0), axis=-1)
    dest = (start_of + rank).reshape(-1).astype(jnp.int32)
    block_row = jnp.arange(n_rows // bm, dtype=jnp.int32) * bm
    block_e = jnp.minimum(jnp.sum((pad_ends[None, :] <= block_row[:, None]).astype(jnp.int32), axis=-1),
                          N_EXPERTS - 1)
    row_end = jnp.sum(jnp.where(block_e[:, None] == ids, pad_starts + counts, 0), axis=-1)
    valid = jnp.clip(row_end - block_row, 0, bm).astype(jnp.int32)
    return dest, block_e, valid


def kernel(x, mem, mem_norm, mix_norm, w_in, b_forget, w_alpha_up, b_alpha, fox_out_gain, gla_out_gain, w_out,
           cross_norm, w_xq, w_xk, w_xv, w_xo, moe_norm, w_router_group, b_router_group, w_router_expert,
           b_router_expert, w_expert_gate, w_expert_up, w_expert_down, final_norm):
    batch, seq, d_model = x.shape
    mem_len = mem.shape[1]
    depth = w_in.shape[0]
    n = batch * seq
    assert seq % FOX_TILE == 0 and seq % IN_TILE == 0 and seq % POST_TILE == 0 and seq % GLA_CHUNK == 0
    assert n % MOVE_TILE == 0 and d_model % LANES == 0

    c0 = 3 * FOX_WIDTH
    c1 = c0 + FOX_HEADS
    c2 = c1 + 2 * GLA_QK + 2 * GLA_V
    w_main = jnp.concatenate([w_in[:, :, :c0], w_in[:, :, c1:c2]], axis=-1).astype(BF16)
    pad = LANES - FOX_HEADS - GLA_RANK
    w_small = jnp.concatenate([w_in[:, :, c0:c1], w_in[:, :, c2:], jnp.zeros((depth, d_model, pad), F32)],
                              axis=-1).astype(BF16)
    w_up = jnp.concatenate([jnp.zeros((depth, FOX_HEADS, GLA_QK), F32), w_alpha_up,
                            jnp.zeros((depth, pad, GLA_QK), F32)], axis=1).astype(BF16)
    b_f = jnp.pad(b_forget, ((0, 0), (0, LANES - FOX_HEADS)))[:, None, :]
    b_a = b_alpha[:, None, :]
    w_r = jnp.concatenate([w_router_group, w_router_expert,
                           jnp.zeros((depth, d_model, LANES - N_GROUPS - N_EXPERTS), F32)], axis=-1)
    w_rh = w_r.astype(BF16)
    w_rl = (w_r - w_rh.astype(F32)).astype(BF16)
    b_r = jnp.pad(jnp.concatenate([b_router_group, b_router_expert], axis=-1),
                  ((0, 0), (0, LANES - N_GROUPS - N_EXPERTS)))[:, None, :]
    w_out_b = w_out.astype(BF16)
    w_xq_b = w_xq.astype(BF16)
    w_xo_b = w_xo.astype(BF16)
    w_eg = w_expert_gate.astype(BF16)
    w_eu = w_expert_up.astype(BF16)
    w_ed = w_expert_down.astype(BF16)

    kmem, vmem = _mem_kv(mem.reshape(batch * mem_len, d_model), mem_norm[None, :],
                         w_xk.astype(BF16), w_xv.astype(BF16), batch, mem_len)

    n_rows = 2 * n + N_EXPERTS * MOE_BLOCK
    nq = seq // FOX_TILE
    h = x.reshape(n, d_model)
    for l in range(depth):
        main, logf, loga = _in_proj(h, mix_norm[l][None, :], w_main[l], w_small[l], w_up[l], b_f[l], b_a[l])
        c = _seq_cumsum(logf, batch, seq)
        c5 = c[:, :FOX_HEADS].reshape(batch, nq, FOX_TILE, FOX_HEADS // 2, 2).transpose(0, 3, 1, 4, 2)
        fox = _fox_attention(main, c5, fox_out_gain[l][None, :], batch, seq)
        gla = _gla(main, loga, gla_out_gain[l][None, :], batch, seq)
        h2, hn2, route, cnt = _post(fox, gla, h, w_out_b[l], cross_norm[l][None, :], w_xq_b[l], kmem[l], vmem[l],
                                    w_xo_b[l], moe_norm[l][None, :], w_rh[l], w_rl[l], b_r[l], seq, mem_len)
        dest, block_e, valid = _routing_tables(route, cnt, n_rows)
        xs = _dispatch(dest, hn2, n_rows)
        y = _experts(block_e, valid, xs, w_eg[l], w_eu[l], w_ed[l])
        h = _combine(dest, h2, route, final_norm[None, :], y, final=(l == depth - 1))
    return h.reshape(batch, seq, d_model)
```

```python
import functools

import jax
import jax.numpy as jnp
from jax import lax
from jax.experimental import pallas as pl
from jax.experimental.pallas import tpu as pltpu
from jax.experimental.pallas import tpu_sc as plsc

F32 = jnp.float32
BF16 = jnp.bfloat16
EPS = 1e-6
LOG2E = 1.4426950408889634

FOX_HEADS = 8
FOX_DIM = 64
FOX_WIDTH = FOX_HEADS * FOX_DIM
GLA_HEADS = 4
GLA_DK = 64
GLA_DV = 128
GLA_QK = GLA_HEADS * GLA_DK
GLA_V = GLA_HEADS * GLA_DV
GLA_RANK = 16
GLA_TAU = 16.0
GLA_CHUNK = 64
X_HEADS = 4
X_DIM = 128
X_WIDTH = X_HEADS * X_DIM
N_GROUPS = 4
GROUP_SIZE = 4
N_EXPERTS = N_GROUPS * GROUP_SIZE
MAIN_WIDTH = 3 * FOX_WIDTH + 2 * GLA_QK + 2 * GLA_V

LANES = 128
ROUTE_WIDTH = 8
VMEM_LIMIT = 56 * 1024 * 1024

IN_TILE = 512
FOX_TILE = 512
FOX_SLAB = 64
POST_TILE = 512
CUMSUM_TILE = 256
MOE_BLOCK = 512
MOVE_TILE = 256
ISSUE_UNROLL = 8
SC_GATHER_WINDOW = 128
SC_ROW_SPLIT = 2


def _cparams(sem):
    return pltpu.CompilerParams(dimension_semantics=sem, vmem_limit_bytes=VMEM_LIMIT)


def _rms(x, gain):
    return x * lax.rsqrt(jnp.mean(x * x, axis=-1, keepdims=True) + EPS) * gain


def _log_sigmoid(x):
    return jnp.minimum(x, 0.0) - jnp.log1p(jnp.exp(-jnp.abs(x)))


def _dot(a, b):
    return jnp.dot(a, b, preferred_element_type=F32)


def _dot_nt(a, b):
    return lax.dot_general(a, b, (((1,), (1,)), ((), ())), preferred_element_type=F32)


def _pack_rows(x):
    half = x.shape[1] // 2
    lo = lax.bitcast_convert_type(x[:, :half].astype(BF16).astype(F32), jnp.uint32)
    hi = lax.bitcast_convert_type(x[:, half:].astype(BF16).astype(F32), jnp.uint32)
    return (lo >> 16) | hi


def _unpack_rows(w):
    lo = lax.bitcast_convert_type(w << 16, F32)
    hi = lax.bitcast_convert_type(w & jnp.uint32(0xFFFF0000), F32)
    return lo, hi


def _split3(x):
    hi = x.astype(BF16)
    r1 = x - hi.astype(F32)
    mid = r1.astype(BF16)
    lo = (r1 - mid.astype(F32)).astype(BF16)
    return hi, mid, lo


def _mem_kv_kernel(mem_ref, gain_ref, wk_ref, wv_ref, k_ref, v_ref):
    mn = _rms(mem_ref[...], gain_ref[...]).astype(BF16)
    for l in range(wk_ref.shape[0]):
        k_ref[l] = _dot(mn, wk_ref[l]).astype(BF16)
        v_ref[l] = _dot(mn, wv_ref[l]).astype(BF16)


def _mem_kv(mem2d, gain, wk, wv, batch, mem_len):
    depth, d_model, width = wk.shape
    out = jax.ShapeDtypeStruct((depth, batch * mem_len, width), BF16)
    return pl.pallas_call(
        _mem_kv_kernel,
        grid=(batch,),
        in_specs=[
            pl.BlockSpec((mem_len, d_model), lambda b: (b, 0)),
            pl.BlockSpec((1, d_model), lambda b: (0, 0)),
            pl.BlockSpec((depth, d_model, width), lambda b: (0, 0, 0)),
            pl.BlockSpec((depth, d_model, width), lambda b: (0, 0, 0)),
        ],
        out_specs=[
            pl.BlockSpec((depth, mem_len, width), lambda b: (0, b, 0)),
            pl.BlockSpec((depth, mem_len, width), lambda b: (0, b, 0)),
        ],
        out_shape=[out, out],
        compiler_params=_cparams(("arbitrary",)),
        name="mem_kv",
    )(mem2d, gain, wk, wv)


def _in_proj_kernel(h_ref, gain_ref, wmain_ref, wsmall_ref, wup_ref, bf_ref, ba_ref,
                    main_ref, logf_ref, loga_ref):
    xn = _rms(h_ref[...], gain_ref[...]).astype(BF16)
    step = 512
    for j in range(MAIN_WIDTH // step):
        main_ref[:, j * step:(j + 1) * step] = _dot(xn, wmain_ref[:, j * step:(j + 1) * step]).astype(BF16)
    small = _dot(xn, wsmall_ref[...])
    lane = lax.broadcasted_iota(jnp.int32, small.shape, 1)
    logf_ref[...] = jnp.where(lane < FOX_HEADS, _log_sigmoid(small + bf_ref[...]), 0.0)
    a = _dot(small.astype(BF16), wup_ref[...]) + ba_ref[...]
    loga_ref[...] = _log_sigmoid(a) * (1.0 / GLA_TAU)


def _in_proj(h, gain, wmain, wsmall, wup, bf, ba):
    n, d_model = h.shape
    tm = IN_TILE
    return pl.pallas_call(
        _in_proj_kernel,
        grid=(n // tm,),
        in_specs=[
            pl.BlockSpec((tm, d_model), lambda i: (i, 0)),
            pl.BlockSpec((1, d_model), lambda i: (0, 0)),
            pl.BlockSpec((d_model, MAIN_WIDTH), lambda i: (0, 0)),
            pl.BlockSpec((d_model, LANES), lambda i: (0, 0)),
            pl.BlockSpec((LANES, GLA_QK), lambda i: (0, 0)),
            pl.BlockSpec((1, LANES), lambda i: (0, 0)),
            pl.BlockSpec((1, GLA_QK), lambda i: (0, 0)),
        ],
        out_specs=[
            pl.BlockSpec((tm, MAIN_WIDTH), lambda i: (i, 0)),
            pl.BlockSpec((tm, LANES), lambda i: (i, 0)),
            pl.BlockSpec((tm, GLA_QK), lambda i: (i, 0)),
        ],
        out_shape=[
            jax.ShapeDtypeStruct((n, MAIN_WIDTH), BF16),
            jax.ShapeDtypeStruct((n, LANES), F32),
            jax.ShapeDtypeStruct((n, GLA_QK), F32),
        ],
        compiler_params=_cparams(("arbitrary",)),
        name="in_proj",
    )(h, gain, wmain, wsmall, wup, bf, ba)


def _cumsum_kernel(x_ref, o_ref):
    t = CUMSUM_TILE
    row = lax.broadcasted_iota(jnp.int32, (t, t), 0)
    col = lax.broadcasted_iota(jnp.int32, (t, t), 1)
    tril = (row >= col).astype(BF16)
    carry = jnp.zeros((1, x_ref.shape[1]), F32)
    for j in range(x_ref.shape[0] // t):
        hi, mid, lo = _split3(x_ref[j * t:(j + 1) * t, :])
        c = _dot(tril, hi) + _dot(tril, mid) + _dot(tril, lo) + carry
        o_ref[j * t:(j + 1) * t, :] = c
        carry = c[t - 1:t, :]


def _seq_cumsum(x, batch, seq):
    return pl.pallas_call(
        _cumsum_kernel,
        grid=(batch,),
        in_specs=[pl.BlockSpec((seq, LANES), lambda b: (b, 0))],
        out_specs=pl.BlockSpec((seq, LANES), lambda b: (b, 0)),
        out_shape=jax.ShapeDtypeStruct(x.shape, F32),
        compiler_params=_cparams(("arbitrary",)),
        name="forget_cumsum",
    )(x)


def _fox_kernel(q_ref, k_ref, v_ref, c_ref, gain_ref, o_ref, q2_ref, s_ref, p_ref, alpha_ref, m_ref, l_ref, acc_ref):
    tq = FOX_TILE
    rows = 2 * tq
    slab = FOX_SLAB
    nq = q_ref.shape[0] // tq
    lane = lax.broadcasted_iota(jnp.int32, (1, LANES), 1)
    first = lane < FOX_DIM
    scale = FOX_DIM ** -0.5 * LOG2E
    for qi in range(nq):
        q = q_ref[qi * tq:(qi + 1) * tq, :].astype(F32) * scale
        q2_ref[qi, :tq, :] = jnp.where(first, q, 0.0).astype(BF16)
        q2_ref[qi, tq:, :] = jnp.where(first, 0.0, q).astype(BF16)

    def scores(qi, j):
        cj = c_ref[j] * LOG2E
        d = _dot_nt(q2_ref[qi], k_ref[j * tq:(j + 1) * tq, :])
        s_ref[:tq, :] = d[:tq] - cj[0:1, :]
        s_ref[tq:, :] = d[tq:] - cj[1:2, :]

    def weighted_values(qi, j):
        par = qi % 2
        acc_ref[par] = alpha_ref[par] * acc_ref[par] + _dot(p_ref[...], v_ref[j * tq:(j + 1) * tq, :])

    def softmax(qi, masked):
        par = qi % 2
        for r in range(rows // slab):
            sl = slice(r * slab, (r + 1) * slab)
            s = s_ref[sl, :]
            if masked:
                row = lax.broadcasted_iota(jnp.int32, (slab, tq), 0) + (r * slab) % tq
                col = lax.broadcasted_iota(jnp.int32, (slab, tq), 1)
                s = jnp.where(row >= col, s, -jnp.inf)
            m_old = m_ref[par, sl, :]
            m_new = jnp.maximum(m_old, jnp.max(s, axis=-1, keepdims=True))
            alpha = jnp.exp2(m_old - m_new)
            p = jnp.exp2(s - jnp.concatenate([m_new] * (tq // LANES), axis=1))
            l_ref[par, sl, :] = alpha * l_ref[par, sl, :] + jnp.sum(p, axis=-1, keepdims=True)
            m_ref[par, sl, :] = m_new
            alpha_ref[par, sl, :] = alpha
            p_ref[sl, :] = p.astype(BF16)

    def finalize(qi):
        par = qi % 2
        o2 = acc_ref[par] / l_ref[par]
        o = jnp.where(first, o2[:tq], o2[tq:])
        sq = o * o
        ss0 = jnp.sum(jnp.where(first, sq, 0.0), axis=-1, keepdims=True)
        ss1 = jnp.sum(jnp.where(first, 0.0, sq), axis=-1, keepdims=True)
        ms = jnp.where(first, ss0, ss1) * (1.0 / FOX_DIM)
        o_ref[qi * tq:(qi + 1) * tq, :] = (o * lax.rsqrt(ms + EPS) * gain_ref[...]).astype(BF16)

    steps = [(qi, j) for qi in range(nq) for j in range(qi + 1)]
    scores(*steps[0])
    for t, (qi, j) in enumerate(steps):
        if t > 0:
            weighted_values(*steps[t - 1])
            if steps[t - 1][0] != qi:
                finalize(steps[t - 1][0])
        if j == 0:
            par = qi % 2
            m_ref[par] = jnp.full(m_ref.shape[1:], -jnp.inf, F32)
            l_ref[par] = jnp.zeros(l_ref.shape[1:], F32)
            acc_ref[par] = jnp.zeros(acc_ref.shape[1:], F32)
        softmax(qi, masked=(j == qi))
        if t + 1 < len(steps):
            scores(*steps[t + 1])
    weighted_values(*steps[-1])
    finalize(steps[-1][0])


def _fox_attention(main, c5, gain, batch, seq):
    n = main.shape[0]
    tq = FOX_TILE
    nq = seq // tq
    pairs = FOX_HEADS // 2
    k_off = FOX_WIDTH // LANES
    v_off = 2 * FOX_WIDTH // LANES
    stat = pltpu.VMEM((2, 2 * tq, LANES), F32)
    return pl.pallas_call(
        _fox_kernel,
        grid=(batch, pairs),
        in_specs=[
            pl.BlockSpec((seq, LANES), lambda b, p: (b, p)),
            pl.BlockSpec((seq, LANES), lambda b, p: (b, k_off + p)),
            pl.BlockSpec((seq, LANES), lambda b, p: (b, v_off + p)),
            pl.BlockSpec((None, None, nq, 2, tq), lambda b, p: (b, p, 0, 0, 0)),
            pl.BlockSpec((1, LANES), lambda b, p: (0, p)),
        ],
        out_specs=pl.BlockSpec((seq, LANES), lambda b, p: (b, p)),
        out_shape=jax.ShapeDtypeStruct((n, FOX_WIDTH), BF16),
        scratch_shapes=[
            pltpu.VMEM((nq, 2 * tq, LANES), BF16),
            pltpu.VMEM((2 * tq, tq), F32),
            pltpu.VMEM((2 * tq, tq), BF16),
            stat, stat, stat, stat,
        ],
        compiler_params=_cparams(("arbitrary", "arbitrary")),
        name="fox_attention",
    )(main, main, main, c5, gain)


def _gla_kernel(q_ref, k_ref, v_ref, gg_ref, la_ref, gain_ref, o_ref, qe_ref, ke_ref, kl_ref, dec_ref, raw_ref):
    seq = q_ref.shape[0]
    cs = GLA_CHUNK
    nc = seq // cs
    width = 2 * GLA_DK

    b = la_ref[...]
    pos = lax.broadcasted_iota(jnp.int32, (seq, width), 0) % cs
    shift = 1
    while shift < cs:
        b = b + jnp.where(pos >= shift, pltpu.roll(b, shift, axis=0), 0.0)
        shift *= 2
    b3 = b.reshape(nc, cs, width)
    b_last = b3[:, cs - 1:cs, :]
    q = q_ref[...].astype(F32)
    k = k_ref[...].astype(F32)
    qe_ref[...] = (q * jnp.exp(b) * (GLA_DK ** -0.5)).astype(BF16)
    ke_ref[...] = (k * jnp.exp(-b)).astype(BF16)
    kl_ref[...] = (k.reshape(nc, cs, width) * jnp.exp(b_last - b3)).reshape(seq, width).astype(BF16)
    dec_ref[...] = jnp.exp(b_last).reshape(nc, width)

    lane = lax.broadcasted_iota(jnp.int32, (1, width), 1)
    first = lane < GLA_DK
    row = lax.broadcasted_iota(jnp.int32, (2 * cs, cs), 0)
    col = lax.broadcasted_iota(jnp.int32, (2 * cs, cs), 1)
    tril2 = jnp.where(row >= cs, row - cs, row) >= col
    srow = lax.broadcasted_iota(jnp.int32, (2 * GLA_DV, width), 0)
    scol = lax.broadcasted_iota(jnp.int32, (2 * GLA_DV, width), 1)
    same_head = (srow >= GLA_DV) == (scol >= GLA_DK)
    unroll = 8

    def chunks(ci, st):
        r0s = [pl.multiple_of((ci * unroll + u) * cs, cs) for u in range(unroll)]
        qes = [qe_ref[pl.ds(r0, cs), :] for r0 in r0s]
        vs = [v_ref[pl.ds(r0, cs), :] for r0 in r0s]
        atts, upds = [], []
        for u in range(unroll):
            zero = jnp.zeros_like(qes[u])
            q2 = jnp.concatenate([jnp.where(first, qes[u], zero), jnp.where(first, zero, qes[u])], axis=0)
            atts.append(jnp.where(tril2, _dot_nt(q2, ke_ref[pl.ds(r0s[u], cs), :]), 0.0).astype(BF16))
        for u in range(unroll):
            upds.append(lax.dot_general(vs[u], kl_ref[pl.ds(r0s[u], cs), :], (((0,), (0,)), ((), ())),
                                        preferred_element_type=F32))
        ois = [_dot(atts[u], vs[u]) for u in range(unroll)]
        for u in range(unroll):
            o = _dot_nt(qes[u], st.astype(BF16))
            o = o + jnp.concatenate([ois[u][:cs, :GLA_DV], ois[u][cs:, GLA_DV:]], axis=1)
            raw_ref[pl.ds(r0s[u], cs), :] = o
            st = st * dec_ref[pl.ds(ci * unroll + u, 1), :] + jnp.where(same_head, upds[u], 0.0)
        return st

    lax.fori_loop(0, nc // unroll, chunks, jnp.zeros((2 * GLA_DV, width), F32))

    o = raw_ref[...]
    normed = []
    for h in range(2):
        oh = o[:, h * GLA_DV:(h + 1) * GLA_DV]
        normed.append(oh * lax.rsqrt(jnp.mean(oh * oh, axis=-1, keepdims=True) + EPS))
    g = gg_ref[...].astype(F32)
    o_ref[...] = (jnp.concatenate(normed, axis=1) * gain_ref[...] * (g * jax.nn.sigmoid(g))).astype(BF16)


def _gla(main, loga, gain, batch, seq):
    n = main.shape[0]
    pairs = GLA_HEADS // 2
    q_off = 3 * FOX_WIDTH // LANES
    k_off = q_off + GLA_QK // LANES
    pv = 2 * GLA_DV
    v_off = (3 * FOX_WIDTH + 2 * GLA_QK) // pv
    g_off = v_off + GLA_V // pv
    return pl.pallas_call(
        _gla_kernel,
        grid=(batch, pairs),
        in_specs=[
            pl.BlockSpec((seq, LANES), lambda b, p: (b, q_off + p)),
            pl.BlockSpec((seq, LANES), lambda b, p: (b, k_off + p)),
            pl.BlockSpec((seq, pv), lambda b, p: (b, v_off + p)),
            pl.BlockSpec((seq, pv), lambda b, p: (b, g_off + p)),
            pl.BlockSpec((seq, LANES), lambda b, p: (b, p)),
            pl.BlockSpec((1, pv), lambda b, p: (0, p)),
        ],
        out_specs=pl.BlockSpec((seq, pv), lambda b, p: (b, p)),
        out_shape=jax.ShapeDtypeStruct((n, GLA_V), BF16),
        scratch_shapes=[
            pltpu.VMEM((seq, LANES), BF16),
            pltpu.VMEM((seq, LANES), BF16),
            pltpu.VMEM((seq, LANES), BF16),
            pltpu.VMEM((seq // GLA_CHUNK, LANES), F32),
            pltpu.VMEM((seq, pv), F32),
        ],
        compiler_params=_cparams(("arbitrary", "arbitrary")),
        name="gla",
    )(main, main, main, main, loga, gain)


def _post_kernel(fox_ref, gla_ref, h_ref, wout_ref, cg_ref, wxq_ref, k_ref, v_ref, wxo_ref, mg_ref,
                 wrh_ref, wrl_ref, br_ref, h2_ref, hn_ref, route_ref, cnt_ref, carry_ref):
    tm = h_ref.shape[0]

    @pl.when(pl.program_id(0) == 0)
    def _():
        carry_ref[...] = jnp.zeros_like(carry_ref)

    y = _dot(fox_ref[...], wout_ref[0:FOX_WIDTH, :]) + _dot(gla_ref[...], wout_ref[FOX_WIDTH:, :])
    h1 = h_ref[...] + y
    hn = _rms(h1, cg_ref[...]).astype(BF16)
    q = _dot(hn, wxq_ref[...]).astype(BF16)
    xscale = X_DIM ** -0.5
    heads = []
    for hh in range(X_HEADS):
        sl = slice(hh * X_DIM, (hh + 1) * X_DIM)
        s = _dot_nt(q[:, sl], k_ref[:, sl]) * xscale
        p = jnp.exp(s - jnp.max(s, axis=-1, keepdims=True))
        heads.append(_dot(p.astype(BF16), v_ref[:, sl]) / jnp.sum(p, axis=-1, keepdims=True))
    o = jnp.concatenate(heads, axis=1).astype(BF16)
    h2 = h1 + _dot(o, wxo_ref[...])
    h2_ref[...] = h2
    hn2 = _rms(h2, mg_ref[...])
    hn_ref[...] = _pack_rows(hn2)

    xh = hn2.astype(BF16)
    xl = (hn2 - xh.astype(F32)).astype(BF16)
    logits = _dot(xh, wrh_ref[...]) + _dot(xl, wrh_ref[...]) + _dot(xh, wrl_ref[...]) + br_ref[...]
    lane = lax.broadcasted_iota(jnp.int32, (tm, LANES), 1)
    neg = -jnp.inf
    gl = jnp.where(lane < N_GROUPS, logits, neg)
    gmax = jnp.max(gl, axis=-1, keepdims=True)
    ge = jnp.exp(gl - gmax)
    gprob = ge / jnp.sum(ge, axis=-1, keepdims=True)
    pmax = jnp.max(gprob, axis=-1, keepdims=True)
    grp = jnp.min(jnp.where(gprob == pmax, lane, LANES), axis=-1, keepdims=True)
    in_grp = (lane >= N_GROUPS) & (lane < N_GROUPS + N_EXPERTS) & (((lane - N_GROUPS) // GROUP_SIZE) == grp)
    el = jnp.where(in_grp, logits, neg)
    emax = jnp.max(el, axis=-1, keepdims=True)
    ee = jnp.exp(el - emax)
    eprob = ee / jnp.sum(ee, axis=-1, keepdims=True)
    p1 = jnp.max(eprob, axis=-1, keepdims=True)
    lane1 = jnp.min(jnp.where(in_grp & (eprob == p1), lane, LANES), axis=-1, keepdims=True)
    rest = jnp.where(in_grp & (lane != lane1), eprob, -1.0)
    p2 = jnp.max(rest, axis=-1, keepdims=True)
    lane2 = jnp.min(jnp.where(rest == p2, lane, LANES), axis=-1, keepdims=True)
    g1 = pmax * p1 / (p1 + p2)
    g2 = pmax * p2 / (p1 + p2)

    oh1 = lane == lane1
    oh2 = lane == lane2
    both = (oh1 | oh2).astype(BF16)
    row = lax.broadcasted_iota(jnp.int32, (tm, tm), 0)
    col = lax.broadcasted_iota(jnp.int32, (tm, tm), 1)
    before = (row > col).astype(BF16)
    seen = _dot(before, both) + carry_ref[...]
    rank1 = jnp.sum(jnp.where(oh1, seen, 0.0), axis=-1, keepdims=True)
    rank2 = jnp.sum(jnp.where(oh2, seen, 0.0), axis=-1, keepdims=True)
    carry = carry_ref[...] + jnp.sum(both.astype(F32), axis=0, keepdims=True)
    carry_ref[...] = carry
    cnt_ref[...] = carry

    e1 = (lane1 - N_GROUPS).astype(F32)
    e2 = (lane2 - N_GROUPS).astype(F32)
    route = jnp.where(lane == 0, e1, 0.0)
    route = jnp.where(lane == 1, e2, route)
    route = jnp.where(lane == 2, g1, route)
    route = jnp.where(lane == 3, g2, route)
    route = jnp.where(lane == 4, rank1, route)
    route = jnp.where(lane == 5, rank2, route)
    route_ref[...] = route[:, :ROUTE_WIDTH]


def _post(fox, gla, h, wout, cg, wxq, kmem, vmem, wxo, mg, wrh, wrl, br, seq, mem_len):
    n, d_model = h.shape
    tm = POST_TILE
    per_seq = seq // tm
    const = lambda i: (0, 0)
    return pl.pallas_call(
        _post_kernel,
        grid=(n // tm,),
        in_specs=[
            pl.BlockSpec((tm, FOX_WIDTH), lambda i: (i, 0)),
            pl.BlockSpec((tm, GLA_V), lambda i: (i, 0)),
            pl.BlockSpec((tm, d_model), lambda i: (i, 0)),
            pl.BlockSpec((FOX_WIDTH + GLA_V, d_model), const),
            pl.BlockSpec((1, d_model), const),
            pl.BlockSpec((d_model, X_WIDTH), const),
            pl.BlockSpec((mem_len, X_WIDTH), lambda i: (i // per_seq, 0)),
            pl.BlockSpec((mem_len, X_WIDTH), lambda i: (i // per_seq, 0)),
            pl.BlockSpec((X_WIDTH, d_model), const),
            pl.BlockSpec((1, d_model), const),
            pl.BlockSpec((d_model, LANES), const),
            pl.BlockSpec((d_model, LANES), const),
            pl.BlockSpec((1, LANES), const),
        ],
        out_specs=[
            pl.BlockSpec((tm, d_model), lambda i: (i, 0)),
            pl.BlockSpec((tm, d_model // 2), lambda i: (i, 0)),
            pl.BlockSpec((tm, ROUTE_WIDTH), lambda i: (i, 0)),
            pl.BlockSpec((1, LANES), const),
        ],
        out_shape=[
            jax.ShapeDtypeStruct((n, d_model), F32),
            jax.ShapeDtypeStruct((n, d_model // 2), jnp.uint32),
            jax.ShapeDtypeStruct((n, ROUTE_WIDTH), F32),
            jax.ShapeDtypeStruct((1, LANES), F32),
        ],
        scratch_shapes=[pltpu.VMEM((1, LANES), F32)],
        compiler_params=_cparams(("arbitrary",)),
        name="post_mixer",
    )(fox, gla, h, wout, cg, wxq, kmem, vmem, wxo, mg, wrh, wrl, br)


def _row_copy(src_ref, src_row, dst_ref, dst_row, sem):
    return pltpu.make_async_copy(src_ref.at[pl.ds(src_row, 1)], dst_ref.at[pl.ds(dst_row, 1)], sem)


def _dispatch_kernel(dest_ref, x_ref, xs_in_ref, xs_ref, sem):
    del xs_in_ref
    td = x_ref.shape[0]
    base = pl.program_id(0) * td

    def issue(t, carry):
        for kk in range(2):
            _row_copy(x_ref, t, xs_ref, dest_ref[(base + t) * 2 + kk], sem).start()
        return carry

    lax.fori_loop(0, td, issue, 0, unroll=ISSUE_UNROLL)
    for kk in range(2):
        pltpu.make_async_copy(x_ref, xs_ref.at[pl.ds(0, td)], sem).wait()


def _dispatch(dest, x, n_rows):
    n, width = x.shape
    td = MOVE_TILE
    return pl.pallas_call(
        _dispatch_kernel,
        grid_spec=pltpu.PrefetchScalarGridSpec(
            num_scalar_prefetch=1,
            grid=(n // td,),
            in_specs=[pl.BlockSpec((td, width), lambda i, dest: (i, 0)), pl.BlockSpec(memory_space=pl.ANY)],
            out_specs=pl.BlockSpec(memory_space=pl.ANY),
            scratch_shapes=[pltpu.SemaphoreType.DMA],
        ),
        out_shape=jax.ShapeDtypeStruct((n_rows, width), x.dtype),
        input_output_aliases={2: 0},
        compiler_params=_cparams(("arbitrary",)),
        name="moe_dispatch",
    )(dest, x, jnp.zeros((n_rows, width), x.dtype))


def _expert_kernel(be_ref, valid_ref, x_ref, wg_ref, wu_ref, wd_ref, y_ref):
    del be_ref
    valid = valid_ref[pl.program_id(0)]

    @pl.when(valid > 0)
    def _():
        lo, hi = _unpack_rows(x_ref[...])
        lo = lo.astype(BF16)
        hi = hi.astype(BF16)
        half = lo.shape[1]
        g = _dot(lo, wg_ref[:half, :]) + _dot(hi, wg_ref[half:, :])
        u = _dot(lo, wu_ref[:half, :]) + _dot(hi, wu_ref[half:, :])
        a = (g * jax.nn.sigmoid(g) * u).astype(BF16)
        y_ref[...] = _pack_rows(_dot(a, wd_ref[...]))

    @pl.when(valid <= 0)
    def _():
        y_ref[...] = jnp.zeros_like(y_ref)


def _experts(block_e, valid, xs, wg, wu, wd):
    r, width = xs.shape
    bm = MOE_BLOCK
    d_model, d_exp = wg.shape[-2:]
    return pl.pallas_call(
        _expert_kernel,
        grid_spec=pltpu.PrefetchScalarGridSpec(
            num_scalar_prefetch=2,
            grid=(r // bm,),
            in_specs=[
                pl.BlockSpec((bm, width), lambda i, be, nu: (i, 0)),
                pl.BlockSpec((None, d_model, d_exp), lambda i, be, nu: (be[i], 0, 0)),
                pl.BlockSpec((None, d_model, d_exp), lambda i, be, nu: (be[i], 0, 0)),
                pl.BlockSpec((None, d_exp, d_model), lambda i, be, nu: (be[i], 0, 0)),
            ],
            out_specs=pl.BlockSpec((bm, width), lambda i, be, nu: (i, 0)),
        ),
        out_shape=jax.ShapeDtypeStruct((r, width), jnp.uint32),
        compiler_params=_cparams(("arbitrary",)),
        name="moe_experts",
    )(block_e, valid, xs, wg, wu, wd)


def _sc_gather_rows(table, idx):
    rows, full_width = table.shape
    split = SC_ROW_SPLIT
    table = table.reshape(rows * split, full_width // split)
    idx = (idx[:, None] * split + jnp.arange(split, dtype=idx.dtype)[None, :]).reshape(-1)
    m = idx.shape[0]
    width = table.shape[1]
    window = SC_GATHER_WINDOW
    mesh = plsc.VectorSubcoreMesh(core_axis_name="core", subcore_axis_name="subcore")

    @functools.partial(pl.kernel, out_type=jax.ShapeDtypeStruct((m, width), table.dtype), mesh=mesh)
    def gather(table_hbm, idx_hbm, out_hbm):
        def body(idx_vmem, out_vmem):
            pltpu.sync_copy(table_hbm.at[idx_vmem.at[0]], out_vmem)

        pltpu.emit_pipeline(
            body,
            grid=(m // window,),
            in_specs=[pl.BlockSpec((1, window), lambda i: (0, i))],
            out_specs=[pl.BlockSpec((window, width), lambda i: (i, 0))],
            core_axis_name=("core", "subcore"),
            dimension_semantics=(pltpu.PARALLEL,),
        )(idx_hbm, out_hbm)

    return gather(table, idx.reshape(1, m)).reshape(m // split, full_width)


def _combine_kernel(h_ref, route_ref, gain_ref, y_ref, o_ref, *, final):
    half = y_ref.shape[1] // 2
    route = route_ref[...]
    y0 = jnp.concatenate(_unpack_rows(y_ref[:, :half]), axis=1)
    y1 = jnp.concatenate(_unpack_rows(y_ref[:, half:]), axis=1)
    out = h_ref[...] + route[:, 2:3] * y0 + route[:, 3:4] * y1
    if final:
        out = _rms(out, gain_ref[...])
    o_ref[...] = out


def _combine(dest, h, route, gain, y, final):
    n, d_model = h.shape
    tc = MOVE_TILE
    pairs = _sc_gather_rows(y, dest).reshape(n, d_model)
    return pl.pallas_call(
        functools.partial(_combine_kernel, final=final),
        grid=(n // tc,),
        in_specs=[
            pl.BlockSpec((tc, d_model), lambda i: (i, 0)),
            pl.BlockSpec((tc, ROUTE_WIDTH), lambda i: (i, 0)),
            pl.BlockSpec((1, d_model), lambda i: (0, 0)),
            pl.BlockSpec((tc, d_model), lambda i: (i, 0)),
        ],
        out_specs=pl.BlockSpec((tc, d_model), lambda i: (i, 0)),
        out_shape=jax.ShapeDtypeStruct((n, d_model), F32),
        compiler_params=_cparams(("arbitrary",)),
        name="moe_combine",
    )(h, route, gain, pairs)


def _routing_tables(route, cnt, n_rows):
    bm = MOE_BLOCK
    expert = route[:, 0:2].astype(jnp.int32)
    rank = route[:, 4:6].astype(jnp.int32)
    counts = cnt[0, N_GROUPS:N_GROUPS + N_EXPERTS].astype(jnp.int32)
    padded = (counts + bm - 1) // bm * bm
    pad_ends = jnp.cumsum(padded)
    pad_starts = pad_ends - padded
    ids = jnp.arange(N_EXPERTS, dtype=jnp.int32)
    start_of = jnp.sum(jnp.where(expert[..., None] == ids, pad_starts, 0), axis=-1)
    dest = (start_of + rank).reshape(-1).astype(jnp.int32)
    block_row = jnp.arange(n_rows // bm, dtype=jnp.int32) * bm
    block_e = jnp.minimum(jnp.sum((pad_ends[None, :] <= block_row[:, None]).astype(jnp.int32), axis=-1),
                          N_EXPERTS - 1)
    row_end = jnp.sum(jnp.where(block_e[:, None] == ids, pad_starts + counts, 0), axis=-1)
    valid = jnp.clip(row_end - block_row, 0, bm).astype(jnp.int32)
    return dest, block_e, valid


def kernel(x, mem, mem_norm, mix_norm, w_in, b_forget, w_alpha_up, b_alpha, fox_out_gain, gla_out_gain, w_out,
           cross_norm, w_xq, w_xk, w_xv, w_xo, moe_norm, w_router_group, b_router_group, w_router_expert,
           b_router_expert, w_expert_gate, w_expert_up, w_expert_down, final_norm):
    batch, seq, d_model = x.shape
    mem_len = mem.shape[1]
    depth = w_in.shape[0]
    n = batch * seq
    assert seq % FOX_TILE == 0 and seq % IN_TILE == 0 and seq % POST_TILE == 0 and seq % GLA_CHUNK == 0
    assert n % MOVE_TILE == 0 and d_model % LANES == 0

    c0 = 3 * FOX_WIDTH
    c1 = c0 + FOX_HEADS
    c2 = c1 + 2 * GLA_QK + 2 * GLA_V
    w_main = jnp.concatenate([w_in[:, :, :c0], w_in[:, :, c1:c2]], axis=-1).astype(BF16)
    pad = LANES - FOX_HEADS - GLA_RANK
    w_small = jnp.concatenate([w_in[:, :, c0:c1], w_in[:, :, c2:], jnp.zeros((depth, d_model, pad), F32)],
                              axis=-1).astype(BF16)
    w_up = jnp.concatenate([jnp.zeros((depth, FOX_HEADS, GLA_QK), F32), w_alpha_up,
                            jnp.zeros((depth, pad, GLA_QK), F32)], axis=1).astype(BF16)
    b_f = jnp.pad(b_forget, ((0, 0), (0, LANES - FOX_HEADS)))[:, None, :]
    b_a = b_alpha[:, None, :]
    w_r = jnp.concatenate([w_router_group, w_router_expert,
                           jnp.zeros((depth, d_model, LANES - N_GROUPS - N_EXPERTS), F32)], axis=-1)
    w_rh = w_r.astype(BF16)
    w_rl = (w_r - w_rh.astype(F32)).astype(BF16)
    b_r = jnp.pad(jnp.concatenate([b_router_group, b_router_expert], axis=-1),
                  ((0, 0), (0, LANES - N_GROUPS - N_EXPERTS)))[:, None, :]
    w_out_b = w_out.astype(BF16)
    w_xq_b = w_xq.astype(BF16)
    w_xo_b = w_xo.astype(BF16)
    w_eg = w_expert_gate.astype(BF16)
    w_eu = w_expert_up.astype(BF16)
    w_ed = w_expert_down.astype(BF16)

    kmem, vmem = _mem_kv(mem.reshape(batch * mem_len, d_model), mem_norm[None, :],
                         w_xk.astype(BF16), w_xv.astype(BF16), batch, mem_len)

    n_rows = 2 * n + N_EXPERTS * MOE_BLOCK
    nq = seq // FOX_TILE
    h = x.reshape(n, d_model)
    for l in range(depth):
        main, logf, loga = _in_proj(h, mix_norm[l][None, :], w_main[l], w_small[l], w_up[l], b_f[l], b_a[l])
        c = _seq_cumsum(logf, batch, seq)
        c5 = c[:, :FOX_HEADS].reshape(batch, nq, FOX_TILE, FOX_HEADS // 2, 2).transpose(0, 3, 1, 4, 2)
        fox = _fox_attention(main, c5, fox_out_gain[l][None, :], batch, seq)
        gla = _gla(main, loga, gla_out_gain[l][None, :], batch, seq)
        h2, hn2, route, cnt = _post(fox, gla, h, w_out_b[l], cross_norm[l][None, :], w_xq_b[l], kmem[l], vmem[l],
                                    w_xo_b[l], moe_norm[l][None, :], w_rh[l], w_rl[l], b_r[l], seq, mem_len)
        dest, block_e, valid = _routing_tables(route, cnt, n_rows)
        xs = _dispatch(dest, hn2, n_rows)
        y = _experts(block_e, valid, xs, w_eg[l], w_eu[l], w_ed[l])
        h = _combine(dest, h2, route, final_norm[None, :], y, final=(l == depth - 1))
    return h.reshape(batch, seq, d_model)
```

```python
import functools

import jax
import jax.numpy as jnp
from jax import lax
from jax.experimental import pallas as pl
from jax.experimental.pallas import tpu as pltpu
from jax.experimental.pallas import tpu_sc as plsc

F32 = jnp.float32
BF16 = jnp.bfloat16
EPS = 1e-6
LOG2E = 1.4426950408889634

FOX_HEADS = 8
FOX_DIM = 64
FOX_WIDTH = FOX_HEADS * FOX_DIM
GLA_HEADS = 4
GLA_DK = 64
GLA_DV = 128
GLA_QK = GLA_HEADS * GLA_DK
GLA_V = GLA_HEADS * GLA_DV
GLA_RANK = 16
GLA_TAU = 16.0
GLA_CHUNK = 64
X_HEADS = 4
X_DIM = 128
X_WIDTH = X_HEADS * X_DIM
N_GROUPS = 4
GROUP_SIZE = 4
N_EXPERTS = N_GROUPS * GROUP_SIZE
MAIN_WIDTH = 3 * FOX_WIDTH + 2 * GLA_QK + 2 * GLA_V

LANES = 128
ROUTE_WIDTH = 8
VMEM_LIMIT = 56 * 1024 * 1024

IN_TILE = 512
FOX_TILE = 512
FOX_SLAB = 64
POST_TILE = 512
CUMSUM_TILE = 256
MOE_BLOCK = 512
MOVE_TILE = 256
SC_GATHER_WINDOW = 128


def _cparams(sem):
    return pltpu.CompilerParams(dimension_semantics=sem, vmem_limit_bytes=VMEM_LIMIT)


def _rms(x, gain):
    return x * lax.rsqrt(jnp.mean(x * x, axis=-1, keepdims=True) + EPS) * gain


def _log_sigmoid(x):
    return jnp.minimum(x, 0.0) - jnp.log1p(jnp.exp(-jnp.abs(x)))


def _dot(a, b):
    return jnp.dot(a, b, preferred_element_type=F32)


def _dot_nt(a, b):
    return lax.dot_general(a, b, (((1,), (1,)), ((), ())), preferred_element_type=F32)


def _pack_rows(x):
    half = x.shape[1] // 2
    lo = lax.bitcast_convert_type(x[:, :half].astype(BF16).astype(F32), jnp.uint32)
    hi = lax.bitcast_convert_type(x[:, half:].astype(BF16).astype(F32), jnp.uint32)
    return (lo >> 16) | hi


def _unpack_rows(w):
    lo = lax.bitcast_convert_type(w << 16, F32)
    hi = lax.bitcast_convert_type(w & jnp.uint32(0xFFFF0000), F32)
    return lo, hi


def _split3(x):
    hi = x.astype(BF16)
    r1 = x - hi.astype(F32)
    mid = r1.astype(BF16)
    lo = (r1 - mid.astype(F32)).astype(BF16)
    return hi, mid, lo


def _mem_kv_kernel(mem_ref, gain_ref, wk_ref, wv_ref, k_ref, v_ref):
    mn = _rms(mem_ref[...], gain_ref[...]).astype(BF16)
    for l in range(wk_ref.shape[0]):
        k_ref[l] = _dot(mn, wk_ref[l]).astype(BF16)
        v_ref[l] = _dot(mn, wv_ref[l]).astype(BF16)


def _mem_kv(mem2d, gain, wk, wv, batch, mem_len):
    depth, d_model, width = wk.shape
    out = jax.ShapeDtypeStruct((depth, batch * mem_len, width), BF16)
    return pl.pallas_call(
        _mem_kv_kernel,
        grid=(batch,),
        in_specs=[
            pl.BlockSpec((mem_len, d_model), lambda b: (b, 0)),
            pl.BlockSpec((1, d_model), lambda b: (0, 0)),
            pl.BlockSpec((depth, d_model, width), lambda b: (0, 0, 0)),
            pl.BlockSpec((depth, d_model, width), lambda b: (0, 0, 0)),
        ],
        out_specs=[
            pl.BlockSpec((depth, mem_len, width), lambda b: (0, b, 0)),
            pl.BlockSpec((depth, mem_len, width), lambda b: (0, b, 0)),
        ],
        out_shape=[out, out],
        compiler_params=_cparams(("arbitrary",)),
        name="mem_kv",
    )(mem2d, gain, wk, wv)


def _in_proj_kernel(h_ref, gain_ref, wmain_ref, wsmall_ref, wup_ref, bf_ref, ba_ref,
                    main_ref, logf_ref, loga_ref):
    xn = _rms(h_ref[...], gain_ref[...]).astype(BF16)
    step = 512
    for j in range(MAIN_WIDTH // step):
        main_ref[:, j * step:(j + 1) * step] = _dot(xn, wmain_ref[:, j * step:(j + 1) * step]).astype(BF16)
    small = _dot(xn, wsmall_ref[...])
    lane = lax.broadcasted_iota(jnp.int32, small.shape, 1)
    logf_ref[...] = jnp.where(lane < FOX_HEADS, _log_sigmoid(small + bf_ref[...]), 0.0)
    a = _dot(small.astype(BF16), wup_ref[...]) + ba_ref[...]
    loga_ref[...] = _log_sigmoid(a) * (1.0 / GLA_TAU)


def _in_proj(h, gain, wmain, wsmall, wup, bf, ba):
    n, d_model = h.shape
    tm = IN_TILE
    return pl.pallas_call(
        _in_proj_kernel,
        grid=(n // tm,),
        in_specs=[
            pl.BlockSpec((tm, d_model), lambda i: (i, 0)),
            pl.BlockSpec((1, d_model), lambda i: (0, 0)),
            pl.BlockSpec((d_model, MAIN_WIDTH), lambda i: (0, 0)),
            pl.BlockSpec((d_model, LANES), lambda i: (0, 0)),
            pl.BlockSpec((LANES, GLA_QK), lambda i: (0, 0)),
            pl.BlockSpec((1, LANES), lambda i: (0, 0)),
            pl.BlockSpec((1, GLA_QK), lambda i: (0, 0)),
        ],
        out_specs=[
            pl.BlockSpec((tm, MAIN_WIDTH), lambda i: (i, 0)),
            pl.BlockSpec((tm, LANES), lambda i: (i, 0)),
            pl.BlockSpec((tm, GLA_QK), lambda i: (i, 0)),
        ],
        out_shape=[
            jax.ShapeDtypeStruct((n, MAIN_WIDTH), BF16),
            jax.ShapeDtypeStruct((n, LANES), F32),
            jax.ShapeDtypeStruct((n, GLA_QK), F32),
        ],
        compiler_params=_cparams(("arbitrary",)),
        name="in_proj",
    )(h, gain, wmain, wsmall, wup, bf, ba)


def _cumsum_kernel(x_ref, o_ref):
    t = CUMSUM_TILE
    row = lax.broadcasted_iota(jnp.int32, (t, t), 0)
    col = lax.broadcasted_iota(jnp.int32, (t, t), 1)
    tril = (row >= col).astype(BF16)
    carry = jnp.zeros((1, x_ref.shape[1]), F32)
    for j in range(x_ref.shape[0] // t):
        hi, mid, lo = _split3(x_ref[j * t:(j + 1) * t, :])
        c = _dot(tril, hi) + _dot(tril, mid) + _dot(tril, lo) + carry
        o_ref[j * t:(j + 1) * t, :] = c
        carry = c[t - 1:t, :]


def _seq_cumsum(x, batch, seq):
    return pl.pallas_call(
        _cumsum_kernel,
        grid=(batch,),
        in_specs=[pl.BlockSpec((seq, LANES), lambda b: (b, 0))],
        out_specs=pl.BlockSpec((seq, LANES), lambda b: (b, 0)),
        out_shape=jax.ShapeDtypeStruct(x.shape, F32),
        compiler_params=_cparams(("arbitrary",)),
        name="forget_cumsum",
    )(x)


def _fox_kernel(q_ref, k_ref, v_ref, c_ref, gain_ref, o_ref, q2_ref, s_ref, p_ref, alpha_ref, m_ref, l_ref, acc_ref):
    tq = FOX_TILE
    rows = 2 * tq
    slab = FOX_SLAB
    nq = q_ref.shape[0] // tq
    lane = lax.broadcasted_iota(jnp.int32, (1, LANES), 1)
    first = lane < FOX_DIM
    scale = FOX_DIM ** -0.5 * LOG2E
    for qi in range(nq):
        q = q_ref[qi * tq:(qi + 1) * tq, :].astype(F32) * scale
        q2_ref[qi, :tq, :] = jnp.where(first, q, 0.0).astype(BF16)
        q2_ref[qi, tq:, :] = jnp.where(first, 0.0, q).astype(BF16)

    def scores(qi, j):
        cj = c_ref[j] * LOG2E
        d = _dot_nt(q2_ref[qi], k_ref[j * tq:(j + 1) * tq, :])
        s_ref[:tq, :] = d[:tq] - cj[0:1, :]
        s_ref[tq:, :] = d[tq:] - cj[1:2, :]

    def weighted_values(qi, j):
        par = qi % 2
        acc_ref[par] = alpha_ref[par] * acc_ref[par] + _dot(p_ref[...], v_ref[j * tq:(j + 1) * tq, :])

    def softmax(qi, masked):
        par = qi % 2
        for r in range(rows // slab):
            sl = slice(r * slab, (r + 1) * slab)
            s = s_ref[sl, :]
            if masked:
                row = lax.broadcasted_iota(jnp.int32, (slab, tq), 0) + (r * slab) % tq
                col = lax.broadcasted_iota(jnp.int32, (slab, tq), 1)
                s = jnp.where(row >= col, s, -jnp.inf)
            m_old = m_ref[par, sl, :]
            m_new = jnp.maximum(m_old, jnp.max(s, axis=-1, keepdims=True))
            alpha = jnp.exp2(m_old - m_new)
            p = jnp.exp2(s - jnp.concatenate([m_new] * (tq // LANES), axis=1))
            l_ref[par, sl, :] = alpha * l_ref[par, sl, :] + jnp.sum(p, axis=-1, keepdims=True)
            m_ref[par, sl, :] = m_new
            alpha_ref[par, sl, :] = alpha
            p_ref[sl, :] = p.astype(BF16)

    def finalize(qi):
        par = qi % 2
        o2 = acc_ref[par] / l_ref[par]
        o = jnp.where(first, o2[:tq], o2[tq:])
        sq = o * o
        ss0 = jnp.sum(jnp.where(first, sq, 0.0), axis=-1, keepdims=True)
        ss1 = jnp.sum(jnp.where(first, 0.0, sq), axis=-1, keepdims=True)
        ms = jnp.where(first, ss0, ss1) * (1.0 / FOX_DIM)
        o_ref[qi * tq:(qi + 1) * tq, :] = (o * lax.rsqrt(ms + EPS) * gain_ref[...]).astype(BF16)

    steps = [(qi, j) for qi in range(nq) for j in range(qi + 1)]
    scores(*steps[0])
    for t, (qi, j) in enumerate(steps):
        if t > 0:
            weighted_values(*steps[t - 1])
            if steps[t - 1][0] != qi:
                finalize(steps[t - 1][0])
        if j == 0:
            par = qi % 2
            m_ref[par] = jnp.full(m_ref.shape[1:], -jnp.inf, F32)
            l_ref[par] = jnp.zeros(l_ref.shape[1:], F32)
            acc_ref[par] = jnp.zeros(acc_ref.shape[1:], F32)
        softmax(qi, masked=(j == qi))
        if t + 1 < len(steps):
            scores(*steps[t + 1])
    weighted_values(*steps[-1])
    finalize(steps[-1][0])


def _fox_attention(main, c5, gain, batch, seq):
    n = main.shape[0]
    tq = FOX_TILE
    nq = seq // tq
    pairs = FOX_HEADS // 2
    k_off = FOX_WIDTH // LANES
    v_off = 2 * FOX_WIDTH // LANES
    stat = pltpu.VMEM((2, 2 * tq, LANES), F32)
    return pl.pallas_call(
        _fox_kernel,
        grid=(batch, pairs),
        in_specs=[
            pl.BlockSpec((seq, LANES), lambda b, p: (b, p)),
            pl.BlockSpec((seq, LANES), lambda b, p: (b, k_off + p)),
            pl.BlockSpec((seq, LANES), lambda b, p: (b, v_off + p)),
            pl.BlockSpec((None, None, nq, 2, tq), lambda b, p: (b, p, 0, 0, 0)),
            pl.BlockSpec((1, LANES), lambda b, p: (0, p)),
        ],
        out_specs=pl.BlockSpec((seq, LANES), lambda b, p: (b, p)),
        out_shape=jax.ShapeDtypeStruct((n, FOX_WIDTH), BF16),
        scratch_shapes=[
            pltpu.VMEM((nq, 2 * tq, LANES), BF16),
            pltpu.VMEM((2 * tq, tq), F32),
            pltpu.VMEM((2 * tq, tq), BF16),
            stat, stat, stat, stat,
        ],
        compiler_params=_cparams(("arbitrary", "arbitrary")),
        name="fox_attention",
    )(main, main, main, c5, gain)


def _gla_kernel(q_ref, k_ref, v_ref, gg_ref, la_ref, gain_ref, o_ref, qe_ref, ke_ref, kl_ref, dec_ref, raw_ref):
    seq = q_ref.shape[0]
    cs = GLA_CHUNK
    nc = seq // cs
    width = 2 * GLA_DK

    b = la_ref[...]
    pos = lax.broadcasted_iota(jnp.int32, (seq, width), 0) % cs
    shift = 1
    while shift < cs:
        b = b + jnp.where(pos >= shift, pltpu.roll(b, shift, axis=0), 0.0)
        shift *= 2
    b3 = b.reshape(nc, cs, width)
    b_last = b3[:, cs - 1:cs, :]
    q = q_ref[...].astype(F32)
    k = k_ref[...].astype(F32)
    qe_ref[...] = (q * jnp.exp(b) * (GLA_DK ** -0.5)).astype(BF16)
    ke_ref[...] = (k * jnp.exp(-b)).astype(BF16)
    kl_ref[...] = (k.reshape(nc, cs, width) * jnp.exp(b_last - b3)).reshape(seq, width).astype(BF16)
    dec_ref[...] = jnp.exp(b_last).reshape(nc, width)

    lane = lax.broadcasted_iota(jnp.int32, (1, width), 1)
    first = lane < GLA_DK
    row = lax.broadcasted_iota(jnp.int32, (2 * cs, cs), 0)
    col = lax.broadcasted_iota(jnp.int32, (2 * cs, cs), 1)
    tril2 = jnp.where(row >= cs, row - cs, row) >= col
    srow = lax.broadcasted_iota(jnp.int32, (2 * GLA_DV, width), 0)
    scol = lax.broadcasted_iota(jnp.int32, (2 * GLA_DV, width), 1)
    same_head = (srow >= GLA_DV) == (scol >= GLA_DK)
    unroll = 8

    def chunks(ci, st):
        r0s = [pl.multiple_of((ci * unroll + u) * cs, cs) for u in range(unroll)]
        qes = [qe_ref[pl.ds(r0, cs), :] for r0 in r0s]
        vs = [v_ref[pl.ds(r0, cs), :] for r0 in r0s]
        atts, upds = [], []
        for u in range(unroll):
            zero = jnp.zeros_like(qes[u])
            q2 = jnp.concatenate([jnp.where(first, qes[u], zero), jnp.where(first, zero, qes[u])], axis=0)
            atts.append(jnp.where(tril2, _dot_nt(q2, ke_ref[pl.ds(r0s[u], cs), :]), 0.0).astype(BF16))
        for u in range(unroll):
            upds.append(lax.dot_general(vs[u], kl_ref[pl.ds(r0s[u], cs), :], (((0,), (0,)), ((), ())),
                                        preferred_element_type=F32))
        ois = [_dot(atts[u], vs[u]) for u in range(unroll)]
        for u in range(unroll):
            o = _dot_nt(qes[u], st.astype(BF16))
            o = o + jnp.concatenate([ois[u][:cs, :GLA_DV], ois[u][cs:, GLA_DV:]], axis=1)
            raw_ref[pl.ds(r0s[u], cs), :] = o
            st = st * dec_ref[pl.ds(ci * unroll + u, 1), :] + jnp.where(same_head, upds[u], 0.0)
        return st

    lax.fori_loop(0, nc // unroll, chunks, jnp.zeros((2 * GLA_DV, width), F32))

    o = raw_ref[...]
    normed = []
    for h in range(2):
        oh = o[:, h * GLA_DV:(h + 1) * GLA_DV]
        normed.append(oh * lax.rsqrt(jnp.mean(oh * oh, axis=-1, keepdims=True) + EPS))
    g = gg_ref[...].astype(F32)
    o_ref[...] = (jnp.concatenate(normed, axis=1) * gain_ref[...] * (g * jax.nn.sigmoid(g))).astype(BF16)


def _gla(main, loga, gain, batch, seq):
    n = main.shape[0]
    pairs = GLA_HEADS // 2
    q_off = 3 * FOX_WIDTH // LANES
    k_off = q_off + GLA_QK // LANES
    pv = 2 * GLA_DV
    v_off = (3 * FOX_WIDTH + 2 * GLA_QK) // pv
    g_off = v_off + GLA_V // pv
    return pl.pallas_call(
        _gla_kernel,
        grid=(batch, pairs),
        in_specs=[
            pl.BlockSpec((seq, LANES), lambda b, p: (b, q_off + p)),
            pl.BlockSpec((seq, LANES), lambda b, p: (b, k_off + p)),
            pl.BlockSpec((seq, pv), lambda b, p: (b, v_off + p)),
            pl.BlockSpec((seq, pv), lambda b, p: (b, g_off + p)),
            pl.BlockSpec((seq, LANES), lambda b, p: (b, p)),
            pl.BlockSpec((1, pv), lambda b, p: (0, p)),
        ],
        out_specs=pl.BlockSpec((seq, pv), lambda b, p: (b, p)),
        out_shape=jax.ShapeDtypeStruct((n, GLA_V), BF16),
        scratch_shapes=[
            pltpu.VMEM((seq, LANES), BF16),
            pltpu.VMEM((seq, LANES), BF16),
            pltpu.VMEM((seq, LANES), BF16),
            pltpu.VMEM((seq // GLA_CHUNK, LANES), F32),
            pltpu.VMEM((seq, pv), F32),
        ],
        compiler_params=_cparams(("arbitrary", "arbitrary")),
        name="gla",
    )(main, main, main, main, loga, gain)


def _post_kernel(fox_ref, gla_ref, h_ref, wout_ref, cg_ref, wxq_ref, k_ref, v_ref, wxo_ref, mg_ref,
                 wrh_ref, wrl_ref, br_ref, h2_ref, hn_ref, route_ref, cnt_ref, carry_ref):
    tm = h_ref.shape[0]

    @pl.when(pl.program_id(0) == 0)
    def _():
        carry_ref[...] = jnp.zeros_like(carry_ref)

    y = _dot(fox_ref[...], wout_ref[0:FOX_WIDTH, :]) + _dot(gla_ref[...], wout_ref[FOX_WIDTH:, :])
    h1 = h_ref[...] + y
    hn = _rms(h1, cg_ref[...]).astype(BF16)
    q = _dot(hn, wxq_ref[...]).astype(BF16)
    xscale = X_DIM ** -0.5
    heads = []
    for hh in range(X_HEADS):
        sl = slice(hh * X_DIM, (hh + 1) * X_DIM)
        s = _dot_nt(q[:, sl], k_ref[:, sl]) * xscale
        p = jnp.exp(s - jnp.max(s, axis=-1, keepdims=True))
        heads.append(_dot(p.astype(BF16), v_ref[:, sl]) / jnp.sum(p, axis=-1, keepdims=True))
    o = jnp.concatenate(heads, axis=1).astype(BF16)
    h2 = h1 + _dot(o, wxo_ref[...])
    h2_ref[...] = h2
    hn2 = _rms(h2, mg_ref[...])
    hn_ref[...] = _pack_rows(hn2)

    xh = hn2.astype(BF16)
    xl = (hn2 - xh.astype(F32)).astype(BF16)
    logits = _dot(xh, wrh_ref[...]) + _dot(xl, wrh_ref[...]) + _dot(xh, wrl_ref[...]) + br_ref[...]
    lane = lax.broadcasted_iota(jnp.int32, (tm, LANES), 1)
    neg = -jnp.inf
    gl = jnp.where(lane < N_GROUPS, logits, neg)
    gmax = jnp.max(gl, axis=-1, keepdims=True)
    ge = jnp.exp(gl - gmax)
    gprob = ge / jnp.sum(ge, axis=-1, keepdims=True)
    pmax = jnp.max(gprob, axis=-1, keepdims=True)
    grp = jnp.min(jnp.where(gprob == pmax, lane, LANES), axis=-1, keepdims=True)
    in_grp = (lane >= N_GROUPS) & (lane < N_GROUPS + N_EXPERTS) & (((lane - N_GROUPS) // GROUP_SIZE) == grp)
    el = jnp.where(in_grp, logits, neg)
    emax = jnp.max(el, axis=-1, keepdims=True)
    ee = jnp.exp(el - emax)
    eprob = ee / jnp.sum(ee, axis=-1, keepdims=True)
    p1 = jnp.max(eprob, axis=-1, keepdims=True)
    lane1 = jnp.min(jnp.where(in_grp & (eprob == p1), lane, LANES), axis=-1, keepdims=True)
    rest = jnp.where(in_grp & (lane != lane1), eprob, -1.0)
    p2 = jnp.max(rest, axis=-1, keepdims=True)
    lane2 = jnp.min(jnp.where(rest == p2, lane, LANES), axis=-1, keepdims=True)
    g1 = pmax * p1 / (p1 + p2)
    g2 = pmax * p2 / (p1 + p2)

    oh1 = lane == lane1
    oh2 = lane == lane2
    both = (oh1 | oh2).astype(BF16)
    row = lax.broadcasted_iota(jnp.int32, (tm, tm), 0)
    col = lax.broadcasted_iota(jnp.int32, (tm, tm), 1)
    before = (row > col).astype(BF16)
    seen = _dot(before, both) + carry_ref[...]
    rank1 = jnp.sum(jnp.where(oh1, seen, 0.0), axis=-1, keepdims=True)
    rank2 = jnp.sum(jnp.where(oh2, seen, 0.0), axis=-1, keepdims=True)
    carry = carry_ref[...] + jnp.sum(both.astype(F32), axis=0, keepdims=True)
    carry_ref[...] = carry
    cnt_ref[...] = carry

    e1 = (lane1 - N_GROUPS).astype(F32)
    e2 = (lane2 - N_GROUPS).astype(F32)
    route = jnp.where(lane == 0, e1, 0.0)
    route = jnp.where(lane == 1, e2, route)
    route = jnp.where(lane == 2, g1, route)
    route = jnp.where(lane == 3, g2, route)
    route = jnp.where(lane == 4, rank1, route)
    route = jnp.where(lane == 5, rank2, route)
    route_ref[...] = route[:, :ROUTE_WIDTH]


def _post(fox, gla, h, wout, cg, wxq, kmem, vmem, wxo, mg, wrh, wrl, br, seq, mem_len):
    n, d_model = h.shape
    tm = POST_TILE
    per_seq = seq // tm
    const = lambda i: (0, 0)
    return pl.pallas_call(
        _post_kernel,
        grid=(n // tm,),
        in_specs=[
            pl.BlockSpec((tm, FOX_WIDTH), lambda i: (i, 0)),
            pl.BlockSpec((tm, GLA_V), lambda i: (i, 0)),
            pl.BlockSpec((tm, d_model), lambda i: (i, 0)),
            pl.BlockSpec((FOX_WIDTH + GLA_V, d_model), const),
            pl.BlockSpec((1, d_model), const),
            pl.BlockSpec((d_model, X_WIDTH), const),
            pl.BlockSpec((mem_len, X_WIDTH), lambda i: (i // per_seq, 0)),
            pl.BlockSpec((mem_len, X_WIDTH), lambda i: (i // per_seq, 0)),
            pl.BlockSpec((X_WIDTH, d_model), const),
            pl.BlockSpec((1, d_model), const),
            pl.BlockSpec((d_model, LANES), const),
            pl.BlockSpec((d_model, LANES), const),
            pl.BlockSpec((1, LANES), const),
        ],
        out_specs=[
            pl.BlockSpec((tm, d_model), lambda i: (i, 0)),
            pl.BlockSpec((tm, d_model // 2), lambda i: (i, 0)),
            pl.BlockSpec((tm, ROUTE_WIDTH), lambda i: (i, 0)),
            pl.BlockSpec((1, LANES), const),
        ],
        out_shape=[
            jax.ShapeDtypeStruct((n, d_model), F32),
            jax.ShapeDtypeStruct((n, d_model // 2), jnp.uint32),
            jax.ShapeDtypeStruct((n, ROUTE_WIDTH), F32),
            jax.ShapeDtypeStruct((1, LANES), F32),
        ],
        scratch_shapes=[pltpu.VMEM((1, LANES), F32)],
        compiler_params=_cparams(("arbitrary",)),
        name="post_mixer",
    )(fox, gla, h, wout, cg, wxq, kmem, vmem, wxo, mg, wrh, wrl, br)


def _dispatch(dest_kmajor, x, n_rows):
    n, width = x.shape
    window = SC_GATHER_WINDOW
    mesh = plsc.VectorSubcoreMesh(core_axis_name="core", subcore_axis_name="subcore")
    workers = mesh.num_cores * mesh.num_subcores
    per_worker = n // workers
    assert n % (workers * window) == 0

    @functools.partial(
        pl.kernel, out_type=jax.ShapeDtypeStruct((n_rows, width), x.dtype), mesh=mesh,
        scratch_types=[pltpu.VMEM((window,), jnp.int32), pltpu.VMEM((window,), jnp.int32),
                       pltpu.VMEM((window, width), x.dtype), pltpu.SemaphoreType.DMA])
    def scatter(x_hbm, idx_hbm, out_hbm, idx0_vmem, idx1_vmem, rows_vmem, sem):
        worker = lax.axis_index("subcore") * mesh.num_cores + lax.axis_index("core")
        base = worker * per_worker

        @pl.loop(0, per_worker // window)
        def _(step):
            off = pl.multiple_of(base + step * window, window)
            pltpu.sync_copy(idx_hbm.at[pl.ds(off, window)], idx0_vmem)
            pltpu.sync_copy(idx_hbm.at[pl.ds(n + off, window)], idx1_vmem)
            pltpu.sync_copy(x_hbm.at[pl.ds(off, window)], rows_vmem)
            pltpu.async_copy(rows_vmem, out_hbm.at[idx0_vmem], sem).wait()
            pltpu.async_copy(rows_vmem, out_hbm.at[idx1_vmem], sem).wait()

    return scatter(x, dest_kmajor)


def _expert_kernel(be_ref, valid_ref, x_ref, wg_ref, wu_ref, wd_ref, y_ref):
    del be_ref
    valid = valid_ref[pl.program_id(0)]

    @pl.when(valid > 0)
    def _():
        row = lax.broadcasted_iota(jnp.int32, x_ref.shape, 0)
        lo, hi = _unpack_rows(jnp.where(row < valid, x_ref[...], jnp.uint32(0)))
        lo = lo.astype(BF16)
        hi = hi.astype(BF16)
        half = lo.shape[1]
        g = _dot(lo, wg_ref[:half, :]) + _dot(hi, wg_ref[half:, :])
        u = _dot(lo, wu_ref[:half, :]) + _dot(hi, wu_ref[half:, :])
        a = (g * jax.nn.sigmoid(g) * u).astype(BF16)
        y_ref[...] = _pack_rows(_dot(a, wd_ref[...]))

    @pl.when(valid <= 0)
    def _():
        y_ref[...] = jnp.zeros_like(y_ref)


def _experts(block_e, valid, xs, wg, wu, wd):
    r, width = xs.shape
    bm = MOE_BLOCK
    d_model, d_exp = wg.shape[-2:]
    return pl.pallas_call(
        _expert_kernel,
        grid_spec=pltpu.PrefetchScalarGridSpec(
            num_scalar_prefetch=2,
            grid=(r // bm,),
            in_specs=[
                pl.BlockSpec((bm, width), lambda i, be, nu: (i, 0)),
                pl.BlockSpec((None, d_model, d_exp), lambda i, be, nu: (be[i], 0, 0)),
                pl.BlockSpec((None, d_model, d_exp), lambda i, be, nu: (be[i], 0, 0)),
                pl.BlockSpec((None, d_exp, d_model), lambda i, be, nu: (be[i], 0, 0)),
            ],
            out_specs=pl.BlockSpec((bm, width), lambda i, be, nu: (i, 0)),
        ),
        out_shape=jax.ShapeDtypeStruct((r, width), jnp.uint32),
        compiler_params=_cparams(("arbitrary",)),
        name="moe_experts",
    )(block_e, valid, xs, wg, wu, wd)


def _sc_gather_rows(table, idx):
    m = idx.shape[0]
    width = table.shape[1]
    window = SC_GATHER_WINDOW
    mesh = plsc.VectorSubcoreMesh(core_axis_name="core", subcore_axis_name="subcore")
    workers = mesh.num_cores * mesh.num_subcores
    per_worker = m // workers
    assert m % (workers * window) == 0

    @functools.partial(
        pl.kernel, out_type=jax.ShapeDtypeStruct((m, width), table.dtype), mesh=mesh,
        scratch_types=[pltpu.VMEM((window,), jnp.int32), pltpu.VMEM((window, width), table.dtype),
                       pltpu.SemaphoreType.DMA])
    def gather(table_hbm, idx_hbm, out_hbm, idx_vmem, rows_vmem, sem):
        worker = lax.axis_index("subcore") * mesh.num_cores + lax.axis_index("core")
        base = worker * per_worker

        @pl.loop(0, per_worker // window)
        def _(step):
            off = pl.multiple_of(base + step * window, window)
            pltpu.sync_copy(idx_hbm.at[pl.ds(off, window)], idx_vmem)
            pltpu.async_copy(table_hbm.at[idx_vmem], rows_vmem, sem).wait()
            pltpu.sync_copy(rows_vmem, out_hbm.at[pl.ds(off, window)])

    return gather(table, idx)


def _combine_kernel(h_ref, route_ref, gain_ref, y0_ref, y1_ref, o_ref, *, final):
    route = route_ref[...]
    y0 = jnp.concatenate(_unpack_rows(y0_ref[...]), axis=1)
    y1 = jnp.concatenate(_unpack_rows(y1_ref[...]), axis=1)
    out = h_ref[...] + route[:, 2:3] * y0 + route[:, 3:4] * y1
    if final:
        out = _rms(out, gain_ref[...])
    o_ref[...] = out


def _combine(dest, h, route, gain, y, final):
    n, d_model = h.shape
    tc = MOVE_TILE
    picked = _sc_gather_rows(y, dest)
    steps = n // tc
    return pl.pallas_call(
        functools.partial(_combine_kernel, final=final),
        grid=(steps,),
        in_specs=[
            pl.BlockSpec((tc, d_model), lambda i: (i, 0)),
            pl.BlockSpec((tc, ROUTE_WIDTH), lambda i: (i, 0)),
            pl.BlockSpec((1, d_model), lambda i: (0, 0)),
            pl.BlockSpec((tc, d_model // 2), lambda i: (i, 0)),
            pl.BlockSpec((tc, d_model // 2), lambda i: (i + steps, 0)),
        ],
        out_specs=pl.BlockSpec((tc, d_model), lambda i: (i, 0)),
        out_shape=jax.ShapeDtypeStruct((n, d_model), F32),
        compiler_params=_cparams(("arbitrary",)),
        name="moe_combine",
    )(h, route, gain, picked, picked)


def _routing_tables(route, cnt, n_rows):
    bm = MOE_BLOCK
    expert = route[:, 0:2].astype(jnp.int32)
    rank = route[:, 4:6].astype(jnp.int32)
    counts = cnt[0, N_GROUPS:N_GROUPS + N_EXPERTS].astype(jnp.int32)
    padded = (counts + bm - 1) // bm * bm
    pad_ends = jnp.cumsum(padded)
    pad_starts = pad_ends - padded
    ids = jnp.arange(N_EXPERTS, dtype=jnp.int32)
    start_of = jnp.sum(jnp.where(expert[..., None] == ids, pad_starts, 0), axis=-1)
    dest = (start_of + rank).T.reshape(-1).astype(jnp.int32)
    block_row = jnp.arange(n_rows // bm, dtype=jnp.int32) * bm
    block_e = jnp.minimum(jnp.sum((pad_ends[None, :] <= block_row[:, None]).astype(jnp.int32), axis=-1),
                          N_EXPERTS - 1)
    row_end = jnp.sum(jnp.where(block_e[:, None] == ids, pad_starts + counts, 0), axis=-1)
    valid = jnp.clip(row_end - block_row, 0, bm).astype(jnp.int32)
    return dest, block_e, valid


def kernel(x, mem, mem_norm, mix_norm, w_in, b_forget, w_alpha_up, b_alpha, fox_out_gain, gla_out_gain, w_out,
           cross_norm, w_xq, w_xk, w_xv, w_xo, moe_norm, w_router_group, b_router_group, w_router_expert,
           b_router_expert, w_expert_gate, w_expert_up, w_expert_down, final_norm):
    batch, seq, d_model = x.shape
    mem_len = mem.shape[1]
    depth = w_in.shape[0]
    n = batch * seq
    assert seq % FOX_TILE == 0 and seq % IN_TILE == 0 and seq % POST_TILE == 0 and seq % GLA_CHUNK == 0
    assert n % MOVE_TILE == 0 and d_model % LANES == 0

    c0 = 3 * FOX_WIDTH
    c1 = c0 + FOX_HEADS
    c2 = c1 + 2 * GLA_QK + 2 * GLA_V
    w_main = jnp.concatenate([w_in[:, :, :c0], w_in[:, :, c1:c2]], axis=-1).astype(BF16)
    pad = LANES - FOX_HEADS - GLA_RANK
    w_small = jnp.concatenate([w_in[:, :, c0:c1], w_in[:, :, c2:], jnp.zeros((depth, d_model, pad), F32)],
                              axis=-1).astype(BF16)
    w_up = jnp.concatenate([jnp.zeros((depth, FOX_HEADS, GLA_QK), F32), w_alpha_up,
                            jnp.zeros((depth, pad, GLA_QK), F32)], axis=1).astype(BF16)
    b_f = jnp.pad(b_forget, ((0, 0), (0, LANES - FOX_HEADS)))[:, None, :]
    b_a = b_alpha[:, None, :]
    w_r = jnp.concatenate([w_router_group, w_router_expert,
                           jnp.zeros((depth, d_model, LANES - N_GROUPS - N_EXPERTS), F32)], axis=-1)
    w_rh = w_r.astype(BF16)
    w_rl = (w_r - w_rh.astype(F32)).astype(BF16)
    b_r = jnp.pad(jnp.concatenate([b_router_group, b_router_expert], axis=-1),
                  ((0, 0), (0, LANES - N_GROUPS - N_EXPERTS)))[:, None, :]
    w_out_b = w_out.astype(BF16)
    w_xq_b = w_xq.astype(BF16)
    w_xo_b = w_xo.astype(BF16)
    w_eg = w_expert_gate.astype(BF16)
    w_eu = w_expert_up.astype(BF16)
    w_ed = w_expert_down.astype(BF16)

    kmem, vmem = _mem_kv(mem.reshape(batch * mem_len, d_model), mem_norm[None, :],
                         w_xk.astype(BF16), w_xv.astype(BF16), batch, mem_len)

    n_rows = 2 * n + N_EXPERTS * MOE_BLOCK
    nq = seq // FOX_TILE
    h = x.reshape(n, d_model)
    for l in range(depth):
        main, logf, loga = _in_proj(h, mix_norm[l][None, :], w_main[l], w_small[l], w_up[l], b_f[l], b_a[l])
        c = _seq_cumsum(logf, batch, seq)
        c5 = c[:, :FOX_HEADS].reshape(batch, nq, FOX_TILE, FOX_HEADS // 2, 2).transpose(0, 3, 1, 4, 2)
        fox = _fox_attention(main, c5, fox_out_gain[l][None, :], batch, seq)
        gla = _gla(main, loga, gla_out_gain[l][None, :], batch, seq)
        h2, hn2, route, cnt = _post(fox, gla, h, w_out_b[l], cross_norm[l][None, :], w_xq_b[l], kmem[l], vmem[l],
                                    w_xo_b[l], moe_norm[l][None, :], w_rh[l], w_rl[l], b_r[l], seq, mem_len)
        dest, block_e, valid = _routing_tables(route, cnt, n_rows)
        xs = _dispatch(dest, hn2, n_rows)
        y = _experts(block_e, valid, xs, w_eg[l], w_eu[l], w_ed[l])
        h = _combine(dest, h2, route, final_norm[None, :], y, final=(l == depth - 1))
    return h.reshape(batch, seq, d_model)
```

```python
import functools

import jax
import jax.numpy as jnp
from jax import lax
from jax.experimental import pallas as pl
from jax.experimental.pallas import tpu as pltpu
from jax.experimental.pallas import tpu_sc as plsc

F32 = jnp.float32
BF16 = jnp.bfloat16
EPS = 1e-6
LOG2E = 1.4426950408889634

FOX_HEADS = 8
FOX_DIM = 64
FOX_WIDTH = FOX_HEADS * FOX_DIM
GLA_HEADS = 4
GLA_DK = 64
GLA_DV = 128
GLA_QK = GLA_HEADS * GLA_DK
GLA_V = GLA_HEADS * GLA_DV
GLA_RANK = 16
GLA_TAU = 16.0
GLA_CHUNK = 64
X_HEADS = 4
X_DIM = 128
X_WIDTH = X_HEADS * X_DIM
N_GROUPS = 4
GROUP_SIZE = 4
N_EXPERTS = N_GROUPS * GROUP_SIZE
MAIN_WIDTH = 3 * FOX_WIDTH + 2 * GLA_QK + 2 * GLA_V

LANES = 128
ROUTE_WIDTH = 8
ROUTE_ROWS = 32
VMEM_LIMIT = 56 * 1024 * 1024

IN_TILE = 512
FOX_TILE = 512
FOX_SLAB = 64
POST_TILE = 512
CUMSUM_TILE = 256
MOE_BLOCK = 512
MOVE_TILE = 256
SC_GATHER_WINDOW = 128


def _cparams(sem):
    return pltpu.CompilerParams(dimension_semantics=sem, vmem_limit_bytes=VMEM_LIMIT)


def _rms(x, gain):
    return x * lax.rsqrt(jnp.mean(x * x, axis=-1, keepdims=True) + EPS) * gain


def _log_sigmoid(x):
    return jnp.minimum(x, 0.0) - jnp.log1p(jnp.exp(-jnp.abs(x)))


def _dot(a, b):
    return jnp.dot(a, b, preferred_element_type=F32)


def _dot_nt(a, b):
    return lax.dot_general(a, b, (((1,), (1,)), ((), ())), preferred_element_type=F32)


def _pack_rows(x):
    half = x.shape[1] // 2
    lo = lax.bitcast_convert_type(x[:, :half].astype(BF16).astype(F32), jnp.uint32)
    hi = lax.bitcast_convert_type(x[:, half:].astype(BF16).astype(F32), jnp.uint32)
    return (lo >> 16) | hi


def _unpack_rows(w):
    lo = lax.bitcast_convert_type(w << 16, F32)
    hi = lax.bitcast_convert_type(w & jnp.uint32(0xFFFF0000), F32)
    return lo, hi


def _split3(x):
    hi = x.astype(BF16)
    r1 = x - hi.astype(F32)
    mid = r1.astype(BF16)
    lo = (r1 - mid.astype(F32)).astype(BF16)
    return hi, mid, lo


def _mem_kv_kernel(mem_ref, gain_ref, wk_ref, wv_ref, k_ref, v_ref):
    mn = _rms(mem_ref[...], gain_ref[...]).astype(BF16)
    for l in range(wk_ref.shape[0]):
        k_ref[l] = _dot(mn, wk_ref[l]).astype(BF16)
        v_ref[l] = _dot(mn, wv_ref[l]).astype(BF16)


def _mem_kv(mem2d, gain, wk, wv, batch, mem_len):
    depth, d_model, width = wk.shape
    out = jax.ShapeDtypeStruct((depth, batch * mem_len, width), BF16)
    return pl.pallas_call(
        _mem_kv_kernel,
        grid=(batch,),
        in_specs=[
            pl.BlockSpec((mem_len, d_model), lambda b: (b, 0)),
            pl.BlockSpec((1, d_model), lambda b: (0, 0)),
            pl.BlockSpec((depth, d_model, width), lambda b: (0, 0, 0)),
            pl.BlockSpec((depth, d_model, width), lambda b: (0, 0, 0)),
        ],
        out_specs=[
            pl.BlockSpec((depth, mem_len, width), lambda b: (0, b, 0)),
            pl.BlockSpec((depth, mem_len, width), lambda b: (0, b, 0)),
        ],
        out_shape=[out, out],
        compiler_params=_cparams(("arbitrary",)),
        name="mem_kv",
    )(mem2d, gain, wk, wv)


def _in_proj_kernel(h_ref, gain_ref, wmain_ref, wsmall_ref, wup_ref, bf_ref, ba_ref,
                    main_ref, logf_ref, loga_ref):
    xn = _rms(h_ref[...], gain_ref[...]).astype(BF16)
    step = 512
    for j in range(MAIN_WIDTH // step):
        main_ref[:, j * step:(j + 1) * step] = _dot(xn, wmain_ref[:, j * step:(j + 1) * step]).astype(BF16)
    small = _dot(xn, wsmall_ref[...])
    lane = lax.broadcasted_iota(jnp.int32, small.shape, 1)
    logf_ref[...] = jnp.where(lane < FOX_HEADS, _log_sigmoid(small + bf_ref[...]), 0.0)
    a = _dot(small.astype(BF16), wup_ref[...]) + ba_ref[...]
    loga_ref[...] = _log_sigmoid(a) * (1.0 / GLA_TAU)


def _in_proj(h, gain, wmain, wsmall, wup, bf, ba, layer):
    n, d_model = h.shape
    tm = IN_TILE
    pick = lambda i: (layer, 0, 0)
    return pl.pallas_call(
        _in_proj_kernel,
        grid=(n // tm,),
        in_specs=[
            pl.BlockSpec((tm, d_model), lambda i: (i, 0)),
            pl.BlockSpec((None, 1, d_model), pick),
            pl.BlockSpec((None, d_model, MAIN_WIDTH), pick),
            pl.BlockSpec((None, d_model, LANES), pick),
            pl.BlockSpec((None, LANES, GLA_QK), pick),
            pl.BlockSpec((None, 1, LANES), pick),
            pl.BlockSpec((None, 1, GLA_QK), pick),
        ],
        out_specs=[
            pl.BlockSpec((tm, MAIN_WIDTH), lambda i: (i, 0)),
            pl.BlockSpec((tm, LANES), lambda i: (i, 0)),
            pl.BlockSpec((tm, GLA_QK), lambda i: (i, 0)),
        ],
        out_shape=[
            jax.ShapeDtypeStruct((n, MAIN_WIDTH), BF16),
            jax.ShapeDtypeStruct((n, LANES), F32),
            jax.ShapeDtypeStruct((n, GLA_QK), F32),
        ],
        compiler_params=_cparams(("arbitrary",)),
        name="in_proj",
    )(h, gain, wmain, wsmall, wup, bf, ba)


def _cumsum_kernel(x_ref, o_ref):
    t = CUMSUM_TILE
    row = lax.broadcasted_iota(jnp.int32, (t, t), 0)
    col = lax.broadcasted_iota(jnp.int32, (t, t), 1)
    tril = (row >= col).astype(BF16)
    carry = jnp.zeros((1, x_ref.shape[1]), F32)
    for j in range(x_ref.shape[0] // t):
        hi, mid, lo = _split3(x_ref[j * t:(j + 1) * t, :])
        c = _dot(tril, hi) + _dot(tril, mid) + _dot(tril, lo) + carry
        o_ref[j * t:(j + 1) * t, :] = c
        carry = c[t - 1:t, :]


def _seq_cumsum(x, batch, seq):
    return pl.pallas_call(
        _cumsum_kernel,
        grid=(batch,),
        in_specs=[pl.BlockSpec((seq, LANES), lambda b: (b, 0))],
        out_specs=pl.BlockSpec((seq, LANES), lambda b: (b, 0)),
        out_shape=jax.ShapeDtypeStruct(x.shape, F32),
        compiler_params=_cparams(("arbitrary",)),
        name="forget_cumsum",
    )(x)


def _fox_kernel(q_ref, k_ref, v_ref, c_ref, gain_ref, o_ref, q2_ref, s_ref, p_ref, alpha_ref, m_ref, l_ref, acc_ref):
    tq = FOX_TILE
    rows = 2 * tq
    slab = FOX_SLAB
    nq = q_ref.shape[0] // tq
    lane = lax.broadcasted_iota(jnp.int32, (1, LANES), 1)
    first = lane < FOX_DIM
    scale = FOX_DIM ** -0.5 * LOG2E
    for qi in range(nq):
        q = q_ref[qi * tq:(qi + 1) * tq, :].astype(F32) * scale
        q2_ref[qi, :tq, :] = jnp.where(first, q, 0.0).astype(BF16)
        q2_ref[qi, tq:, :] = jnp.where(first, 0.0, q).astype(BF16)

    def scores(qi, j):
        cj = c_ref[j] * LOG2E
        d = _dot_nt(q2_ref[qi], k_ref[j * tq:(j + 1) * tq, :])
        s_ref[:tq, :] = d[:tq] - cj[0:1, :]
        s_ref[tq:, :] = d[tq:] - cj[1:2, :]

    def weighted_values(qi, j):
        par = qi % 2
        acc_ref[par] = alpha_ref[par] * acc_ref[par] + _dot(p_ref[...], v_ref[j * tq:(j + 1) * tq, :])

    def softmax(qi, masked):
        par = qi % 2
        for r in range(rows // slab):
            sl = slice(r * slab, (r + 1) * slab)
            s = s_ref[sl, :]
            if masked:
                row = lax.broadcasted_iota(jnp.int32, (slab, tq), 0) + (r * slab) % tq
                col = lax.broadcasted_iota(jnp.int32, (slab, tq), 1)
                s = jnp.where(row >= col, s, -jnp.inf)
            m_old = m_ref[par, sl, :]
            m_new = jnp.maximum(m_old, jnp.max(s, axis=-1, keepdims=True))
            alpha = jnp.exp2(m_old - m_new)
            p = jnp.exp2(s - jnp.concatenate([m_new] * (tq // LANES), axis=1))
            l_ref[par, sl, :] = alpha * l_ref[par, sl, :] + jnp.sum(p, axis=-1, keepdims=True)
            m_ref[par, sl, :] = m_new
            alpha_ref[par, sl, :] = alpha
            p_ref[sl, :] = p.astype(BF16)

    def finalize(qi):
        par = qi % 2
        o2 = acc_ref[par] / l_ref[par]
        o = jnp.where(first, o2[:tq], o2[tq:])
        sq = o * o
        ss0 = jnp.sum(jnp.where(first, sq, 0.0), axis=-1, keepdims=True)
        ss1 = jnp.sum(jnp.where(first, 0.0, sq), axis=-1, keepdims=True)
        ms = jnp.where(first, ss0, ss1) * (1.0 / FOX_DIM)
        o_ref[qi * tq:(qi + 1) * tq, :] = (o * lax.rsqrt(ms + EPS) * gain_ref[...]).astype(BF16)

    steps = [(qi, j) for qi in range(nq) for j in range(qi + 1)]
    scores(*steps[0])
    for t, (qi, j) in enumerate(steps):
        if t > 0:
            weighted_values(*steps[t - 1])
            if steps[t - 1][0] != qi:
                finalize(steps[t - 1][0])
        if j == 0:
            par = qi % 2
            m_ref[par] = jnp.full(m_ref.shape[1:], -jnp.inf, F32)
            l_ref[par] = jnp.zeros(l_ref.shape[1:], F32)
            acc_ref[par] = jnp.zeros(acc_ref.shape[1:], F32)
        softmax(qi, masked=(j == qi))
        if t + 1 < len(steps):
            scores(*steps[t + 1])
    weighted_values(*steps[-1])
    finalize(steps[-1][0])


def _fox_attention(main, c5, gain, batch, seq):
    n = main.shape[0]
    tq = FOX_TILE
    nq = seq // tq
    pairs = FOX_HEADS // 2
    k_off = FOX_WIDTH // LANES
    v_off = 2 * FOX_WIDTH // LANES
    stat = pltpu.VMEM((2, 2 * tq, LANES), F32)
    return pl.pallas_call(
        _fox_kernel,
        grid=(batch, pairs),
        in_specs=[
            pl.BlockSpec((seq, LANES), lambda b, p: (b, p)),
            pl.BlockSpec((seq, LANES), lambda b, p: (b, k_off + p)),
            pl.BlockSpec((seq, LANES), lambda b, p: (b, v_off + p)),
            pl.BlockSpec((None, None, nq, 2, tq), lambda b, p: (b, p, 0, 0, 0)),
            pl.BlockSpec((1, LANES), lambda b, p: (0, p)),
        ],
        out_specs=pl.BlockSpec((seq, LANES), lambda b, p: (b, p)),
        out_shape=jax.ShapeDtypeStruct((n, FOX_WIDTH), BF16),
        scratch_shapes=[
            pltpu.VMEM((nq, 2 * tq, LANES), BF16),
            pltpu.VMEM((2 * tq, tq), F32),
            pltpu.VMEM((2 * tq, tq), BF16),
            stat, stat, stat, stat,
        ],
        compiler_params=_cparams(("arbitrary", "arbitrary")),
        name="fox_attention",
    )(main, main, main, c5, gain)


def _gla_kernel(q_ref, k_ref, v_ref, gg_ref, la_ref, gain_ref, o_ref, qe_ref, ke_ref, kl_ref, dec_ref, raw_ref):
    seq = q_ref.shape[0]
    cs = GLA_CHUNK
    nc = seq // cs
    width = 2 * GLA_DK

    b = la_ref[...]
    pos = lax.broadcasted_iota(jnp.int32, (seq, width), 0) % cs
    shift = 1
    while shift < cs:
        b = b + jnp.where(pos >= shift, pltpu.roll(b, shift, axis=0), 0.0)
        shift *= 2
    b3 = b.reshape(nc, cs, width)
    b_last = b3[:, cs - 1:cs, :]
    q = q_ref[...].astype(F32)
    k = k_ref[...].astype(F32)
    qe_ref[...] = (q * jnp.exp(b) * (GLA_DK ** -0.5)).astype(BF16)
    ke_ref[...] = (k * jnp.exp(-b)).astype(BF16)
    kl_ref[...] = (k.reshape(nc, cs, width) * jnp.exp(b_last - b3)).reshape(seq, width).astype(BF16)
    dec_ref[...] = jnp.exp(b_last).reshape(nc, width)

    lane = lax.broadcasted_iota(jnp.int32, (1, width), 1)
    first = lane < GLA_DK
    row = lax.broadcasted_iota(jnp.int32, (2 * cs, cs), 0)
    col = lax.broadcasted_iota(jnp.int32, (2 * cs, cs), 1)
    tril2 = jnp.where(row >= cs, row - cs, row) >= col
    srow = lax.broadcasted_iota(jnp.int32, (2 * GLA_DV, width), 0)
    scol = lax.broadcasted_iota(jnp.int32, (2 * GLA_DV, width), 1)
    same_head = (srow >= GLA_DV) == (scol >= GLA_DK)
    unroll = 8

    def chunks(ci, st):
        r0s = [pl.multiple_of((ci * unroll + u) * cs, cs) for u in range(unroll)]
        qes = [qe_ref[pl.ds(r0, cs), :] for r0 in r0s]
        vs = [v_ref[pl.ds(r0, cs), :] for r0 in r0s]
        atts, upds = [], []
        for u in range(unroll):
            zero = jnp.zeros_like(qes[u])
            q2 = jnp.concatenate([jnp.where(first, qes[u], zero), jnp.where(first, zero, qes[u])], axis=0)
            atts.append(jnp.where(tril2, _dot_nt(q2, ke_ref[pl.ds(r0s[u], cs), :]), 0.0).astype(BF16))
        for u in range(unroll):
            upds.append(lax.dot_general(vs[u], kl_ref[pl.ds(r0s[u], cs), :], (((0,), (0,)), ((), ())),
                                        preferred_element_type=F32))
        ois = [_dot(atts[u], vs[u]) for u in range(unroll)]
        for u in range(unroll):
            o = _dot_nt(qes[u], st.astype(BF16))
            o = o + jnp.concatenate([ois[u][:cs, :GLA_DV], ois[u][cs:, GLA_DV:]], axis=1)
            raw_ref[pl.ds(r0s[u], cs), :] = o
            st = st * dec_ref[pl.ds(ci * unroll + u, 1), :] + jnp.where(same_head, upds[u], 0.0)
        return st

    lax.fori_loop(0, nc // unroll, chunks, jnp.zeros((2 * GLA_DV, width), F32))

    o = raw_ref[...]
    normed = []
    for h in range(2):
        oh = o[:, h * GLA_DV:(h + 1) * GLA_DV]
        normed.append(oh * lax.rsqrt(jnp.mean(oh * oh, axis=-1, keepdims=True) + EPS))
    g = gg_ref[...].astype(F32)
    o_ref[...] = (jnp.concatenate(normed, axis=1) * gain_ref[...] * (g * jax.nn.sigmoid(g))).astype(BF16)


def _gla(main, loga, gain, batch, seq):
    n = main.shape[0]
    pairs = GLA_HEADS // 2
    q_off = 3 * FOX_WIDTH // LANES
    k_off = q_off + GLA_QK // LANES
    pv = 2 * GLA_DV
    v_off = (3 * FOX_WIDTH + 2 * GLA_QK) // pv
    g_off = v_off + GLA_V // pv
    return pl.pallas_call(
        _gla_kernel,
        grid=(batch, pairs),
        in_specs=[
            pl.BlockSpec((seq, LANES), lambda b, p: (b, q_off + p)),
            pl.BlockSpec((seq, LANES), lambda b, p: (b, k_off + p)),
            pl.BlockSpec((seq, pv), lambda b, p: (b, v_off + p)),
            pl.BlockSpec((seq, pv), lambda b, p: (b, g_off + p)),
            pl.BlockSpec((seq, LANES), lambda b, p: (b, p)),
            pl.BlockSpec((1, pv), lambda b, p: (0, p)),
        ],
        out_specs=pl.BlockSpec((seq, pv), lambda b, p: (b, p)),
        out_shape=jax.ShapeDtypeStruct((n, GLA_V), BF16),
        scratch_shapes=[
            pltpu.VMEM((seq, LANES), BF16),
            pltpu.VMEM((seq, LANES), BF16),
            pltpu.VMEM((seq, LANES), BF16),
            pltpu.VMEM((seq // GLA_CHUNK, LANES), F32),
            pltpu.VMEM((seq, pv), F32),
        ],
        compiler_params=_cparams(("arbitrary", "arbitrary")),
        name="gla",
    )(main, main, main, main, loga, gain)


def _post_kernel(fox_ref, gla_ref, h_ref, wout_ref, cg_ref, wxq_ref, k_ref, v_ref, wxo_ref, mg_ref,
                 wr_ref, br_ref, h2_ref, hn_ref, route_ref, cnt_ref, carry_ref):
    tm = h_ref.shape[0]

    @pl.when(pl.program_id(0) == 0)
    def _():
        carry_ref[...] = jnp.zeros_like(carry_ref)

    y = _dot(fox_ref[...], wout_ref[0:FOX_WIDTH, :]) + _dot(gla_ref[...], wout_ref[FOX_WIDTH:, :])
    h1 = h_ref[...] + y
    hn = _rms(h1, cg_ref[...]).astype(BF16)
    q = _dot(hn, wxq_ref[...]).astype(BF16)
    xscale = X_DIM ** -0.5
    heads = []
    for hh in range(X_HEADS):
        sl = slice(hh * X_DIM, (hh + 1) * X_DIM)
        s = _dot_nt(q[:, sl], k_ref[:, sl]) * xscale
        p = jnp.exp(s - jnp.max(s, axis=-1, keepdims=True))
        heads.append(_dot(p.astype(BF16), v_ref[:, sl]) / jnp.sum(p, axis=-1, keepdims=True))
    o = jnp.concatenate(heads, axis=1).astype(BF16)
    h2 = h1 + _dot(o, wxo_ref[...])
    h2_ref[...] = h2
    hn2 = _rms(h2, mg_ref[...])
    hn_ref[...] = _pack_rows(hn2)

    xh = hn2.astype(BF16)
    xl = (hn2 - xh.astype(F32)).astype(BF16)
    both_w = _dot(jnp.concatenate([xh, xl], axis=0), wr_ref[...])
    logits = both_w[:tm, :LANES] + both_w[:tm, LANES:] + both_w[tm:, :LANES] + both_w[tm:, LANES:] + br_ref[...]
    lt = jnp.transpose(logits)[:ROUTE_ROWS, :]
    row = lax.broadcasted_iota(jnp.int32, (ROUTE_ROWS, tm), 0)
    neg = -jnp.inf
    gl = jnp.where(row < N_GROUPS, lt, neg)
    gmax = jnp.max(gl, axis=0, keepdims=True)
    ge = jnp.exp(gl - gmax)
    gprob = ge / jnp.sum(ge, axis=0, keepdims=True)
    pmax = jnp.max(gprob, axis=0, keepdims=True)
    grp = jnp.min(jnp.where(gprob == pmax, row, ROUTE_ROWS), axis=0, keepdims=True)
    in_grp = (row >= N_GROUPS) & (row < N_GROUPS + N_EXPERTS) & (((row - N_GROUPS) // GROUP_SIZE) == grp)
    el = jnp.where(in_grp, lt, neg)
    emax = jnp.max(el, axis=0, keepdims=True)
    ee = jnp.exp(el - emax)
    eprob = ee / jnp.sum(ee, axis=0, keepdims=True)
    p1 = jnp.max(eprob, axis=0, keepdims=True)
    row1 = jnp.min(jnp.where(in_grp & (eprob == p1), row, ROUTE_ROWS), axis=0, keepdims=True)
    rest = jnp.where(in_grp & (row != row1), eprob, -1.0)
    p2 = jnp.max(rest, axis=0, keepdims=True)
    row2 = jnp.min(jnp.where(rest == p2, row, ROUTE_ROWS), axis=0, keepdims=True)
    g1 = pmax * p1 / (p1 + p2)
    g2 = pmax * p2 / (p1 + p2)

    oh1 = row == row1
    oh2 = row == row2
    both = (oh1 | oh2).astype(BF16)
    srow = lax.broadcasted_iota(jnp.int32, (tm, tm), 0)
    scol = lax.broadcasted_iota(jnp.int32, (tm, tm), 1)
    earlier = (srow < scol).astype(BF16)
    carry = carry_ref[...]
    seen = _dot(both, earlier) + jnp.concatenate([carry] * (tm // LANES), axis=1)
    rank1 = jnp.sum(jnp.where(oh1, seen, 0.0), axis=0, keepdims=True)
    rank2 = jnp.sum(jnp.where(oh2, seen, 0.0), axis=0, keepdims=True)
    carry = carry + jnp.sum(both.astype(F32), axis=1, keepdims=True)
    carry_ref[...] = carry
    cnt_ref[...] = carry

    e1 = (row1 - N_GROUPS).astype(F32)
    e2 = (row2 - N_GROUPS).astype(F32)
    zero = jnp.zeros_like(g1)
    route_ref[...] = jnp.concatenate([e1, e2, g1, g2, rank1, rank2, zero, zero], axis=0)


def _post(fox, gla, h, wout, cg, wxq, kmem, vmem, wxo, mg, wr, br, seq, mem_len, layer):
    n, d_model = h.shape
    tm = POST_TILE
    per_seq = seq // tm
    const = lambda i: (0, 0)
    pick = lambda i: (layer, 0, 0)
    return pl.pallas_call(
        _post_kernel,
        grid=(n // tm,),
        in_specs=[
            pl.BlockSpec((tm, FOX_WIDTH), lambda i: (i, 0)),
            pl.BlockSpec((tm, GLA_V), lambda i: (i, 0)),
            pl.BlockSpec((tm, d_model), lambda i: (i, 0)),
            pl.BlockSpec((None, FOX_WIDTH + GLA_V, d_model), pick),
            pl.BlockSpec((None, 1, d_model), pick),
            pl.BlockSpec((None, d_model, X_WIDTH), pick),
            pl.BlockSpec((None, mem_len, X_WIDTH), lambda i: (layer, i // per_seq, 0)),
            pl.BlockSpec((None, mem_len, X_WIDTH), lambda i: (layer, i // per_seq, 0)),
            pl.BlockSpec((None, X_WIDTH, d_model), pick),
            pl.BlockSpec((None, 1, d_model), pick),
            pl.BlockSpec((None, d_model, 2 * LANES), pick),
            pl.BlockSpec((None, 1, LANES), pick),
        ],
        out_specs=[
            pl.BlockSpec((tm, d_model), lambda i: (i, 0)),
            pl.BlockSpec((tm, d_model // 2), lambda i: (i, 0)),
            pl.BlockSpec((ROUTE_WIDTH, tm), lambda i: (0, i)),
            pl.BlockSpec((ROUTE_ROWS, LANES), const),
        ],
        out_shape=[
            jax.ShapeDtypeStruct((n, d_model), F32),
            jax.ShapeDtypeStruct((n, d_model // 2), jnp.uint32),
            jax.ShapeDtypeStruct((ROUTE_WIDTH, n), F32),
            jax.ShapeDtypeStruct((ROUTE_ROWS, LANES), F32),
        ],
        scratch_shapes=[pltpu.VMEM((ROUTE_ROWS, LANES), F32)],
        compiler_params=_cparams(("arbitrary",)),
        name="post_mixer",
    )(fox, gla, h, wout, cg, wxq, kmem, vmem, wxo, mg, wr, br)


def _dispatch(dest_kmajor, x, n_rows):
    n, width = x.shape
    window = SC_GATHER_WINDOW
    mesh = plsc.VectorSubcoreMesh(core_axis_name="core", subcore_axis_name="subcore")
    workers = mesh.num_cores * mesh.num_subcores
    per_worker = n // workers
    assert n % (workers * window) == 0

    @functools.partial(
        pl.kernel, out_type=jax.ShapeDtypeStruct((n_rows, width), x.dtype), mesh=mesh,
        scratch_types=[pltpu.VMEM((window,), jnp.int32), pltpu.VMEM((window,), jnp.int32),
                       pltpu.VMEM((window, width), x.dtype), pltpu.SemaphoreType.DMA])
    def scatter(x_hbm, idx_hbm, out_hbm, idx0_vmem, idx1_vmem, rows_vmem, sem):
        worker = lax.axis_index("subcore") * mesh.num_cores + lax.axis_index("core")
        base = worker * per_worker

        @pl.loop(0, per_worker // window)
        def _(step):
            off = pl.multiple_of(base + step * window, window)
            pltpu.sync_copy(idx_hbm.at[pl.ds(off, window)], idx0_vmem)
            pltpu.sync_copy(idx_hbm.at[pl.ds(n + off, window)], idx1_vmem)
            pltpu.sync_copy(x_hbm.at[pl.ds(off, window)], rows_vmem)
            pltpu.async_copy(rows_vmem, out_hbm.at[idx0_vmem], sem).wait()
            pltpu.async_copy(rows_vmem, out_hbm.at[idx1_vmem], sem).wait()

    return scatter(x, dest_kmajor)


def _expert_kernel(be_ref, valid_ref, fresh_ref, x_ref, wg_ref, wu_ref, wd_ref, y_ref, wg_b, wu_b, wd_b):
    del be_ref
    i = pl.program_id(0)
    valid = valid_ref[i]

    @pl.when(fresh_ref[i] > 0)
    def _():
        wg_b[...] = wg_ref[...].astype(BF16)
        wu_b[...] = wu_ref[...].astype(BF16)
        wd_b[...] = wd_ref[...].astype(BF16)

    @pl.when(valid > 0)
    def _():
        row = lax.broadcasted_iota(jnp.int32, x_ref.shape, 0)
        lo, hi = _unpack_rows(jnp.where(row < valid, x_ref[...], jnp.uint32(0)))
        lo = lo.astype(BF16)
        hi = hi.astype(BF16)
        half = lo.shape[1]
        g = _dot(lo, wg_b[:half, :]) + _dot(hi, wg_b[half:, :])
        u = _dot(lo, wu_b[:half, :]) + _dot(hi, wu_b[half:, :])
        a = (g * jax.nn.sigmoid(g) * u).astype(BF16)
        y_ref[...] = _pack_rows(_dot(a, wd_b[...]))

    @pl.when(valid <= 0)
    def _():
        y_ref[...] = jnp.zeros_like(y_ref)


def _experts(block_e, valid, xs, wg, wu, wd, layer):
    r, width = xs.shape
    bm = MOE_BLOCK
    d_model, d_exp = wg.shape[-2:]
    fresh = jnp.concatenate([jnp.ones((1,), jnp.int32), (block_e[1:] != block_e[:-1]).astype(jnp.int32)])
    pick = lambda i, be, va, fr: (layer, be[i], 0, 0)
    return pl.pallas_call(
        _expert_kernel,
        grid_spec=pltpu.PrefetchScalarGridSpec(
            num_scalar_prefetch=3,
            grid=(r // bm,),
            in_specs=[
                pl.BlockSpec((bm, width), lambda i, be, va, fr: (i, 0)),
                pl.BlockSpec((None, None, d_model, d_exp), pick),
                pl.BlockSpec((None, None, d_model, d_exp), pick),
                pl.BlockSpec((None, None, d_exp, d_model), pick),
            ],
            out_specs=pl.BlockSpec((bm, width), lambda i, be, va, fr: (i, 0)),
            scratch_shapes=[
                pltpu.VMEM((d_model, d_exp), BF16),
                pltpu.VMEM((d_model, d_exp), BF16),
                pltpu.VMEM((d_exp, d_model), BF16),
            ],
        ),
        out_shape=jax.ShapeDtypeStruct((r, width), jnp.uint32),
        compiler_params=_cparams(("arbitrary",)),
        name="moe_experts",
    )(block_e, valid, fresh, xs, wg, wu, wd)


def _sc_gather_rows(table, idx):
    m = idx.shape[0]
    width = table.shape[1]
    window = SC_GATHER_WINDOW
    mesh = plsc.VectorSubcoreMesh(core_axis_name="core", subcore_axis_name="subcore")
    workers = mesh.num_cores * mesh.num_subcores
    per_worker = m // workers
    assert m % (workers * window) == 0

    @functools.partial(
        pl.kernel, out_type=jax.ShapeDtypeStruct((m, width), table.dtype), mesh=mesh,
        scratch_types=[pltpu.VMEM((window,), jnp.int32), pltpu.VMEM((window, width), table.dtype),
                       pltpu.SemaphoreType.DMA])
    def gather(table_hbm, idx_hbm, out_hbm, idx_vmem, rows_vmem, sem):
        worker = lax.axis_index("subcore") * mesh.num_cores + lax.axis_index("core")
        base = worker * per_worker

        @pl.loop(0, per_worker // window)
        def _(step):
            off = pl.multiple_of(base + step * window, window)
            pltpu.sync_copy(idx_hbm.at[pl.ds(off, window)], idx_vmem)
            pltpu.async_copy(table_hbm.at[idx_vmem], rows_vmem, sem).wait()
            pltpu.sync_copy(rows_vmem, out_hbm.at[pl.ds(off, window)])

    return gather(table, idx)


def _combine_kernel(h_ref, gate_ref, gain_ref, y0_ref, y1_ref, o_ref, *, final):
    gate = gate_ref[...]
    y0 = jnp.concatenate(_unpack_rows(y0_ref[...]), axis=1)
    y1 = jnp.concatenate(_unpack_rows(y1_ref[...]), axis=1)
    out = h_ref[...] + gate[:, 0:1] * y0 + gate[:, 1:2] * y1
    if final:
        out = _rms(out, gain_ref[...])
    o_ref[...] = out


def _combine(dest, h, gates, gain, y, final):
    n, d_model = h.shape
    tc = MOVE_TILE
    picked = _sc_gather_rows(y, dest)
    steps = n // tc
    return pl.pallas_call(
        functools.partial(_combine_kernel, final=final),
        grid=(steps,),
        in_specs=[
            pl.BlockSpec((tc, d_model), lambda i: (i, 0)),
            pl.BlockSpec((tc, 2), lambda i: (i, 0)),
            pl.BlockSpec((1, d_model), lambda i: (0, 0)),
            pl.BlockSpec((tc, d_model // 2), lambda i: (i, 0)),
            pl.BlockSpec((tc, d_model // 2), lambda i: (i + steps, 0)),
        ],
        out_specs=pl.BlockSpec((tc, d_model), lambda i: (i, 0)),
        out_shape=jax.ShapeDtypeStruct((n, d_model), F32),
        compiler_params=_cparams(("arbitrary",)),
        name="moe_combine",
    )(h, gates, gain, picked, picked)


def _routing_tables(route, cnt, n_rows):
    bm = MOE_BLOCK
    expert = route[0:2].astype(jnp.int32)
    rank = route[4:6].astype(jnp.int32)
    counts = cnt[N_GROUPS:N_GROUPS + N_EXPERTS, 0].astype(jnp.int32)
    padded = (counts + bm - 1) // bm * bm
    pad_ends = jnp.cumsum(padded)
    pad_starts = pad_ends - padded
    ids = jnp.arange(N_EXPERTS, dtype=jnp.int32)
    start_of = jnp.sum(jnp.where(expert[..., None] == ids, pad_starts, 0), axis=-1)
    dest = (start_of + rank).reshape(-1).astype(jnp.int32)
    block_row = jnp.arange(n_rows // bm, dtype=jnp.int32) * bm
    block_e = jnp.minimum(jnp.sum((pad_ends[None, :] <= block_row[:, None]).astype(jnp.int32), axis=-1),
                          N_EXPERTS - 1)
    row_end = jnp.sum(jnp.where(block_e[:, None] == ids, pad_starts + counts, 0), axis=-1)
    valid = jnp.clip(row_end - block_row, 0, bm).astype(jnp.int32)
    return dest, block_e, valid


def kernel(x, mem, mem_norm, mix_norm, w_in, b_forget, w_alpha_up, b_alpha, fox_out_gain, gla_out_gain, w_out,
           cross_norm, w_xq, w_xk, w_xv, w_xo, moe_norm, w_router_group, b_router_group, w_router_expert,
           b_router_expert, w_expert_gate, w_expert_up, w_expert_down, final_norm):
    batch, seq, d_model = x.shape
    mem_len = mem.shape[1]
    depth = w_in.shape[0]
    n = batch * seq
    assert seq % FOX_TILE == 0 and seq % IN_TILE == 0 and seq % POST_TILE == 0 and seq % GLA_CHUNK == 0
    assert n % MOVE_TILE == 0 and d_model % LANES == 0

    c0 = 3 * FOX_WIDTH
    c1 = c0 + FOX_HEADS
    c2 = c1 + 2 * GLA_QK + 2 * GLA_V
    w_main = jnp.concatenate([w_in[:, :, :c0], w_in[:, :, c1:c2]], axis=-1).astype(BF16)
    pad = LANES - FOX_HEADS - GLA_RANK
    w_small = jnp.concatenate([w_in[:, :, c0:c1], w_in[:, :, c2:], jnp.zeros((depth, d_model, pad), F32)],
                              axis=-1).astype(BF16)
    w_up = jnp.concatenate([jnp.zeros((depth, FOX_HEADS, GLA_QK), F32), w_alpha_up,
                            jnp.zeros((depth, pad, GLA_QK), F32)], axis=1).astype(BF16)
    b_f = jnp.pad(b_forget, ((0, 0), (0, LANES - FOX_HEADS)))[:, None, :]
    b_a = b_alpha[:, None, :]
    w_r = jnp.concatenate([w_router_group, w_router_expert,
                           jnp.zeros((depth, d_model, LANES - N_GROUPS - N_EXPERTS), F32)], axis=-1)
    w_rh = w_r.astype(BF16)
    w_rs = jnp.concatenate([w_rh, (w_r - w_rh.astype(F32)).astype(BF16)], axis=-1)
    b_r = jnp.pad(jnp.concatenate([b_router_group, b_router_expert], axis=-1),
                  ((0, 0), (0, LANES - N_GROUPS - N_EXPERTS)))[:, None, :]
    w_out_b = w_out.astype(BF16)
    w_xq_b = w_xq.astype(BF16)
    w_xo_b = w_xo.astype(BF16)
    mix_g = mix_norm[:, None, :]
    cross_g = cross_norm[:, None, :]
    moe_g = moe_norm[:, None, :]

    kmem, vmem = _mem_kv(mem.reshape(batch * mem_len, d_model), mem_norm[None, :],
                         w_xk.astype(BF16), w_xv.astype(BF16), batch, mem_len)

    n_rows = 2 * n + N_EXPERTS * MOE_BLOCK
    nq = seq // FOX_TILE
    h = x.reshape(n, d_model)
    for l in range(depth):
        main, logf, loga = _in_proj(h, mix_g, w_main, w_small, w_up, b_f, b_a, l)
        c = _seq_cumsum(logf, batch, seq)
        c5 = c[:, :FOX_HEADS].reshape(batch, nq, FOX_TILE, FOX_HEADS // 2, 2).transpose(0, 3, 1, 4, 2)
        fox = _fox_attention(main, c5, fox_out_gain[l][None, :], batch, seq)
        gla = _gla(main, loga, gla_out_gain[l][None, :], batch, seq)
        h2, hn2, route, cnt = _post(fox, gla, h, w_out_b, cross_g, w_xq_b, kmem, vmem, w_xo_b, moe_g, w_rs, b_r,
                                    seq, mem_len, l)
        dest, block_e, valid = _routing_tables(route, cnt, n_rows)
        xs = _dispatch(dest, hn2, n_rows)
        y = _experts(block_e, valid, xs, w_expert_gate, w_expert_up, w_expert_down, l)
        h = _combine(dest, h2, route[2:4].T, final_norm[None, :], y, final=(l == depth - 1))
    return h.reshape(batch, seq, d_model)
```

```python
import functools

import jax
import jax.numpy as jnp
from jax import lax
from jax.experimental import pallas as pl
from jax.experimental.pallas import tpu as pltpu
from jax.experimental.pallas import tpu_sc as plsc

F32 = jnp.float32
BF16 = jnp.bfloat16
EPS = 1e-6
LOG2E = 1.4426950408889634

FOX_HEADS = 8
FOX_DIM = 64
FOX_WIDTH = FOX_HEADS * FOX_DIM
GLA_HEADS = 4
GLA_DK = 64
GLA_DV = 128
GLA_QK = GLA_HEADS * GLA_DK
GLA_V = GLA_HEADS * GLA_DV
GLA_RANK = 16
GLA_TAU = 16.0
GLA_CHUNK = 64
X_HEADS = 4
X_DIM = 128
X_WIDTH = X_HEADS * X_DIM
N_GROUPS = 4
GROUP_SIZE = 4
N_EXPERTS = N_GROUPS * GROUP_SIZE
MAIN_WIDTH = 3 * FOX_WIDTH + 2 * GLA_QK + 2 * GLA_V

LANES = 128
ROUTE_WIDTH = 8
ROUTE_ROWS = 32
VMEM_LIMIT = 56 * 1024 * 1024

IN_TILE = 512
FOX_TILE = 512
FOX_SLAB = 64
POST_TILE = 512
CUMSUM_TILE = 256
MOE_BLOCK = 512
MOVE_TILE = 256
SC_GATHER_WINDOW = 64


def _cparams(sem):
    return pltpu.CompilerParams(dimension_semantics=sem, vmem_limit_bytes=VMEM_LIMIT)


def _rms(x, gain):
    return x * lax.rsqrt(jnp.mean(x * x, axis=-1, keepdims=True) + EPS) * gain


def _log_sigmoid(x):
    return jnp.minimum(x, 0.0) - jnp.log1p(jnp.exp(-jnp.abs(x)))


def _dot(a, b):
    return jnp.dot(a, b, preferred_element_type=F32)


def _dot_nt(a, b):
    return lax.dot_general(a, b, (((1,), (1,)), ((), ())), preferred_element_type=F32)


def _pack_rows(x):
    half = x.shape[1] // 2
    lo = lax.bitcast_convert_type(x[:, :half].astype(BF16).astype(F32), jnp.uint32)
    hi = lax.bitcast_convert_type(x[:, half:].astype(BF16).astype(F32), jnp.uint32)
    return (lo >> 16) | hi


def _unpack_rows(w):
    lo = lax.bitcast_convert_type(w << 16, F32)
    hi = lax.bitcast_convert_type(w & jnp.uint32(0xFFFF0000), F32)
    return lo, hi


def _split3(x):
    hi = x.astype(BF16)
    r1 = x - hi.astype(F32)
    mid = r1.astype(BF16)
    lo = (r1 - mid.astype(F32)).astype(BF16)
    return hi, mid, lo


def _mem_kv_kernel(mem_ref, gain_ref, wk_ref, wv_ref, k_ref, v_ref):
    mn = _rms(mem_ref[...], gain_ref[...]).astype(BF16)
    for l in range(wk_ref.shape[0]):
        k_ref[l] = _dot(mn, wk_ref[l]).astype(BF16)
        v_ref[l] = _dot(mn, wv_ref[l]).astype(BF16)


def _mem_kv(mem2d, gain, wk, wv, batch, mem_len):
    depth, d_model, width = wk.shape
    out = jax.ShapeDtypeStruct((depth, batch * mem_len, width), BF16)
    return pl.pallas_call(
        _mem_kv_kernel,
        grid=(batch,),
        in_specs=[
            pl.BlockSpec((mem_len, d_model), lambda b: (b, 0)),
            pl.BlockSpec((1, d_model), lambda b: (0, 0)),
            pl.BlockSpec((depth, d_model, width), lambda b: (0, 0, 0)),
            pl.BlockSpec((depth, d_model, width), lambda b: (0, 0, 0)),
        ],
        out_specs=[
            pl.BlockSpec((depth, mem_len, width), lambda b: (0, b, 0)),
            pl.BlockSpec((depth, mem_len, width), lambda b: (0, b, 0)),
        ],
        out_shape=[out, out],
        compiler_params=_cparams(("arbitrary",)),
        name="mem_kv",
    )(mem2d, gain, wk, wv)


def _moe_sum(h, gate, y0_packed, y1_packed):
    y0 = jnp.concatenate(_unpack_rows(y0_packed), axis=1)
    y1 = jnp.concatenate(_unpack_rows(y1_packed), axis=1)
    return h + gate[:, 0:1] * y0 + gate[:, 1:2] * y1


def _in_proj_body(h, gain_ref, wmain_ref, wsmall_ref, wup_ref, bf_ref, ba_ref, main_ref, logf_ref, loga_ref):
    xn = _rms(h, gain_ref[...]).astype(BF16)
    step = 512
    for j in range(MAIN_WIDTH // step):
        main_ref[:, j * step:(j + 1) * step] = _dot(xn, wmain_ref[:, j * step:(j + 1) * step]).astype(BF16)
    small = _dot(xn, wsmall_ref[...])
    lane = lax.broadcasted_iota(jnp.int32, small.shape, 1)
    logf_ref[...] = jnp.where(lane < FOX_HEADS, _log_sigmoid(small + bf_ref[...]), 0.0)
    a = _dot(small.astype(BF16), wup_ref[...]) + ba_ref[...]
    loga_ref[...] = _log_sigmoid(a) * (1.0 / GLA_TAU)


def _in_proj_kernel(h_ref, *refs):
    _in_proj_body(h_ref[...], *refs)


def _in_proj_after_moe_kernel(h_ref, gate_ref, y0_ref, y1_ref, gain_ref, wmain_ref, wsmall_ref, wup_ref, bf_ref,
                              ba_ref, hout_ref, main_ref, logf_ref, loga_ref):
    h = _moe_sum(h_ref[...], gate_ref[...], y0_ref[...], y1_ref[...])
    hout_ref[...] = h
    _in_proj_body(h, gain_ref, wmain_ref, wsmall_ref, wup_ref, bf_ref, ba_ref, main_ref, logf_ref, loga_ref)


def _in_proj(h, gain, wmain, wsmall, wup, bf, ba, layer, moe=None):
    n, d_model = h.shape
    tm = IN_TILE
    steps = n // tm
    pick = lambda i: (layer, 0, 0)
    row_block = lambda width: pl.BlockSpec((tm, width), lambda i: (i, 0))
    weight_specs = [
        pl.BlockSpec((None, 1, d_model), pick),
        pl.BlockSpec((None, d_model, MAIN_WIDTH), pick),
        pl.BlockSpec((None, d_model, LANES), pick),
        pl.BlockSpec((None, LANES, GLA_QK), pick),
        pl.BlockSpec((None, 1, LANES), pick),
        pl.BlockSpec((None, 1, GLA_QK), pick),
    ]
    out_specs = [row_block(MAIN_WIDTH), row_block(LANES), row_block(GLA_QK)]
    out_shape = [
        jax.ShapeDtypeStruct((n, MAIN_WIDTH), BF16),
        jax.ShapeDtypeStruct((n, LANES), F32),
        jax.ShapeDtypeStruct((n, GLA_QK), F32),
    ]
    weights = (gain, wmain, wsmall, wup, bf, ba)
    if moe is None:
        return pl.pallas_call(
            _in_proj_kernel, grid=(steps,), in_specs=[row_block(d_model)] + weight_specs,
            out_specs=out_specs, out_shape=out_shape,
            compiler_params=_cparams(("arbitrary",)), name="in_proj",
        )(h, *weights)
    gates, picked = moe
    half = d_model // 2
    return pl.pallas_call(
        _in_proj_after_moe_kernel, grid=(steps,),
        in_specs=[row_block(d_model), row_block(2), row_block(half),
                  pl.BlockSpec((tm, half), lambda i: (i + steps, 0))] + weight_specs,
        out_specs=[row_block(d_model)] + out_specs,
        out_shape=[jax.ShapeDtypeStruct((n, d_model), F32)] + out_shape,
        compiler_params=_cparams(("arbitrary",)), name="in_proj_after_moe",
    )(h, gates, picked, picked, *weights)


def _cumsum_kernel(x_ref, o_ref):
    t = CUMSUM_TILE
    row = lax.broadcasted_iota(jnp.int32, (t, t), 0)
    col = lax.broadcasted_iota(jnp.int32, (t, t), 1)
    tril = (row >= col).astype(BF16)
    carry = jnp.zeros((1, x_ref.shape[1]), F32)
    for j in range(x_ref.shape[0] // t):
        hi, mid, lo = _split3(x_ref[j * t:(j + 1) * t, :])
        c = _dot(tril, hi) + _dot(tril, mid) + _dot(tril, lo) + carry
        o_ref[j * t:(j + 1) * t, :] = c
        carry = c[t - 1:t, :]


def _seq_cumsum(x, batch, seq):
    return pl.pallas_call(
        _cumsum_kernel,
        grid=(batch,),
        in_specs=[pl.BlockSpec((seq, LANES), lambda b: (b, 0))],
        out_specs=pl.BlockSpec((seq, LANES), lambda b: (b, 0)),
        out_shape=jax.ShapeDtypeStruct(x.shape, F32),
        compiler_params=_cparams(("arbitrary",)),
        name="forget_cumsum",
    )(x)


def _fox_kernel(q_ref, k_ref, v_ref, c_ref, gain_ref, o_ref, q2_ref, s_ref, p_ref, alpha_ref, m_ref, l_ref, acc_ref):
    tq = FOX_TILE
    rows = 2 * tq
    slab = FOX_SLAB
    nq = q_ref.shape[0] // tq
    lane = lax.broadcasted_iota(jnp.int32, (1, LANES), 1)
    first = lane < FOX_DIM
    scale = FOX_DIM ** -0.5 * LOG2E
    for qi in range(nq):
        q = q_ref[qi * tq:(qi + 1) * tq, :].astype(F32) * scale
        q2_ref[qi, :tq, :] = jnp.where(first, q, 0.0).astype(BF16)
        q2_ref[qi, tq:, :] = jnp.where(first, 0.0, q).astype(BF16)

    def scores(qi, j):
        cj = c_ref[j] * LOG2E
        d = _dot_nt(q2_ref[qi], k_ref[j * tq:(j + 1) * tq, :])
        s_ref[:tq, :] = d[:tq] - cj[0:1, :]
        s_ref[tq:, :] = d[tq:] - cj[1:2, :]

    def weighted_values(qi, j):
        par = qi % 2
        acc_ref[par] = alpha_ref[par] * acc_ref[par] + _dot(p_ref[...], v_ref[j * tq:(j + 1) * tq, :])

    def softmax(qi, masked):
        par = qi % 2
        for r in range(rows // slab):
            sl = slice(r * slab, (r + 1) * slab)
            s = s_ref[sl, :]
            if masked:
                row = lax.broadcasted_iota(jnp.int32, (slab, tq), 0) + (r * slab) % tq
                col = lax.broadcasted_iota(jnp.int32, (slab, tq), 1)
                s = jnp.where(row >= col, s, -jnp.inf)
            m_old = m_ref[par, sl, :]
            m_new = jnp.maximum(m_old, jnp.max(s, axis=-1, keepdims=True))
            alpha = jnp.exp2(m_old - m_new)
            p = jnp.exp2(s - jnp.concatenate([m_new] * (tq // LANES), axis=1))
            l_ref[par, sl, :] = alpha * l_ref[par, sl, :] + jnp.sum(p, axis=-1, keepdims=True)
            m_ref[par, sl, :] = m_new
            alpha_ref[par, sl, :] = alpha
            p_ref[sl, :] = p.astype(BF16)

    def finalize(qi):
        par = qi % 2
        o2 = acc_ref[par] / l_ref[par]
        o = jnp.where(first, o2[:tq], o2[tq:])
        sq = o * o
        ss0 = jnp.sum(jnp.where(first, sq, 0.0), axis=-1, keepdims=True)
        ss1 = jnp.sum(jnp.where(first, 0.0, sq), axis=-1, keepdims=True)
        ms = jnp.where(first, ss0, ss1) * (1.0 / FOX_DIM)
        o_ref[qi * tq:(qi + 1) * tq, :] = (o * lax.rsqrt(ms + EPS) * gain_ref[...]).astype(BF16)

    steps = [(qi, j) for qi in range(nq) for j in range(qi + 1)]
    scores(*steps[0])
    for t, (qi, j) in enumerate(steps):
        if t > 0:
            weighted_values(*steps[t - 1])
            if steps[t - 1][0] != qi:
                finalize(steps[t - 1][0])
        if j == 0:
            par = qi % 2
            m_ref[par] = jnp.full(m_ref.shape[1:], -jnp.inf, F32)
            l_ref[par] = jnp.zeros(l_ref.shape[1:], F32)
            acc_ref[par] = jnp.zeros(acc_ref.shape[1:], F32)
        softmax(qi, masked=(j == qi))
        if t + 1 < len(steps):
            scores(*steps[t + 1])
    weighted_values(*steps[-1])
    finalize(steps[-1][0])


def _fox_attention(main, c5, gain, batch, seq):
    n = main.shape[0]
    tq = FOX_TILE
    nq = seq // tq
    pairs = FOX_HEADS // 2
    k_off = FOX_WIDTH // LANES
    v_off = 2 * FOX_WIDTH // LANES
    stat = pltpu.VMEM((2, 2 * tq, LANES), F32)
    return pl.pallas_call(
        _fox_kernel,
        grid=(batch, pairs),
        in_specs=[
            pl.BlockSpec((seq, LANES), lambda b, p: (b, p)),
            pl.BlockSpec((seq, LANES), lambda b, p: (b, k_off + p)),
            pl.BlockSpec((seq, LANES), lambda b, p: (b, v_off + p)),
            pl.BlockSpec((None, None, nq, 2, tq), lambda b, p: (b, p, 0, 0, 0)),
            pl.BlockSpec((1, LANES), lambda b, p: (0, p)),
        ],
        out_specs=pl.BlockSpec((seq, LANES), lambda b, p: (b, p)),
        out_shape=jax.ShapeDtypeStruct((n, FOX_WIDTH), BF16),
        scratch_shapes=[
            pltpu.VMEM((nq, 2 * tq, LANES), BF16),
            pltpu.VMEM((2 * tq, tq), F32),
            pltpu.VMEM((2 * tq, tq), BF16),
            stat, stat, stat, stat,
        ],
        compiler_params=_cparams(("arbitrary", "arbitrary")),
        name="fox_attention",
    )(main, main, main, c5, gain)


def _gla_kernel(q_ref, k_ref, v_ref, gg_ref, la_ref, gain_ref, o_ref, qe_ref, ke_ref, kl_ref, dec_ref, raw_ref):
    seq = q_ref.shape[0]
    cs = GLA_CHUNK
    nc = seq // cs
    width = 2 * GLA_DK

    b = la_ref[...]
    pos = lax.broadcasted_iota(jnp.int32, (seq, width), 0) % cs
    shift = 1
    while shift < cs:
        b = b + jnp.where(pos >= shift, pltpu.roll(b, shift, axis=0), 0.0)
        shift *= 2
    b3 = b.reshape(nc, cs, width)
    b_last = b3[:, cs - 1:cs, :]
    q = q_ref[...].astype(F32)
    k = k_ref[...].astype(F32)
    qe_ref[...] = (q * jnp.exp(b) * (GLA_DK ** -0.5)).astype(BF16)
    ke_ref[...] = (k * jnp.exp(-b)).astype(BF16)
    kl_ref[...] = (k.reshape(nc, cs, width) * jnp.exp(b_last - b3)).reshape(seq, width).astype(BF16)
    dec_ref[...] = jnp.exp(b_last).reshape(nc, width)

    lane = lax.broadcasted_iota(jnp.int32, (1, width), 1)
    first = lane < GLA_DK
    row = lax.broadcasted_iota(jnp.int32, (2 * cs, cs), 0)
    col = lax.broadcasted_iota(jnp.int32, (2 * cs, cs), 1)
    tril2 = jnp.where(row >= cs, row - cs, row) >= col
    srow = lax.broadcasted_iota(jnp.int32, (2 * GLA_DV, width), 0)
    scol = lax.broadcasted_iota(jnp.int32, (2 * GLA_DV, width), 1)
    same_head = (srow >= GLA_DV) == (scol >= GLA_DK)
    unroll = 8

    def chunks(ci, st):
        r0s = [pl.multiple_of((ci * unroll + u) * cs, cs) for u in range(unroll)]
        qes = [qe_ref[pl.ds(r0, cs), :] for r0 in r0s]
        vs = [v_ref[pl.ds(r0, cs), :] for r0 in r0s]
        atts, upds = [], []
        for u in range(unroll):
            zero = jnp.zeros_like(qes[u])
            q2 = jnp.concatenate([jnp.where(first, qes[u], zero), jnp.where(first, zero, qes[u])], axis=0)
            atts.append(jnp.where(tril2, _dot_nt(q2, ke_ref[pl.ds(r0s[u], cs), :]), 0.0).astype(BF16))
        for u in range(unroll):
            upds.append(lax.dot_general(vs[u], kl_ref[pl.ds(r0s[u], cs), :], (((0,), (0,)), ((), ())),
                                        preferred_element_type=F32))
        ois = [_dot(atts[u], vs[u]) for u in range(unroll)]
        for u in range(unroll):
            o = _dot_nt(qes[u], st.astype(BF16))
            o = o + jnp.concatenate([ois[u][:cs, :GLA_DV], ois[u][cs:, GLA_DV:]], axis=1)
            raw_ref[pl.ds(r0s[u], cs), :] = o
            st = st * dec_ref[pl.ds(ci * unroll + u, 1), :] + jnp.where(same_head, upds[u], 0.0)
        return st

    lax.fori_loop(0, nc // unroll, chunks, jnp.zeros((2 * GLA_DV, width), F32))

    o = raw_ref[...]
    normed = []
    for h in range(2):
        oh = o[:, h * GLA_DV:(h + 1) * GLA_DV]
        normed.append(oh * lax.rsqrt(jnp.mean(oh * oh, axis=-1, keepdims=True) + EPS))
    g = gg_ref[...].astype(F32)
    o_ref[...] = (jnp.concatenate(normed, axis=1) * gain_ref[...] * (g * jax.nn.sigmoid(g))).astype(BF16)


def _gla(main, loga, gain, batch, seq):
    n = main.shape[0]
    pairs = GLA_HEADS // 2
    q_off = 3 * FOX_WIDTH // LANES
    k_off = q_off + GLA_QK // LANES
    pv = 2 * GLA_DV
    v_off = (3 * FOX_WIDTH + 2 * GLA_QK) // pv
    g_off = v_off + GLA_V // pv
    return pl.pallas_call(
        _gla_kernel,
        grid=(batch, pairs),
        in_specs=[
            pl.BlockSpec((seq, LANES), lambda b, p: (b, q_off + p)),
            pl.BlockSpec((seq, LANES), lambda b, p: (b, k_off + p)),
            pl.BlockSpec((seq, pv), lambda b, p: (b, v_off + p)),
            pl.BlockSpec((seq, pv), lambda b, p: (b, g_off + p)),
            pl.BlockSpec((seq, LANES), lambda b, p: (b, p)),
            pl.BlockSpec((1, pv), lambda b, p: (0, p)),
        ],
        out_specs=pl.BlockSpec((seq, pv), lambda b, p: (b, p)),
        out_shape=jax.ShapeDtypeStruct((n, GLA_V), BF16),
        scratch_shapes=[
            pltpu.VMEM((seq, LANES), BF16),
            pltpu.VMEM((seq, LANES), BF16),
            pltpu.VMEM((seq, LANES), BF16),
            pltpu.VMEM((seq // GLA_CHUNK, LANES), F32),
            pltpu.VMEM((seq, pv), F32),
        ],
        compiler_params=_cparams(("arbitrary", "arbitrary")),
        name="gla",
    )(main, main, main, main, loga, gain)


def _post_kernel(fox_ref, gla_ref, h_ref, wout_ref, cg_ref, wxq_ref, k_ref, v_ref, wxo_ref, mg_ref,
                 wr_ref, br_ref, h2_ref, hn_ref, route_ref, cnt_ref, carry_ref):
    tm = h_ref.shape[0]

    @pl.when(pl.program_id(0) == 0)
    def _():
        carry_ref[...] = jnp.zeros_like(carry_ref)

    y = _dot(fox_ref[...], wout_ref[0:FOX_WIDTH, :]) + _dot(gla_ref[...], wout_ref[FOX_WIDTH:, :])
    h1 = h_ref[...] + y
    hn = _rms(h1, cg_ref[...]).astype(BF16)
    q = _dot(hn, wxq_ref[...]).astype(BF16)
    xscale = X_DIM ** -0.5
    heads = []
    for hh in range(X_HEADS):
        sl = slice(hh * X_DIM, (hh + 1) * X_DIM)
        s = _dot_nt(q[:, sl], k_ref[:, sl]) * xscale
        p = jnp.exp(s - jnp.max(s, axis=-1, keepdims=True))
        heads.append(_dot(p.astype(BF16), v_ref[:, sl]) / jnp.sum(p, axis=-1, keepdims=True))
    o = jnp.concatenate(heads, axis=1).astype(BF16)
    h2 = h1 + _dot(o, wxo_ref[...])
    h2_ref[...] = h2
    hn2 = _rms(h2, mg_ref[...])
    hn_ref[...] = _pack_rows(hn2)

    xh = hn2.astype(BF16)
    xl = (hn2 - xh.astype(F32)).astype(BF16)
    both_w = _dot(jnp.concatenate([xh, xl], axis=0), wr_ref[...])
    logits = both_w[:tm, :LANES] + both_w[:tm, LANES:] + both_w[tm:, :LANES] + both_w[tm:, LANES:] + br_ref[...]
    lt = jnp.transpose(logits)[:ROUTE_ROWS, :]
    row = lax.broadcasted_iota(jnp.int32, (ROUTE_ROWS, tm), 0)
    neg = -jnp.inf
    gl = jnp.where(row < N_GROUPS, lt, neg)
    gmax = jnp.max(gl, axis=0, keepdims=True)
    ge = jnp.exp(gl - gmax)
    gprob = ge / jnp.sum(ge, axis=0, keepdims=True)
    pmax = jnp.max(gprob, axis=0, keepdims=True)
    grp = jnp.min(jnp.where(gprob == pmax, row, ROUTE_ROWS), axis=0, keepdims=True)
    in_grp = (row >= N_GROUPS) & (row < N_GROUPS + N_EXPERTS) & (((row - N_GROUPS) // GROUP_SIZE) == grp)
    el = jnp.where(in_grp, lt, neg)
    emax = jnp.max(el, axis=0, keepdims=True)
    ee = jnp.exp(el - emax)
    eprob = ee / jnp.sum(ee, axis=0, keepdims=True)
    p1 = jnp.max(eprob, axis=0, keepdims=True)
    row1 = jnp.min(jnp.where(in_grp & (eprob == p1), row, ROUTE_ROWS), axis=0, keepdims=True)
    rest = jnp.where(in_grp & (row != row1), eprob, -1.0)
    p2 = jnp.max(rest, axis=0, keepdims=True)
    row2 = jnp.min(jnp.where(rest == p2, row, ROUTE_ROWS), axis=0, keepdims=True)
    g1 = pmax * p1 / (p1 + p2)
    g2 = pmax * p2 / (p1 + p2)

    oh1 = row == row1
    oh2 = row == row2
    both = (oh1 | oh2).astype(BF16)
    srow = lax.broadcasted_iota(jnp.int32, (tm, tm), 0)
    scol = lax.broadcasted_iota(jnp.int32, (tm, tm), 1)
    earlier = (srow < scol).astype(BF16)
    carry = carry_ref[...]
    seen = _dot(both, earlier) + jnp.concatenate([carry] * (tm // LANES), axis=1)
    rank1 = jnp.sum(jnp.where(oh1, seen, 0.0), axis=0, keepdims=True)
    rank2 = jnp.sum(jnp.where(oh2, seen, 0.0), axis=0, keepdims=True)
    carry = carry + jnp.sum(both.astype(F32), axis=1, keepdims=True)
    carry_ref[...] = carry
    cnt_ref[...] = carry

    e1 = (row1 - N_GROUPS).astype(F32)
    e2 = (row2 - N_GROUPS).astype(F32)
    zero = jnp.zeros_like(g1)
    route_ref[...] = jnp.concatenate([e1, e2, g1, g2, rank1, rank2, zero, zero], axis=0)


def _post(fox, gla, h, wout, cg, wxq, kmem, vmem, wxo, mg, wr, br, seq, mem_len, layer):
    n, d_model = h.shape
    tm = POST_TILE
    per_seq = seq // tm
    const = lambda i: (0, 0)
    pick = lambda i: (layer, 0, 0)
    return pl.pallas_call(
        _post_kernel,
        grid=(n // tm,),
        in_specs=[
            pl.BlockSpec((tm, FOX_WIDTH), lambda i: (i, 0)),
            pl.BlockSpec((tm, GLA_V), lambda i: (i, 0)),
            pl.BlockSpec((tm, d_model), lambda i: (i, 0)),
            pl.BlockSpec((None, FOX_WIDTH + GLA_V, d_model), pick),
            pl.BlockSpec((None, 1, d_model), pick),
            pl.BlockSpec((None, d_model, X_WIDTH), pick),
            pl.BlockSpec((None, mem_len, X_WIDTH), lambda i: (layer, i // per_seq, 0)),
            pl.BlockSpec((None, mem_len, X_WIDTH), lambda i: (layer, i // per_seq, 0)),
            pl.BlockSpec((None, X_WIDTH, d_model), pick),
            pl.BlockSpec((None, 1, d_model), pick),
            pl.BlockSpec((None, d_model, 2 * LANES), pick),
            pl.BlockSpec((None, 1, LANES), pick),
        ],
        out_specs=[
            pl.BlockSpec((tm, d_model), lambda i: (i, 0)),
            pl.BlockSpec((tm, d_model // 2), lambda i: (i, 0)),
            pl.BlockSpec((ROUTE_WIDTH, tm), lambda i: (0, i)),
            pl.BlockSpec((ROUTE_ROWS, LANES), const),
        ],
        out_shape=[
            jax.ShapeDtypeStruct((n, d_model), F32),
            jax.ShapeDtypeStruct((n, d_model // 2), jnp.uint32),
            jax.ShapeDtypeStruct((ROUTE_WIDTH, n), F32),
            jax.ShapeDtypeStruct((ROUTE_ROWS, LANES), F32),
        ],
        scratch_shapes=[pltpu.VMEM((ROUTE_ROWS, LANES), F32)],
        compiler_params=_cparams(("arbitrary",)),
        name="post_mixer",
    )(fox, gla, h, wout, cg, wxq, kmem, vmem, wxo, mg, wr, br)


def _dispatch(dest_kmajor, x, n_rows):
    n, width = x.shape
    window = SC_GATHER_WINDOW
    mesh = plsc.VectorSubcoreMesh(core_axis_name="core", subcore_axis_name="subcore")
    workers = mesh.num_cores * mesh.num_subcores
    per_worker = n // workers
    assert n % (workers * window) == 0

    steps = per_worker // window
    assert steps % 2 == 0
    index_buf = pltpu.VMEM((window,), jnp.int32)
    row_buf = pltpu.VMEM((window, width), x.dtype)
    dma = pltpu.SemaphoreType.DMA

    @functools.partial(
        pl.kernel, out_type=jax.ShapeDtypeStruct((n_rows, width), x.dtype), mesh=mesh,
        scratch_types=[index_buf, index_buf, index_buf, index_buf, row_buf, row_buf, dma, dma, dma])
    def scatter(x_hbm, idx_hbm, out_hbm, idx0_a, idx1_a, idx0_b, idx1_b, rows_a, rows_b, sem_a, sem_b, sem_out):
        worker = lax.axis_index("subcore") * mesh.num_cores + lax.axis_index("core")
        base = worker * per_worker
        slots = ((idx0_a, idx1_a, rows_a, sem_a), (idx0_b, idx1_b, rows_b, sem_b))

        def load(step, slot):
            idx0, idx1, rows, sem = slots[slot]
            off = pl.multiple_of(base + step * window, window)
            pltpu.sync_copy(idx_hbm.at[pl.ds(off, window)], idx0)
            pltpu.sync_copy(idx_hbm.at[pl.ds(n + off, window)], idx1)
            pltpu.async_copy(x_hbm.at[pl.ds(off, window)], rows, sem)

        def store(slot):
            idx0, idx1, rows, sem = slots[slot]
            pltpu.make_async_copy(x_hbm.at[pl.ds(0, window)], rows, sem).wait()
            first = pltpu.async_copy(rows, out_hbm.at[idx0], sem_out)
            second = pltpu.async_copy(rows, out_hbm.at[idx1], sem_out)
            first.wait()
            second.wait()

        load(0, 0)

        @pl.loop(0, steps, step=2)
        def _(step):
            load(step + 1, 1)
            store(0)

            @pl.when(step + 2 < steps)
            def _():
                load(step + 2, 0)

            store(1)

    return scatter(x, dest_kmajor)


def _expert_kernel(be_ref, valid_ref, fresh_ref, x_ref, wg_ref, wu_ref, wd_ref, y_ref, wg_b, wu_b, wd_b):
    del be_ref
    i = pl.program_id(0)
    valid = valid_ref[i]

    @pl.when(fresh_ref[i] > 0)
    def _():
        wg_b[...] = wg_ref[...].astype(BF16)
        wu_b[...] = wu_ref[...].astype(BF16)
        wd_b[...] = wd_ref[...].astype(BF16)

    @pl.when(valid > 0)
    def _():
        row = lax.broadcasted_iota(jnp.int32, x_ref.shape, 0)
        lo, hi = _unpack_rows(jnp.where(row < valid, x_ref[...], jnp.uint32(0)))
        lo = lo.astype(BF16)
        hi = hi.astype(BF16)
        half = lo.shape[1]
        g = _dot(lo, wg_b[:half, :]) + _dot(hi, wg_b[half:, :])
        u = _dot(lo, wu_b[:half, :]) + _dot(hi, wu_b[half:, :])
        a = (g * jax.nn.sigmoid(g) * u).astype(BF16)
        y_ref[...] = _pack_rows(_dot(a, wd_b[...]))

    @pl.when(valid <= 0)
    def _():
        y_ref[...] = jnp.zeros_like(y_ref)


def _experts(block_e, valid, xs, wg, wu, wd, layer):
    r, width = xs.shape
    bm = MOE_BLOCK
    d_model, d_exp = wg.shape[-2:]
    fresh = jnp.concatenate([jnp.ones((1,), jnp.int32), (block_e[1:] != block_e[:-1]).astype(jnp.int32)])
    pick = lambda i, be, va, fr: (layer, be[i], 0, 0)
    return pl.pallas_call(
        _expert_kernel,
        grid_spec=pltpu.PrefetchScalarGridSpec(
            num_scalar_prefetch=3,
            grid=(r // bm,),
            in_specs=[
                pl.BlockSpec((bm, width), lambda i, be, va, fr: (i, 0)),
                pl.BlockSpec((None, None, d_model, d_exp), pick),
                pl.BlockSpec((None, None, d_model, d_exp), pick),
                pl.BlockSpec((None, None, d_exp, d_model), pick),
            ],
            out_specs=pl.BlockSpec((bm, width), lambda i, be, va, fr: (i, 0)),
            scratch_shapes=[
                pltpu.VMEM((d_model, d_exp), BF16),
                pltpu.VMEM((d_model, d_exp), BF16),
                pltpu.VMEM((d_exp, d_model), BF16),
            ],
        ),
        out_shape=jax.ShapeDtypeStruct((r, width), jnp.uint32),
        compiler_params=_cparams(("arbitrary",)),
        name="moe_experts",
    )(block_e, valid, fresh, xs, wg, wu, wd)


def _sc_gather_rows(table, idx):
    m = idx.shape[0]
    width = table.shape[1]
    window = SC_GATHER_WINDOW
    mesh = plsc.VectorSubcoreMesh(core_axis_name="core", subcore_axis_name="subcore")
    workers = mesh.num_cores * mesh.num_subcores
    per_worker = m // workers
    assert m % (workers * window) == 0

    steps = per_worker // window
    assert steps % 2 == 0
    index_buf = pltpu.VMEM((window,), jnp.int32)
    row_buf = pltpu.VMEM((window, width), table.dtype)
    dma = pltpu.SemaphoreType.DMA

    @functools.partial(
        pl.kernel, out_type=jax.ShapeDtypeStruct((m, width), table.dtype), mesh=mesh,
        scratch_types=[index_buf, index_buf, row_buf, row_buf, dma, dma])
    def gather(table_hbm, idx_hbm, out_hbm, idx_a, idx_b, rows_a, rows_b, sem_a, sem_b):
        worker = lax.axis_index("subcore") * mesh.num_cores + lax.axis_index("core")
        base = worker * per_worker
        slots = ((idx_a, rows_a, sem_a), (idx_b, rows_b, sem_b))

        def fetch(step, slot):
            idx, rows, sem = slots[slot]
            off = pl.multiple_of(base + step * window, window)
            pltpu.sync_copy(idx_hbm.at[pl.ds(off, window)], idx)
            pltpu.async_copy(table_hbm.at[idx], rows, sem)

        def flush(step, slot):
            idx, rows, sem = slots[slot]
            off = pl.multiple_of(base + step * window, window)
            pltpu.make_async_copy(table_hbm.at[idx], rows, sem).wait()
            pltpu.sync_copy(rows, out_hbm.at[pl.ds(off, window)])

        fetch(0, 0)

        @pl.loop(0, steps, step=2)
        def _(step):
            fetch(step + 1, 1)
            flush(step, 0)

            @pl.when(step + 2 < steps)
            def _():
                fetch(step + 2, 0)

            flush(step + 1, 1)

    return gather(table, idx)


def _final_kernel(h_ref, gate_ref, gain_ref, y0_ref, y1_ref, o_ref):
    o_ref[...] = _rms(_moe_sum(h_ref[...], gate_ref[...], y0_ref[...], y1_ref[...]), gain_ref[...])


def _final(h, gates, gain, picked):
    n, d_model = h.shape
    tc = MOVE_TILE
    steps = n // tc
    return pl.pallas_call(
        _final_kernel,
        grid=(steps,),
        in_specs=[
            pl.BlockSpec((tc, d_model), lambda i: (i, 0)),
            pl.BlockSpec((tc, 2), lambda i: (i, 0)),
            pl.BlockSpec((1, d_model), lambda i: (0, 0)),
            pl.BlockSpec((tc, d_model // 2), lambda i: (i, 0)),
            pl.BlockSpec((tc, d_model // 2), lambda i: (i + steps, 0)),
        ],
        out_specs=pl.BlockSpec((tc, d_model), lambda i: (i, 0)),
        out_shape=jax.ShapeDtypeStruct((n, d_model), F32),
        compiler_params=_cparams(("arbitrary",)),
        name="moe_final",
    )(h, gates, gain, picked, picked)


def _routing_tables(route, cnt, n_rows):
    bm = MOE_BLOCK
    expert = route[0:2].astype(jnp.int32)
    rank = route[4:6].astype(jnp.int32)
    counts = cnt[N_GROUPS:N_GROUPS + N_EXPERTS, 0].astype(jnp.int32)
    padded = (counts + bm - 1) // bm * bm
    pad_ends = jnp.cumsum(padded)
    pad_starts = pad_ends - padded
    ids = jnp.arange(N_EXPERTS, dtype=jnp.int32)
    start_of = jnp.sum(jnp.where(expert[..., None] == ids, pad_starts, 0), axis=-1)
    dest = (start_of + rank).reshape(-1).astype(jnp.int32)
    block_row = jnp.arange(n_rows // bm, dtype=jnp.int32) * bm
    block_e = jnp.minimum(jnp.sum((pad_ends[None, :] <= block_row[:, None]).astype(jnp.int32), axis=-1),
                          N_EXPERTS - 1)
    row_end = jnp.sum(jnp.where(block_e[:, None] == ids, pad_starts + counts, 0), axis=-1)
    valid = jnp.clip(row_end - block_row, 0, bm).astype(jnp.int32)
    return dest, block_e, valid


def kernel(x, mem, mem_norm, mix_norm, w_in, b_forget, w_alpha_up, b_alpha, fox_out_gain, gla_out_gain, w_out,
           cross_norm, w_xq, w_xk, w_xv, w_xo, moe_norm, w_router_group, b_router_group, w_router_expert,
           b_router_expert, w_expert_gate, w_expert_up, w_expert_down, final_norm):
    batch, seq, d_model = x.shape
    mem_len = mem.shape[1]
    depth = w_in.shape[0]
    n = batch * seq
    assert seq % FOX_TILE == 0 and seq % IN_TILE == 0 and seq % POST_TILE == 0 and seq % GLA_CHUNK == 0
    assert n % MOVE_TILE == 0 and d_model % LANES == 0

    c0 = 3 * FOX_WIDTH
    c1 = c0 + FOX_HEADS
    c2 = c1 + 2 * GLA_QK + 2 * GLA_V
    w_main = jnp.concatenate([w_in[:, :, :c0], w_in[:, :, c1:c2]], axis=-1).astype(BF16)
    pad = LANES - FOX_HEADS - GLA_RANK
    w_small = jnp.concatenate([w_in[:, :, c0:c1], w_in[:, :, c2:], jnp.zeros((depth, d_model, pad), F32)],
                              axis=-1).astype(BF16)
    w_up = jnp.concatenate([jnp.zeros((depth, FOX_HEADS, GLA_QK), F32), w_alpha_up,
                            jnp.zeros((depth, pad, GLA_QK), F32)], axis=1).astype(BF16)
    b_f = jnp.pad(b_forget, ((0, 0), (0, LANES - FOX_HEADS)))[:, None, :]
    b_a = b_alpha[:, None, :]
    w_r = jnp.concatenate([w_router_group, w_router_expert,
                           jnp.zeros((depth, d_model, LANES - N_GROUPS - N_EXPERTS), F32)], axis=-1)
    w_rh = w_r.astype(BF16)
    w_rs = jnp.concatenate([w_rh, (w_r - w_rh.astype(F32)).astype(BF16)], axis=-1)
    b_r = jnp.pad(jnp.concatenate([b_router_group, b_router_expert], axis=-1),
                  ((0, 0), (0, LANES - N_GROUPS - N_EXPERTS)))[:, None, :]
    w_out_b = w_out.astype(BF16)
    w_xq_b = w_xq.astype(BF16)
    w_xo_b = w_xo.astype(BF16)
    mix_g = mix_norm[:, None, :]
    cross_g = cross_norm[:, None, :]
    moe_g = moe_norm[:, None, :]

    kmem, vmem = _mem_kv(mem.reshape(batch * mem_len, d_model), mem_norm[None, :],
                         w_xk.astype(BF16), w_xv.astype(BF16), batch, mem_len)

    n_rows = 2 * n + N_EXPERTS * MOE_BLOCK
    nq = seq // FOX_TILE
    h = x.reshape(n, d_model)
    moe = None
    for l in range(depth):
        if moe is None:
            main, logf, loga = _in_proj(h, mix_g, w_main, w_small, w_up, b_f, b_a, l)
        else:
            h, main, logf, loga = _in_proj(h, mix_g, w_main, w_small, w_up, b_f, b_a, l, moe)
        c = _seq_cumsum(logf, batch, seq)
        c5 = c[:, :FOX_HEADS].reshape(batch, nq, FOX_TILE, FOX_HEADS // 2, 2).transpose(0, 3, 1, 4, 2)
        fox = _fox_attention(main, c5, fox_out_gain[l][None, :], batch, seq)
        gla = _gla(main, loga, gla_out_gain[l][None, :], batch, seq)
        h2, hn2, route, cnt = _post(fox, gla, h, w_out_b, cross_g, w_xq_b, kmem, vmem, w_xo_b, moe_g, w_rs, b_r,
                                    seq, mem_len, l)
        dest, block_e, valid = _routing_tables(route, cnt, n_rows)
        xs = _dispatch(dest, hn2, n_rows)
        y = _experts(block_e, valid, xs, w_expert_gate, w_expert_up, w_expert_down, l)
        h, moe = h2, (route[2:4].T, _sc_gather_rows(y, dest))
    return _final(h, moe[0], final_norm[None, :], moe[1]).reshape(batch, seq, d_model)
```

```python
import functools

import jax
import jax.numpy as jnp
from jax import lax
from jax.experimental import pallas as pl
from jax.experimental.pallas import tpu as pltpu
from jax.experimental.pallas import tpu_sc as plsc

F32 = jnp.float32
BF16 = jnp.bfloat16
EPS = 1e-6
LOG2E = 1.4426950408889634

FOX_HEADS = 8
FOX_DIM = 64
FOX_WIDTH = FOX_HEADS * FOX_DIM
GLA_HEADS = 4
GLA_DK = 64
GLA_DV = 128
GLA_QK = GLA_HEADS * GLA_DK
GLA_V = GLA_HEADS * GLA_DV
GLA_RANK = 16
GLA_TAU = 16.0
GLA_CHUNK = 64
X_HEADS = 4
X_DIM = 128
X_WIDTH = X_HEADS * X_DIM
N_GROUPS = 4
GROUP_SIZE = 4
N_EXPERTS = N_GROUPS * GROUP_SIZE
MAIN_WIDTH = 3 * FOX_WIDTH + 2 * GLA_QK + 2 * GLA_V

LANES = 128
ROUTE_WIDTH = 8
ROUTE_ROWS = 32
VMEM_LIMIT = 56 * 1024 * 1024

IN_TILE = 1024
FOX_TILE = 512
FOX_SLAB = 64
POST_TILE = 1024
CUMSUM_TILE = 256
MOE_BLOCK = 512
MOVE_TILE = 256
SC_GATHER_WINDOW = 64


def _cparams(sem):
    return pltpu.CompilerParams(dimension_semantics=sem, vmem_limit_bytes=VMEM_LIMIT)


def _rms(x, gain):
    return x * lax.rsqrt(jnp.mean(x * x, axis=-1, keepdims=True) + EPS) * gain


def _log_sigmoid(x):
    return jnp.minimum(x, 0.0) - jnp.log1p(jnp.exp(-jnp.abs(x)))


def _dot(a, b):
    return jnp.dot(a, b, preferred_element_type=F32)


def _dot_nt(a, b):
    return lax.dot_general(a, b, (((1,), (1,)), ((), ())), preferred_element_type=F32)


def _pack_rows(x):
    half = x.shape[1] // 2
    lo = lax.bitcast_convert_type(x[:, :half].astype(BF16).astype(F32), jnp.uint32)
    hi = lax.bitcast_convert_type(x[:, half:].astype(BF16).astype(F32), jnp.uint32)
    return (lo >> 16) | hi


def _unpack_rows(w):
    lo = lax.bitcast_convert_type(w << 16, F32)
    hi = lax.bitcast_convert_type(w & jnp.uint32(0xFFFF0000), F32)
    return lo, hi


def _split3(x):
    hi = x.astype(BF16)
    r1 = x - hi.astype(F32)
    mid = r1.astype(BF16)
    lo = (r1 - mid.astype(F32)).astype(BF16)
    return hi, mid, lo


def _mem_kv_kernel(mem_ref, gain_ref, wk_ref, wv_ref, k_ref, v_ref):
    mn = _rms(mem_ref[...], gain_ref[...]).astype(BF16)
    for l in range(wk_ref.shape[0]):
        k_ref[l] = _dot(mn, wk_ref[l]).astype(BF16)
        v_ref[l] = _dot(mn, wv_ref[l]).astype(BF16)


def _mem_kv(mem2d, gain, wk, wv, batch, mem_len):
    depth, d_model, width = wk.shape
    out = jax.ShapeDtypeStruct((depth, batch * mem_len, width), BF16)
    return pl.pallas_call(
        _mem_kv_kernel,
        grid=(batch,),
        in_specs=[
            pl.BlockSpec((mem_len, d_model), lambda b: (b, 0)),
            pl.BlockSpec((1, d_model), lambda b: (0, 0)),
            pl.BlockSpec((depth, d_model, width), lambda b: (0, 0, 0)),
            pl.BlockSpec((depth, d_model, width), lambda b: (0, 0, 0)),
        ],
        out_specs=[
            pl.BlockSpec((depth, mem_len, width), lambda b: (0, b, 0)),
            pl.BlockSpec((depth, mem_len, width), lambda b: (0, b, 0)),
        ],
        out_shape=[out, out],
        compiler_params=_cparams(("arbitrary",)),
        name="mem_kv",
    )(mem2d, gain, wk, wv)


def _moe_sum(h, gate, y0_packed, y1_packed):
    y0 = jnp.concatenate(_unpack_rows(y0_packed), axis=1)
    y1 = jnp.concatenate(_unpack_rows(y1_packed), axis=1)
    return h + gate[:, 0:1] * y0 + gate[:, 1:2] * y1


def _in_proj_body(h, gain_ref, wmain_ref, wsmall_ref, wup_ref, bf_ref, ba_ref, main_ref, logf_ref, loga_ref):
    xn = _rms(h, gain_ref[...]).astype(BF16)
    step = 512
    for j in range(MAIN_WIDTH // step):
        main_ref[:, j * step:(j + 1) * step] = _dot(xn, wmain_ref[:, j * step:(j + 1) * step]).astype(BF16)
    small = _dot(xn, wsmall_ref[...])
    lane = lax.broadcasted_iota(jnp.int32, small.shape, 1)
    logf_ref[...] = jnp.where(lane < FOX_HEADS, _log_sigmoid(small + bf_ref[...]), 0.0)
    a = _dot(small.astype(BF16), wup_ref[...]) + ba_ref[...]
    loga_ref[...] = _log_sigmoid(a) * (1.0 / GLA_TAU)


def _in_proj_kernel(h_ref, *refs):
    _in_proj_body(h_ref[...], *refs)


def _in_proj_after_moe_kernel(h_ref, gate_ref, y0_ref, y1_ref, gain_ref, wmain_ref, wsmall_ref, wup_ref, bf_ref,
                              ba_ref, hout_ref, main_ref, logf_ref, loga_ref):
    h = _moe_sum(h_ref[...], gate_ref[...], y0_ref[...], y1_ref[...])
    hout_ref[...] = h
    _in_proj_body(h, gain_ref, wmain_ref, wsmall_ref, wup_ref, bf_ref, ba_ref, main_ref, logf_ref, loga_ref)


def _in_proj(h, gain, wmain, wsmall, wup, bf, ba, layer, moe=None):
    n, d_model = h.shape
    tm = IN_TILE
    steps = n // tm
    pick = lambda i: (layer, 0, 0)
    row_block = lambda width: pl.BlockSpec((tm, width), lambda i: (i, 0))
    weight_specs = [
        pl.BlockSpec((None, 1, d_model), pick),
        pl.BlockSpec((None, d_model, MAIN_WIDTH), pick),
        pl.BlockSpec((None, d_model, LANES), pick),
        pl.BlockSpec((None, LANES, GLA_QK), pick),
        pl.BlockSpec((None, 1, LANES), pick),
        pl.BlockSpec((None, 1, GLA_QK), pick),
    ]
    out_specs = [row_block(MAIN_WIDTH), row_block(LANES), row_block(GLA_QK)]
    out_shape = [
        jax.ShapeDtypeStruct((n, MAIN_WIDTH), BF16),
        jax.ShapeDtypeStruct((n, LANES), F32),
        jax.ShapeDtypeStruct((n, GLA_QK), F32),
    ]
    weights = (gain, wmain, wsmall, wup, bf, ba)
    if moe is None:
        return pl.pallas_call(
            _in_proj_kernel, grid=(steps,), in_specs=[row_block(d_model)] + weight_specs,
            out_specs=out_specs, out_shape=out_shape,
            compiler_params=_cparams(("arbitrary",)), name="in_proj",
        )(h, *weights)
    gates, picked = moe
    half = d_model // 2
    return pl.pallas_call(
        _in_proj_after_moe_kernel, grid=(steps,),
        in_specs=[row_block(d_model), row_block(2), row_block(half),
                  pl.BlockSpec((tm, half), lambda i: (i + steps, 0))] + weight_specs,
        out_specs=[row_block(d_model)] + out_specs,
        out_shape=[jax.ShapeDtypeStruct((n, d_model), F32)] + out_shape,
        compiler_params=_cparams(("arbitrary",)), name="in_proj_after_moe",
    )(h, gates, picked, picked, *weights)


def _cumsum_kernel(x_ref, o_ref):
    t = CUMSUM_TILE
    row = lax.broadcasted_iota(jnp.int32, (t, t), 0)
    col = lax.broadcasted_iota(jnp.int32, (t, t), 1)
    tril = (row >= col).astype(BF16)
    carry = jnp.zeros((1, x_ref.shape[1]), F32)
    for j in range(x_ref.shape[0] // t):
        hi, mid, lo = _split3(x_ref[j * t:(j + 1) * t, :])
        c = _dot(tril, hi) + _dot(tril, mid) + _dot(tril, lo) + carry
        o_ref[j * t:(j + 1) * t, :] = c
        carry = c[t - 1:t, :]


def _seq_cumsum(x, batch, seq):
    return pl.pallas_call(
        _cumsum_kernel,
        grid=(batch,),
        in_specs=[pl.BlockSpec((seq, LANES), lambda b: (b, 0))],
        out_specs=pl.BlockSpec((seq, LANES), lambda b: (b, 0)),
        out_shape=jax.ShapeDtypeStruct(x.shape, F32),
        compiler_params=_cparams(("arbitrary",)),
        name="forget_cumsum",
    )(x)


def _fox_kernel(q_ref, k_ref, v_ref, c_ref, gain_ref, o_ref, q2_ref, s_ref, p_ref, alpha_ref, m_ref, l_ref, acc_ref):
    tq = FOX_TILE
    rows = 2 * tq
    slab = FOX_SLAB
    nq = q_ref.shape[0] // tq
    lane = lax.broadcasted_iota(jnp.int32, (1, LANES), 1)
    first = lane < FOX_DIM
    scale = FOX_DIM ** -0.5 * LOG2E
    for qi in range(nq):
        q = q_ref[qi * tq:(qi + 1) * tq, :].astype(F32) * scale
        q2_ref[qi, :tq, :] = jnp.where(first, q, 0.0).astype(BF16)
        q2_ref[qi, tq:, :] = jnp.where(first, 0.0, q).astype(BF16)

    def scores(qi, j):
        cj = c_ref[j] * LOG2E
        d = _dot_nt(q2_ref[qi], k_ref[j * tq:(j + 1) * tq, :])
        s_ref[:tq, :] = d[:tq] - cj[0:1, :]
        s_ref[tq:, :] = d[tq:] - cj[1:2, :]

    def weighted_values(qi, j):
        par = qi % 2
        acc_ref[par] = alpha_ref[par] * acc_ref[par] + _dot(p_ref[...], v_ref[j * tq:(j + 1) * tq, :])

    def softmax(qi, masked):
        par = qi % 2
        for r in range(rows // slab):
            sl = slice(r * slab, (r + 1) * slab)
            s = s_ref[sl, :]
            if masked:
                row = lax.broadcasted_iota(jnp.int32, (slab, tq), 0) + (r * slab) % tq
                col = lax.broadcasted_iota(jnp.int32, (slab, tq), 1)
                s = jnp.where(row >= col, s, -jnp.inf)
            m_old = m_ref[par, sl, :]
            m_new = jnp.maximum(m_old, jnp.max(s, axis=-1, keepdims=True))
            alpha = jnp.exp2(m_old - m_new)
            p = jnp.exp2(s - jnp.concatenate([m_new] * (tq // LANES), axis=1))
            l_ref[par, sl, :] = alpha * l_ref[par, sl, :] + jnp.sum(p, axis=-1, keepdims=True)
            m_ref[par, sl, :] = m_new
            alpha_ref[par, sl, :] = alpha
            p_ref[sl, :] = p.astype(BF16)

    def finalize(qi):
        par = qi % 2
        o2 = acc_ref[par] / l_ref[par]
        o = jnp.where(first, o2[:tq], o2[tq:])
        sq = o * o
        ss0 = jnp.sum(jnp.where(first, sq, 0.0), axis=-1, keepdims=True)
        ss1 = jnp.sum(jnp.where(first, 0.0, sq), axis=-1, keepdims=True)
        ms = jnp.where(first, ss0, ss1) * (1.0 / FOX_DIM)
        o_ref[qi * tq:(qi + 1) * tq, :] = (o * lax.rsqrt(ms + EPS) * gain_ref[...]).astype(BF16)

    steps = [(qi, j) for qi in range(nq) for j in range(qi + 1)]
    scores(*steps[0])
    for t, (qi, j) in enumerate(steps):
        if t > 0:
            weighted_values(*steps[t - 1])
            if steps[t - 1][0] != qi:
                finalize(steps[t - 1][0])
        if j == 0:
            par = qi % 2
            m_ref[par] = jnp.full(m_ref.shape[1:], -jnp.inf, F32)
            l_ref[par] = jnp.zeros(l_ref.shape[1:], F32)
            acc_ref[par] = jnp.zeros(acc_ref.shape[1:], F32)
        softmax(qi, masked=(j == qi))
        if t + 1 < len(steps):
            scores(*steps[t + 1])
    weighted_values(*steps[-1])
    finalize(steps[-1][0])


def _fox_attention(main, c5, gain, batch, seq):
    n = main.shape[0]
    tq = FOX_TILE
    nq = seq // tq
    pairs = FOX_HEADS // 2
    k_off = FOX_WIDTH // LANES
    v_off = 2 * FOX_WIDTH // LANES
    stat = pltpu.VMEM((2, 2 * tq, LANES), F32)
    return pl.pallas_call(
        _fox_kernel,
        grid=(batch, pairs),
        in_specs=[
            pl.BlockSpec((seq, LANES), lambda b, p: (b, p)),
            pl.BlockSpec((seq, LANES), lambda b, p: (b, k_off + p)),
            pl.BlockSpec((seq, LANES), lambda b, p: (b, v_off + p)),
            pl.BlockSpec((None, None, nq, 2, tq), lambda b, p: (b, p, 0, 0, 0)),
            pl.BlockSpec((1, LANES), lambda b, p: (0, p)),
        ],
        out_specs=pl.BlockSpec((seq, LANES), lambda b, p: (b, p)),
        out_shape=jax.ShapeDtypeStruct((n, FOX_WIDTH), BF16),
        scratch_shapes=[
            pltpu.VMEM((nq, 2 * tq, LANES), BF16),
            pltpu.VMEM((2 * tq, tq), F32),
            pltpu.VMEM((2 * tq, tq), BF16),
            stat, stat, stat, stat,
        ],
        compiler_params=_cparams(("arbitrary", "arbitrary")),
        name="fox_attention",
    )(main, main, main, c5, gain)


def _gla_kernel(q_ref, k_ref, v_ref, gg_ref, la_ref, gain_ref, o_ref, qe_ref, ke_ref, kl_ref, dec_ref, raw_ref):
    seq = q_ref.shape[0]
    cs = GLA_CHUNK
    nc = seq // cs
    width = 2 * GLA_DK

    b = la_ref[...]
    pos = lax.broadcasted_iota(jnp.int32, (seq, width), 0) % cs
    shift = 1
    while shift < cs:
        b = b + jnp.where(pos >= shift, pltpu.roll(b, shift, axis=0), 0.0)
        shift *= 2
    b3 = b.reshape(nc, cs, width)
    b_last = b3[:, cs - 1:cs, :]
    q = q_ref[...].astype(F32)
    k = k_ref[...].astype(F32)
    qe_ref[...] = (q * jnp.exp(b) * (GLA_DK ** -0.5)).astype(BF16)
    ke_ref[...] = (k * jnp.exp(-b)).astype(BF16)
    kl_ref[...] = (k.reshape(nc, cs, width) * jnp.exp(b_last - b3)).reshape(seq, width).astype(BF16)
    dec_ref[...] = jnp.exp(b_last).reshape(nc, width)

    lane = lax.broadcasted_iota(jnp.int32, (1, width), 1)
    first = lane < GLA_DK
    row = lax.broadcasted_iota(jnp.int32, (2 * cs, cs), 0)
    col = lax.broadcasted_iota(jnp.int32, (2 * cs, cs), 1)
    tril2 = jnp.where(row >= cs, row - cs, row) >= col
    srow = lax.broadcasted_iota(jnp.int32, (2 * GLA_DV, width), 0)
    scol = lax.broadcasted_iota(jnp.int32, (2 * GLA_DV, width), 1)
    same_head = (srow >= GLA_DV) == (scol >= GLA_DK)
    unroll = 8

    def chunks(ci, st):
        r0s = [pl.multiple_of((ci * unroll + u) * cs, cs) for u in range(unroll)]
        qes = [qe_ref[pl.ds(r0, cs), :] for r0 in r0s]
        vs = [v_ref[pl.ds(r0, cs), :] for r0 in r0s]
        atts, upds = [], []
        for u in range(unroll):
            zero = jnp.zeros_like(qes[u])
            q2 = jnp.concatenate([jnp.where(first, qes[u], zero), jnp.where(first, zero, qes[u])], axis=0)
            atts.append(jnp.where(tril2, _dot_nt(q2, ke_ref[pl.ds(r0s[u], cs), :]), 0.0).astype(BF16))
        for u in range(unroll):
            upds.append(lax.dot_general(vs[u], kl_ref[pl.ds(r0s[u], cs), :], (((0,), (0,)), ((), ())),
                                        preferred_element_type=F32))
        ois = [_dot(atts[u], vs[u]) for u in range(unroll)]
        for u in range(unroll):
            o = _dot_nt(qes[u], st.astype(BF16))
            o = o + jnp.concatenate([ois[u][:cs, :GLA_DV], ois[u][cs:, GLA_DV:]], axis=1)
            raw_ref[pl.ds(r0s[u], cs), :] = o
            st = st * dec_ref[pl.ds(ci * unroll + u, 1), :] + jnp.where(same_head, upds[u], 0.0)
        return st

    lax.fori_loop(0, nc // unroll, chunks, jnp.zeros((2 * GLA_DV, width), F32))

    o = raw_ref[...]
    normed = []
    for h in range(2):
        oh = o[:, h * GLA_DV:(h + 1) * GLA_DV]
        normed.append(oh * lax.rsqrt(jnp.mean(oh * oh, axis=-1, keepdims=True) + EPS))
    g = gg_ref[...].astype(F32)
    o_ref[...] = (jnp.concatenate(normed, axis=1) * gain_ref[...] * (g * jax.nn.sigmoid(g))).astype(BF16)


def _gla(main, loga, gain, batch, seq):
    n = main.shape[0]
    pairs = GLA_HEADS // 2
    q_off = 3 * FOX_WIDTH // LANES
    k_off = q_off + GLA_QK // LANES
    pv = 2 * GLA_DV
    v_off = (3 * FOX_WIDTH + 2 * GLA_QK) // pv
    g_off = v_off + GLA_V // pv
    return pl.pallas_call(
        _gla_kernel,
        grid=(batch, pairs),
        in_specs=[
            pl.BlockSpec((seq, LANES), lambda b, p: (b, q_off + p)),
            pl.BlockSpec((seq, LANES), lambda b, p: (b, k_off + p)),
            pl.BlockSpec((seq, pv), lambda b, p: (b, v_off + p)),
            pl.BlockSpec((seq, pv), lambda b, p: (b, g_off + p)),
            pl.BlockSpec((seq, LANES), lambda b, p: (b, p)),
            pl.BlockSpec((1, pv), lambda b, p: (0, p)),
        ],
        out_specs=pl.BlockSpec((seq, pv), lambda b, p: (b, p)),
        out_shape=jax.ShapeDtypeStruct((n, GLA_V), BF16),
        scratch_shapes=[
            pltpu.VMEM((seq, LANES), BF16),
            pltpu.VMEM((seq, LANES), BF16),
            pltpu.VMEM((seq, LANES), BF16),
            pltpu.VMEM((seq // GLA_CHUNK, LANES), F32),
            pltpu.VMEM((seq, pv), F32),
        ],
        compiler_params=_cparams(("arbitrary", "arbitrary")),
        name="gla",
    )(main, main, main, main, loga, gain)


def _post_kernel(fox_ref, gla_ref, h_ref, wout_ref, cg_ref, wxq_ref, k_ref, v_ref, wxo_ref, mg_ref,
                 wr_ref, br_ref, h2_ref, hn_ref, route_ref, cnt_ref, carry_ref):
    tm = h_ref.shape[0]

    @pl.when(pl.program_id(0) == 0)
    def _():
        carry_ref[...] = jnp.zeros_like(carry_ref)

    y = _dot(fox_ref[...], wout_ref[0:FOX_WIDTH, :]) + _dot(gla_ref[...], wout_ref[FOX_WIDTH:, :])
    h1 = h_ref[...] + y
    hn = _rms(h1, cg_ref[...]).astype(BF16)
    q = _dot(hn, wxq_ref[...]).astype(BF16)
    xscale = X_DIM ** -0.5
    heads = []
    for hh in range(X_HEADS):
        sl = slice(hh * X_DIM, (hh + 1) * X_DIM)
        s = _dot_nt(q[:, sl], k_ref[:, sl]) * xscale
        p = jnp.exp(s - jnp.max(s, axis=-1, keepdims=True))
        heads.append(_dot(p.astype(BF16), v_ref[:, sl]) / jnp.sum(p, axis=-1, keepdims=True))
    o = jnp.concatenate(heads, axis=1).astype(BF16)
    h2 = h1 + _dot(o, wxo_ref[...])
    h2_ref[...] = h2
    hn2 = _rms(h2, mg_ref[...])
    hn_ref[...] = _pack_rows(hn2)

    xh = hn2.astype(BF16)
    xl = (hn2 - xh.astype(F32)).astype(BF16)
    both_w = _dot(jnp.concatenate([xh, xl], axis=0), wr_ref[...])
    logits = both_w[:tm, :LANES] + both_w[:tm, LANES:] + both_w[tm:, :LANES] + both_w[tm:, LANES:] + br_ref[...]
    lt = jnp.transpose(logits)[:ROUTE_ROWS, :]
    row = lax.broadcasted_iota(jnp.int32, (ROUTE_ROWS, tm), 0)
    neg = -jnp.inf
    gl = jnp.where(row < N_GROUPS, lt, neg)
    gmax = jnp.max(gl, axis=0, keepdims=True)
    ge = jnp.exp(gl - gmax)
    gprob = ge / jnp.sum(ge, axis=0, keepdims=True)
    pmax = jnp.max(gprob, axis=0, keepdims=True)
    grp = jnp.min(jnp.where(gprob == pmax, row, ROUTE_ROWS), axis=0, keepdims=True)
    in_grp = (row >= N_GROUPS) & (row < N_GROUPS + N_EXPERTS) & (((row - N_GROUPS) // GROUP_SIZE) == grp)
    el = jnp.where(in_grp, lt, neg)
    emax = jnp.max(el, axis=0, keepdims=True)
    ee = jnp.exp(el - emax)
    eprob = ee / jnp.sum(ee, axis=0, keepdims=True)
    p1 = jnp.max(eprob, axis=0, keepdims=True)
    row1 = jnp.min(jnp.where(in_grp & (eprob == p1), row, ROUTE_ROWS), axis=0, keepdims=True)
    rest = jnp.where(in_grp & (row != row1), eprob, -1.0)
    p2 = jnp.max(rest, axis=0, keepdims=True)
    row2 = jnp.min(jnp.where(rest == p2, row, ROUTE_ROWS), axis=0, keepdims=True)
    g1 = pmax * p1 / (p1 + p2)
    g2 = pmax * p2 / (p1 + p2)

    oh1 = row == row1
    oh2 = row == row2
    both = (oh1 | oh2).astype(BF16)
    srow = lax.broadcasted_iota(jnp.int32, (tm, tm), 0)
    scol = lax.broadcasted_iota(jnp.int32, (tm, tm), 1)
    earlier = (srow < scol).astype(BF16)
    carry = carry_ref[...]
    seen = _dot(both, earlier) + jnp.concatenate([carry] * (tm // LANES), axis=1)
    rank1 = jnp.sum(jnp.where(oh1, seen, 0.0), axis=0, keepdims=True)
    rank2 = jnp.sum(jnp.where(oh2, seen, 0.0), axis=0, keepdims=True)
    carry = carry + jnp.sum(both.astype(F32), axis=1, keepdims=True)
    carry_ref[...] = carry
    cnt_ref[...] = carry

    e1 = (row1 - N_GROUPS).astype(F32)
    e2 = (row2 - N_GROUPS).astype(F32)
    zero = jnp.zeros_like(g1)
    route_ref[...] = jnp.concatenate([e1, e2, g1, g2, rank1, rank2, zero, zero], axis=0)


def _post(fox, gla, h, wout, cg, wxq, kmem, vmem, wxo, mg, wr, br, seq, mem_len, layer):
    n, d_model = h.shape
    tm = POST_TILE
    per_seq = seq // tm
    const = lambda i: (0, 0)
    pick = lambda i: (layer, 0, 0)
    return pl.pallas_call(
        _post_kernel,
        grid=(n // tm,),
        in_specs=[
            pl.BlockSpec((tm, FOX_WIDTH), lambda i: (i, 0)),
            pl.BlockSpec((tm, GLA_V), lambda i: (i, 0)),
            pl.BlockSpec((tm, d_model), lambda i: (i, 0)),
            pl.BlockSpec((None, FOX_WIDTH + GLA_V, d_model), pick),
            pl.BlockSpec((None, 1, d_model), pick),
            pl.BlockSpec((None, d_model, X_WIDTH), pick),
            pl.BlockSpec((None, mem_len, X_WIDTH), lambda i: (layer, i // per_seq, 0)),
            pl.BlockSpec((None, mem_len, X_WIDTH), lambda i: (layer, i // per_seq, 0)),
            pl.BlockSpec((None, X_WIDTH, d_model), pick),
            pl.BlockSpec((None, 1, d_model), pick),
            pl.BlockSpec((None, d_model, 2 * LANES), pick),
            pl.BlockSpec((None, 1, LANES), pick),
        ],
        out_specs=[
            pl.BlockSpec((tm, d_model), lambda i: (i, 0)),
            pl.BlockSpec((tm, d_model // 2), lambda i: (i, 0)),
            pl.BlockSpec((ROUTE_WIDTH, tm), lambda i: (0, i)),
            pl.BlockSpec((ROUTE_ROWS, LANES), const),
        ],
        out_shape=[
            jax.ShapeDtypeStruct((n, d_model), F32),
            jax.ShapeDtypeStruct((n, d_model // 2), jnp.uint32),
            jax.ShapeDtypeStruct((ROUTE_WIDTH, n), F32),
            jax.ShapeDtypeStruct((ROUTE_ROWS, LANES), F32),
        ],
        scratch_shapes=[pltpu.VMEM((ROUTE_ROWS, LANES), F32)],
        compiler_params=_cparams(("arbitrary",)),
        name="post_mixer",
    )(fox, gla, h, wout, cg, wxq, kmem, vmem, wxo, mg, wr, br)


def _dispatch(dest_kmajor, x, n_rows):
    n, width = x.shape
    window = SC_GATHER_WINDOW
    mesh = plsc.VectorSubcoreMesh(core_axis_name="core", subcore_axis_name="subcore")
    workers = mesh.num_cores * mesh.num_subcores
    per_worker = n // workers
    assert n % (workers * window) == 0

    steps = per_worker // window
    assert steps % 2 == 0
    index_buf = pltpu.VMEM((window,), jnp.int32)
    row_buf = pltpu.VMEM((window, width), x.dtype)
    dma = pltpu.SemaphoreType.DMA

    @functools.partial(
        pl.kernel, out_type=jax.ShapeDtypeStruct((n_rows, width), x.dtype), mesh=mesh,
        scratch_types=[index_buf, index_buf, index_buf, index_buf, row_buf, row_buf, dma, dma, dma])
    def scatter(x_hbm, idx_hbm, out_hbm, idx0_a, idx1_a, idx0_b, idx1_b, rows_a, rows_b, sem_a, sem_b, sem_out):
        worker = lax.axis_index("subcore") * mesh.num_cores + lax.axis_index("core")
        base = worker * per_worker
        slots = ((idx0_a, idx1_a, rows_a, sem_a), (idx0_b, idx1_b, rows_b, sem_b))

        def load(step, slot):
            idx0, idx1, rows, sem = slots[slot]
            off = pl.multiple_of(base + step * window, window)
            pltpu.sync_copy(idx_hbm.at[pl.ds(off, window)], idx0)
            pltpu.sync_copy(idx_hbm.at[pl.ds(n + off, window)], idx1)
            pltpu.async_copy(x_hbm.at[pl.ds(off, window)], rows, sem)

        def store(slot):
            idx0, idx1, rows, sem = slots[slot]
            pltpu.make_async_copy(x_hbm.at[pl.ds(0, window)], rows, sem).wait()
            first = pltpu.async_copy(rows, out_hbm.at[idx0], sem_out)
            second = pltpu.async_copy(rows, out_hbm.at[idx1], sem_out)
            first.wait()
            second.wait()

        load(0, 0)

        @pl.loop(0, steps, step=2)
        def _(step):
            load(step + 1, 1)
            store(0)

            @pl.when(step + 2 < steps)
            def _():
                load(step + 2, 0)

            store(1)

    return scatter(x, dest_kmajor)


def _expert_kernel(be_ref, valid_ref, fresh_ref, x_ref, wg_ref, wu_ref, wd_ref, y_ref, wg_b, wu_b, wd_b):
    del be_ref
    i = pl.program_id(0)
    valid = valid_ref[i]

    @pl.when(fresh_ref[i] > 0)
    def _():
        wg_b[...] = wg_ref[...].astype(BF16)
        wu_b[...] = wu_ref[...].astype(BF16)
        wd_b[...] = wd_ref[...].astype(BF16)

    @pl.when(valid > 0)
    def _():
        row = lax.broadcasted_iota(jnp.int32, x_ref.shape, 0)
        lo, hi = _unpack_rows(jnp.where(row < valid, x_ref[...], jnp.uint32(0)))
        lo = lo.astype(BF16)
        hi = hi.astype(BF16)
        half = lo.shape[1]
        g = _dot(lo, wg_b[:half, :]) + _dot(hi, wg_b[half:, :])
        u = _dot(lo, wu_b[:half, :]) + _dot(hi, wu_b[half:, :])
        a = (g * jax.nn.sigmoid(g) * u).astype(BF16)
        y_ref[...] = _pack_rows(_dot(a, wd_b[...]))

    @pl.when(valid <= 0)
    def _():
        y_ref[...] = jnp.zeros_like(y_ref)


def _experts(block_e, valid, xs, wg, wu, wd, layer):
    r, width = xs.shape
    bm = MOE_BLOCK
    d_model, d_exp = wg.shape[-2:]
    fresh = jnp.concatenate([jnp.ones((1,), jnp.int32), (block_e[1:] != block_e[:-1]).astype(jnp.int32)])
    pick = lambda i, be, va, fr: (layer, be[i], 0, 0)
    return pl.pallas_call(
        _expert_kernel,
        grid_spec=pltpu.PrefetchScalarGridSpec(
            num_scalar_prefetch=3,
            grid=(r // bm,),
            in_specs=[
                pl.BlockSpec((bm, width), lambda i, be, va, fr: (i, 0)),
                pl.BlockSpec((None, None, d_model, d_exp), pick),
                pl.BlockSpec((None, None, d_model, d_exp), pick),
                pl.BlockSpec((None, None, d_exp, d_model), pick),
            ],
            out_specs=pl.BlockSpec((bm, width), lambda i, be, va, fr: (i, 0)),
            scratch_shapes=[
                pltpu.VMEM((d_model, d_exp), BF16),
                pltpu.VMEM((d_model, d_exp), BF16),
                pltpu.VMEM((d_exp, d_model), BF16),
            ],
        ),
        out_shape=jax.ShapeDtypeStruct((r, width), jnp.uint32),
        compiler_params=_cparams(("arbitrary",)),
        name="moe_experts",
    )(block_e, valid, fresh, xs, wg, wu, wd)


def _sc_gather_rows(table, idx):
    m = idx.shape[0]
    width = table.shape[1]
    window = SC_GATHER_WINDOW
    mesh = plsc.VectorSubcoreMesh(core_axis_name="core", subcore_axis_name="subcore")
    workers = mesh.num_cores * mesh.num_subcores
    per_worker = m // workers
    assert m % (workers * window) == 0

    steps = per_worker // window
    assert steps % 2 == 0
    index_buf = pltpu.VMEM((window,), jnp.int32)
    row_buf = pltpu.VMEM((window, width), table.dtype)
    dma = pltpu.SemaphoreType.DMA

    @functools.partial(
        pl.kernel, out_type=jax.ShapeDtypeStruct((m, width), table.dtype), mesh=mesh,
        scratch_types=[index_buf, index_buf, row_buf, row_buf, dma, dma])
    def gather(table_hbm, idx_hbm, out_hbm, idx_a, idx_b, rows_a, rows_b, sem_a, sem_b):
        worker = lax.axis_index("subcore") * mesh.num_cores + lax.axis_index("core")
        base = worker * per_worker
        slots = ((idx_a, rows_a, sem_a), (idx_b, rows_b, sem_b))

        def fetch(step, slot):
            idx, rows, sem = slots[slot]
            off = pl.multiple_of(base + step * window, window)
            pltpu.sync_copy(idx_hbm.at[pl.ds(off, window)], idx)
            pltpu.async_copy(table_hbm.at[idx], rows, sem)

        def flush(step, slot):
            idx, rows, sem = slots[slot]
            off = pl.multiple_of(base + step * window, window)
            pltpu.make_async_copy(table_hbm.at[idx], rows, sem).wait()
            pltpu.sync_copy(rows, out_hbm.at[pl.ds(off, window)])

        fetch(0, 0)

        @pl.loop(0, steps, step=2)
        def _(step):
            fetch(step + 1, 1)
            flush(step, 0)

            @pl.when(step + 2 < steps)
            def _():
                fetch(step + 2, 0)

            flush(step + 1, 1)

    return gather(table, idx)


def _final_kernel(h_ref, gate_ref, gain_ref, y0_ref, y1_ref, o_ref):
    o_ref[...] = _rms(_moe_sum(h_ref[...], gate_ref[...], y0_ref[...], y1_ref[...]), gain_ref[...])


def _final(h, gates, gain, picked):
    n, d_model = h.shape
    tc = MOVE_TILE
    steps = n // tc
    return pl.pallas_call(
        _final_kernel,
        grid=(steps,),
        in_specs=[
            pl.BlockSpec((tc, d_model), lambda i: (i, 0)),
            pl.BlockSpec((tc, 2), lambda i: (i, 0)),
            pl.BlockSpec((1, d_model), lambda i: (0, 0)),
            pl.BlockSpec((tc, d_model // 2), lambda i: (i, 0)),
            pl.BlockSpec((tc, d_model // 2), lambda i: (i + steps, 0)),
        ],
        out_specs=pl.BlockSpec((tc, d_model), lambda i: (i, 0)),
        out_shape=jax.ShapeDtypeStruct((n, d_model), F32),
        compiler_params=_cparams(("arbitrary",)),
        name="moe_final",
    )(h, gates, gain, picked, picked)


def _routing_tables(route, cnt, n_rows):
    bm = MOE_BLOCK
    expert = route[0:2].astype(jnp.int32)
    rank = route[4:6].astype(jnp.int32)
    counts = cnt[N_GROUPS:N_GROUPS + N_EXPERTS, 0].astype(jnp.int32)
    padded = (counts + bm - 1) // bm * bm
    pad_ends = jnp.cumsum(padded)
    pad_starts = pad_ends - padded
    ids = jnp.arange(N_EXPERTS, dtype=jnp.int32)
    start_of = jnp.sum(jnp.where(expert[..., None] == ids, pad_starts, 0), axis=-1)
    dest = (start_of + rank).reshape(-1).astype(jnp.int32)
    block_row = jnp.arange(n_rows // bm, dtype=jnp.int32) * bm
    block_e = jnp.minimum(jnp.sum((pad_ends[None, :] <= block_row[:, None]).astype(jnp.int32), axis=-1),
                          N_EXPERTS - 1)
    row_end = jnp.sum(jnp.where(block_e[:, None] == ids, pad_starts + counts, 0), axis=-1)
    valid = jnp.clip(row_end - block_row, 0, bm).astype(jnp.int32)
    return dest, block_e, valid


def kernel(x, mem, mem_norm, mix_norm, w_in, b_forget, w_alpha_up, b_alpha, fox_out_gain, gla_out_gain, w_out,
           cross_norm, w_xq, w_xk, w_xv, w_xo, moe_norm, w_router_group, b_router_group, w_router_expert,
           b_router_expert, w_expert_gate, w_expert_up, w_expert_down, final_norm):
    batch, seq, d_model = x.shape
    mem_len = mem.shape[1]
    depth = w_in.shape[0]
    n = batch * seq
    assert seq % FOX_TILE == 0 and seq % IN_TILE == 0 and seq % POST_TILE == 0 and seq % GLA_CHUNK == 0
    assert n % MOVE_TILE == 0 and d_model % LANES == 0

    c0 = 3 * FOX_WIDTH
    c1 = c0 + FOX_HEADS
    c2 = c1 + 2 * GLA_QK + 2 * GLA_V
    w_main = jnp.concatenate([w_in[:, :, :c0], w_in[:, :, c1:c2]], axis=-1).astype(BF16)
    pad = LANES - FOX_HEADS - GLA_RANK
    w_small = jnp.concatenate([w_in[:, :, c0:c1], w_in[:, :, c2:], jnp.zeros((depth, d_model, pad), F32)],
                              axis=-1).astype(BF16)
    w_up = jnp.concatenate([jnp.zeros((depth, FOX_HEADS, GLA_QK), F32), w_alpha_up,
                            jnp.zeros((depth, pad, GLA_QK), F32)], axis=1).astype(BF16)
    b_f = jnp.pad(b_forget, ((0, 0), (0, LANES - FOX_HEADS)))[:, None, :]
    b_a = b_alpha[:, None, :]
    w_r = jnp.concatenate([w_router_group, w_router_expert,
                           jnp.zeros((depth, d_model, LANES - N_GROUPS - N_EXPERTS), F32)], axis=-1)
    w_rh = w_r.astype(BF16)
    w_rs = jnp.concatenate([w_rh, (w_r - w_rh.astype(F32)).astype(BF16)], axis=-1)
    b_r = jnp.pad(jnp.concatenate([b_router_group, b_router_expert], axis=-1),
                  ((0, 0), (0, LANES - N_GROUPS - N_EXPERTS)))[:, None, :]
    w_out_b = w_out.astype(BF16)
    w_xq_b = w_xq.astype(BF16)
    w_xo_b = w_xo.astype(BF16)
    mix_g = mix_norm[:, None, :]
    cross_g = cross_norm[:, None, :]
    moe_g = moe_norm[:, None, :]

    kmem, vmem = _mem_kv(mem.reshape(batch * mem_len, d_model), mem_norm[None, :],
                         w_xk.astype(BF16), w_xv.astype(BF16), batch, mem_len)

    n_rows = 2 * n + N_EXPERTS * MOE_BLOCK
    nq = seq // FOX_TILE
    h = x.reshape(n, d_model)
    moe = None
    for l in range(depth):
        if moe is None:
            main, logf, loga = _in_proj(h, mix_g, w_main, w_small, w_up, b_f, b_a, l)
        else:
            h, main, logf, loga = _in_proj(h, mix_g, w_main, w_small, w_up, b_f, b_a, l, moe)
        c = _seq_cumsum(logf, batch, seq)
        c5 = c[:, :FOX_HEADS].reshape(batch, nq, FOX_TILE, FOX_HEADS // 2, 2).transpose(0, 3, 1, 4, 2)
        fox = _fox_attention(main, c5, fox_out_gain[l][None, :], batch, seq)
        gla = _gla(main, loga, gla_out_gain[l][None, :], batch, seq)
        h2, hn2, route, cnt = _post(fox, gla, h, w_out_b, cross_g, w_xq_b, kmem, vmem, w_xo_b, moe_g, w_rs, b_r,
                                    seq, mem_len, l)
        dest, block_e, valid = _routing_tables(route, cnt, n_rows)
        xs = _dispatch(dest, hn2, n_rows)
        y = _experts(block_e, valid, xs, w_expert_gate, w_expert_up, w_expert_down, l)
        h, moe = h2, (route[2:4].T, _sc_gather_rows(y, dest))
    return _final(h, moe[0], final_norm[None, :], moe[1]).reshape(batch, seq, d_model)
```

```python
import functools

import jax
import jax.numpy as jnp
from jax import lax
from jax.experimental import pallas as pl
from jax.experimental.pallas import tpu as pltpu
from jax.experimental.pallas import tpu_sc as plsc

F32 = jnp.float32
BF16 = jnp.bfloat16
EPS = 1e-6
LOG2E = 1.4426950408889634

FOX_HEADS = 8
FOX_DIM = 64
FOX_WIDTH = FOX_HEADS * FOX_DIM
GLA_HEADS = 4
GLA_DK = 64
GLA_DV = 128
GLA_QK = GLA_HEADS * GLA_DK
GLA_V = GLA_HEADS * GLA_DV
GLA_RANK = 16
GLA_TAU = 16.0
GLA_CHUNK = 64
X_HEADS = 4
X_DIM = 128
X_WIDTH = X_HEADS * X_DIM
N_GROUPS = 4
GROUP_SIZE = 4
N_EXPERTS = N_GROUPS * GROUP_SIZE
MAIN_WIDTH = 3 * FOX_WIDTH + 2 * GLA_QK + 2 * GLA_V

LANES = 128
ROUTE_WIDTH = 8
ROUTE_ROWS = 32
VMEM_LIMIT = 56 * 1024 * 1024

IN_TILE = 1024
FOX_TILE = 512
FOX_SLAB = 64
POST_TILE = 1024
CUMSUM_TILE = 256
MOE_BLOCK = 512
MOVE_TILE = 1024
SC_GATHER_WINDOW = 64


def _cparams(sem):
    return pltpu.CompilerParams(dimension_semantics=sem, vmem_limit_bytes=VMEM_LIMIT)


def _rms(x, gain):
    return x * lax.rsqrt(jnp.mean(x * x, axis=-1, keepdims=True) + EPS) * gain


def _log_sigmoid(x):
    return jnp.minimum(x, 0.0) - jnp.log1p(jnp.exp(-jnp.abs(x)))


def _dot(a, b):
    return jnp.dot(a, b, preferred_element_type=F32)


def _dot_nt(a, b):
    return lax.dot_general(a, b, (((1,), (1,)), ((), ())), preferred_element_type=F32)


def _pack_rows(x):
    half = x.shape[1] // 2
    lo = lax.bitcast_convert_type(x[:, :half].astype(BF16).astype(F32), jnp.uint32)
    hi = lax.bitcast_convert_type(x[:, half:].astype(BF16).astype(F32), jnp.uint32)
    return (lo >> 16) | hi


def _unpack_rows(w):
    lo = lax.bitcast_convert_type(w << 16, F32)
    hi = lax.bitcast_convert_type(w & jnp.uint32(0xFFFF0000), F32)
    return lo, hi


def _split3(x):
    hi = x.astype(BF16)
    r1 = x - hi.astype(F32)
    mid = r1.astype(BF16)
    lo = (r1 - mid.astype(F32)).astype(BF16)
    return hi, mid, lo


def _mem_kv_kernel(mem_ref, gain_ref, wk_ref, wv_ref, k_ref, v_ref):
    mn = _rms(mem_ref[...], gain_ref[...]).astype(BF16)
    for l in range(wk_ref.shape[0]):
        k_ref[l] = _dot(mn, wk_ref[l]).astype(BF16)
        v_ref[l] = _dot(mn, wv_ref[l]).astype(BF16)


def _mem_kv(mem2d, gain, wk, wv, batch, mem_len):
    depth, d_model, width = wk.shape
    out = jax.ShapeDtypeStruct((depth, batch * mem_len, width), BF16)
    return pl.pallas_call(
        _mem_kv_kernel,
        grid=(batch,),
        in_specs=[
            pl.BlockSpec((mem_len, d_model), lambda b: (b, 0)),
            pl.BlockSpec((1, d_model), lambda b: (0, 0)),
            pl.BlockSpec((depth, d_model, width), lambda b: (0, 0, 0)),
            pl.BlockSpec((depth, d_model, width), lambda b: (0, 0, 0)),
        ],
        out_specs=[
            pl.BlockSpec((depth, mem_len, width), lambda b: (0, b, 0)),
            pl.BlockSpec((depth, mem_len, width), lambda b: (0, b, 0)),
        ],
        out_shape=[out, out],
        compiler_params=_cparams(("arbitrary",)),
        name="mem_kv",
    )(mem2d, gain, wk, wv)


def _moe_sum(h, gate, y0_packed, y1_packed):
    y0 = jnp.concatenate(_unpack_rows(y0_packed), axis=1)
    y1 = jnp.concatenate(_unpack_rows(y1_packed), axis=1)
    return h + gate[:, 0:1] * y0 + gate[:, 1:2] * y1


def _in_proj_body(h, gain_ref, wmain_ref, wsmall_ref, wup_ref, bf_ref, ba_ref, main_ref, logf_ref, loga_ref):
    xn = _rms(h, gain_ref[...]).astype(BF16)
    step = 512
    for j in range(MAIN_WIDTH // step):
        main_ref[:, j * step:(j + 1) * step] = _dot(xn, wmain_ref[:, j * step:(j + 1) * step]).astype(BF16)
    small = _dot(xn, wsmall_ref[...])
    lane = lax.broadcasted_iota(jnp.int32, small.shape, 1)
    logf_ref[...] = jnp.where(lane < FOX_HEADS, _log_sigmoid(small + bf_ref[...]), 0.0)
    a = _dot(small.astype(BF16), wup_ref[...]) + ba_ref[...]
    loga_ref[...] = _log_sigmoid(a) * (1.0 / GLA_TAU)


def _in_proj_kernel(h_ref, *refs):
    _in_proj_body(h_ref[...], *refs)


def _in_proj_after_moe_kernel(h_ref, gate_ref, y0_ref, y1_ref, gain_ref, wmain_ref, wsmall_ref, wup_ref, bf_ref,
                              ba_ref, hout_ref, main_ref, logf_ref, loga_ref):
    h = _moe_sum(h_ref[...], gate_ref[...], y0_ref[...], y1_ref[...])
    hout_ref[...] = h
    _in_proj_body(h, gain_ref, wmain_ref, wsmall_ref, wup_ref, bf_ref, ba_ref, main_ref, logf_ref, loga_ref)


def _in_proj(h, gain, wmain, wsmall, wup, bf, ba, layer, moe=None):
    n, d_model = h.shape
    tm = IN_TILE
    steps = n // tm
    pick = lambda i: (layer, 0, 0)
    row_block = lambda width: pl.BlockSpec((tm, width), lambda i: (i, 0))
    weight_specs = [
        pl.BlockSpec((None, 1, d_model), pick),
        pl.BlockSpec((None, d_model, MAIN_WIDTH), pick),
        pl.BlockSpec((None, d_model, LANES), pick),
        pl.BlockSpec((None, LANES, GLA_QK), pick),
        pl.BlockSpec((None, 1, LANES), pick),
        pl.BlockSpec((None, 1, GLA_QK), pick),
    ]
    out_specs = [row_block(MAIN_WIDTH), row_block(LANES), row_block(GLA_QK)]
    out_shape = [
        jax.ShapeDtypeStruct((n, MAIN_WIDTH), BF16),
        jax.ShapeDtypeStruct((n, LANES), F32),
        jax.ShapeDtypeStruct((n, GLA_QK), F32),
    ]
    weights = (gain, wmain, wsmall, wup, bf, ba)
    if moe is None:
        return pl.pallas_call(
            _in_proj_kernel, grid=(steps,), in_specs=[row_block(d_model)] + weight_specs,
            out_specs=out_specs, out_shape=out_shape,
            compiler_params=_cparams(("arbitrary",)), name="in_proj",
        )(h, *weights)
    gates, picked = moe
    half = d_model // 2
    return pl.pallas_call(
        _in_proj_after_moe_kernel, grid=(steps,),
        in_specs=[row_block(d_model), row_block(2), row_block(half),
                  pl.BlockSpec((tm, half), lambda i: (i + steps, 0))] + weight_specs,
        out_specs=[row_block(d_model)] + out_specs,
        out_shape=[jax.ShapeDtypeStruct((n, d_model), F32)] + out_shape,
        compiler_params=_cparams(("arbitrary",)), name="in_proj_after_moe",
    )(h, gates, picked, picked, *weights)


def _cumsum_kernel(x_ref, o_ref):
    t = CUMSUM_TILE
    row = lax.broadcasted_iota(jnp.int32, (t, t), 0)
    col = lax.broadcasted_iota(jnp.int32, (t, t), 1)
    tril = (row >= col).astype(BF16)
    carry = jnp.zeros((1, x_ref.shape[1]), F32)
    for j in range(x_ref.shape[0] // t):
        hi, mid, lo = _split3(x_ref[j * t:(j + 1) * t, :])
        c = _dot(tril, hi) + _dot(tril, mid) + _dot(tril, lo) + carry
        o_ref[j * t:(j + 1) * t, :] = c
        carry = c[t - 1:t, :]


def _seq_cumsum(x, batch, seq):
    return pl.pallas_call(
        _cumsum_kernel,
        grid=(batch,),
        in_specs=[pl.BlockSpec((seq, LANES), lambda b: (b, 0))],
        out_specs=pl.BlockSpec((seq, LANES), lambda b: (b, 0)),
        out_shape=jax.ShapeDtypeStruct(x.shape, F32),
        compiler_params=_cparams(("arbitrary",)),
        name="forget_cumsum",
    )(x)


def _fox_kernel(q_ref, k_ref, v_ref, c_ref, gain_ref, o_ref, q2_ref, s_ref, p_ref, alpha_ref, m_ref, l_ref, acc_ref):
    tq = FOX_TILE
    rows = 2 * tq
    slab = FOX_SLAB
    nq = q_ref.shape[0] // tq
    lane = lax.broadcasted_iota(jnp.int32, (1, LANES), 1)
    first = lane < FOX_DIM
    scale = FOX_DIM ** -0.5 * LOG2E
    for qi in range(nq):
        q = q_ref[qi * tq:(qi + 1) * tq, :].astype(F32) * scale
        q2_ref[qi, :tq, :] = jnp.where(first, q, 0.0).astype(BF16)
        q2_ref[qi, tq:, :] = jnp.where(first, 0.0, q).astype(BF16)

    def scores(qi, j):
        cj = c_ref[j] * LOG2E
        d = _dot_nt(q2_ref[qi], k_ref[j * tq:(j + 1) * tq, :])
        s_ref[:tq, :] = d[:tq] - cj[0:1, :]
        s_ref[tq:, :] = d[tq:] - cj[1:2, :]

    def weighted_values(qi, j):
        par = qi % 2
        acc_ref[par] = alpha_ref[par] * acc_ref[par] + _dot(p_ref[...], v_ref[j * tq:(j + 1) * tq, :])

    def softmax(qi, masked):
        par = qi % 2
        for r in range(rows // slab):
            sl = slice(r * slab, (r + 1) * slab)
            s = s_ref[sl, :]
            if masked:
                row = lax.broadcasted_iota(jnp.int32, (slab, tq), 0) + (r * slab) % tq
                col = lax.broadcasted_iota(jnp.int32, (slab, tq), 1)
                s = jnp.where(row >= col, s, -jnp.inf)
            m_old = m_ref[par, sl, :]
            m_new = jnp.maximum(m_old, jnp.max(s, axis=-1, keepdims=True))
            alpha = jnp.exp2(m_old - m_new)
            p = jnp.exp2(s - jnp.concatenate([m_new] * (tq // LANES), axis=1))
            l_ref[par, sl, :] = alpha * l_ref[par, sl, :] + jnp.sum(p, axis=-1, keepdims=True)
            m_ref[par, sl, :] = m_new
            alpha_ref[par, sl, :] = alpha
            p_ref[sl, :] = p.astype(BF16)

    def finalize(qi):
        par = qi % 2
        o2 = acc_ref[par] / l_ref[par]
        o = jnp.where(first, o2[:tq], o2[tq:])
        sq = o * o
        ss0 = jnp.sum(jnp.where(first, sq, 0.0), axis=-1, keepdims=True)
        ss1 = jnp.sum(jnp.where(first, 0.0, sq), axis=-1, keepdims=True)
        ms = jnp.where(first, ss0, ss1) * (1.0 / FOX_DIM)
        o_ref[qi * tq:(qi + 1) * tq, :] = (o * lax.rsqrt(ms + EPS) * gain_ref[...]).astype(BF16)

    steps = [(qi, j) for qi in range(nq) for j in range(qi + 1)]
    scores(*steps[0])
    for t, (qi, j) in enumerate(steps):
        if t > 0:
            weighted_values(*steps[t - 1])
            if steps[t - 1][0] != qi:
                finalize(steps[t - 1][0])
        if j == 0:
            par = qi % 2
            m_ref[par] = jnp.full(m_ref.shape[1:], -jnp.inf, F32)
            l_ref[par] = jnp.zeros(l_ref.shape[1:], F32)
            acc_ref[par] = jnp.zeros(acc_ref.shape[1:], F32)
        softmax(qi, masked=(j == qi))
        if t + 1 < len(steps):
            scores(*steps[t + 1])
    weighted_values(*steps[-1])
    finalize(steps[-1][0])


def _fox_attention(main, c5, gain, batch, seq):
    n = main.shape[0]
    tq = FOX_TILE
    nq = seq // tq
    pairs = FOX_HEADS // 2
    k_off = FOX_WIDTH // LANES
    v_off = 2 * FOX_WIDTH // LANES
    stat = pltpu.VMEM((2, 2 * tq, LANES), F32)
    return pl.pallas_call(
        _fox_kernel,
        grid=(batch, pairs),
        in_specs=[
            pl.BlockSpec((seq, LANES), lambda b, p: (b, p)),
            pl.BlockSpec((seq, LANES), lambda b, p: (b, k_off + p)),
            pl.BlockSpec((seq, LANES), lambda b, p: (b, v_off + p)),
            pl.BlockSpec((None, None, nq, 2, tq), lambda b, p: (b, p, 0, 0, 0)),
            pl.BlockSpec((1, LANES), lambda b, p: (0, p)),
        ],
        out_specs=pl.BlockSpec((seq, LANES), lambda b, p: (b, p)),
        out_shape=jax.ShapeDtypeStruct((n, FOX_WIDTH), BF16),
        scratch_shapes=[
            pltpu.VMEM((nq, 2 * tq, LANES), BF16),
            pltpu.VMEM((2 * tq, tq), F32),
            pltpu.VMEM((2 * tq, tq), BF16),
            stat, stat, stat, stat,
        ],
        compiler_params=_cparams(("arbitrary", "arbitrary")),
        name="fox_attention",
    )(main, main, main, c5, gain)


def _gla_kernel(q_ref, k_ref, v_ref, gg_ref, la_ref, gain_ref, o_ref, qe_ref, ke_ref, kl_ref, dec_ref, raw_ref):
    seq = q_ref.shape[0]
    cs = GLA_CHUNK
    nc = seq // cs
    width = 2 * GLA_DK

    b = la_ref[...]
    pos = lax.broadcasted_iota(jnp.int32, (seq, width), 0) % cs
    shift = 1
    while shift < cs:
        b = b + jnp.where(pos >= shift, pltpu.roll(b, shift, axis=0), 0.0)
        shift *= 2
    b3 = b.reshape(nc, cs, width)
    b_last = b3[:, cs - 1:cs, :]
    q = q_ref[...].astype(F32)
    k = k_ref[...].astype(F32)
    qe_ref[...] = (q * jnp.exp(b) * (GLA_DK ** -0.5)).astype(BF16)
    ke_ref[...] = (k * jnp.exp(-b)).astype(BF16)
    kl_ref[...] = (k.reshape(nc, cs, width) * jnp.exp(b_last - b3)).reshape(seq, width).astype(BF16)
    dec_ref[...] = jnp.exp(b_last).reshape(nc, width)

    lane = lax.broadcasted_iota(jnp.int32, (1, width), 1)
    first = lane < GLA_DK
    row = lax.broadcasted_iota(jnp.int32, (2 * cs, cs), 0)
    col = lax.broadcasted_iota(jnp.int32, (2 * cs, cs), 1)
    tril2 = jnp.where(row >= cs, row - cs, row) >= col
    srow = lax.broadcasted_iota(jnp.int32, (2 * GLA_DV, width), 0)
    scol = lax.broadcasted_iota(jnp.int32, (2 * GLA_DV, width), 1)
    same_head = (srow >= GLA_DV) == (scol >= GLA_DK)
    unroll = 8

    def chunks(ci, st):
        r0s = [pl.multiple_of((ci * unroll + u) * cs, cs) for u in range(unroll)]
        qes = [qe_ref[pl.ds(r0, cs), :] for r0 in r0s]
        vs = [v_ref[pl.ds(r0, cs), :] for r0 in r0s]
        atts, upds = [], []
        for u in range(unroll):
            zero = jnp.zeros_like(qes[u])
            q2 = jnp.concatenate([jnp.where(first, qes[u], zero), jnp.where(first, zero, qes[u])], axis=0)
            atts.append(jnp.where(tril2, _dot_nt(q2, ke_ref[pl.ds(r0s[u], cs), :]), 0.0).astype(BF16))
        for u in range(unroll):
            upds.append(lax.dot_general(vs[u], kl_ref[pl.ds(r0s[u], cs), :], (((0,), (0,)), ((), ())),
                                        preferred_element_type=F32))
        ois = [_dot(atts[u], vs[u]) for u in range(unroll)]
        for u in range(unroll):
            o = _dot_nt(qes[u], st.astype(BF16))
            o = o + jnp.concatenate([ois[u][:cs, :GLA_DV], ois[u][cs:, GLA_DV:]], axis=1)
            raw_ref[pl.ds(r0s[u], cs), :] = o
            st = st * dec_ref[pl.ds(ci * unroll + u, 1), :] + jnp.where(same_head, upds[u], 0.0)
        return st

    lax.fori_loop(0, nc // unroll, chunks, jnp.zeros((2 * GLA_DV, width), F32))

    o = raw_ref[...]
    normed = []
    for h in range(2):
        oh = o[:, h * GLA_DV:(h + 1) * GLA_DV]
        normed.append(oh * lax.rsqrt(jnp.mean(oh * oh, axis=-1, keepdims=True) + EPS))
    g = gg_ref[...].astype(F32)
    o_ref[...] = (jnp.concatenate(normed, axis=1) * gain_ref[...] * (g * jax.nn.sigmoid(g))).astype(BF16)


def _gla(main, loga, gain, batch, seq):
    n = main.shape[0]
    pairs = GLA_HEADS // 2
    q_off = 3 * FOX_WIDTH // LANES
    k_off = q_off + GLA_QK // LANES
    pv = 2 * GLA_DV
    v_off = (3 * FOX_WIDTH + 2 * GLA_QK) // pv
    g_off = v_off + GLA_V // pv
    return pl.pallas_call(
        _gla_kernel,
        grid=(batch, pairs),
        in_specs=[
            pl.BlockSpec((seq, LANES), lambda b, p: (b, q_off + p)),
            pl.BlockSpec((seq, LANES), lambda b, p: (b, k_off + p)),
            pl.BlockSpec((seq, pv), lambda b, p: (b, v_off + p)),
            pl.BlockSpec((seq, pv), lambda b, p: (b, g_off + p)),
            pl.BlockSpec((seq, LANES), lambda b, p: (b, p)),
            pl.BlockSpec((1, pv), lambda b, p: (0, p)),
        ],
        out_specs=pl.BlockSpec((seq, pv), lambda b, p: (b, p)),
        out_shape=jax.ShapeDtypeStruct((n, GLA_V), BF16),
        scratch_shapes=[
            pltpu.VMEM((seq, LANES), BF16),
            pltpu.VMEM((seq, LANES), BF16),
            pltpu.VMEM((seq, LANES), BF16),
            pltpu.VMEM((seq // GLA_CHUNK, LANES), F32),
            pltpu.VMEM((seq, pv), F32),
        ],
        compiler_params=_cparams(("arbitrary", "arbitrary")),
        name="gla",
    )(main, main, main, main, loga, gain)


def _post_kernel(fox_ref, gla_ref, h_ref, wout_ref, cg_ref, wxq_ref, k_ref, v_ref, wxo_ref, mg_ref,
                 wr_ref, br_ref, h2_ref, hn_ref, route_ref, cnt_ref, carry_ref):
    tm = h_ref.shape[0]

    @pl.when(pl.program_id(0) == 0)
    def _():
        carry_ref[...] = jnp.zeros_like(carry_ref)

    y = _dot(fox_ref[...], wout_ref[0:FOX_WIDTH, :]) + _dot(gla_ref[...], wout_ref[FOX_WIDTH:, :])
    h1 = h_ref[...] + y
    hn = _rms(h1, cg_ref[...]).astype(BF16)
    q = _dot(hn, wxq_ref[...]).astype(BF16)
    xscale = X_DIM ** -0.5
    heads = []
    for hh in range(X_HEADS):
        sl = slice(hh * X_DIM, (hh + 1) * X_DIM)
        s = _dot_nt(q[:, sl], k_ref[:, sl]) * xscale
        p = jnp.exp(s - jnp.max(s, axis=-1, keepdims=True))
        heads.append(_dot(p.astype(BF16), v_ref[:, sl]) / jnp.sum(p, axis=-1, keepdims=True))
    o = jnp.concatenate(heads, axis=1).astype(BF16)
    h2 = h1 + _dot(o, wxo_ref[...])
    h2_ref[...] = h2
    hn2 = _rms(h2, mg_ref[...])
    hn_ref[...] = _pack_rows(hn2)

    xh = hn2.astype(BF16)
    xl = (hn2 - xh.astype(F32)).astype(BF16)
    both_w = _dot(jnp.concatenate([xh, xl], axis=0), wr_ref[...])
    logits = both_w[:tm, :LANES] + both_w[:tm, LANES:] + both_w[tm:, :LANES] + both_w[tm:, LANES:] + br_ref[...]
    lt = jnp.transpose(logits)[:ROUTE_ROWS, :]
    row = lax.broadcasted_iota(jnp.int32, (ROUTE_ROWS, tm), 0)
    neg = -jnp.inf
    gl = jnp.where(row < N_GROUPS, lt, neg)
    gmax = jnp.max(gl, axis=0, keepdims=True)
    ge = jnp.exp(gl - gmax)
    gprob = ge / jnp.sum(ge, axis=0, keepdims=True)
    pmax = jnp.max(gprob, axis=0, keepdims=True)
    grp = jnp.min(jnp.where(gprob == pmax, row, ROUTE_ROWS), axis=0, keepdims=True)
    in_grp = (row >= N_GROUPS) & (row < N_GROUPS + N_EXPERTS) & (((row - N_GROUPS) // GROUP_SIZE) == grp)
    el = jnp.where(in_grp, lt, neg)
    emax = jnp.max(el, axis=0, keepdims=True)
    ee = jnp.exp(el - emax)
    eprob = ee / jnp.sum(ee, axis=0, keepdims=True)
    p1 = jnp.max(eprob, axis=0, keepdims=True)
    row1 = jnp.min(jnp.where(in_grp & (eprob == p1), row, ROUTE_ROWS), axis=0, keepdims=True)
    rest = jnp.where(in_grp & (row != row1), eprob, -1.0)
    p2 = jnp.max(rest, axis=0, keepdims=True)
    row2 = jnp.min(jnp.where(rest == p2, row, ROUTE_ROWS), axis=0, keepdims=True)
    g1 = pmax * p1 / (p1 + p2)
    g2 = pmax * p2 / (p1 + p2)

    oh1 = row == row1
    oh2 = row == row2
    both = (oh1 | oh2).astype(BF16)
    srow = lax.broadcasted_iota(jnp.int32, (tm, tm), 0)
    scol = lax.broadcasted_iota(jnp.int32, (tm, tm), 1)
    earlier = (srow < scol).astype(BF16)
    carry = carry_ref[...]
    seen = _dot(both, earlier) + jnp.concatenate([carry] * (tm // LANES), axis=1)
    rank1 = jnp.sum(jnp.where(oh1, seen, 0.0), axis=0, keepdims=True)
    rank2 = jnp.sum(jnp.where(oh2, seen, 0.0), axis=0, keepdims=True)
    carry = carry + jnp.sum(both.astype(F32), axis=1, keepdims=True)
    carry_ref[...] = carry
    cnt_ref[...] = carry

    e1 = (row1 - N_GROUPS).astype(F32)
    e2 = (row2 - N_GROUPS).astype(F32)
    zero = jnp.zeros_like(g1)
    route_ref[...] = jnp.concatenate([e1, e2, g1, g2, rank1, rank2, zero, zero], axis=0)


def _post(fox, gla, h, wout, cg, wxq, kmem, vmem, wxo, mg, wr, br, seq, mem_len, layer):
    n, d_model = h.shape
    tm = POST_TILE
    per_seq = seq // tm
    const = lambda i: (0, 0)
    pick = lambda i: (layer, 0, 0)
    return pl.pallas_call(
        _post_kernel,
        grid=(n // tm,),
        in_specs=[
            pl.BlockSpec((tm, FOX_WIDTH), lambda i: (i, 0)),
            pl.BlockSpec((tm, GLA_V), lambda i: (i, 0)),
            pl.BlockSpec((tm, d_model), lambda i: (i, 0)),
            pl.BlockSpec((None, FOX_WIDTH + GLA_V, d_model), pick),
            pl.BlockSpec((None, 1, d_model), pick),
            pl.BlockSpec((None, d_model, X_WIDTH), pick),
            pl.BlockSpec((None, mem_len, X_WIDTH), lambda i: (layer, i // per_seq, 0)),
            pl.BlockSpec((None, mem_len, X_WIDTH), lambda i: (layer, i // per_seq, 0)),
            pl.BlockSpec((None, X_WIDTH, d_model), pick),
            pl.BlockSpec((None, 1, d_model), pick),
            pl.BlockSpec((None, d_model, 2 * LANES), pick),
            pl.BlockSpec((None, 1, LANES), pick),
        ],
        out_specs=[
            pl.BlockSpec((tm, d_model), lambda i: (i, 0)),
            pl.BlockSpec((tm, d_model // 2), lambda i: (i, 0)),
            pl.BlockSpec((ROUTE_WIDTH, tm), lambda i: (0, i)),
            pl.BlockSpec((ROUTE_ROWS, LANES), const),
        ],
        out_shape=[
            jax.ShapeDtypeStruct((n, d_model), F32),
            jax.ShapeDtypeStruct((n, d_model // 2), jnp.uint32),
            jax.ShapeDtypeStruct((ROUTE_WIDTH, n), F32),
            jax.ShapeDtypeStruct((ROUTE_ROWS, LANES), F32),
        ],
        scratch_shapes=[pltpu.VMEM((ROUTE_ROWS, LANES), F32)],
        compiler_params=_cparams(("arbitrary",)),
        name="post_mixer",
    )(fox, gla, h, wout, cg, wxq, kmem, vmem, wxo, mg, wr, br)


def _dispatch(dest_kmajor, x, n_rows):
    n, width = x.shape
    window = SC_GATHER_WINDOW
    mesh = plsc.VectorSubcoreMesh(core_axis_name="core", subcore_axis_name="subcore")
    workers = mesh.num_cores * mesh.num_subcores
    per_worker = n // workers
    assert n % (workers * window) == 0

    steps = per_worker // window
    assert steps % 2 == 0
    index_buf = pltpu.VMEM((window,), jnp.int32)
    row_buf = pltpu.VMEM((window, width), x.dtype)
    dma = pltpu.SemaphoreType.DMA

    @functools.partial(
        pl.kernel, out_type=jax.ShapeDtypeStruct((n_rows, width), x.dtype), mesh=mesh,
        scratch_types=[index_buf, index_buf, index_buf, index_buf, row_buf, row_buf, dma, dma, dma])
    def scatter(x_hbm, idx_hbm, out_hbm, idx0_a, idx1_a, idx0_b, idx1_b, rows_a, rows_b, sem_a, sem_b, sem_out):
        worker = lax.axis_index("subcore") * mesh.num_cores + lax.axis_index("core")
        base = worker * per_worker
        slots = ((idx0_a, idx1_a, rows_a, sem_a), (idx0_b, idx1_b, rows_b, sem_b))

        def load(step, slot):
            idx0, idx1, rows, sem = slots[slot]
            off = pl.multiple_of(base + step * window, window)
            pltpu.sync_copy(idx_hbm.at[pl.ds(off, window)], idx0)
            pltpu.sync_copy(idx_hbm.at[pl.ds(n + off, window)], idx1)
            pltpu.async_copy(x_hbm.at[pl.ds(off, window)], rows, sem)

        def store(slot):
            idx0, idx1, rows, sem = slots[slot]
            pltpu.make_async_copy(x_hbm.at[pl.ds(0, window)], rows, sem).wait()
            first = pltpu.async_copy(rows, out_hbm.at[idx0], sem_out)
            second = pltpu.async_copy(rows, out_hbm.at[idx1], sem_out)
            first.wait()
            second.wait()

        load(0, 0)

        @pl.loop(0, steps, step=2)
        def _(step):
            load(step + 1, 1)
            store(0)

            @pl.when(step + 2 < steps)
            def _():
                load(step + 2, 0)

            store(1)

    return scatter(x, dest_kmajor)


def _expert_kernel(be_ref, valid_ref, fresh_ref, x_ref, wg_ref, wu_ref, wd_ref, y_ref, wg_b, wu_b, wd_b):
    del be_ref
    i = pl.program_id(0)
    valid = valid_ref[i]

    @pl.when(fresh_ref[i] > 0)
    def _():
        wg_b[...] = wg_ref[...].astype(BF16)
        wu_b[...] = wu_ref[...].astype(BF16)
        wd_b[...] = wd_ref[...].astype(BF16)

    @pl.when(valid > 0)
    def _():
        row = lax.broadcasted_iota(jnp.int32, x_ref.shape, 0)
        lo, hi = _unpack_rows(jnp.where(row < valid, x_ref[...], jnp.uint32(0)))
        lo = lo.astype(BF16)
        hi = hi.astype(BF16)
        half = lo.shape[1]
        g = _dot(lo, wg_b[:half, :]) + _dot(hi, wg_b[half:, :])
        u = _dot(lo, wu_b[:half, :]) + _dot(hi, wu_b[half:, :])
        a = (g * jax.nn.sigmoid(g) * u).astype(BF16)
        y_ref[...] = _pack_rows(_dot(a, wd_b[...]))

    @pl.when(valid <= 0)
    def _():
        y_ref[...] = jnp.zeros_like(y_ref)


def _experts(block_e, valid, xs, wg, wu, wd, layer):
    r, width = xs.shape
    bm = MOE_BLOCK
    d_model, d_exp = wg.shape[-2:]
    fresh = jnp.concatenate([jnp.ones((1,), jnp.int32), (block_e[1:] != block_e[:-1]).astype(jnp.int32)])
    pick = lambda i, be, va, fr: (layer, be[i], 0, 0)
    return pl.pallas_call(
        _expert_kernel,
        grid_spec=pltpu.PrefetchScalarGridSpec(
            num_scalar_prefetch=3,
            grid=(r // bm,),
            in_specs=[
                pl.BlockSpec((bm, width), lambda i, be, va, fr: (i, 0)),
                pl.BlockSpec((None, None, d_model, d_exp), pick),
                pl.BlockSpec((None, None, d_model, d_exp), pick),
                pl.BlockSpec((None, None, d_exp, d_model), pick),
            ],
            out_specs=pl.BlockSpec((bm, width), lambda i, be, va, fr: (i, 0)),
            scratch_shapes=[
                pltpu.VMEM((d_model, d_exp), BF16),
                pltpu.VMEM((d_model, d_exp), BF16),
                pltpu.VMEM((d_exp, d_model), BF16),
            ],
        ),
        out_shape=jax.ShapeDtypeStruct((r, width), jnp.uint32),
        compiler_params=_cparams(("arbitrary",)),
        name="moe_experts",
    )(block_e, valid, fresh, xs, wg, wu, wd)


def _sc_gather_rows(table, idx):
    m = idx.shape[0]
    width = table.shape[1]
    window = SC_GATHER_WINDOW
    mesh = plsc.VectorSubcoreMesh(core_axis_name="core", subcore_axis_name="subcore")
    workers = mesh.num_cores * mesh.num_subcores
    per_worker = m // workers
    assert m % (workers * window) == 0

    steps = per_worker // window
    assert steps % 2 == 0
    index_buf = pltpu.VMEM((window,), jnp.int32)
    row_buf = pltpu.VMEM((window, width), table.dtype)
    dma = pltpu.SemaphoreType.DMA

    @functools.partial(
        pl.kernel, out_type=jax.ShapeDtypeStruct((m, width), table.dtype), mesh=mesh,
        scratch_types=[index_buf, index_buf, row_buf, row_buf, dma, dma])
    def gather(table_hbm, idx_hbm, out_hbm, idx_a, idx_b, rows_a, rows_b, sem_a, sem_b):
        worker = lax.axis_index("subcore") * mesh.num_cores + lax.axis_index("core")
        base = worker * per_worker
        slots = ((idx_a, rows_a, sem_a), (idx_b, rows_b, sem_b))

        def fetch(step, slot):
            idx, rows, sem = slots[slot]
            off = pl.multiple_of(base + step * window, window)
            pltpu.sync_copy(idx_hbm.at[pl.ds(off, window)], idx)
            pltpu.async_copy(table_hbm.at[idx], rows, sem)

        def flush(step, slot):
            idx, rows, sem = slots[slot]
            off = pl.multiple_of(base + step * window, window)
            pltpu.make_async_copy(table_hbm.at[idx], rows, sem).wait()
            pltpu.sync_copy(rows, out_hbm.at[pl.ds(off, window)])

        fetch(0, 0)

        @pl.loop(0, steps, step=2)
        def _(step):
            fetch(step + 1, 1)
            flush(step, 0)

            @pl.when(step + 2 < steps)
            def _():
                fetch(step + 2, 0)

            flush(step + 1, 1)

    return gather(table, idx)


def _final_kernel(h_ref, gate_ref, gain_ref, y0_ref, y1_ref, o_ref):
    o_ref[...] = _rms(_moe_sum(h_ref[...], gate_ref[...], y0_ref[...], y1_ref[...]), gain_ref[...])


def _final(h, gates, gain, picked):
    n, d_model = h.shape
    tc = MOVE_TILE
    steps = n // tc
    return pl.pallas_call(
        _final_kernel,
        grid=(steps,),
        in_specs=[
            pl.BlockSpec((tc, d_model), lambda i: (i, 0)),
            pl.BlockSpec((tc, 2), lambda i: (i, 0)),
            pl.BlockSpec((1, d_model), lambda i: (0, 0)),
            pl.BlockSpec((tc, d_model // 2), lambda i: (i, 0)),
            pl.BlockSpec((tc, d_model // 2), lambda i: (i + steps, 0)),
        ],
        out_specs=pl.BlockSpec((tc, d_model), lambda i: (i, 0)),
        out_shape=jax.ShapeDtypeStruct((n, d_model), F32),
        compiler_params=_cparams(("arbitrary",)),
        name="moe_final",
    )(h, gates, gain, picked, picked)


def _routing_tables(route, cnt, n_rows):
    bm = MOE_BLOCK
    expert = route[0:2].astype(jnp.int32)
    rank = route[4:6].astype(jnp.int32)
    counts = cnt[N_GROUPS:N_GROUPS + N_EXPERTS, 0].astype(jnp.int32)
    padded = (counts + bm - 1) // bm * bm
    pad_ends = jnp.cumsum(padded)
    pad_starts = pad_ends - padded
    ids = jnp.arange(N_EXPERTS, dtype=jnp.int32)
    start_of = jnp.sum(jnp.where(expert[..., None] == ids, pad_starts, 0), axis=-1)
    dest = (start_of + rank).reshape(-1).astype(jnp.int32)
    block_row = jnp.arange(n_rows // bm, dtype=jnp.int32) * bm
    block_e = jnp.minimum(jnp.sum((pad_ends[None, :] <= block_row[:, None]).astype(jnp.int32), axis=-1),
                          N_EXPERTS - 1)
    row_end = jnp.sum(jnp.where(block_e[:, None] == ids, pad_starts + counts, 0), axis=-1)
    valid = jnp.clip(row_end - block_row, 0, bm).astype(jnp.int32)
    return dest, block_e, valid


def kernel(x, mem, mem_norm, mix_norm, w_in, b_forget, w_alpha_up, b_alpha, fox_out_gain, gla_out_gain, w_out,
           cross_norm, w_xq, w_xk, w_xv, w_xo, moe_norm, w_router_group, b_router_group, w_router_expert,
           b_router_expert, w_expert_gate, w_expert_up, w_expert_down, final_norm):
    batch, seq, d_model = x.shape
    mem_len = mem.shape[1]
    depth = w_in.shape[0]
    n = batch * seq
    assert seq % FOX_TILE == 0 and seq % IN_TILE == 0 and seq % POST_TILE == 0 and seq % GLA_CHUNK == 0
    assert n % MOVE_TILE == 0 and d_model % LANES == 0

    c0 = 3 * FOX_WIDTH
    c1 = c0 + FOX_HEADS
    c2 = c1 + 2 * GLA_QK + 2 * GLA_V
    w_main = jnp.concatenate([w_in[:, :, :c0], w_in[:, :, c1:c2]], axis=-1).astype(BF16)
    pad = LANES - FOX_HEADS - GLA_RANK
    w_small = jnp.concatenate([w_in[:, :, c0:c1], w_in[:, :, c2:], jnp.zeros((depth, d_model, pad), F32)],
                              axis=-1).astype(BF16)
    w_up = jnp.concatenate([jnp.zeros((depth, FOX_HEADS, GLA_QK), F32), w_alpha_up,
                            jnp.zeros((depth, pad, GLA_QK), F32)], axis=1).astype(BF16)
    b_f = jnp.pad(b_forget, ((0, 0), (0, LANES - FOX_HEADS)))[:, None, :]
    b_a = b_alpha[:, None, :]
    w_r = jnp.concatenate([w_router_group, w_router_expert,
                           jnp.zeros((depth, d_model, LANES - N_GROUPS - N_EXPERTS), F32)], axis=-1)
    w_rh = w_r.astype(BF16)
    w_rs = jnp.concatenate([w_rh, (w_r - w_rh.astype(F32)).astype(BF16)], axis=-1)
    b_r = jnp.pad(jnp.concatenate([b_router_group, b_router_expert], axis=-1),
                  ((0, 0), (0, LANES - N_GROUPS - N_EXPERTS)))[:, None, :]
    w_out_b = w_out.astype(BF16)
    w_xq_b = w_xq.astype(BF16)
    w_xo_b = w_xo.astype(BF16)
    mix_g = mix_norm[:, None, :]
    cross_g = cross_norm[:, None, :]
    moe_g = moe_norm[:, None, :]

    kmem, vmem = _mem_kv(mem.reshape(batch * mem_len, d_model), mem_norm[None, :],
                         w_xk.astype(BF16), w_xv.astype(BF16), batch, mem_len)

    n_rows = 2 * n + N_EXPERTS * MOE_BLOCK
    nq = seq // FOX_TILE
    h = x.reshape(n, d_model)
    moe = None
    for l in range(depth):
        if moe is None:
            main, logf, loga = _in_proj(h, mix_g, w_main, w_small, w_up, b_f, b_a, l)
        else:
            h, main, logf, loga = _in_proj(h, mix_g, w_main, w_small, w_up, b_f, b_a, l, moe)
        c = _seq_cumsum(logf, batch, seq)
        c5 = c[:, :FOX_HEADS].reshape(batch, nq, FOX_TILE, FOX_HEADS // 2, 2).transpose(0, 3, 1, 4, 2)
        fox = _fox_attention(main, c5, fox_out_gain[l][None, :], batch, seq)
        gla = _gla(main, loga, gla_out_gain[l][None, :], batch, seq)
        h2, hn2, route, cnt = _post(fox, gla, h, w_out_b, cross_g, w_xq_b, kmem, vmem, w_xo_b, moe_g, w_rs, b_r,
                                    seq, mem_len, l)
        dest, block_e, valid = _routing_tables(route, cnt, n_rows)
        xs = _dispatch(dest, hn2, n_rows)
        y = _experts(block_e, valid, xs, w_expert_gate, w_expert_up, w_expert_down, l)
        h, moe = h2, (route[2:4].T, _sc_gather_rows(y, dest))
    return _final(h, moe[0], final_norm[None, :], moe[1]).reshape(batch, seq, d_model)
```

```python
import functools

import jax
import jax.numpy as jnp
from jax import lax
from jax.experimental import pallas as pl
from jax.experimental.pallas import tpu as pltpu
from jax.experimental.pallas import tpu_sc as plsc

F32 = jnp.float32
BF16 = jnp.bfloat16
EPS = 1e-6
LOG2E = 1.4426950408889634

FOX_HEADS = 8
FOX_DIM = 64
FOX_WIDTH = FOX_HEADS * FOX_DIM
GLA_HEADS = 4
GLA_DK = 64
GLA_DV = 128
GLA_QK = GLA_HEADS * GLA_DK
GLA_V = GLA_HEADS * GLA_DV
GLA_RANK = 16
GLA_TAU = 16.0
GLA_CHUNK = 64
X_HEADS = 4
X_DIM = 128
X_WIDTH = X_HEADS * X_DIM
N_GROUPS = 4
GROUP_SIZE = 4
N_EXPERTS = N_GROUPS * GROUP_SIZE
MAIN_WIDTH = 3 * FOX_WIDTH + 2 * GLA_QK + 2 * GLA_V

LANES = 128
ROUTE_WIDTH = 8
ROUTE_ROWS = 32
VMEM_LIMIT = 56 * 1024 * 1024

IN_TILE = 1024
FOX_TILE = 512
FOX_SLAB = 64
POST_TILE = 1024
CUMSUM_TILE = 256
MOE_BLOCK = 512
MOVE_TILE = 1024
SC_GATHER_WINDOW = 64
MOE_GATHER_PARTS = 2


def _cparams(sem):
    return pltpu.CompilerParams(dimension_semantics=sem, vmem_limit_bytes=VMEM_LIMIT)


def _rms(x, gain):
    return x * lax.rsqrt(jnp.mean(x * x, axis=-1, keepdims=True) + EPS) * gain


def _log_sigmoid(x):
    return jnp.minimum(x, 0.0) - jnp.log1p(jnp.exp(-jnp.abs(x)))


def _dot(a, b):
    return jnp.dot(a, b, preferred_element_type=F32)


def _dot_nt(a, b):
    return lax.dot_general(a, b, (((1,), (1,)), ((), ())), preferred_element_type=F32)


def _pack_rows(x):
    half = x.shape[1] // 2
    lo = lax.bitcast_convert_type(x[:, :half].astype(BF16).astype(F32), jnp.uint32)
    hi = lax.bitcast_convert_type(x[:, half:].astype(BF16).astype(F32), jnp.uint32)
    return (lo >> 16) | hi


def _unpack_rows(w):
    lo = lax.bitcast_convert_type(w << 16, F32)
    hi = lax.bitcast_convert_type(w & jnp.uint32(0xFFFF0000), F32)
    return lo, hi


def _split3(x):
    hi = x.astype(BF16)
    r1 = x - hi.astype(F32)
    mid = r1.astype(BF16)
    lo = (r1 - mid.astype(F32)).astype(BF16)
    return hi, mid, lo


def _mem_kv_kernel(mem_ref, gain_ref, wk_ref, wv_ref, k_ref, v_ref):
    mn = _rms(mem_ref[...], gain_ref[...]).astype(BF16)
    for l in range(wk_ref.shape[0]):
        k_ref[l] = _dot(mn, wk_ref[l]).astype(BF16)
        v_ref[l] = _dot(mn, wv_ref[l]).astype(BF16)


def _mem_kv(mem2d, gain, wk, wv, batch, mem_len):
    depth, d_model, width = wk.shape
    out = jax.ShapeDtypeStruct((depth, batch * mem_len, width), BF16)
    return pl.pallas_call(
        _mem_kv_kernel,
        grid=(batch,),
        in_specs=[
            pl.BlockSpec((mem_len, d_model), lambda b: (b, 0)),
            pl.BlockSpec((1, d_model), lambda b: (0, 0)),
            pl.BlockSpec((depth, d_model, width), lambda b: (0, 0, 0)),
            pl.BlockSpec((depth, d_model, width), lambda b: (0, 0, 0)),
        ],
        out_specs=[
            pl.BlockSpec((depth, mem_len, width), lambda b: (0, b, 0)),
            pl.BlockSpec((depth, mem_len, width), lambda b: (0, b, 0)),
        ],
        out_shape=[out, out],
        compiler_params=_cparams(("arbitrary",)),
        name="mem_kv",
    )(mem2d, gain, wk, wv)


def _moe_sum(h, gate, y0_packed, y1_packed):
    y0 = jnp.concatenate(_unpack_rows(y0_packed), axis=1)
    y1 = jnp.concatenate(_unpack_rows(y1_packed), axis=1)
    return h + gate[:, 0:1] * y0 + gate[:, 1:2] * y1


def _in_proj_body(h, gain_ref, wmain_ref, wsmall_ref, wup_ref, bf_ref, ba_ref, main_ref, logf_ref, loga_ref):
    xn = _rms(h, gain_ref[...]).astype(BF16)
    step = 512
    for j in range(MAIN_WIDTH // step):
        main_ref[:, j * step:(j + 1) * step] = _dot(xn, wmain_ref[:, j * step:(j + 1) * step]).astype(BF16)
    small = _dot(xn, wsmall_ref[...])
    lane = lax.broadcasted_iota(jnp.int32, small.shape, 1)
    logf_ref[...] = jnp.where(lane < FOX_HEADS, _log_sigmoid(small + bf_ref[...]), 0.0)
    a = _dot(small.astype(BF16), wup_ref[...]) + ba_ref[...]
    loga_ref[...] = _log_sigmoid(a) * (1.0 / GLA_TAU)


def _in_proj_kernel(h_ref, *refs):
    _in_proj_body(h_ref[...], *refs)


def _in_proj_after_moe_kernel(h_ref, gate_ref, y0_ref, y1_ref, gain_ref, wmain_ref, wsmall_ref, wup_ref, bf_ref,
                              ba_ref, *rest):
    hout_ref, main_ref, logf_ref, loga_ref = rest[-4:]
    h = _moe_sum(h_ref[...], gate_ref[...], y0_ref[...], y1_ref[...])
    hout_ref[...] = h
    _in_proj_body(h, gain_ref, wmain_ref, wsmall_ref, wup_ref, bf_ref, ba_ref, main_ref, logf_ref, loga_ref)


def _in_proj(h, gain, wmain, wsmall, wup, bf, ba, layer, moe=None):
    n, d_model = h.shape
    tm = IN_TILE
    steps = n // tm
    pick = lambda i: (layer, 0, 0)
    row_block = lambda width, off=0: pl.BlockSpec((tm, width), lambda i: (i + off, 0))
    weight_specs = [
        pl.BlockSpec((None, 1, d_model), pick),
        pl.BlockSpec((None, d_model, MAIN_WIDTH), pick),
        pl.BlockSpec((None, d_model, LANES), pick),
        pl.BlockSpec((None, LANES, GLA_QK), pick),
        pl.BlockSpec((None, 1, LANES), pick),
        pl.BlockSpec((None, 1, GLA_QK), pick),
    ]
    out_specs = [row_block(MAIN_WIDTH), row_block(LANES), row_block(GLA_QK)]
    out_shape = [
        jax.ShapeDtypeStruct((n, MAIN_WIDTH), BF16),
        jax.ShapeDtypeStruct((n, LANES), F32),
        jax.ShapeDtypeStruct((n, GLA_QK), F32),
    ]
    weights = (gain, wmain, wsmall, wup, bf, ba)
    if moe is None:
        return pl.pallas_call(
            _in_proj_kernel, grid=(steps,), in_specs=[row_block(d_model)] + weight_specs,
            out_specs=out_specs, out_shape=out_shape,
            compiler_params=_cparams(("arbitrary",)), name="in_proj",
        )(h, *weights)
    gates, parts = moe
    half = d_model // 2
    part_steps = steps // len(parts)
    out_shape = [jax.ShapeDtypeStruct((n, d_model), F32)] + out_shape
    outs = ()
    for p, picked in enumerate(parts):
        off = p * part_steps
        shared = [pl.BlockSpec(memory_space=pl.ANY)] * len(outs)
        n_in = 4 + len(weights)
        outs = pl.pallas_call(
            _in_proj_after_moe_kernel, grid=(part_steps,),
            in_specs=[row_block(d_model, off), row_block(2, off), row_block(half),
                      row_block(half, part_steps)] + weight_specs + shared,
            out_specs=[row_block(d_model, off), row_block(MAIN_WIDTH, off), row_block(LANES, off),
                       row_block(GLA_QK, off)],
            out_shape=out_shape,
            input_output_aliases={n_in + j: j for j in range(len(outs))},
            compiler_params=_cparams(("arbitrary",)), name="in_proj_after_moe",
        )(h, gates, picked, picked, *weights, *outs)
    return outs


def _cumsum_kernel(x_ref, o_ref):
    t = CUMSUM_TILE
    row = lax.broadcasted_iota(jnp.int32, (t, t), 0)
    col = lax.broadcasted_iota(jnp.int32, (t, t), 1)
    tril = (row >= col).astype(BF16)
    carry = jnp.zeros((1, x_ref.shape[1]), F32)
    for j in range(x_ref.shape[0] // t):
        hi, mid, lo = _split3(x_ref[j * t:(j + 1) * t, :])
        c = _dot(tril, hi) + _dot(tril, mid) + _dot(tril, lo) + carry
        o_ref[j * t:(j + 1) * t, :] = c
        carry = c[t - 1:t, :]


def _seq_cumsum(x, batch, seq):
    return pl.pallas_call(
        _cumsum_kernel,
        grid=(batch,),
        in_specs=[pl.BlockSpec((seq, LANES), lambda b: (b, 0))],
        out_specs=pl.BlockSpec((seq, LANES), lambda b: (b, 0)),
        out_shape=jax.ShapeDtypeStruct(x.shape, F32),
        compiler_params=_cparams(("arbitrary",)),
        name="forget_cumsum",
    )(x)


def _fox_kernel(q_ref, k_ref, v_ref, c_ref, gain_ref, o_ref, q2_ref, s_ref, p_ref, alpha_ref, m_ref, l_ref, acc_ref):
    tq = FOX_TILE
    rows = 2 * tq
    slab = FOX_SLAB
    nq = q_ref.shape[0] // tq
    lane = lax.broadcasted_iota(jnp.int32, (1, LANES), 1)
    first = lane < FOX_DIM
    scale = FOX_DIM ** -0.5 * LOG2E
    for qi in range(nq):
        q = q_ref[qi * tq:(qi + 1) * tq, :].astype(F32) * scale
        q2_ref[qi, :tq, :] = jnp.where(first, q, 0.0).astype(BF16)
        q2_ref[qi, tq:, :] = jnp.where(first, 0.0, q).astype(BF16)

    def scores(qi, j):
        cj = c_ref[j] * LOG2E
        d = _dot_nt(q2_ref[qi], k_ref[j * tq:(j + 1) * tq, :])
        s_ref[:tq, :] = d[:tq] - cj[0:1, :]
        s_ref[tq:, :] = d[tq:] - cj[1:2, :]

    def weighted_values(qi, j):
        par = qi % 2
        acc_ref[par] = alpha_ref[par] * acc_ref[par] + _dot(p_ref[...], v_ref[j * tq:(j + 1) * tq, :])

    def softmax(qi, masked):
        par = qi % 2
        for r in range(rows // slab):
            sl = slice(r * slab, (r + 1) * slab)
            s = s_ref[sl, :]
            if masked:
                row = lax.broadcasted_iota(jnp.int32, (slab, tq), 0) + (r * slab) % tq
                col = lax.broadcasted_iota(jnp.int32, (slab, tq), 1)
                s = jnp.where(row >= col, s, -jnp.inf)
            m_old = m_ref[par, sl, :]
            m_new = jnp.maximum(m_old, jnp.max(s, axis=-1, keepdims=True))
            alpha = jnp.exp2(m_old - m_new)
            p = jnp.exp2(s - jnp.concatenate([m_new] * (tq // LANES), axis=1))
            l_ref[par, sl, :] = alpha * l_ref[par, sl, :] + jnp.sum(p, axis=-1, keepdims=True)
            m_ref[par, sl, :] = m_new
            alpha_ref[par, sl, :] = alpha
            p_ref[sl, :] = p.astype(BF16)

    def finalize(qi):
        par = qi % 2
        o2 = acc_ref[par] / l_ref[par]
        o = jnp.where(first, o2[:tq], o2[tq:])
        sq = o * o
        ss0 = jnp.sum(jnp.where(first, sq, 0.0), axis=-1, keepdims=True)
        ss1 = jnp.sum(jnp.where(first, 0.0, sq), axis=-1, keepdims=True)
        ms = jnp.where(first, ss0, ss1) * (1.0 / FOX_DIM)
        o_ref[qi * tq:(qi + 1) * tq, :] = (o * lax.rsqrt(ms + EPS) * gain_ref[...]).astype(BF16)

    steps = [(qi, j) for qi in range(nq) for j in range(qi + 1)]
    scores(*steps[0])
    for t, (qi, j) in enumerate(steps):
        if t > 0:
            weighted_values(*steps[t - 1])
            if steps[t - 1][0] != qi:
                finalize(steps[t - 1][0])
        if j == 0:
            par = qi % 2
            m_ref[par] = jnp.full(m_ref.shape[1:], -jnp.inf, F32)
            l_ref[par] = jnp.zeros(l_ref.shape[1:], F32)
            acc_ref[par] = jnp.zeros(acc_ref.shape[1:], F32)
        softmax(qi, masked=(j == qi))
        if t + 1 < len(steps):
            scores(*steps[t + 1])
    weighted_values(*steps[-1])
    finalize(steps[-1][0])


def _fox_attention(main, c5, gain, batch, seq):
    n = main.shape[0]
    tq = FOX_TILE
    nq = seq // tq
    pairs = FOX_HEADS // 2
    k_off = FOX_WIDTH // LANES
    v_off = 2 * FOX_WIDTH // LANES
    stat = pltpu.VMEM((2, 2 * tq, LANES), F32)
    return pl.pallas_call(
        _fox_kernel,
        grid=(batch, pairs),
        in_specs=[
            pl.BlockSpec((seq, LANES), lambda b, p: (b, p)),
            pl.BlockSpec((seq, LANES), lambda b, p: (b, k_off + p)),
            pl.BlockSpec((seq, LANES), lambda b, p: (b, v_off + p)),
            pl.BlockSpec((None, None, nq, 2, tq), lambda b, p: (b, p, 0, 0, 0)),
            pl.BlockSpec((1, LANES), lambda b, p: (0, p)),
        ],
        out_specs=pl.BlockSpec((seq, LANES), lambda b, p: (b, p)),
        out_shape=jax.ShapeDtypeStruct((n, FOX_WIDTH), BF16),
        scratch_shapes=[
            pltpu.VMEM((nq, 2 * tq, LANES), BF16),
            pltpu.VMEM((2 * tq, tq), F32),
            pltpu.VMEM((2 * tq, tq), BF16),
            stat, stat, stat, stat,
        ],
        compiler_params=_cparams(("arbitrary", "arbitrary")),
        name="fox_attention",
    )(main, main, main, c5, gain)


def _gla_kernel(q_ref, k_ref, v_ref, gg_ref, la_ref, gain_ref, o_ref, qe_ref, ke_ref, kl_ref, dec_ref, raw_ref):
    seq = q_ref.shape[0]
    cs = GLA_CHUNK
    nc = seq // cs
    width = 2 * GLA_DK

    b = la_ref[...]
    pos = lax.broadcasted_iota(jnp.int32, (seq, width), 0) % cs
    shift = 1
    while shift < cs:
        b = b + jnp.where(pos >= shift, pltpu.roll(b, shift, axis=0), 0.0)
        shift *= 2
    b3 = b.reshape(nc, cs, width)
    b_last = b3[:, cs - 1:cs, :]
    q = q_ref[...].astype(F32)
    k = k_ref[...].astype(F32)
    qe_ref[...] = (q * jnp.exp(b) * (GLA_DK ** -0.5)).astype(BF16)
    ke_ref[...] = (k * jnp.exp(-b)).astype(BF16)
    kl_ref[...] = (k.reshape(nc, cs, width) * jnp.exp(b_last - b3)).reshape(seq, width).astype(BF16)
    dec_ref[...] = jnp.exp(b_last).reshape(nc, width)

    lane = lax.broadcasted_iota(jnp.int32, (1, width), 1)
    first = lane < GLA_DK
    row = lax.broadcasted_iota(jnp.int32, (2 * cs, cs), 0)
    col = lax.broadcasted_iota(jnp.int32, (2 * cs, cs), 1)
    tril2 = jnp.where(row >= cs, row - cs, row) >= col
    srow = lax.broadcasted_iota(jnp.int32, (2 * GLA_DV, width), 0)
    scol = lax.broadcasted_iota(jnp.int32, (2 * GLA_DV, width), 1)
    same_head = (srow >= GLA_DV) == (scol >= GLA_DK)
    unroll = 8

    def chunks(ci, st):
        r0s = [pl.multiple_of((ci * unroll + u) * cs, cs) for u in range(unroll)]
        qes = [qe_ref[pl.ds(r0, cs), :] for r0 in r0s]
        vs = [v_ref[pl.ds(r0, cs), :] for r0 in r0s]
        atts, upds = [], []
        for u in range(unroll):
            zero = jnp.zeros_like(qes[u])
            q2 = jnp.concatenate([jnp.where(first, qes[u], zero), jnp.where(first, zero, qes[u])], axis=0)
            atts.append(jnp.where(tril2, _dot_nt(q2, ke_ref[pl.ds(r0s[u], cs), :]), 0.0).astype(BF16))
        for u in range(unroll):
            upds.append(lax.dot_general(vs[u], kl_ref[pl.ds(r0s[u], cs), :], (((0,), (0,)), ((), ())),
                                        preferred_element_type=F32))
        ois = [_dot(atts[u], vs[u]) for u in range(unroll)]
        for u in range(unroll):
            o = _dot_nt(qes[u], st.astype(BF16))
            o = o + jnp.concatenate([ois[u][:cs, :GLA_DV], ois[u][cs:, GLA_DV:]], axis=1)
            raw_ref[pl.ds(r0s[u], cs), :] = o
            st = st * dec_ref[pl.ds(ci * unroll + u, 1), :] + jnp.where(same_head, upds[u], 0.0)
        return st

    lax.fori_loop(0, nc // unroll, chunks, jnp.zeros((2 * GLA_DV, width), F32))

    o = raw_ref[...]
    normed = []
    for h in range(2):
        oh = o[:, h * GLA_DV:(h + 1) * GLA_DV]
        normed.append(oh * lax.rsqrt(jnp.mean(oh * oh, axis=-1, keepdims=True) + EPS))
    g = gg_ref[...].astype(F32)
    o_ref[...] = (jnp.concatenate(normed, axis=1) * gain_ref[...] * (g * jax.nn.sigmoid(g))).astype(BF16)


def _gla(main, loga, gain, batch, seq):
    n = main.shape[0]
    pairs = GLA_HEADS // 2
    q_off = 3 * FOX_WIDTH // LANES
    k_off = q_off + GLA_QK // LANES
    pv = 2 * GLA_DV
    v_off = (3 * FOX_WIDTH + 2 * GLA_QK) // pv
    g_off = v_off + GLA_V // pv
    return pl.pallas_call(
        _gla_kernel,
        grid=(batch, pairs),
        in_specs=[
            pl.BlockSpec((seq, LANES), lambda b, p: (b, q_off + p)),
            pl.BlockSpec((seq, LANES), lambda b, p: (b, k_off + p)),
            pl.BlockSpec((seq, pv), lambda b, p: (b, v_off + p)),
            pl.BlockSpec((seq, pv), lambda b, p: (b, g_off + p)),
            pl.BlockSpec((seq, LANES), lambda b, p: (b, p)),
            pl.BlockSpec((1, pv), lambda b, p: (0, p)),
        ],
        out_specs=pl.BlockSpec((seq, pv), lambda b, p: (b, p)),
        out_shape=jax.ShapeDtypeStruct((n, GLA_V), BF16),
        scratch_shapes=[
            pltpu.VMEM((seq, LANES), BF16),
            pltpu.VMEM((seq, LANES), BF16),
            pltpu.VMEM((seq, LANES), BF16),
            pltpu.VMEM((seq // GLA_CHUNK, LANES), F32),
            pltpu.VMEM((seq, pv), F32),
        ],
        compiler_params=_cparams(("arbitrary", "arbitrary")),
        name="gla",
    )(main, main, main, main, loga, gain)


def _post_kernel(fox_ref, gla_ref, h_ref, wout_ref, cg_ref, wxq_ref, k_ref, v_ref, wxo_ref, mg_ref,
                 wr_ref, br_ref, h2_ref, hn_ref, route_ref, cnt_ref, carry_ref):
    tm = h_ref.shape[0]

    @pl.when(pl.program_id(0) == 0)
    def _():
        carry_ref[...] = jnp.zeros_like(carry_ref)

    y = _dot(fox_ref[...], wout_ref[0:FOX_WIDTH, :]) + _dot(gla_ref[...], wout_ref[FOX_WIDTH:, :])
    h1 = h_ref[...] + y
    hn = _rms(h1, cg_ref[...]).astype(BF16)
    q = _dot(hn, wxq_ref[...]).astype(BF16)
    xscale = X_DIM ** -0.5
    heads = []
    for hh in range(X_HEADS):
        sl = slice(hh * X_DIM, (hh + 1) * X_DIM)
        s = _dot_nt(q[:, sl], k_ref[:, sl]) * xscale
        p = jnp.exp(s - jnp.max(s, axis=-1, keepdims=True))
        heads.append(_dot(p.astype(BF16), v_ref[:, sl]) / jnp.sum(p, axis=-1, keepdims=True))
    o = jnp.concatenate(heads, axis=1).astype(BF16)
    h2 = h1 + _dot(o, wxo_ref[...])
    h2_ref[...] = h2
    hn2 = _rms(h2, mg_ref[...])
    hn_ref[...] = _pack_rows(hn2)

    xh = hn2.astype(BF16)
    xl = (hn2 - xh.astype(F32)).astype(BF16)
    both_w = _dot(jnp.concatenate([xh, xl], axis=0), wr_ref[...])
    logits = both_w[:tm, :LANES] + both_w[:tm, LANES:] + both_w[tm:, :LANES] + both_w[tm:, LANES:] + br_ref[...]
    lt = jnp.transpose(logits)[:ROUTE_ROWS, :]
    row = lax.broadcasted_iota(jnp.int32, (ROUTE_ROWS, tm), 0)
    neg = -jnp.inf
    gl = jnp.where(row < N_GROUPS, lt, neg)
    gmax = jnp.max(gl, axis=0, keepdims=True)
    ge = jnp.exp(gl - gmax)
    gprob = ge / jnp.sum(ge, axis=0, keepdims=True)
    pmax = jnp.max(gprob, axis=0, keepdims=True)
    grp = jnp.min(jnp.where(gprob == pmax, row, ROUTE_ROWS), axis=0, keepdims=True)
    in_grp = (row >= N_GROUPS) & (row < N_GROUPS + N_EXPERTS) & (((row - N_GROUPS) // GROUP_SIZE) == grp)
    el = jnp.where(in_grp, lt, neg)
    emax = jnp.max(el, axis=0, keepdims=True)
    ee = jnp.exp(el - emax)
    eprob = ee / jnp.sum(ee, axis=0, keepdims=True)
    p1 = jnp.max(eprob, axis=0, keepdims=True)
    row1 = jnp.min(jnp.where(in_grp & (eprob == p1), row, ROUTE_ROWS), axis=0, keepdims=True)
    rest = jnp.where(in_grp & (row != row1), eprob, -1.0)
    p2 = jnp.max(rest, axis=0, keepdims=True)
    row2 = jnp.min(jnp.where(rest == p2, row, ROUTE_ROWS), axis=0, keepdims=True)
    g1 = pmax * p1 / (p1 + p2)
    g2 = pmax * p2 / (p1 + p2)

    oh1 = row == row1
    oh2 = row == row2
    both = (oh1 | oh2).astype(BF16)
    srow = lax.broadcasted_iota(jnp.int32, (tm, tm), 0)
    scol = lax.broadcasted_iota(jnp.int32, (tm, tm), 1)
    earlier = (srow < scol).astype(BF16)
    carry = carry_ref[...]
    seen = _dot(both, earlier) + jnp.concatenate([carry] * (tm // LANES), axis=1)
    rank1 = jnp.sum(jnp.where(oh1, seen, 0.0), axis=0, keepdims=True)
    rank2 = jnp.sum(jnp.where(oh2, seen, 0.0), axis=0, keepdims=True)
    carry = carry + jnp.sum(both.astype(F32), axis=1, keepdims=True)
    carry_ref[...] = carry
    cnt_ref[...] = carry

    e1 = (row1 - N_GROUPS).astype(F32)
    e2 = (row2 - N_GROUPS).astype(F32)
    zero = jnp.zeros_like(g1)
    route_ref[...] = jnp.concatenate([e1, e2, g1, g2, rank1, rank2, zero, zero], axis=0)


def _post(fox, gla, h, wout, cg, wxq, kmem, vmem, wxo, mg, wr, br, seq, mem_len, layer):
    n, d_model = h.shape
    tm = POST_TILE
    per_seq = seq // tm
    const = lambda i: (0, 0)
    pick = lambda i: (layer, 0, 0)
    return pl.pallas_call(
        _post_kernel,
        grid=(n // tm,),
        in_specs=[
            pl.BlockSpec((tm, FOX_WIDTH), lambda i: (i, 0)),
            pl.BlockSpec((tm, GLA_V), lambda i: (i, 0)),
            pl.BlockSpec((tm, d_model), lambda i: (i, 0)),
            pl.BlockSpec((None, FOX_WIDTH + GLA_V, d_model), pick),
            pl.BlockSpec((None, 1, d_model), pick),
            pl.BlockSpec((None, d_model, X_WIDTH), pick),
            pl.BlockSpec((None, mem_len, X_WIDTH), lambda i: (layer, i // per_seq, 0)),
            pl.BlockSpec((None, mem_len, X_WIDTH), lambda i: (layer, i // per_seq, 0)),
            pl.BlockSpec((None, X_WIDTH, d_model), pick),
            pl.BlockSpec((None, 1, d_model), pick),
            pl.BlockSpec((None, d_model, 2 * LANES), pick),
            pl.BlockSpec((None, 1, LANES), pick),
        ],
        out_specs=[
            pl.BlockSpec((tm, d_model), lambda i: (i, 0)),
            pl.BlockSpec((tm, d_model // 2), lambda i: (i, 0)),
            pl.BlockSpec((ROUTE_WIDTH, tm), lambda i: (0, i)),
            pl.BlockSpec((ROUTE_ROWS, LANES), const),
        ],
        out_shape=[
            jax.ShapeDtypeStruct((n, d_model), F32),
            jax.ShapeDtypeStruct((n, d_model // 2), jnp.uint32),
            jax.ShapeDtypeStruct((ROUTE_WIDTH, n), F32),
            jax.ShapeDtypeStruct((ROUTE_ROWS, LANES), F32),
        ],
        scratch_shapes=[pltpu.VMEM((ROUTE_ROWS, LANES), F32)],
        compiler_params=_cparams(("arbitrary",)),
        name="post_mixer",
    )(fox, gla, h, wout, cg, wxq, kmem, vmem, wxo, mg, wr, br)


def _dispatch(dest_kmajor, x, n_rows):
    n, width = x.shape
    window = SC_GATHER_WINDOW
    mesh = plsc.VectorSubcoreMesh(core_axis_name="core", subcore_axis_name="subcore")
    workers = mesh.num_cores * mesh.num_subcores
    per_worker = n // workers
    assert n % (workers * window) == 0

    steps = per_worker // window
    assert steps % 2 == 0
    index_buf = pltpu.VMEM((window,), jnp.int32)
    row_buf = pltpu.VMEM((window, width), x.dtype)
    dma = pltpu.SemaphoreType.DMA

    @functools.partial(
        pl.kernel, out_type=jax.ShapeDtypeStruct((n_rows, width), x.dtype), mesh=mesh,
        scratch_types=[index_buf, index_buf, index_buf, index_buf, row_buf, row_buf, dma, dma, dma])
    def scatter(x_hbm, idx_hbm, out_hbm, idx0_a, idx1_a, idx0_b, idx1_b, rows_a, rows_b, sem_a, sem_b, sem_out):
        worker = lax.axis_index("subcore") * mesh.num_cores + lax.axis_index("core")
        base = worker * per_worker
        slots = ((idx0_a, idx1_a, rows_a, sem_a), (idx0_b, idx1_b, rows_b, sem_b))

        def load(step, slot):
            idx0, idx1, rows, sem = slots[slot]
            off = pl.multiple_of(base + step * window, window)
            pltpu.sync_copy(idx_hbm.at[pl.ds(off, window)], idx0)
            pltpu.sync_copy(idx_hbm.at[pl.ds(n + off, window)], idx1)
            pltpu.async_copy(x_hbm.at[pl.ds(off, window)], rows, sem)

        def store(slot):
            idx0, idx1, rows, sem = slots[slot]
            pltpu.make_async_copy(x_hbm.at[pl.ds(0, window)], rows, sem).wait()
            first = pltpu.async_copy(rows, out_hbm.at[idx0], sem_out)
            second = pltpu.async_copy(rows, out_hbm.at[idx1], sem_out)
            first.wait()
            second.wait()

        load(0, 0)

        @pl.loop(0, steps, step=2)
        def _(step):
            load(step + 1, 1)
            store(0)

            @pl.when(step + 2 < steps)
            def _():
                load(step + 2, 0)

            store(1)

    return scatter(x, dest_kmajor)


def _expert_kernel(be_ref, valid_ref, fresh_ref, x_ref, wg_ref, wu_ref, wd_ref, y_ref, wg_b, wu_b, wd_b):
    del be_ref
    i = pl.program_id(0)
    valid = valid_ref[i]

    @pl.when(fresh_ref[i] > 0)
    def _():
        wg_b[...] = wg_ref[...].astype(BF16)
        wu_b[...] = wu_ref[...].astype(BF16)
        wd_b[...] = wd_ref[...].astype(BF16)

    @pl.when(valid > 0)
    def _():
        row = lax.broadcasted_iota(jnp.int32, x_ref.shape, 0)
        lo, hi = _unpack_rows(jnp.where(row < valid, x_ref[...], jnp.uint32(0)))
        lo = lo.astype(BF16)
        hi = hi.astype(BF16)
        half = lo.shape[1]
        g = _dot(lo, wg_b[:half, :]) + _dot(hi, wg_b[half:, :])
        u = _dot(lo, wu_b[:half, :]) + _dot(hi, wu_b[half:, :])
        a = (g * jax.nn.sigmoid(g) * u).astype(BF16)
        y_ref[...] = _pack_rows(_dot(a, wd_b[...]))

    @pl.when(valid <= 0)
    def _():
        y_ref[...] = jnp.zeros_like(y_ref)


def _experts(block_e, valid, xs, wg, wu, wd, layer):
    r, width = xs.shape
    bm = MOE_BLOCK
    d_model, d_exp = wg.shape[-2:]
    fresh = jnp.concatenate([jnp.ones((1,), jnp.int32), (block_e[1:] != block_e[:-1]).astype(jnp.int32)])
    pick = lambda i, be, va, fr: (layer, be[i], 0, 0)
    return pl.pallas_call(
        _expert_kernel,
        grid_spec=pltpu.PrefetchScalarGridSpec(
            num_scalar_prefetch=3,
            grid=(r // bm,),
            in_specs=[
                pl.BlockSpec((bm, width), lambda i, be, va, fr: (i, 0)),
                pl.BlockSpec((None, None, d_model, d_exp), pick),
                pl.BlockSpec((None, None, d_model, d_exp), pick),
                pl.BlockSpec((None, None, d_exp, d_model), pick),
            ],
            out_specs=pl.BlockSpec((bm, width), lambda i, be, va, fr: (i, 0)),
            scratch_shapes=[
                pltpu.VMEM((d_model, d_exp), BF16),
                pltpu.VMEM((d_model, d_exp), BF16),
                pltpu.VMEM((d_exp, d_model), BF16),
            ],
        ),
        out_shape=jax.ShapeDtypeStruct((r, width), jnp.uint32),
        compiler_params=_cparams(("arbitrary",)),
        name="moe_experts",
    )(block_e, valid, fresh, xs, wg, wu, wd)


def _sc_gather_rows(table, idx):
    m = idx.shape[0]
    width = table.shape[1]
    window = SC_GATHER_WINDOW
    mesh = plsc.VectorSubcoreMesh(core_axis_name="core", subcore_axis_name="subcore")
    workers = mesh.num_cores * mesh.num_subcores
    per_worker = m // workers
    assert m % (workers * window) == 0

    steps = per_worker // window
    assert steps % 2 == 0
    index_buf = pltpu.VMEM((window,), jnp.int32)
    row_buf = pltpu.VMEM((window, width), table.dtype)
    dma = pltpu.SemaphoreType.DMA

    @functools.partial(
        pl.kernel, out_type=jax.ShapeDtypeStruct((m, width), table.dtype), mesh=mesh,
        scratch_types=[index_buf, index_buf, row_buf, row_buf, dma, dma])
    def gather(table_hbm, idx_hbm, out_hbm, idx_a, idx_b, rows_a, rows_b, sem_a, sem_b):
        worker = lax.axis_index("subcore") * mesh.num_cores + lax.axis_index("core")
        base = worker * per_worker
        slots = ((idx_a, rows_a, sem_a), (idx_b, rows_b, sem_b))

        def fetch(step, slot):
            idx, rows, sem = slots[slot]
            off = pl.multiple_of(base + step * window, window)
            pltpu.sync_copy(idx_hbm.at[pl.ds(off, window)], idx)
            pltpu.async_copy(table_hbm.at[idx], rows, sem)

        def flush(step, slot):
            idx, rows, sem = slots[slot]
            off = pl.multiple_of(base + step * window, window)
            pltpu.make_async_copy(table_hbm.at[idx], rows, sem).wait()
            pltpu.sync_copy(rows, out_hbm.at[pl.ds(off, window)])

        fetch(0, 0)

        @pl.loop(0, steps, step=2)
        def _(step):
            fetch(step + 1, 1)
            flush(step, 0)

            @pl.when(step + 2 < steps)
            def _():
                fetch(step + 2, 0)

            flush(step + 1, 1)

    return gather(table, idx)


def _gather_parts(y, dest_kmajor, n):
    part = n // MOE_GATHER_PARTS
    return [_sc_gather_rows(y, jnp.concatenate([dest_kmajor[p * part:(p + 1) * part],
                                                dest_kmajor[n + p * part:n + (p + 1) * part]]))
            for p in range(MOE_GATHER_PARTS)]


def _final_kernel(h_ref, gate_ref, gain_ref, y0_ref, y1_ref, *rest):
    o_ref = rest[-1]
    o_ref[...] = _rms(_moe_sum(h_ref[...], gate_ref[...], y0_ref[...], y1_ref[...]), gain_ref[...])


def _final(h, gates, gain, parts):
    n, d_model = h.shape
    tc = MOVE_TILE
    part_steps = n // tc // len(parts)
    row_block = lambda width, off=0: pl.BlockSpec((tc, width), lambda i: (i + off, 0))
    out = ()
    for p, picked in enumerate(parts):
        off = p * part_steps
        out = (pl.pallas_call(
            _final_kernel,
            grid=(part_steps,),
            in_specs=[row_block(d_model, off), row_block(2, off), pl.BlockSpec((1, d_model), lambda i: (0, 0)),
                      row_block(d_model // 2), row_block(d_model // 2, part_steps)]
                     + [pl.BlockSpec(memory_space=pl.ANY)] * len(out),
            out_specs=row_block(d_model, off),
            out_shape=jax.ShapeDtypeStruct((n, d_model), F32),
            input_output_aliases={5: 0} if out else {},
            compiler_params=_cparams(("arbitrary",)),
            name="moe_final",
        )(h, gates, gain, picked, picked, *out),)
    return out[0]


def _routing_tables(route, cnt, n_rows):
    bm = MOE_BLOCK
    expert = route[0:2].astype(jnp.int32)
    rank = route[4:6].astype(jnp.int32)
    counts = cnt[N_GROUPS:N_GROUPS + N_EXPERTS, 0].astype(jnp.int32)
    padded = (counts + bm - 1) // bm * bm
    pad_ends = jnp.cumsum(padded)
    pad_starts = pad_ends - padded
    ids = jnp.arange(N_EXPERTS, dtype=jnp.int32)
    start_of = jnp.sum(jnp.where(expert[..., None] == ids, pad_starts, 0), axis=-1)
    dest = (start_of + rank).reshape(-1).astype(jnp.int32)
    block_row = jnp.arange(n_rows // bm, dtype=jnp.int32) * bm
    block_e = jnp.minimum(jnp.sum((pad_ends[None, :] <= block_row[:, None]).astype(jnp.int32), axis=-1),
                          N_EXPERTS - 1)
    row_end = jnp.sum(jnp.where(block_e[:, None] == ids, pad_starts + counts, 0), axis=-1)
    valid = jnp.clip(row_end - block_row, 0, bm).astype(jnp.int32)
    return dest, block_e, valid


def kernel(x, mem, mem_norm, mix_norm, w_in, b_forget, w_alpha_up, b_alpha, fox_out_gain, gla_out_gain, w_out,
           cross_norm, w_xq, w_xk, w_xv, w_xo, moe_norm, w_router_group, b_router_group, w_router_expert,
           b_router_expert, w_expert_gate, w_expert_up, w_expert_down, final_norm):
    batch, seq, d_model = x.shape
    mem_len = mem.shape[1]
    depth = w_in.shape[0]
    n = batch * seq
    assert seq % FOX_TILE == 0 and seq % IN_TILE == 0 and seq % POST_TILE == 0 and seq % GLA_CHUNK == 0
    assert n % MOVE_TILE == 0 and d_model % LANES == 0

    c0 = 3 * FOX_WIDTH
    c1 = c0 + FOX_HEADS
    c2 = c1 + 2 * GLA_QK + 2 * GLA_V
    w_main = jnp.concatenate([w_in[:, :, :c0], w_in[:, :, c1:c2]], axis=-1).astype(BF16)
    pad = LANES - FOX_HEADS - GLA_RANK
    w_small = jnp.concatenate([w_in[:, :, c0:c1], w_in[:, :, c2:], jnp.zeros((depth, d_model, pad), F32)],
                              axis=-1).astype(BF16)
    w_up = jnp.concatenate([jnp.zeros((depth, FOX_HEADS, GLA_QK), F32), w_alpha_up,
                            jnp.zeros((depth, pad, GLA_QK), F32)], axis=1).astype(BF16)
    b_f = jnp.pad(b_forget, ((0, 0), (0, LANES - FOX_HEADS)))[:, None, :]
    b_a = b_alpha[:, None, :]
    w_r = jnp.concatenate([w_router_group, w_router_expert,
                           jnp.zeros((depth, d_model, LANES - N_GROUPS - N_EXPERTS), F32)], axis=-1)
    w_rh = w_r.astype(BF16)
    w_rs = jnp.concatenate([w_rh, (w_r - w_rh.astype(F32)).astype(BF16)], axis=-1)
    b_r = jnp.pad(jnp.concatenate([b_router_group, b_router_expert], axis=-1),
                  ((0, 0), (0, LANES - N_GROUPS - N_EXPERTS)))[:, None, :]
    w_out_b = w_out.astype(BF16)
    w_xq_b = w_xq.astype(BF16)
    w_xo_b = w_xo.astype(BF16)
    mix_g = mix_norm[:, None, :]
    cross_g = cross_norm[:, None, :]
    moe_g = moe_norm[:, None, :]

    kmem, vmem = _mem_kv(mem.reshape(batch * mem_len, d_model), mem_norm[None, :],
                         w_xk.astype(BF16), w_xv.astype(BF16), batch, mem_len)

    n_rows = 2 * n + N_EXPERTS * MOE_BLOCK
    nq = seq // FOX_TILE
    h = x.reshape(n, d_model)
    moe = None
    for l in range(depth):
        if moe is None:
            main, logf, loga = _in_proj(h, mix_g, w_main, w_small, w_up, b_f, b_a, l)
        else:
            h, main, logf, loga = _in_proj(h, mix_g, w_main, w_small, w_up, b_f, b_a, l, moe)
        c = _seq_cumsum(logf, batch, seq)
        c5 = c[:, :FOX_HEADS].reshape(batch, nq, FOX_TILE, FOX_HEADS // 2, 2).transpose(0, 3, 1, 4, 2)
        fox = _fox_attention(main, c5, fox_out_gain[l][None, :], batch, seq)
        gla = _gla(main, loga, gla_out_gain[l][None, :], batch, seq)
        h2, hn2, route, cnt = _post(fox, gla, h, w_out_b, cross_g, w_xq_b, kmem, vmem, w_xo_b, moe_g, w_rs, b_r,
                                    seq, mem_len, l)
        dest, block_e, valid = _routing_tables(route, cnt, n_rows)
        xs = _dispatch(dest, hn2, n_rows)
        y = _experts(block_e, valid, xs, w_expert_gate, w_expert_up, w_expert_down, l)
        h, moe = h2, (route[2:4].T, _gather_parts(y, dest, n))
    return _final(h, moe[0], final_norm[None, :], moe[1]).reshape(batch, seq, d_model)
```

```python
import functools

import jax
import jax.numpy as jnp
from jax import lax
from jax.experimental import pallas as pl
from jax.experimental.pallas import tpu as pltpu
from jax.experimental.pallas import tpu_sc as plsc

F32 = jnp.float32
BF16 = jnp.bfloat16
EPS = 1e-6
LOG2E = 1.4426950408889634

FOX_HEADS = 8
FOX_DIM = 64
FOX_WIDTH = FOX_HEADS * FOX_DIM
GLA_HEADS = 4
GLA_DK = 64
GLA_DV = 128
GLA_QK = GLA_HEADS * GLA_DK
GLA_V = GLA_HEADS * GLA_DV
GLA_RANK = 16
GLA_TAU = 16.0
GLA_CHUNK = 64
X_HEADS = 4
X_DIM = 128
X_WIDTH = X_HEADS * X_DIM
N_GROUPS = 4
GROUP_SIZE = 4
N_EXPERTS = N_GROUPS * GROUP_SIZE
MAIN_WIDTH = 3 * FOX_WIDTH + 2 * GLA_QK + 2 * GLA_V

LANES = 128
ROUTE_WIDTH = 8
ROUTE_ROWS = 32
VMEM_LIMIT = 56 * 1024 * 1024

IN_TILE = 1024
FOX_TILE = 512
FOX_SLAB = 64
POST_TILE = 1024
CUMSUM_TILE = 256
MOE_BLOCK = 512
MOVE_TILE = 1024
SC_GATHER_WINDOW = 64

def _cparams(sem):
    return pltpu.CompilerParams(dimension_semantics=sem, vmem_limit_bytes=VMEM_LIMIT)


def _rms(x, gain):
    return x * lax.rsqrt(jnp.mean(x * x, axis=-1, keepdims=True) + EPS) * gain


def _log_sigmoid(x):
    return jnp.minimum(x, 0.0) - jnp.log1p(jnp.exp(-jnp.abs(x)))


def _dot(a, b):
    return jnp.dot(a, b, preferred_element_type=F32)


def _dot_nt(a, b):
    return lax.dot_general(a, b, (((1,), (1,)), ((), ())), preferred_element_type=F32)


def _pack_rows(x):
    half = x.shape[1] // 2
    lo = lax.bitcast_convert_type(x[:, :half].astype(BF16).astype(F32), jnp.uint32)
    hi = lax.bitcast_convert_type(x[:, half:].astype(BF16).astype(F32), jnp.uint32)
    return (lo >> 16) | hi


def _unpack_rows(w):
    lo = lax.bitcast_convert_type(w << 16, F32)
    hi = lax.bitcast_convert_type(w & jnp.uint32(0xFFFF0000), F32)
    return lo, hi


def _split3(x):
    hi = x.astype(BF16)
    r1 = x - hi.astype(F32)
    mid = r1.astype(BF16)
    lo = (r1 - mid.astype(F32)).astype(BF16)
    return hi, mid, lo


def _mem_kv_kernel(mem_ref, gain_ref, wk_ref, wv_ref, k_ref, v_ref):
    mn = _rms(mem_ref[...], gain_ref[...]).astype(BF16)
    for l in range(wk_ref.shape[0]):
        k_ref[l] = _dot(mn, wk_ref[l]).astype(BF16)
        v_ref[l] = _dot(mn, wv_ref[l]).astype(BF16)


def _mem_kv(mem2d, gain, wk, wv, batch, mem_len):
    depth, d_model, width = wk.shape
    out = jax.ShapeDtypeStruct((depth, batch * mem_len, width), BF16)
    return pl.pallas_call(
        _mem_kv_kernel,
        grid=(batch,),
        in_specs=[
            pl.BlockSpec((mem_len, d_model), lambda b: (b, 0)),
            pl.BlockSpec((1, d_model), lambda b: (0, 0)),
            pl.BlockSpec((depth, d_model, width), lambda b: (0, 0, 0)),
            pl.BlockSpec((depth, d_model, width), lambda b: (0, 0, 0)),
        ],
        out_specs=[
            pl.BlockSpec((depth, mem_len, width), lambda b: (0, b, 0)),
            pl.BlockSpec((depth, mem_len, width), lambda b: (0, b, 0)),
        ],
        out_shape=[out, out],
        compiler_params=_cparams(("arbitrary",)),
        name="mem_kv",
    )(mem2d, gain, wk, wv)


def _moe_sum(h, gate, y0_packed, y1_packed):
    y0 = jnp.concatenate(_unpack_rows(y0_packed), axis=1)
    y1 = jnp.concatenate(_unpack_rows(y1_packed), axis=1)
    return h + gate[:, 0:1] * y0 + gate[:, 1:2] * y1


def _in_proj_body(h, gain_ref, wmain_ref, wsmall_ref, wup_ref, bf_ref, ba_ref, main_ref, logf_ref, kl_ref, dec_ref):
    tm = h.shape[0]
    cs = GLA_CHUNK
    xn = _rms(h, gain_ref[...]).astype(BF16)
    small = _dot(xn, wsmall_ref[...])
    lane = lax.broadcasted_iota(jnp.int32, small.shape, 1)
    logf_ref[...] = jnp.where(lane < FOX_HEADS, _log_sigmoid(small + bf_ref[...]), 0.0)
    a = _dot(small.astype(BF16), wup_ref[...]) + ba_ref[...]
    b = _log_sigmoid(a) * (1.0 / GLA_TAU)
    pos = lax.broadcasted_iota(jnp.int32, b.shape, 0) % cs
    shift = 1
    while shift < cs:
        b = b + jnp.where(pos >= shift, pltpu.roll(b, shift, axis=0), 0.0)
        shift *= 2
    dec = jnp.exp(b.reshape(tm // cs, cs, GLA_QK)[:, cs - 1:cs, :])
    q0 = 3 * FOX_WIDTH
    k0 = q0 + GLA_QK
    qk = _dot(xn, wmain_ref[:, q0:k0 + GLA_QK])
    main_ref[:, q0:k0] = (qk[:, :GLA_QK] * jnp.exp(b) * (GLA_DK ** -0.5)).astype(BF16)
    ke = qk[:, GLA_QK:] * jnp.exp(-b)
    main_ref[:, k0:k0 + GLA_QK] = ke.astype(BF16)
    kl_ref[...] = (ke.reshape(tm // cs, cs, GLA_QK) * dec).reshape(tm, GLA_QK).astype(BF16)
    dec_ref[...] = dec.reshape(tm // cs, GLA_QK)
    step = 512
    for lo in list(range(0, q0, step)) + list(range(k0 + GLA_QK, MAIN_WIDTH, step)):
        main_ref[:, lo:lo + step] = _dot(xn, wmain_ref[:, lo:lo + step]).astype(BF16)


def _in_proj_kernel(h_ref, *refs):
    _in_proj_body(h_ref[...], *refs)


def _in_proj_after_moe_kernel(h_ref, gate_ref, y0_ref, y1_ref, gain_ref, wmain_ref, wsmall_ref, wup_ref, bf_ref,
                              ba_ref, hout_ref, main_ref, logf_ref, kl_ref, dec_ref):
    h = _moe_sum(h_ref[...], gate_ref[...], y0_ref[...], y1_ref[...])
    hout_ref[...] = h
    _in_proj_body(h, gain_ref, wmain_ref, wsmall_ref, wup_ref, bf_ref, ba_ref, main_ref, logf_ref, kl_ref, dec_ref)


def _in_proj(h, gain, wmain, wsmall, wup, bf, ba, layer, moe=None):
    n, d_model = h.shape
    tm = IN_TILE
    steps = n // tm
    chunks = tm // GLA_CHUNK
    pick = lambda i: (layer, 0, 0)
    row_block = lambda width, rows=tm: pl.BlockSpec((rows, width), lambda i: (i, 0))
    weight_specs = [
        pl.BlockSpec((None, 1, d_model), pick),
        pl.BlockSpec((None, d_model, MAIN_WIDTH), pick),
        pl.BlockSpec((None, d_model, LANES), pick),
        pl.BlockSpec((None, LANES, GLA_QK), pick),
        pl.BlockSpec((None, 1, LANES), pick),
        pl.BlockSpec((None, 1, GLA_QK), pick),
    ]
    out_specs = [row_block(MAIN_WIDTH), row_block(LANES), row_block(GLA_QK), row_block(GLA_QK, chunks)]
    out_shape = [
        jax.ShapeDtypeStruct((n, MAIN_WIDTH), BF16),
        jax.ShapeDtypeStruct((n, LANES), F32),
        jax.ShapeDtypeStruct((n, GLA_QK), BF16),
        jax.ShapeDtypeStruct((n // GLA_CHUNK, GLA_QK), F32),
    ]
    weights = (gain, wmain, wsmall, wup, bf, ba)
    if moe is None:
        return pl.pallas_call(
            _in_proj_kernel, grid=(steps,), in_specs=[row_block(d_model)] + weight_specs,
            out_specs=out_specs, out_shape=out_shape,
            compiler_params=_cparams(("arbitrary",)), name="in_proj",
        )(h, *weights)
    gates, picked = moe
    half = d_model // 2
    return pl.pallas_call(
        _in_proj_after_moe_kernel, grid=(steps,),
        in_specs=[row_block(d_model), row_block(2), row_block(half),
                  pl.BlockSpec((tm, half), lambda i: (i + steps, 0))] + weight_specs,
        out_specs=[row_block(d_model)] + out_specs,
        out_shape=[jax.ShapeDtypeStruct((n, d_model), F32)] + out_shape,
        compiler_params=_cparams(("arbitrary",)), name="in_proj_after_moe",
    )(h, gates, picked, picked, *weights)


def _cumsum_kernel(x_ref, o_ref):
    t = CUMSUM_TILE
    row = lax.broadcasted_iota(jnp.int32, (t, t), 0)
    col = lax.broadcasted_iota(jnp.int32, (t, t), 1)
    tril = (row >= col).astype(BF16)
    carry = jnp.zeros((1, x_ref.shape[1]), F32)
    for j in range(x_ref.shape[0] // t):
        hi, mid, lo = _split3(x_ref[j * t:(j + 1) * t, :])
        c = _dot(tril, hi) + _dot(tril, mid) + _dot(tril, lo) + carry
        o_ref[j * t:(j + 1) * t, :] = c
        carry = c[t - 1:t, :]


def _seq_cumsum(x, batch, seq):
    return pl.pallas_call(
        _cumsum_kernel,
        grid=(batch,),
        in_specs=[pl.BlockSpec((seq, LANES), lambda b: (b, 0))],
        out_specs=pl.BlockSpec((seq, LANES), lambda b: (b, 0)),
        out_shape=jax.ShapeDtypeStruct(x.shape, F32),
        compiler_params=_cparams(("arbitrary",)),
        name="forget_cumsum",
    )(x)


def _fox_kernel(q_ref, k_ref, v_ref, c_ref, gain_ref, o_ref, q2_ref, s_ref, p_ref, alpha_ref, m_ref, l_ref, acc_ref):
    tq = FOX_TILE
    rows = 2 * tq
    slab = FOX_SLAB
    nq = q_ref.shape[0] // tq
    lane = lax.broadcasted_iota(jnp.int32, (1, LANES), 1)
    first = lane < FOX_DIM
    scale = FOX_DIM ** -0.5 * LOG2E
    for qi in range(nq):
        q = q_ref[qi * tq:(qi + 1) * tq, :].astype(F32) * scale
        q2_ref[qi, :tq, :] = jnp.where(first, q, 0.0).astype(BF16)
        q2_ref[qi, tq:, :] = jnp.where(first, 0.0, q).astype(BF16)

    def scores(qi, j):
        cj = c_ref[j] * LOG2E
        d = _dot_nt(q2_ref[qi], k_ref[j * tq:(j + 1) * tq, :])
        s_ref[:tq, :] = d[:tq] - cj[0:1, :]
        s_ref[tq:, :] = d[tq:] - cj[1:2, :]

    def weighted_values(qi, j):
        par = qi % 2
        acc_ref[par] = alpha_ref[par] * acc_ref[par] + _dot(p_ref[...], v_ref[j * tq:(j + 1) * tq, :])

    def softmax(qi, masked):
        par = qi % 2
        for r in range(rows // slab):
            sl = slice(r * slab, (r + 1) * slab)
            s = s_ref[sl, :]
            if masked:
                row = lax.broadcasted_iota(jnp.int32, (slab, tq), 0) + (r * slab) % tq
                col = lax.broadcasted_iota(jnp.int32, (slab, tq), 1)
                s = jnp.where(row >= col, s, -jnp.inf)
            m_old = m_ref[par, sl, :]
            m_new = jnp.maximum(m_old, jnp.max(s, axis=-1, keepdims=True))
            alpha = jnp.exp2(m_old - m_new)
            p = jnp.exp2(s - jnp.concatenate([m_new] * (tq // LANES), axis=1))
            l_ref[par, sl, :] = alpha * l_ref[par, sl, :] + jnp.sum(p, axis=-1, keepdims=True)
            m_ref[par, sl, :] = m_new
            alpha_ref[par, sl, :] = alpha
            p_ref[sl, :] = p.astype(BF16)

    def finalize(qi):
        par = qi % 2
        o2 = acc_ref[par] / l_ref[par]
        o = jnp.where(first, o2[:tq], o2[tq:])
        sq = o * o
        ss0 = jnp.sum(jnp.where(first, sq, 0.0), axis=-1, keepdims=True)
        ss1 = jnp.sum(jnp.where(first, 0.0, sq), axis=-1, keepdims=True)
        ms = jnp.where(first, ss0, ss1) * (1.0 / FOX_DIM)
        o_ref[qi * tq:(qi + 1) * tq, :] = (o * lax.rsqrt(ms + EPS) * gain_ref[...]).astype(BF16)

    steps = [(qi, j) for qi in range(nq) for j in range(qi + 1)]
    scores(*steps[0])
    for t, (qi, j) in enumerate(steps):
        if t > 0:
            weighted_values(*steps[t - 1])
            if steps[t - 1][0] != qi:
                finalize(steps[t - 1][0])
        if j == 0:
            par = qi % 2
            m_ref[par] = jnp.full(m_ref.shape[1:], -jnp.inf, F32)
            l_ref[par] = jnp.zeros(l_ref.shape[1:], F32)
            acc_ref[par] = jnp.zeros(acc_ref.shape[1:], F32)
        softmax(qi, masked=(j == qi))
        if t + 1 < len(steps):
            scores(*steps[t + 1])
    weighted_values(*steps[-1])
    finalize(steps[-1][0])


def _fox_attention(main, c5, gain, batch, seq):
    n = main.shape[0]
    tq = FOX_TILE
    nq = seq // tq
    pairs = FOX_HEADS // 2
    k_off = FOX_WIDTH // LANES
    v_off = 2 * FOX_WIDTH // LANES
    stat = pltpu.VMEM((2, 2 * tq, LANES), F32)
    return pl.pallas_call(
        _fox_kernel,
        grid=(batch, pairs),
        in_specs=[
            pl.BlockSpec((seq, LANES), lambda b, p: (b, p)),
            pl.BlockSpec((seq, LANES), lambda b, p: (b, k_off + p)),
            pl.BlockSpec((seq, LANES), lambda b, p: (b, v_off + p)),
            pl.BlockSpec((None, None, nq, 2, tq), lambda b, p: (b, p, 0, 0, 0)),
            pl.BlockSpec((1, LANES), lambda b, p: (0, p)),
        ],
        out_specs=pl.BlockSpec((seq, LANES), lambda b, p: (b, p)),
        out_shape=jax.ShapeDtypeStruct((n, FOX_WIDTH), BF16),
        scratch_shapes=[
            pltpu.VMEM((nq, 2 * tq, LANES), BF16),
            pltpu.VMEM((2 * tq, tq), F32),
            pltpu.VMEM((2 * tq, tq), BF16),
            stat, stat, stat, stat,
        ],
        compiler_params=_cparams(("arbitrary", "arbitrary")),
        name="fox_attention",
    )(main, main, main, c5, gain)


def _gla_kernel(qe_ref, ke_ref, v_ref, gg_ref, kl_ref, dec_ref, gain_ref, o_ref, raw_ref):
    seq = qe_ref.shape[0]
    cs = GLA_CHUNK
    nc = seq // cs
    width = 2 * GLA_DK

    lane = lax.broadcasted_iota(jnp.int32, (1, width), 1)
    first = lane < GLA_DK
    row = lax.broadcasted_iota(jnp.int32, (2 * cs, cs), 0)
    col = lax.broadcasted_iota(jnp.int32, (2 * cs, cs), 1)
    tril2 = jnp.where(row >= cs, row - cs, row) >= col
    srow = lax.broadcasted_iota(jnp.int32, (2 * GLA_DV, width), 0)
    scol = lax.broadcasted_iota(jnp.int32, (2 * GLA_DV, width), 1)
    same_head = (srow >= GLA_DV) == (scol >= GLA_DK)
    unroll = 8

    def chunks(ci, st):
        r0s = [pl.multiple_of((ci * unroll + u) * cs, cs) for u in range(unroll)]
        qes = [qe_ref[pl.ds(r0, cs), :] for r0 in r0s]
        vs = [v_ref[pl.ds(r0, cs), :] for r0 in r0s]
        atts, upds = [], []
        for u in range(unroll):
            zero = jnp.zeros_like(qes[u])
            q2 = jnp.concatenate([jnp.where(first, qes[u], zero), jnp.where(first, zero, qes[u])], axis=0)
            atts.append(jnp.where(tril2, _dot_nt(q2, ke_ref[pl.ds(r0s[u], cs), :]), 0.0).astype(BF16))
        for u in range(unroll):
            upds.append(lax.dot_general(vs[u], kl_ref[pl.ds(r0s[u], cs), :], (((0,), (0,)), ((), ())),
                                        preferred_element_type=F32))
        ois = [_dot(atts[u], vs[u]) for u in range(unroll)]
        for u in range(unroll):
            o = _dot_nt(qes[u], st.astype(BF16))
            o = o + jnp.concatenate([ois[u][:cs, :GLA_DV], ois[u][cs:, GLA_DV:]], axis=1)
            raw_ref[pl.ds(r0s[u], cs), :] = o
            st = st * dec_ref[pl.ds(ci * unroll + u, 1), :] + jnp.where(same_head, upds[u], 0.0)
        return st

    lax.fori_loop(0, nc // unroll, chunks, jnp.zeros((2 * GLA_DV, width), F32))

    o = raw_ref[...]
    normed = []
    for h in range(2):
        oh = o[:, h * GLA_DV:(h + 1) * GLA_DV]
        normed.append(oh * lax.rsqrt(jnp.mean(oh * oh, axis=-1, keepdims=True) + EPS))
    g = gg_ref[...].astype(F32)
    o_ref[...] = (jnp.concatenate(normed, axis=1) * gain_ref[...] * (g * jax.nn.sigmoid(g))).astype(BF16)


def _gla(main, kl, dec, gain, batch, seq):
    n = main.shape[0]
    pairs = GLA_HEADS // 2
    q_off = 3 * FOX_WIDTH // LANES
    k_off = q_off + GLA_QK // LANES
    pv = 2 * GLA_DV
    v_off = (3 * FOX_WIDTH + 2 * GLA_QK) // pv
    g_off = v_off + GLA_V // pv
    return pl.pallas_call(
        _gla_kernel,
        grid=(batch, pairs),
        in_specs=[
            pl.BlockSpec((seq, LANES), lambda b, p: (b, q_off + p)),
            pl.BlockSpec((seq, LANES), lambda b, p: (b, k_off + p)),
            pl.BlockSpec((seq, pv), lambda b, p: (b, v_off + p)),
            pl.BlockSpec((seq, pv), lambda b, p: (b, g_off + p)),
            pl.BlockSpec((seq, LANES), lambda b, p: (b, p)),
            pl.BlockSpec((seq // GLA_CHUNK, LANES), lambda b, p: (b, p)),
            pl.BlockSpec((1, pv), lambda b, p: (0, p)),
        ],
        out_specs=pl.BlockSpec((seq, pv), lambda b, p: (b, p)),
        out_shape=jax.ShapeDtypeStruct((n, GLA_V), BF16),
        scratch_shapes=[pltpu.VMEM((seq, pv), F32)],
        compiler_params=_cparams(("arbitrary", "arbitrary")),
        name="gla",
    )(main, main, main, main, kl, dec, gain)


def _post_kernel(fox_ref, gla_ref, h_ref, wout_ref, cg_ref, wxq_ref, k_ref, v_ref, wxo_ref, mg_ref,
                 wr_ref, br_ref, h2_ref, hn_ref, route_ref, cnt_ref, carry_ref):
    tm = h_ref.shape[0]

    @pl.when(pl.program_id(0) == 0)
    def _():
        carry_ref[...] = jnp.zeros_like(carry_ref)

    y = _dot(fox_ref[...], wout_ref[0:FOX_WIDTH, :]) + _dot(gla_ref[...], wout_ref[FOX_WIDTH:, :])
    h1 = h_ref[...] + y
    hn = _rms(h1, cg_ref[...]).astype(BF16)
    q = _dot(hn, wxq_ref[...]).astype(BF16)
    xscale = X_DIM ** -0.5
    heads = []
    for hh in range(X_HEADS):
        sl = slice(hh * X_DIM, (hh + 1) * X_DIM)
        s = _dot_nt(q[:, sl], k_ref[:, sl]) * xscale
        p = jnp.exp(s - jnp.max(s, axis=-1, keepdims=True))
        heads.append(_dot(p.astype(BF16), v_ref[:, sl]) / jnp.sum(p, axis=-1, keepdims=True))
    o = jnp.concatenate(heads, axis=1).astype(BF16)
    h2 = h1 + _dot(o, wxo_ref[...])
    h2_ref[...] = h2
    hn2 = _rms(h2, mg_ref[...])
    hn_ref[...] = _pack_rows(hn2)

    xh = hn2.astype(BF16)
    xl = (hn2 - xh.astype(F32)).astype(BF16)
    both_w = _dot(jnp.concatenate([xh, xl], axis=0), wr_ref[...])
    logits = both_w[:tm, :LANES] + both_w[:tm, LANES:] + both_w[tm:, :LANES] + both_w[tm:, LANES:] + br_ref[...]
    lt = jnp.transpose(logits)[:ROUTE_ROWS, :]
    row = lax.broadcasted_iota(jnp.int32, (ROUTE_ROWS, tm), 0)
    neg = -jnp.inf
    gl = jnp.where(row < N_GROUPS, lt, neg)
    gmax = jnp.max(gl, axis=0, keepdims=True)
    ge = jnp.exp(gl - gmax)
    gprob = ge / jnp.sum(ge, axis=0, keepdims=True)
    pmax = jnp.max(gprob, axis=0, keepdims=True)
    grp = jnp.min(jnp.where(gprob == pmax, row, ROUTE_ROWS), axis=0, keepdims=True)
    in_grp = (row >= N_GROUPS) & (row < N_GROUPS + N_EXPERTS) & (((row - N_GROUPS) // GROUP_SIZE) == grp)
    el = jnp.where(in_grp, lt, neg)
    emax = jnp.max(el, axis=0, keepdims=True)
    ee = jnp.exp(el - emax)
    eprob = ee / jnp.sum(ee, axis=0, keepdims=True)
    p1 = jnp.max(eprob, axis=0, keepdims=True)
    row1 = jnp.min(jnp.where(in_grp & (eprob == p1), row, ROUTE_ROWS), axis=0, keepdims=True)
    rest = jnp.where(in_grp & (row != row1), eprob, -1.0)
    p2 = jnp.max(rest, axis=0, keepdims=True)
    row2 = jnp.min(jnp.where(rest == p2, row, ROUTE_ROWS), axis=0, keepdims=True)
    g1 = pmax * p1 / (p1 + p2)
    g2 = pmax * p2 / (p1 + p2)

    oh1 = row == row1
    oh2 = row == row2
    both = (oh1 | oh2).astype(BF16)
    srow = lax.broadcasted_iota(jnp.int32, (tm, tm), 0)
    scol = lax.broadcasted_iota(jnp.int32, (tm, tm), 1)
    earlier = (srow < scol).astype(BF16)
    carry = carry_ref[...]
    seen = _dot(both, earlier) + jnp.concatenate([carry] * (tm // LANES), axis=1)
    rank1 = jnp.sum(jnp.where(oh1, seen, 0.0), axis=0, keepdims=True)
    rank2 = jnp.sum(jnp.where(oh2, seen, 0.0), axis=0, keepdims=True)
    carry = carry + jnp.sum(both.astype(F32), axis=1, keepdims=True)
    carry_ref[...] = carry
    cnt_ref[...] = carry

    e1 = (row1 - N_GROUPS).astype(F32)
    e2 = (row2 - N_GROUPS).astype(F32)
    zero = jnp.zeros_like(g1)
    route_ref[...] = jnp.concatenate([e1, e2, g1, g2, rank1, rank2, zero, zero], axis=0)


def _post(fox, gla, h, wout, cg, wxq, kmem, vmem, wxo, mg, wr, br, seq, mem_len, layer):
    n, d_model = h.shape
    tm = POST_TILE
    per_seq = seq // tm
    const = lambda i: (0, 0)
    pick = lambda i: (layer, 0, 0)
    return pl.pallas_call(
        _post_kernel,
        grid=(n // tm,),
        in_specs=[
            pl.BlockSpec((tm, FOX_WIDTH), lambda i: (i, 0)),
            pl.BlockSpec((tm, GLA_V), lambda i: (i, 0)),
            pl.BlockSpec((tm, d_model), lambda i: (i, 0)),
            pl.BlockSpec((None, FOX_WIDTH + GLA_V, d_model), pick),
            pl.BlockSpec((None, 1, d_model), pick),
            pl.BlockSpec((None, d_model, X_WIDTH), pick),
            pl.BlockSpec((None, mem_len, X_WIDTH), lambda i: (layer, i // per_seq, 0)),
            pl.BlockSpec((None, mem_len, X_WIDTH), lambda i: (layer, i // per_seq, 0)),
            pl.BlockSpec((None, X_WIDTH, d_model), pick),
            pl.BlockSpec((None, 1, d_model), pick),
            pl.BlockSpec((None, d_model, 2 * LANES), pick),
            pl.BlockSpec((None, 1, LANES), pick),
        ],
        out_specs=[
            pl.BlockSpec((tm, d_model), lambda i: (i, 0)),
            pl.BlockSpec((tm, d_model // 2), lambda i: (i, 0)),
            pl.BlockSpec((ROUTE_WIDTH, tm), lambda i: (0, i)),
            pl.BlockSpec((ROUTE_ROWS, LANES), const),
        ],
        out_shape=[
            jax.ShapeDtypeStruct((n, d_model), F32),
            jax.ShapeDtypeStruct((n, d_model // 2), jnp.uint32),
            jax.ShapeDtypeStruct((ROUTE_WIDTH, n), F32),
            jax.ShapeDtypeStruct((ROUTE_ROWS, LANES), F32),
        ],
        scratch_shapes=[pltpu.VMEM((ROUTE_ROWS, LANES), F32)],
        compiler_params=_cparams(("arbitrary",)),
        name="post_mixer",
    )(fox, gla, h, wout, cg, wxq, kmem, vmem, wxo, mg, wr, br)


def _dispatch(dest_kmajor, x, n_rows):
    n, width = x.shape
    window = SC_GATHER_WINDOW
    mesh = plsc.VectorSubcoreMesh(core_axis_name="core", subcore_axis_name="subcore")
    workers = mesh.num_cores * mesh.num_subcores
    per_worker = n // workers
    assert n % (workers * window) == 0

    steps = per_worker // window
    assert steps % 2 == 0
    index_buf = pltpu.VMEM((window,), jnp.int32)
    row_buf = pltpu.VMEM((window, width), x.dtype)
    dma = pltpu.SemaphoreType.DMA

    @functools.partial(
        pl.kernel, out_type=jax.ShapeDtypeStruct((n_rows, width), x.dtype), mesh=mesh,
        scratch_types=[index_buf, index_buf, index_buf, index_buf, row_buf, row_buf, dma, dma, dma])
    def scatter(x_hbm, idx_hbm, out_hbm, idx0_a, idx1_a, idx0_b, idx1_b, rows_a, rows_b, sem_a, sem_b, sem_out):
        worker = lax.axis_index("subcore") * mesh.num_cores + lax.axis_index("core")
        base = worker * per_worker
        slots = ((idx0_a, idx1_a, rows_a, sem_a), (idx0_b, idx1_b, rows_b, sem_b))

        def load(step, slot):
            idx0, idx1, rows, sem = slots[slot]
            off = pl.multiple_of(base + step * window, window)
            pltpu.sync_copy(idx_hbm.at[pl.ds(off, window)], idx0)
            pltpu.sync_copy(idx_hbm.at[pl.ds(n + off, window)], idx1)
            pltpu.async_copy(x_hbm.at[pl.ds(off, window)], rows, sem)

        def store(slot):
            idx0, idx1, rows, sem = slots[slot]
            pltpu.make_async_copy(x_hbm.at[pl.ds(0, window)], rows, sem).wait()
            first = pltpu.async_copy(rows, out_hbm.at[idx0], sem_out)
            second = pltpu.async_copy(rows, out_hbm.at[idx1], sem_out)
            first.wait()
            second.wait()

        load(0, 0)

        @pl.loop(0, steps, step=2)
        def _(step):
            load(step + 1, 1)
            store(0)

            @pl.when(step + 2 < steps)
            def _():
                load(step + 2, 0)

            store(1)

    return scatter(x, dest_kmajor)


def _expert_kernel(be_ref, valid_ref, fresh_ref, x_ref, wg_ref, wu_ref, wd_ref, y_ref, wg_b, wu_b, wd_b):
    del be_ref
    i = pl.program_id(0)
    valid = valid_ref[i]

    @pl.when(fresh_ref[i] > 0)
    def _():
        wg_b[...] = wg_ref[...].astype(BF16)
        wu_b[...] = wu_ref[...].astype(BF16)
        wd_b[...] = wd_ref[...].astype(BF16)

    @pl.when(valid > 0)
    def _():
        row = lax.broadcasted_iota(jnp.int32, x_ref.shape, 0)
        lo, hi = _unpack_rows(jnp.where(row < valid, x_ref[...], jnp.uint32(0)))
        lo = lo.astype(BF16)
        hi = hi.astype(BF16)
        half = lo.shape[1]
        g = _dot(lo, wg_b[:half, :]) + _dot(hi, wg_b[half:, :])
        u = _dot(lo, wu_b[:half, :]) + _dot(hi, wu_b[half:, :])
        a = (g * jax.nn.sigmoid(g) * u).astype(BF16)
        y_ref[...] = _pack_rows(_dot(a, wd_b[...]))

    @pl.when(valid <= 0)
    def _():
        y_ref[...] = jnp.zeros_like(y_ref)


def _experts(block_e, valid, xs, wg, wu, wd, layer):
    r, width = xs.shape
    bm = MOE_BLOCK
    d_model, d_exp = wg.shape[-2:]
    fresh = jnp.concatenate([jnp.ones((1,), jnp.int32), (block_e[1:] != block_e[:-1]).astype(jnp.int32)])
    pick = lambda i, be, va, fr: (layer, be[i], 0, 0)
    return pl.pallas_call(
        _expert_kernel,
        grid_spec=pltpu.PrefetchScalarGridSpec(
            num_scalar_prefetch=3,
            grid=(r // bm,),
            in_specs=[
                pl.BlockSpec((bm, width), lambda i, be, va, fr: (i, 0)),
                pl.BlockSpec((None, None, d_model, d_exp), pick),
                pl.BlockSpec((None, None, d_model, d_exp), pick),
                pl.BlockSpec((None, None, d_exp, d_model), pick),
            ],
            out_specs=pl.BlockSpec((bm, width), lambda i, be, va, fr: (i, 0)),
            scratch_shapes=[
                pltpu.VMEM((d_model, d_exp), BF16),
                pltpu.VMEM((d_model, d_exp), BF16),
                pltpu.VMEM((d_exp, d_model), BF16),
            ],
        ),
        out_shape=jax.ShapeDtypeStruct((r, width), jnp.uint32),
        compiler_params=_cparams(("arbitrary",)),
        name="moe_experts",
    )(block_e, valid, fresh, xs, wg, wu, wd)


def _sc_gather_rows(table, idx):
    m = idx.shape[0]
    width = table.shape[1]
    window = SC_GATHER_WINDOW
    mesh = plsc.VectorSubcoreMesh(core_axis_name="core", subcore_axis_name="subcore")
    workers = mesh.num_cores * mesh.num_subcores
    per_worker = m // workers
    assert m % (workers * window) == 0

    steps = per_worker // window
    assert steps % 2 == 0
    index_buf = pltpu.VMEM((window,), jnp.int32)
    row_buf = pltpu.VMEM((window, width), table.dtype)
    dma = pltpu.SemaphoreType.DMA

    @functools.partial(
        pl.kernel, out_type=jax.ShapeDtypeStruct((m, width), table.dtype), mesh=mesh,
        scratch_types=[index_buf, index_buf, row_buf, row_buf, dma, dma])
    def gather(table_hbm, idx_hbm, out_hbm, idx_a, idx_b, rows_a, rows_b, sem_a, sem_b):
        worker = lax.axis_index("subcore") * mesh.num_cores + lax.axis_index("core")
        base = worker * per_worker
        slots = ((idx_a, rows_a, sem_a), (idx_b, rows_b, sem_b))

        def fetch(step, slot):
            idx, rows, sem = slots[slot]
            off = pl.multiple_of(base + step * window, window)
            pltpu.sync_copy(idx_hbm.at[pl.ds(off, window)], idx)
            pltpu.async_copy(table_hbm.at[idx], rows, sem)

        def flush(step, slot):
            idx, rows, sem = slots[slot]
            off = pl.multiple_of(base + step * window, window)
            pltpu.make_async_copy(table_hbm.at[idx], rows, sem).wait()
            pltpu.sync_copy(rows, out_hbm.at[pl.ds(off, window)])

        fetch(0, 0)

        @pl.loop(0, steps, step=2)
        def _(step):
            fetch(step + 1, 1)
            flush(step, 0)

            @pl.when(step + 2 < steps)
            def _():
                fetch(step + 2, 0)

            flush(step + 1, 1)

    return gather(table, idx)


def _final_kernel(h_ref, gate_ref, gain_ref, y0_ref, y1_ref, o_ref):
    o_ref[...] = _rms(_moe_sum(h_ref[...], gate_ref[...], y0_ref[...], y1_ref[...]), gain_ref[...])


def _final(h, gates, gain, picked):
    n, d_model = h.shape
    tc = MOVE_TILE
    steps = n // tc
    return pl.pallas_call(
        _final_kernel,
        grid=(steps,),
        in_specs=[
            pl.BlockSpec((tc, d_model), lambda i: (i, 0)),
            pl.BlockSpec((tc, 2), lambda i: (i, 0)),
            pl.BlockSpec((1, d_model), lambda i: (0, 0)),
            pl.BlockSpec((tc, d_model // 2), lambda i: (i, 0)),
            pl.BlockSpec((tc, d_model // 2), lambda i: (i + steps, 0)),
        ],
        out_specs=pl.BlockSpec((tc, d_model), lambda i: (i, 0)),
        out_shape=jax.ShapeDtypeStruct((n, d_model), F32),
        compiler_params=_cparams(("arbitrary",)),
        name="moe_final",
    )(h, gates, gain, picked, picked)


def _routing_tables(route, cnt, n_rows):
    bm = MOE_BLOCK
    expert = route[0:2].astype(jnp.int32)
    rank = route[4:6].astype(jnp.int32)
    counts = cnt[N_GROUPS:N_GROUPS + N_EXPERTS, 0].astype(jnp.int32)
    padded = (counts + bm - 1) // bm * bm
    pad_ends = jnp.cumsum(padded)
    pad_starts = pad_ends - padded
    ids = jnp.arange(N_EXPERTS, dtype=jnp.int32)
    start_of = jnp.sum(jnp.where(expert[..., None] == ids, pad_starts, 0), axis=-1)
    dest = (start_of + rank).reshape(-1).astype(jnp.int32)
    block_row = jnp.arange(n_rows // bm, dtype=jnp.int32) * bm
    block_e = jnp.minimum(jnp.sum((pad_ends[None, :] <= block_row[:, None]).astype(jnp.int32), axis=-1),
                          N_EXPERTS - 1)
    row_end = jnp.sum(jnp.where(block_e[:, None] == ids, pad_starts + counts, 0), axis=-1)
    valid = jnp.clip(row_end - block_row, 0, bm).astype(jnp.int32)
    return dest, block_e, valid


def kernel(x, mem, mem_norm, mix_norm, w_in, b_forget, w_alpha_up, b_alpha, fox_out_gain, gla_out_gain, w_out,
           cross_norm, w_xq, w_xk, w_xv, w_xo, moe_norm, w_router_group, b_router_group, w_router_expert,
           b_router_expert, w_expert_gate, w_expert_up, w_expert_down, final_norm):
    batch, seq, d_model = x.shape
    mem_len = mem.shape[1]
    depth = w_in.shape[0]
    n = batch * seq
    assert seq % FOX_TILE == 0 and seq % IN_TILE == 0 and seq % POST_TILE == 0 and seq % GLA_CHUNK == 0
    assert n % MOVE_TILE == 0 and d_model % LANES == 0

    c0 = 3 * FOX_WIDTH
    c1 = c0 + FOX_HEADS
    c2 = c1 + 2 * GLA_QK + 2 * GLA_V
    w_main = jnp.concatenate([w_in[:, :, :c0], w_in[:, :, c1:c2]], axis=-1).astype(BF16)
    pad = LANES - FOX_HEADS - GLA_RANK
    w_small = jnp.concatenate([w_in[:, :, c0:c1], w_in[:, :, c2:], jnp.zeros((depth, d_model, pad), F32)],
                              axis=-1).astype(BF16)
    w_up = jnp.concatenate([jnp.zeros((depth, FOX_HEADS, GLA_QK), F32), w_alpha_up,
                            jnp.zeros((depth, pad, GLA_QK), F32)], axis=1).astype(BF16)
    b_f = jnp.pad(b_forget, ((0, 0), (0, LANES - FOX_HEADS)))[:, None, :]
    b_a = b_alpha[:, None, :]
    w_r = jnp.concatenate([w_router_group, w_router_expert,
                           jnp.zeros((depth, d_model, LANES - N_GROUPS - N_EXPERTS), F32)], axis=-1)
    w_rh = w_r.astype(BF16)
    w_rs = jnp.concatenate([w_rh, (w_r - w_rh.astype(F32)).astype(BF16)], axis=-1)
    b_r = jnp.pad(jnp.concatenate([b_router_group, b_router_expert], axis=-1),
                  ((0, 0), (0, LANES - N_GROUPS - N_EXPERTS)))[:, None, :]
    w_out_b = w_out.astype(BF16)
    w_xq_b = w_xq.astype(BF16)
    w_xo_b = w_xo.astype(BF16)
    mix_g = mix_norm[:, None, :]
    cross_g = cross_norm[:, None, :]
    moe_g = moe_norm[:, None, :]

    kmem, vmem = _mem_kv(mem.reshape(batch * mem_len, d_model), mem_norm[None, :],
                         w_xk.astype(BF16), w_xv.astype(BF16), batch, mem_len)

    n_rows = 2 * n + N_EXPERTS * MOE_BLOCK
    nq = seq // FOX_TILE
    h = x.reshape(n, d_model)
    moe = None
    for l in range(depth):
        if moe is None:
            main, logf, kl, dec = _in_proj(h, mix_g, w_main, w_small, w_up, b_f, b_a, l)
        else:
            h, main, logf, kl, dec = _in_proj(h, mix_g, w_main, w_small, w_up, b_f, b_a, l, moe)
        c = _seq_cumsum(logf, batch, seq)
        c5 = c[:, :FOX_HEADS].reshape(batch, nq, FOX_TILE, FOX_HEADS // 2, 2).transpose(0, 3, 1, 4, 2)
        fox = _fox_attention(main, c5, fox_out_gain[l][None, :], batch, seq)
        gla = _gla(main, kl, dec, gla_out_gain[l][None, :], batch, seq)
        h2, hn2, route, cnt = _post(fox, gla, h, w_out_b, cross_g, w_xq_b, kmem, vmem, w_xo_b, moe_g, w_rs, b_r,
                                    seq, mem_len, l)
        dest, block_e, valid = _routing_tables(route, cnt, n_rows)
        xs = _dispatch(dest, hn2, n_rows)
        y = _experts(block_e, valid, xs, w_expert_gate, w_expert_up, w_expert_down, l)
        h, moe = h2, (route[2:4].T, _sc_gather_rows(y, dest))
    return _final(h, moe[0], final_norm[None, :], moe[1]).reshape(batch, seq, d_model)
```

```python
import functools

import jax
import jax.numpy as jnp
from jax import lax
from jax.experimental import pallas as pl
from jax.experimental.pallas import tpu as pltpu
from jax.experimental.pallas import tpu_sc as plsc

F32 = jnp.float32
BF16 = jnp.bfloat16
EPS = 1e-6
LOG2E = 1.4426950408889634

FOX_HEADS = 8
FOX_DIM = 64
FOX_WIDTH = FOX_HEADS * FOX_DIM
GLA_HEADS = 4
GLA_DK = 64
GLA_DV = 128
GLA_QK = GLA_HEADS * GLA_DK
GLA_V = GLA_HEADS * GLA_DV
GLA_RANK = 16
GLA_TAU = 16.0
GLA_CHUNK = 64
X_HEADS = 4
X_DIM = 128
X_WIDTH = X_HEADS * X_DIM
N_GROUPS = 4
GROUP_SIZE = 4
N_EXPERTS = N_GROUPS * GROUP_SIZE
MAIN_WIDTH = 3 * FOX_WIDTH + 2 * GLA_QK + 2 * GLA_V

LANES = 128
ROUTE_WIDTH = 8
ROUTE_ROWS = 32
VMEM_LIMIT = 56 * 1024 * 1024

IN_TILE = 1024
FOX_TILE = 512
FOX_SLAB = 64
POST_TILE = 1024
CUMSUM_TILE = 256
MOE_BLOCK = 512
MOVE_TILE = 1024
SC_GATHER_WINDOW = 64

def _cparams(sem):
    return pltpu.CompilerParams(dimension_semantics=sem, vmem_limit_bytes=VMEM_LIMIT)


def _rms(x, gain):
    return x * lax.rsqrt(jnp.mean(x * x, axis=-1, keepdims=True) + EPS) * gain


def _log_sigmoid(x):
    return jnp.minimum(x, 0.0) - jnp.log1p(jnp.exp(-jnp.abs(x)))


def _dot(a, b):
    return jnp.dot(a, b, preferred_element_type=F32)


def _dot_nt(a, b):
    return lax.dot_general(a, b, (((1,), (1,)), ((), ())), preferred_element_type=F32)


def _pack_rows(x):
    half = x.shape[1] // 2
    lo = lax.bitcast_convert_type(x[:, :half].astype(BF16).astype(F32), jnp.uint32)
    hi = lax.bitcast_convert_type(x[:, half:].astype(BF16).astype(F32), jnp.uint32)
    return (lo >> 16) | hi


def _unpack_rows(w):
    lo = lax.bitcast_convert_type(w << 16, F32)
    hi = lax.bitcast_convert_type(w & jnp.uint32(0xFFFF0000), F32)
    return lo, hi


def _split3(x):
    hi = x.astype(BF16)
    r1 = x - hi.astype(F32)
    mid = r1.astype(BF16)
    lo = (r1 - mid.astype(F32)).astype(BF16)
    return hi, mid, lo


def _mem_kv_kernel(mem_ref, gain_ref, wk_ref, wv_ref, k_ref, v_ref):
    mn = _rms(mem_ref[...], gain_ref[...]).astype(BF16)
    for l in range(wk_ref.shape[0]):
        k_ref[l] = _dot(mn, wk_ref[l]).astype(BF16)
        v_ref[l] = _dot(mn, wv_ref[l]).astype(BF16)


def _mem_kv(mem2d, gain, wk, wv, batch, mem_len):
    depth, d_model, width = wk.shape
    out = jax.ShapeDtypeStruct((depth, batch * mem_len, width), BF16)
    return pl.pallas_call(
        _mem_kv_kernel,
        grid=(batch,),
        in_specs=[
            pl.BlockSpec((mem_len, d_model), lambda b: (b, 0)),
            pl.BlockSpec((1, d_model), lambda b: (0, 0)),
            pl.BlockSpec((depth, d_model, width), lambda b: (0, 0, 0)),
            pl.BlockSpec((depth, d_model, width), lambda b: (0, 0, 0)),
        ],
        out_specs=[
            pl.BlockSpec((depth, mem_len, width), lambda b: (0, b, 0)),
            pl.BlockSpec((depth, mem_len, width), lambda b: (0, b, 0)),
        ],
        out_shape=[out, out],
        compiler_params=_cparams(("arbitrary",)),
        name="mem_kv",
    )(mem2d, gain, wk, wv)


def _moe_sum(h, gate, y0_packed, y1_packed):
    y0 = jnp.concatenate(_unpack_rows(y0_packed), axis=1)
    y1 = jnp.concatenate(_unpack_rows(y1_packed), axis=1)
    return h + gate[:, 0:1] * y0 + gate[:, 1:2] * y1


def _in_proj_body(h, gain_ref, wmain_ref, wsmall_ref, wup_ref, bf_ref, ba_ref, main_ref, logf_ref, kl_ref, dec_ref):
    tm = h.shape[0]
    cs = GLA_CHUNK
    xn = _rms(h, gain_ref[...]).astype(BF16)
    small = _dot(xn, wsmall_ref[...])
    lane = lax.broadcasted_iota(jnp.int32, small.shape, 1)
    logf_ref[...] = jnp.where(lane < FOX_HEADS, _log_sigmoid(small + bf_ref[...]), 0.0)
    a = _dot(small.astype(BF16), wup_ref[...]) + ba_ref[...]
    b = _log_sigmoid(a) * (1.0 / GLA_TAU)
    pos = lax.broadcasted_iota(jnp.int32, b.shape, 0) % cs
    shift = 1
    while shift < cs:
        b = b + jnp.where(pos >= shift, pltpu.roll(b, shift, axis=0), 0.0)
        shift *= 2
    dec = jnp.exp(b.reshape(tm // cs, cs, GLA_QK)[:, cs - 1:cs, :])
    q0 = 3 * FOX_WIDTH
    k0 = q0 + GLA_QK
    qk = _dot(xn, wmain_ref[:, q0:k0 + GLA_QK])
    main_ref[:, q0:k0] = (qk[:, :GLA_QK] * jnp.exp(b) * (GLA_DK ** -0.5)).astype(BF16)
    ke = qk[:, GLA_QK:] * jnp.exp(-b)
    main_ref[:, k0:k0 + GLA_QK] = ke.astype(BF16)
    kl_ref[...] = (ke.reshape(tm // cs, cs, GLA_QK) * dec).reshape(tm, GLA_QK).astype(BF16)
    dec_ref[...] = dec.reshape(tm // cs, GLA_QK)
    step = 512
    for lo in list(range(0, q0, step)) + list(range(k0 + GLA_QK, MAIN_WIDTH, step)):
        main_ref[:, lo:lo + step] = _dot(xn, wmain_ref[:, lo:lo + step]).astype(BF16)


def _in_proj_kernel(h_ref, *refs):
    _in_proj_body(h_ref[...], *refs)


def _in_proj_after_moe_kernel(h_ref, gate_ref, y0_ref, y1_ref, gain_ref, wmain_ref, wsmall_ref, wup_ref, bf_ref,
                              ba_ref, hout_ref, main_ref, logf_ref, kl_ref, dec_ref):
    h = _moe_sum(h_ref[...], gate_ref[...], y0_ref[...], y1_ref[...])
    hout_ref[...] = h
    _in_proj_body(h, gain_ref, wmain_ref, wsmall_ref, wup_ref, bf_ref, ba_ref, main_ref, logf_ref, kl_ref, dec_ref)


def _in_proj(h, gain, wmain, wsmall, wup, bf, ba, layer, moe=None):
    n, d_model = h.shape
    tm = IN_TILE
    steps = n // tm
    chunks = tm // GLA_CHUNK
    pick = lambda i: (layer, 0, 0)
    row_block = lambda width, rows=tm: pl.BlockSpec((rows, width), lambda i: (i, 0))
    weight_specs = [
        pl.BlockSpec((None, 1, d_model), pick),
        pl.BlockSpec((None, d_model, MAIN_WIDTH), pick),
        pl.BlockSpec((None, d_model, LANES), pick),
        pl.BlockSpec((None, LANES, GLA_QK), pick),
        pl.BlockSpec((None, 1, LANES), pick),
        pl.BlockSpec((None, 1, GLA_QK), pick),
    ]
    out_specs = [row_block(MAIN_WIDTH), row_block(LANES), row_block(GLA_QK), row_block(GLA_QK, chunks)]
    out_shape = [
        jax.ShapeDtypeStruct((n, MAIN_WIDTH), BF16),
        jax.ShapeDtypeStruct((n, LANES), F32),
        jax.ShapeDtypeStruct((n, GLA_QK), BF16),
        jax.ShapeDtypeStruct((n // GLA_CHUNK, GLA_QK), F32),
    ]
    weights = (gain, wmain, wsmall, wup, bf, ba)
    if moe is None:
        return pl.pallas_call(
            _in_proj_kernel, grid=(steps,), in_specs=[row_block(d_model)] + weight_specs,
            out_specs=out_specs, out_shape=out_shape,
            compiler_params=_cparams(("arbitrary",)), name="in_proj",
        )(h, *weights)
    gates, picked = moe
    half = d_model // 2
    return pl.pallas_call(
        _in_proj_after_moe_kernel, grid=(steps,),
        in_specs=[row_block(d_model), row_block(2), row_block(half),
                  pl.BlockSpec((tm, half), lambda i: (i + steps, 0))] + weight_specs,
        out_specs=[row_block(d_model)] + out_specs,
        out_shape=[jax.ShapeDtypeStruct((n, d_model), F32)] + out_shape,
        compiler_params=_cparams(("arbitrary",)), name="in_proj_after_moe",
    )(h, gates, picked, picked, *weights)


def _cumsum_kernel(x_ref, o_ref):
    t = CUMSUM_TILE
    row = lax.broadcasted_iota(jnp.int32, (t, t), 0)
    col = lax.broadcasted_iota(jnp.int32, (t, t), 1)
    tril = (row >= col).astype(BF16)
    carry = jnp.zeros((1, x_ref.shape[1]), F32)
    for j in range(x_ref.shape[0] // t):
        hi, mid, lo = _split3(x_ref[j * t:(j + 1) * t, :])
        c = _dot(tril, hi) + _dot(tril, mid) + _dot(tril, lo) + carry
        o_ref[j * t:(j + 1) * t, :] = c
        carry = c[t - 1:t, :]


def _seq_cumsum(x, batch, seq):
    return pl.pallas_call(
        _cumsum_kernel,
        grid=(batch,),
        in_specs=[pl.BlockSpec((seq, LANES), lambda b: (b, 0))],
        out_specs=pl.BlockSpec((seq, LANES), lambda b: (b, 0)),
        out_shape=jax.ShapeDtypeStruct(x.shape, F32),
        compiler_params=_cparams(("arbitrary",)),
        name="forget_cumsum",
    )(x)


def _fox_kernel(q_ref, k_ref, v_ref, c_ref, gain_ref, o_ref, q2_ref, s_ref, p_ref, alpha_ref, m_ref, l_ref, acc_ref):
    tq = FOX_TILE
    rows = 2 * tq
    slab = FOX_SLAB
    nq = q_ref.shape[0] // tq
    lane = lax.broadcasted_iota(jnp.int32, (1, LANES), 1)
    first = lane < FOX_DIM
    scale = FOX_DIM ** -0.5 * LOG2E
    for qi in range(nq):
        q = q_ref[qi * tq:(qi + 1) * tq, :].astype(F32) * scale
        q2_ref[qi, :tq, :] = jnp.where(first, q, 0.0).astype(BF16)
        q2_ref[qi, tq:, :] = jnp.where(first, 0.0, q).astype(BF16)

    def scores(qi, j):
        cj = c_ref[j] * LOG2E
        d = _dot_nt(q2_ref[qi], k_ref[j * tq:(j + 1) * tq, :])
        s_ref[:tq, :] = d[:tq] - cj[0:1, :]
        s_ref[tq:, :] = d[tq:] - cj[1:2, :]

    def weighted_values(qi, j):
        par = qi % 2
        acc_ref[par] = alpha_ref[par] * acc_ref[par] + _dot(p_ref[...], v_ref[j * tq:(j + 1) * tq, :])

    def softmax(qi, masked):
        par = qi % 2
        for r in range(rows // slab):
            sl = slice(r * slab, (r + 1) * slab)
            s = s_ref[sl, :]
            if masked:
                row = lax.broadcasted_iota(jnp.int32, (slab, tq), 0) + (r * slab) % tq
                col = lax.broadcasted_iota(jnp.int32, (slab, tq), 1)
                s = jnp.where(row >= col, s, -jnp.inf)
            m_old = m_ref[par, sl, :]
            m_new = jnp.maximum(m_old, jnp.max(s, axis=-1, keepdims=True))
            alpha = jnp.exp2(m_old - m_new)
            p = jnp.exp2(s - jnp.concatenate([m_new] * (tq // LANES), axis=1))
            l_ref[par, sl, :] = alpha * l_ref[par, sl, :] + jnp.sum(p, axis=-1, keepdims=True)
            m_ref[par, sl, :] = m_new
            alpha_ref[par, sl, :] = alpha
            p_ref[sl, :] = p.astype(BF16)

    def finalize(qi):
        par = qi % 2
        o2 = acc_ref[par] / l_ref[par]
        o = jnp.where(first, o2[:tq], o2[tq:])
        sq = o * o
        ss0 = jnp.sum(jnp.where(first, sq, 0.0), axis=-1, keepdims=True)
        ss1 = jnp.sum(jnp.where(first, 0.0, sq), axis=-1, keepdims=True)
        ms = jnp.where(first, ss0, ss1) * (1.0 / FOX_DIM)
        o_ref[qi * tq:(qi + 1) * tq, :] = (o * lax.rsqrt(ms + EPS) * gain_ref[...]).astype(BF16)

    steps = [(qi, j) for qi in range(nq) for j in range(qi + 1)]
    scores(*steps[0])
    for t, (qi, j) in enumerate(steps):
        if t > 0:
            weighted_values(*steps[t - 1])
            if steps[t - 1][0] != qi:
                finalize(steps[t - 1][0])
        if j == 0:
            par = qi % 2
            m_ref[par] = jnp.full(m_ref.shape[1:], -jnp.inf, F32)
            l_ref[par] = jnp.zeros(l_ref.shape[1:], F32)
            acc_ref[par] = jnp.zeros(acc_ref.shape[1:], F32)
        softmax(qi, masked=(j == qi))
        if t + 1 < len(steps):
            scores(*steps[t + 1])
    weighted_values(*steps[-1])
    finalize(steps[-1][0])


def _fox_attention(main, c5, gain, batch, seq):
    n = main.shape[0]
    tq = FOX_TILE
    nq = seq // tq
    pairs = FOX_HEADS // 2
    k_off = FOX_WIDTH // LANES
    v_off = 2 * FOX_WIDTH // LANES
    stat = pltpu.VMEM((2, 2 * tq, LANES), F32)
    return pl.pallas_call(
        _fox_kernel,
        grid=(batch, pairs),
        in_specs=[
            pl.BlockSpec((seq, LANES), lambda b, p: (b, p)),
            pl.BlockSpec((seq, LANES), lambda b, p: (b, k_off + p)),
            pl.BlockSpec((seq, LANES), lambda b, p: (b, v_off + p)),
            pl.BlockSpec((None, None, nq, 2, tq), lambda b, p: (b, p, 0, 0, 0)),
            pl.BlockSpec((1, LANES), lambda b, p: (0, p)),
        ],
        out_specs=pl.BlockSpec((seq, LANES), lambda b, p: (b, p)),
        out_shape=jax.ShapeDtypeStruct((n, FOX_WIDTH), BF16),
        scratch_shapes=[
            pltpu.VMEM((nq, 2 * tq, LANES), BF16),
            pltpu.VMEM((2 * tq, tq), F32),
            pltpu.VMEM((2 * tq, tq), BF16),
            stat, stat, stat, stat,
        ],
        compiler_params=_cparams(("arbitrary", "arbitrary")),
        name="fox_attention",
    )(main, main, main, c5, gain)


def _gla_kernel(qe_ref, ke_ref, v_ref, kl_ref, dec_ref, o_ref):
    seq = qe_ref.shape[0]
    cs = GLA_CHUNK
    nc = seq // cs
    width = 2 * GLA_DK

    lane = lax.broadcasted_iota(jnp.int32, (1, width), 1)
    first = lane < GLA_DK
    row = lax.broadcasted_iota(jnp.int32, (2 * cs, cs), 0)
    col = lax.broadcasted_iota(jnp.int32, (2 * cs, cs), 1)
    tril2 = jnp.where(row >= cs, row - cs, row) >= col
    srow = lax.broadcasted_iota(jnp.int32, (2 * GLA_DV, width), 0)
    scol = lax.broadcasted_iota(jnp.int32, (2 * GLA_DV, width), 1)
    same_head = (srow >= GLA_DV) == (scol >= GLA_DK)
    unroll = 8

    def chunks(ci, st):
        r0s = [pl.multiple_of((ci * unroll + u) * cs, cs) for u in range(unroll)]
        qes = [qe_ref[pl.ds(r0, cs), :] for r0 in r0s]
        vs = [v_ref[pl.ds(r0, cs), :] for r0 in r0s]
        atts, upds = [], []
        for u in range(unroll):
            zero = jnp.zeros_like(qes[u])
            q2 = jnp.concatenate([jnp.where(first, qes[u], zero), jnp.where(first, zero, qes[u])], axis=0)
            atts.append(jnp.where(tril2, _dot_nt(q2, ke_ref[pl.ds(r0s[u], cs), :]), 0.0).astype(BF16))
        for u in range(unroll):
            upds.append(lax.dot_general(vs[u], kl_ref[pl.ds(r0s[u], cs), :], (((0,), (0,)), ((), ())),
                                        preferred_element_type=F32))
        ois = [_dot(atts[u], vs[u]) for u in range(unroll)]
        for u in range(unroll):
            o = _dot_nt(qes[u], st.astype(BF16))
            o = o + jnp.concatenate([ois[u][:cs, :GLA_DV], ois[u][cs:, GLA_DV:]], axis=1)
            o_ref[pl.ds(r0s[u], cs), :] = o.astype(BF16)
            st = st * dec_ref[pl.ds(ci * unroll + u, 1), :] + jnp.where(same_head, upds[u], 0.0)
        return st

    lax.fori_loop(0, nc // unroll, chunks, jnp.zeros((2 * GLA_DV, width), F32))


def _gla(main, kl, dec, batch, seq):
    n = main.shape[0]
    pairs = GLA_HEADS // 2
    q_off = 3 * FOX_WIDTH // LANES
    k_off = q_off + GLA_QK // LANES
    pv = 2 * GLA_DV
    v_off = (3 * FOX_WIDTH + 2 * GLA_QK) // pv
    return pl.pallas_call(
        _gla_kernel,
        grid=(batch, pairs),
        in_specs=[
            pl.BlockSpec((seq, LANES), lambda b, p: (b, q_off + p)),
            pl.BlockSpec((seq, LANES), lambda b, p: (b, k_off + p)),
            pl.BlockSpec((seq, pv), lambda b, p: (b, v_off + p)),
            pl.BlockSpec((seq, LANES), lambda b, p: (b, p)),
            pl.BlockSpec((seq // GLA_CHUNK, LANES), lambda b, p: (b, p)),
        ],
        out_specs=pl.BlockSpec((seq, pv), lambda b, p: (b, p)),
        out_shape=jax.ShapeDtypeStruct((n, GLA_V), BF16),
        compiler_params=_cparams(("arbitrary", "arbitrary")),
        name="gla",
    )(main, main, main, kl, dec)


def _post_kernel(fox_ref, gla_ref, gg_ref, h_ref, gg_gain_ref, wout_ref, cg_ref, wxq_ref, k_ref, v_ref, wxo_ref,
                 mg_ref, wr_ref, br_ref, h2_ref, hn_ref, route_ref, cnt_ref, carry_ref):
    tm = h_ref.shape[0]

    @pl.when(pl.program_id(0) == 0)
    def _():
        carry_ref[...] = jnp.zeros_like(carry_ref)

    raw = gla_ref[...].astype(F32)
    normed = []
    for hh in range(GLA_HEADS):
        oh = raw[:, hh * GLA_DV:(hh + 1) * GLA_DV]
        normed.append(oh * lax.rsqrt(jnp.mean(oh * oh, axis=-1, keepdims=True) + EPS))
    g = gg_ref[...].astype(F32)
    gla = (jnp.concatenate(normed, axis=1) * gg_gain_ref[...] * (g * jax.nn.sigmoid(g))).astype(BF16)
    y = _dot(fox_ref[...], wout_ref[0:FOX_WIDTH, :]) + _dot(gla, wout_ref[FOX_WIDTH:, :])
    h1 = h_ref[...] + y
    hn = _rms(h1, cg_ref[...]).astype(BF16)
    q = _dot(hn, wxq_ref[...]).astype(BF16)
    xscale = X_DIM ** -0.5
    heads = []
    for hh in range(X_HEADS):
        sl = slice(hh * X_DIM, (hh + 1) * X_DIM)
        s = _dot_nt(q[:, sl], k_ref[:, sl]) * xscale
        p = jnp.exp(s - jnp.max(s, axis=-1, keepdims=True))
        heads.append(_dot(p.astype(BF16), v_ref[:, sl]) / jnp.sum(p, axis=-1, keepdims=True))
    o = jnp.concatenate(heads, axis=1).astype(BF16)
    h2 = h1 + _dot(o, wxo_ref[...])
    h2_ref[...] = h2
    hn2 = _rms(h2, mg_ref[...])
    hn_ref[...] = _pack_rows(hn2)

    xh = hn2.astype(BF16)
    xl = (hn2 - xh.astype(F32)).astype(BF16)
    both_w = _dot(jnp.concatenate([xh, xl], axis=0), wr_ref[...])
    logits = both_w[:tm, :LANES] + both_w[:tm, LANES:] + both_w[tm:, :LANES] + both_w[tm:, LANES:] + br_ref[...]
    lt = jnp.transpose(logits)[:ROUTE_ROWS, :]
    row = lax.broadcasted_iota(jnp.int32, (ROUTE_ROWS, tm), 0)
    neg = -jnp.inf
    gl = jnp.where(row < N_GROUPS, lt, neg)
    gmax = jnp.max(gl, axis=0, keepdims=True)
    ge = jnp.exp(gl - gmax)
    gprob = ge / jnp.sum(ge, axis=0, keepdims=True)
    pmax = jnp.max(gprob, axis=0, keepdims=True)
    grp = jnp.min(jnp.where(gprob == pmax, row, ROUTE_ROWS), axis=0, keepdims=True)
    in_grp = (row >= N_GROUPS) & (row < N_GROUPS + N_EXPERTS) & (((row - N_GROUPS) // GROUP_SIZE) == grp)
    el = jnp.where(in_grp, lt, neg)
    emax = jnp.max(el, axis=0, keepdims=True)
    ee = jnp.exp(el - emax)
    eprob = ee / jnp.sum(ee, axis=0, keepdims=True)
    p1 = jnp.max(eprob, axis=0, keepdims=True)
    row1 = jnp.min(jnp.where(in_grp & (eprob == p1), row, ROUTE_ROWS), axis=0, keepdims=True)
    rest = jnp.where(in_grp & (row != row1), eprob, -1.0)
    p2 = jnp.max(rest, axis=0, keepdims=True)
    row2 = jnp.min(jnp.where(rest == p2, row, ROUTE_ROWS), axis=0, keepdims=True)
    g1 = pmax * p1 / (p1 + p2)
    g2 = pmax * p2 / (p1 + p2)

    oh1 = row == row1
    oh2 = row == row2
    both = (oh1 | oh2).astype(BF16)
    srow = lax.broadcasted_iota(jnp.int32, (tm, tm), 0)
    scol = lax.broadcasted_iota(jnp.int32, (tm, tm), 1)
    earlier = (srow < scol).astype(BF16)
    carry = carry_ref[...]
    seen = _dot(both, earlier) + jnp.concatenate([carry] * (tm // LANES), axis=1)
    rank1 = jnp.sum(jnp.where(oh1, seen, 0.0), axis=0, keepdims=True)
    rank2 = jnp.sum(jnp.where(oh2, seen, 0.0), axis=0, keepdims=True)
    carry = carry + jnp.sum(both.astype(F32), axis=1, keepdims=True)
    carry_ref[...] = carry
    cnt_ref[...] = carry

    e1 = (row1 - N_GROUPS).astype(F32)
    e2 = (row2 - N_GROUPS).astype(F32)
    zero = jnp.zeros_like(g1)
    route_ref[...] = jnp.concatenate([e1, e2, g1, g2, rank1, rank2, zero, zero], axis=0)


def _post(fox, gla, main, h, gla_gain, wout, cg, wxq, kmem, vmem, wxo, mg, wr, br, seq, mem_len, layer):
    n, d_model = h.shape
    tm = POST_TILE
    per_seq = seq // tm
    const = lambda i: (0, 0)
    pick = lambda i: (layer, 0, 0)
    return pl.pallas_call(
        _post_kernel,
        grid=(n // tm,),
        in_specs=[
            pl.BlockSpec((tm, FOX_WIDTH), lambda i: (i, 0)),
            pl.BlockSpec((tm, GLA_V), lambda i: (i, 0)),
            pl.BlockSpec((tm, GLA_V), lambda i: (i, MAIN_WIDTH // GLA_V - 1)),
            pl.BlockSpec((tm, d_model), lambda i: (i, 0)),
            pl.BlockSpec((None, 1, GLA_V), pick),
            pl.BlockSpec((None, FOX_WIDTH + GLA_V, d_model), pick),
            pl.BlockSpec((None, 1, d_model), pick),
            pl.BlockSpec((None, d_model, X_WIDTH), pick),
            pl.BlockSpec((None, mem_len, X_WIDTH), lambda i: (layer, i // per_seq, 0)),
            pl.BlockSpec((None, mem_len, X_WIDTH), lambda i: (layer, i // per_seq, 0)),
            pl.BlockSpec((None, X_WIDTH, d_model), pick),
            pl.BlockSpec((None, 1, d_model), pick),
            pl.BlockSpec((None, d_model, 2 * LANES), pick),
            pl.BlockSpec((None, 1, LANES), pick),
        ],
        out_specs=[
            pl.BlockSpec((tm, d_model), lambda i: (i, 0)),
            pl.BlockSpec((tm, d_model // 2), lambda i: (i, 0)),
            pl.BlockSpec((ROUTE_WIDTH, tm), lambda i: (0, i)),
            pl.BlockSpec((ROUTE_ROWS, LANES), const),
        ],
        out_shape=[
            jax.ShapeDtypeStruct((n, d_model), F32),
            jax.ShapeDtypeStruct((n, d_model // 2), jnp.uint32),
            jax.ShapeDtypeStruct((ROUTE_WIDTH, n), F32),
            jax.ShapeDtypeStruct((ROUTE_ROWS, LANES), F32),
        ],
        scratch_shapes=[pltpu.VMEM((ROUTE_ROWS, LANES), F32)],
        compiler_params=_cparams(("arbitrary",)),
        name="post_mixer",
    )(fox, gla, main, h, gla_gain, wout, cg, wxq, kmem, vmem, wxo, mg, wr, br)


def _dispatch(dest_kmajor, x, n_rows):
    n, width = x.shape
    window = SC_GATHER_WINDOW
    mesh = plsc.VectorSubcoreMesh(core_axis_name="core", subcore_axis_name="subcore")
    workers = mesh.num_cores * mesh.num_subcores
    per_worker = n // workers
    assert n % (workers * window) == 0

    steps = per_worker // window
    assert steps % 2 == 0
    index_buf = pltpu.VMEM((window,), jnp.int32)
    row_buf = pltpu.VMEM((window, width), x.dtype)
    dma = pltpu.SemaphoreType.DMA

    @functools.partial(
        pl.kernel, out_type=jax.ShapeDtypeStruct((n_rows, width), x.dtype), mesh=mesh,
        scratch_types=[index_buf, index_buf, index_buf, index_buf, row_buf, row_buf, dma, dma, dma])
    def scatter(x_hbm, idx_hbm, out_hbm, idx0_a, idx1_a, idx0_b, idx1_b, rows_a, rows_b, sem_a, sem_b, sem_out):
        worker = lax.axis_index("subcore") * mesh.num_cores + lax.axis_index("core")
        base = worker * per_worker
        slots = ((idx0_a, idx1_a, rows_a, sem_a), (idx0_b, idx1_b, rows_b, sem_b))

        def load(step, slot):
            idx0, idx1, rows, sem = slots[slot]
            off = pl.multiple_of(base + step * window, window)
            pltpu.sync_copy(idx_hbm.at[pl.ds(off, window)], idx0)
            pltpu.sync_copy(idx_hbm.at[pl.ds(n + off, window)], idx1)
            pltpu.async_copy(x_hbm.at[pl.ds(off, window)], rows, sem)

        def store(slot):
            idx0, idx1, rows, sem = slots[slot]
            pltpu.make_async_copy(x_hbm.at[pl.ds(0, window)], rows, sem).wait()
            first = pltpu.async_copy(rows, out_hbm.at[idx0], sem_out)
            second = pltpu.async_copy(rows, out_hbm.at[idx1], sem_out)
            first.wait()
            second.wait()

        load(0, 0)

        @pl.loop(0, steps, step=2)
        def _(step):
            load(step + 1, 1)
            store(0)

            @pl.when(step + 2 < steps)
            def _():
                load(step + 2, 0)

            store(1)

    return scatter(x, dest_kmajor)


def _expert_kernel(be_ref, valid_ref, fresh_ref, x_ref, wg_ref, wu_ref, wd_ref, y_ref, wg_b, wu_b, wd_b):
    del be_ref
    i = pl.program_id(0)
    valid = valid_ref[i]

    @pl.when(fresh_ref[i] > 0)
    def _():
        wg_b[...] = wg_ref[...].astype(BF16)
        wu_b[...] = wu_ref[...].astype(BF16)
        wd_b[...] = wd_ref[...].astype(BF16)

    @pl.when(valid > 0)
    def _():
        row = lax.broadcasted_iota(jnp.int32, x_ref.shape, 0)
        lo, hi = _unpack_rows(jnp.where(row < valid, x_ref[...], jnp.uint32(0)))
        lo = lo.astype(BF16)
        hi = hi.astype(BF16)
        half = lo.shape[1]
        g = _dot(lo, wg_b[:half, :]) + _dot(hi, wg_b[half:, :])
        u = _dot(lo, wu_b[:half, :]) + _dot(hi, wu_b[half:, :])
        a = (g * jax.nn.sigmoid(g) * u).astype(BF16)
        y_ref[...] = _pack_rows(_dot(a, wd_b[...]))

    @pl.when(valid <= 0)
    def _():
        y_ref[...] = jnp.zeros_like(y_ref)


def _experts(block_e, valid, xs, wg, wu, wd, layer):
    r, width = xs.shape
    bm = MOE_BLOCK
    d_model, d_exp = wg.shape[-2:]
    fresh = jnp.concatenate([jnp.ones((1,), jnp.int32), (block_e[1:] != block_e[:-1]).astype(jnp.int32)])
    pick = lambda i, be, va, fr: (layer, be[i], 0, 0)
    return pl.pallas_call(
        _expert_kernel,
        grid_spec=pltpu.PrefetchScalarGridSpec(
            num_scalar_prefetch=3,
            grid=(r // bm,),
            in_specs=[
                pl.BlockSpec((bm, width), lambda i, be, va, fr: (i, 0)),
                pl.BlockSpec((None, None, d_model, d_exp), pick),
                pl.BlockSpec((None, None, d_model, d_exp), pick),
                pl.BlockSpec((None, None, d_exp, d_model), pick),
            ],
            out_specs=pl.BlockSpec((bm, width), lambda i, be, va, fr: (i, 0)),
            scratch_shapes=[
                pltpu.VMEM((d_model, d_exp), BF16),
                pltpu.VMEM((d_model, d_exp), BF16),
                pltpu.VMEM((d_exp, d_model), BF16),
            ],
        ),
        out_shape=jax.ShapeDtypeStruct((r, width), jnp.uint32),
        compiler_params=_cparams(("arbitrary",)),
        name="moe_experts",
    )(block_e, valid, fresh, xs, wg, wu, wd)


def _sc_gather_rows(table, idx):
    m = idx.shape[0]
    width = table.shape[1]
    window = SC_GATHER_WINDOW
    mesh = plsc.VectorSubcoreMesh(core_axis_name="core", subcore_axis_name="subcore")
    workers = mesh.num_cores * mesh.num_subcores
    per_worker = m // workers
    assert m % (workers * window) == 0

    steps = per_worker // window
    assert steps % 2 == 0
    index_buf = pltpu.VMEM((window,), jnp.int32)
    row_buf = pltpu.VMEM((window, width), table.dtype)
    dma = pltpu.SemaphoreType.DMA

    @functools.partial(
        pl.kernel, out_type=jax.ShapeDtypeStruct((m, width), table.dtype), mesh=mesh,
        scratch_types=[index_buf, index_buf, row_buf, row_buf, dma, dma])
    def gather(table_hbm, idx_hbm, out_hbm, idx_a, idx_b, rows_a, rows_b, sem_a, sem_b):
        worker = lax.axis_index("subcore") * mesh.num_cores + lax.axis_index("core")
        base = worker * per_worker
        slots = ((idx_a, rows_a, sem_a), (idx_b, rows_b, sem_b))

        def fetch(step, slot):
            idx, rows, sem = slots[slot]
            off = pl.multiple_of(base + step * window, window)
            pltpu.sync_copy(idx_hbm.at[pl.ds(off, window)], idx)
            pltpu.async_copy(table_hbm.at[idx], rows, sem)

        def flush(step, slot):
            idx, rows, sem = slots[slot]
            off = pl.multiple_of(base + step * window, window)
            pltpu.make_async_copy(table_hbm.at[idx], rows, sem).wait()
            pltpu.sync_copy(rows, out_hbm.at[pl.ds(off, window)])

        fetch(0, 0)

        @pl.loop(0, steps, step=2)
        def _(step):
            fetch(step + 1, 1)
            flush(step, 0)

            @pl.when(step + 2 < steps)
            def _():
                fetch(step + 2, 0)

            flush(step + 1, 1)

    return gather(table, idx)


def _final_kernel(h_ref, gate_ref, gain_ref, y0_ref, y1_ref, o_ref):
    o_ref[...] = _rms(_moe_sum(h_ref[...], gate_ref[...], y0_ref[...], y1_ref[...]), gain_ref[...])


def _final(h, gates, gain, picked):
    n, d_model = h.shape
    tc = MOVE_TILE
    steps = n // tc
    return pl.pallas_call(
        _final_kernel,
        grid=(steps,),
        in_specs=[
            pl.BlockSpec((tc, d_model), lambda i: (i, 0)),
            pl.BlockSpec((tc, 2), lambda i: (i, 0)),
            pl.BlockSpec((1, d_model), lambda i: (0, 0)),
            pl.BlockSpec((tc, d_model // 2), lambda i: (i, 0)),
            pl.BlockSpec((tc, d_model // 2), lambda i: (i + steps, 0)),
        ],
        out_specs=pl.BlockSpec((tc, d_model), lambda i: (i, 0)),
        out_shape=jax.ShapeDtypeStruct((n, d_model), F32),
        compiler_params=_cparams(("arbitrary",)),
        name="moe_final",
    )(h, gates, gain, picked, picked)


def _routing_tables(route, cnt, n_rows):
    bm = MOE_BLOCK
    expert = route[0:2].astype(jnp.int32)
    rank = route[4:6].astype(jnp.int32)
    counts = cnt[N_GROUPS:N_GROUPS + N_EXPERTS, 0].astype(jnp.int32)
    padded = (counts + bm - 1) // bm * bm
    pad_ends = jnp.cumsum(padded)
    pad_starts = pad_ends - padded
    ids = jnp.arange(N_EXPERTS, dtype=jnp.int32)
    start_of = jnp.sum(jnp.where(expert[..., None] == ids, pad_starts, 0), axis=-1)
    dest = (start_of + rank).reshape(-1).astype(jnp.int32)
    block_row = jnp.arange(n_rows // bm, dtype=jnp.int32) * bm
    block_e = jnp.minimum(jnp.sum((pad_ends[None, :] <= block_row[:, None]).astype(jnp.int32), axis=-1),
                          N_EXPERTS - 1)
    row_end = jnp.sum(jnp.where(block_e[:, None] == ids, pad_starts + counts, 0), axis=-1)
    valid = jnp.clip(row_end - block_row, 0, bm).astype(jnp.int32)
    return dest, block_e, valid


def kernel(x, mem, mem_norm, mix_norm, w_in, b_forget, w_alpha_up, b_alpha, fox_out_gain, gla_out_gain, w_out,
           cross_norm, w_xq, w_xk, w_xv, w_xo, moe_norm, w_router_group, b_router_group, w_router_expert,
           b_router_expert, w_expert_gate, w_expert_up, w_expert_down, final_norm):
    batch, seq, d_model = x.shape
    mem_len = mem.shape[1]
    depth = w_in.shape[0]
    n = batch * seq
    assert seq % FOX_TILE == 0 and seq % IN_TILE == 0 and seq % POST_TILE == 0 and seq % GLA_CHUNK == 0
    assert n % MOVE_TILE == 0 and d_model % LANES == 0

    c0 = 3 * FOX_WIDTH
    c1 = c0 + FOX_HEADS
    c2 = c1 + 2 * GLA_QK + 2 * GLA_V
    w_main = jnp.concatenate([w_in[:, :, :c0], w_in[:, :, c1:c2]], axis=-1).astype(BF16)
    pad = LANES - FOX_HEADS - GLA_RANK
    w_small = jnp.concatenate([w_in[:, :, c0:c1], w_in[:, :, c2:], jnp.zeros((depth, d_model, pad), F32)],
                              axis=-1).astype(BF16)
    w_up = jnp.concatenate([jnp.zeros((depth, FOX_HEADS, GLA_QK), F32), w_alpha_up,
                            jnp.zeros((depth, pad, GLA_QK), F32)], axis=1).astype(BF16)
    b_f = jnp.pad(b_forget, ((0, 0), (0, LANES - FOX_HEADS)))[:, None, :]
    b_a = b_alpha[:, None, :]
    w_r = jnp.concatenate([w_router_group, w_router_expert,
                           jnp.zeros((depth, d_model, LANES - N_GROUPS - N_EXPERTS), F32)], axis=-1)
    w_rh = w_r.astype(BF16)
    w_rs = jnp.concatenate([w_rh, (w_r - w_rh.astype(F32)).astype(BF16)], axis=-1)
    b_r = jnp.pad(jnp.concatenate([b_router_group, b_router_expert], axis=-1),
                  ((0, 0), (0, LANES - N_GROUPS - N_EXPERTS)))[:, None, :]
    w_out_b = w_out.astype(BF16)
    w_xq_b = w_xq.astype(BF16)
    w_xo_b = w_xo.astype(BF16)
    mix_g = mix_norm[:, None, :]
    cross_g = cross_norm[:, None, :]
    moe_g = moe_norm[:, None, :]
    gla_g = gla_out_gain[:, None, :]

    kmem, vmem = _mem_kv(mem.reshape(batch * mem_len, d_model), mem_norm[None, :],
                         w_xk.astype(BF16), w_xv.astype(BF16), batch, mem_len)

    n_rows = 2 * n + N_EXPERTS * MOE_BLOCK
    nq = seq // FOX_TILE
    h = x.reshape(n, d_model)
    moe = None
    for l in range(depth):
        if moe is None:
            main, logf, kl, dec = _in_proj(h, mix_g, w_main, w_small, w_up, b_f, b_a, l)
        else:
            h, main, logf, kl, dec = _in_proj(h, mix_g, w_main, w_small, w_up, b_f, b_a, l, moe)
        c = _seq_cumsum(logf, batch, seq)
        c5 = c[:, :FOX_HEADS].reshape(batch, nq, FOX_TILE, FOX_HEADS // 2, 2).transpose(0, 3, 1, 4, 2)
        fox = _fox_attention(main, c5, fox_out_gain[l][None, :], batch, seq)
        gla = _gla(main, kl, dec, batch, seq)
        h2, hn2, route, cnt = _post(fox, gla, main, h, gla_g, w_out_b, cross_g, w_xq_b, kmem, vmem, w_xo_b, moe_g,
                                    w_rs, b_r, seq, mem_len, l)
        dest, block_e, valid = _routing_tables(route, cnt, n_rows)
        xs = _dispatch(dest, hn2, n_rows)
        y = _experts(block_e, valid, xs, w_expert_gate, w_expert_up, w_expert_down, l)
        h, moe = h2, (route[2:4].T, _sc_gather_rows(y, dest))
    return _final(h, moe[0], final_norm[None, :], moe[1]).reshape(batch, seq, d_model)
```

```python
import functools

import jax
import jax.numpy as jnp
from jax import lax
from jax.experimental import pallas as pl
from jax.experimental.pallas import tpu as pltpu
from jax.experimental.pallas import tpu_sc as plsc

F32 = jnp.float32
BF16 = jnp.bfloat16
EPS = 1e-6
LOG2E = 1.4426950408889634

FOX_HEADS = 8
FOX_DIM = 64
FOX_WIDTH = FOX_HEADS * FOX_DIM
GLA_HEADS = 4
GLA_DK = 64
GLA_DV = 128
GLA_QK = GLA_HEADS * GLA_DK
GLA_V = GLA_HEADS * GLA_DV
GLA_RANK = 16
GLA_TAU = 16.0
GLA_CHUNK = 64
X_HEADS = 4
X_DIM = 128
X_WIDTH = X_HEADS * X_DIM
N_GROUPS = 4
GROUP_SIZE = 4
N_EXPERTS = N_GROUPS * GROUP_SIZE
MAIN_WIDTH = 3 * FOX_WIDTH + 2 * GLA_QK + 2 * GLA_V

LANES = 128
ROUTE_WIDTH = 8
ROUTE_ROWS = 32
VMEM_LIMIT = 56 * 1024 * 1024

IN_TILE = 1024
FOX_TILE = 512
FOX_SLAB = 64
POST_TILE = 1024
MOE_BLOCK = 512
MOVE_TILE = 1024
SC_GATHER_WINDOW = 64

def _cparams(sem):
    return pltpu.CompilerParams(dimension_semantics=sem, vmem_limit_bytes=VMEM_LIMIT)


def _rms(x, gain):
    return x * lax.rsqrt(jnp.mean(x * x, axis=-1, keepdims=True) + EPS) * gain


def _log_sigmoid(x):
    return jnp.minimum(x, 0.0) - jnp.log1p(jnp.exp(-jnp.abs(x)))


def _dot(a, b):
    return jnp.dot(a, b, preferred_element_type=F32)


def _dot_nt(a, b):
    return lax.dot_general(a, b, (((1,), (1,)), ((), ())), preferred_element_type=F32)


def _pack_rows(x):
    half = x.shape[1] // 2
    lo = lax.bitcast_convert_type(x[:, :half].astype(BF16).astype(F32), jnp.uint32)
    hi = lax.bitcast_convert_type(x[:, half:].astype(BF16).astype(F32), jnp.uint32)
    return (lo >> 16) | hi


def _unpack_rows(w):
    lo = lax.bitcast_convert_type(w << 16, F32)
    hi = lax.bitcast_convert_type(w & jnp.uint32(0xFFFF0000), F32)
    return lo, hi


def _mem_kv_kernel(mem_ref, gain_ref, wk_ref, wv_ref, k_ref, v_ref):
    mn = _rms(mem_ref[...], gain_ref[...]).astype(BF16)
    for l in range(wk_ref.shape[0]):
        k_ref[l] = _dot(mn, wk_ref[l]).astype(BF16)
        v_ref[l] = _dot(mn, wv_ref[l]).astype(BF16)


def _mem_kv(mem2d, gain, wk, wv, batch, mem_len):
    depth, d_model, width = wk.shape
    out = jax.ShapeDtypeStruct((depth, batch * mem_len, width), BF16)
    return pl.pallas_call(
        _mem_kv_kernel,
        grid=(batch,),
        in_specs=[
            pl.BlockSpec((mem_len, d_model), lambda b: (b, 0)),
            pl.BlockSpec((1, d_model), lambda b: (0, 0)),
            pl.BlockSpec((depth, d_model, width), lambda b: (0, 0, 0)),
            pl.BlockSpec((depth, d_model, width), lambda b: (0, 0, 0)),
        ],
        out_specs=[
            pl.BlockSpec((depth, mem_len, width), lambda b: (0, b, 0)),
            pl.BlockSpec((depth, mem_len, width), lambda b: (0, b, 0)),
        ],
        out_shape=[out, out],
        compiler_params=_cparams(("arbitrary",)),
        name="mem_kv",
    )(mem2d, gain, wk, wv)


def _moe_sum(h, gate, y0_packed, y1_packed):
    y0 = jnp.concatenate(_unpack_rows(y0_packed), axis=1)
    y1 = jnp.concatenate(_unpack_rows(y1_packed), axis=1)
    return h + gate[:, 0:1] * y0 + gate[:, 1:2] * y1


def _in_proj_body(h, gain_ref, wmain_ref, wsmall_ref, wup_ref, bf_ref, ba_ref, main_ref, c_ref, kl_ref, dec_ref,
                  carry_ref, tiles_per_seq):
    tm = h.shape[0]
    cs = GLA_CHUNK
    xn = _rms(h, gain_ref[...]).astype(BF16)
    small = _dot(xn, wsmall_ref[...])
    lane = lax.broadcasted_iota(jnp.int32, small.shape, 1)
    c = jnp.where(lane < FOX_HEADS, _log_sigmoid(small + bf_ref[...]), 0.0)
    rowi = lax.broadcasted_iota(jnp.int32, c.shape, 0)
    shift = 1
    while shift < tm:
        c = c + jnp.where(rowi >= shift, pltpu.roll(c, shift, axis=0), 0.0)
        shift *= 2
    starts_sequence = pl.program_id(0) % tiles_per_seq == 0
    c = c + jnp.where(starts_sequence, 0.0, carry_ref[...])
    carry_ref[...] = c[tm - 1:tm, :]
    c_ref[...] = jnp.transpose(c)[:FOX_HEADS, :]
    a = _dot(small.astype(BF16), wup_ref[...]) + ba_ref[...]
    b = _log_sigmoid(a) * (1.0 / GLA_TAU)
    pos = lax.broadcasted_iota(jnp.int32, b.shape, 0) % cs
    shift = 1
    while shift < cs:
        b = b + jnp.where(pos >= shift, pltpu.roll(b, shift, axis=0), 0.0)
        shift *= 2
    dec = jnp.exp(b.reshape(tm // cs, cs, GLA_QK)[:, cs - 1:cs, :])
    q0 = 3 * FOX_WIDTH
    k0 = q0 + GLA_QK
    qk = _dot(xn, wmain_ref[:, q0:k0 + GLA_QK])
    main_ref[:, q0:k0] = (qk[:, :GLA_QK] * jnp.exp(b) * (GLA_DK ** -0.5)).astype(BF16)
    ke = qk[:, GLA_QK:] * jnp.exp(-b)
    main_ref[:, k0:k0 + GLA_QK] = ke.astype(BF16)
    kl_ref[...] = (ke.reshape(tm // cs, cs, GLA_QK) * dec).reshape(tm, GLA_QK).astype(BF16)
    dec_ref[...] = dec.reshape(tm // cs, GLA_QK)
    step = 512
    for lo in list(range(0, q0, step)) + list(range(k0 + GLA_QK, MAIN_WIDTH, step)):
        main_ref[:, lo:lo + step] = _dot(xn, wmain_ref[:, lo:lo + step]).astype(BF16)


def _in_proj_kernel(h_ref, *refs, tiles_per_seq):
    _in_proj_body(h_ref[...], *refs, tiles_per_seq)


def _in_proj_after_moe_kernel(h_ref, gate_ref, y0_ref, y1_ref, gain_ref, wmain_ref, wsmall_ref, wup_ref, bf_ref,
                              ba_ref, hout_ref, main_ref, c_ref, kl_ref, dec_ref, carry_ref, *, tiles_per_seq):
    h = _moe_sum(h_ref[...], gate_ref[...], y0_ref[...], y1_ref[...])
    hout_ref[...] = h
    _in_proj_body(h, gain_ref, wmain_ref, wsmall_ref, wup_ref, bf_ref, ba_ref, main_ref, c_ref, kl_ref, dec_ref,
                  carry_ref, tiles_per_seq)


def _in_proj(h, gain, wmain, wsmall, wup, bf, ba, layer, seq, moe=None):
    n, d_model = h.shape
    tm = IN_TILE
    steps = n // tm
    chunks = tm // GLA_CHUNK
    tiles_per_seq = seq // tm
    pick = lambda i: (layer, 0, 0)
    row_block = lambda width, rows=tm: pl.BlockSpec((rows, width), lambda i: (i, 0))
    weight_specs = [
        pl.BlockSpec((None, 1, d_model), pick),
        pl.BlockSpec((None, d_model, MAIN_WIDTH), pick),
        pl.BlockSpec((None, d_model, LANES), pick),
        pl.BlockSpec((None, LANES, GLA_QK), pick),
        pl.BlockSpec((None, 1, LANES), pick),
        pl.BlockSpec((None, 1, GLA_QK), pick),
    ]
    out_specs = [row_block(MAIN_WIDTH), pl.BlockSpec((FOX_HEADS, tm), lambda i: (0, i)), row_block(GLA_QK),
                 row_block(GLA_QK, chunks)]
    out_shape = [
        jax.ShapeDtypeStruct((n, MAIN_WIDTH), BF16),
        jax.ShapeDtypeStruct((FOX_HEADS, n), F32),
        jax.ShapeDtypeStruct((n, GLA_QK), BF16),
        jax.ShapeDtypeStruct((n // GLA_CHUNK, GLA_QK), F32),
    ]
    weights = (gain, wmain, wsmall, wup, bf, ba)
    carry = [pltpu.VMEM((1, LANES), F32)]
    if moe is None:
        return pl.pallas_call(
            functools.partial(_in_proj_kernel, tiles_per_seq=tiles_per_seq),
            grid=(steps,), in_specs=[row_block(d_model)] + weight_specs,
            out_specs=out_specs, out_shape=out_shape, scratch_shapes=carry,
            compiler_params=_cparams(("arbitrary",)), name="in_proj",
        )(h, *weights)
    gates, picked = moe
    half = d_model // 2
    return pl.pallas_call(
        functools.partial(_in_proj_after_moe_kernel, tiles_per_seq=tiles_per_seq), grid=(steps,),
        in_specs=[row_block(d_model), row_block(2), row_block(half),
                  pl.BlockSpec((tm, half), lambda i: (i + steps, 0))] + weight_specs,
        out_specs=[row_block(d_model)] + out_specs,
        out_shape=[jax.ShapeDtypeStruct((n, d_model), F32)] + out_shape, scratch_shapes=carry,
        compiler_params=_cparams(("arbitrary",)), name="in_proj_after_moe",
    )(h, gates, picked, picked, *weights)


def _fox_kernel(q_ref, k_ref, v_ref, c_ref, gain_ref, o_ref, q2_ref, s_ref, p_ref, alpha_ref, m_ref, l_ref, acc_ref):
    tq = FOX_TILE
    rows = 2 * tq
    slab = FOX_SLAB
    nq = q_ref.shape[0] // tq
    lane = lax.broadcasted_iota(jnp.int32, (1, LANES), 1)
    first = lane < FOX_DIM
    scale = FOX_DIM ** -0.5 * LOG2E
    for qi in range(nq):
        q = q_ref[qi * tq:(qi + 1) * tq, :].astype(F32) * scale
        q2_ref[qi, :tq, :] = jnp.where(first, q, 0.0).astype(BF16)
        q2_ref[qi, tq:, :] = jnp.where(first, 0.0, q).astype(BF16)

    head_row = lax.broadcasted_iota(jnp.int32, (FOX_HEADS, tq), 0)
    pair = pl.program_id(1)

    def scores(qi, j):
        cj = c_ref[:, j * tq:(j + 1) * tq] * LOG2E
        c0 = jnp.sum(jnp.where(head_row == 2 * pair, cj, 0.0), axis=0, keepdims=True)
        c1 = jnp.sum(jnp.where(head_row == 2 * pair + 1, cj, 0.0), axis=0, keepdims=True)
        d = _dot_nt(q2_ref[qi], k_ref[j * tq:(j + 1) * tq, :])
        s_ref[:tq, :] = d[:tq] - c0
        s_ref[tq:, :] = d[tq:] - c1

    def weighted_values(qi, j):
        par = qi % 2
        acc_ref[par] = alpha_ref[par] * acc_ref[par] + _dot(p_ref[...], v_ref[j * tq:(j + 1) * tq, :])

    def softmax(qi, masked):
        par = qi % 2
        for r in range(rows // slab):
            sl = slice(r * slab, (r + 1) * slab)
            s = s_ref[sl, :]
            if masked:
                row = lax.broadcasted_iota(jnp.int32, (slab, tq), 0) + (r * slab) % tq
                col = lax.broadcasted_iota(jnp.int32, (slab, tq), 1)
                s = jnp.where(row >= col, s, -jnp.inf)
            m_old = m_ref[par, sl, :]
            m_new = jnp.maximum(m_old, jnp.max(s, axis=-1, keepdims=True))
            alpha = jnp.exp2(m_old - m_new)
            p = jnp.exp2(s - jnp.concatenate([m_new] * (tq // LANES), axis=1))
            l_ref[par, sl, :] = alpha * l_ref[par, sl, :] + jnp.sum(p, axis=-1, keepdims=True)
            m_ref[par, sl, :] = m_new
            alpha_ref[par, sl, :] = alpha
            p_ref[sl, :] = p.astype(BF16)

    def finalize(qi):
        par = qi % 2
        o2 = acc_ref[par] / l_ref[par]
        o = jnp.where(first, o2[:tq], o2[tq:])
        sq = o * o
        ss0 = jnp.sum(jnp.where(first, sq, 0.0), axis=-1, keepdims=True)
        ss1 = jnp.sum(jnp.where(first, 0.0, sq), axis=-1, keepdims=True)
        ms = jnp.where(first, ss0, ss1) * (1.0 / FOX_DIM)
        o_ref[qi * tq:(qi + 1) * tq, :] = (o * lax.rsqrt(ms + EPS) * gain_ref[...]).astype(BF16)

    steps = [(qi, j) for qi in range(nq) for j in range(qi + 1)]
    scores(*steps[0])
    for t, (qi, j) in enumerate(steps):
        if t > 0:
            weighted_values(*steps[t - 1])
            if steps[t - 1][0] != qi:
                finalize(steps[t - 1][0])
        if j == 0:
            par = qi % 2
            m_ref[par] = jnp.full(m_ref.shape[1:], -jnp.inf, F32)
            l_ref[par] = jnp.zeros(l_ref.shape[1:], F32)
            acc_ref[par] = jnp.zeros(acc_ref.shape[1:], F32)
        softmax(qi, masked=(j == qi))
        if t + 1 < len(steps):
            scores(*steps[t + 1])
    weighted_values(*steps[-1])
    finalize(steps[-1][0])


def _fox_attention(main, c, gain, batch, seq):
    n = main.shape[0]
    tq = FOX_TILE
    nq = seq // tq
    pairs = FOX_HEADS // 2
    k_off = FOX_WIDTH // LANES
    v_off = 2 * FOX_WIDTH // LANES
    stat = pltpu.VMEM((2, 2 * tq, LANES), F32)
    return pl.pallas_call(
        _fox_kernel,
        grid=(batch, pairs),
        in_specs=[
            pl.BlockSpec((seq, LANES), lambda b, p: (b, p)),
            pl.BlockSpec((seq, LANES), lambda b, p: (b, k_off + p)),
            pl.BlockSpec((seq, LANES), lambda b, p: (b, v_off + p)),
            pl.BlockSpec((FOX_HEADS, seq), lambda b, p: (0, b)),
            pl.BlockSpec((1, LANES), lambda b, p: (0, p)),
        ],
        out_specs=pl.BlockSpec((seq, LANES), lambda b, p: (b, p)),
        out_shape=jax.ShapeDtypeStruct((n, FOX_WIDTH), BF16),
        scratch_shapes=[
            pltpu.VMEM((nq, 2 * tq, LANES), BF16),
            pltpu.VMEM((2 * tq, tq), F32),
            pltpu.VMEM((2 * tq, tq), BF16),
            stat, stat, stat, stat,
        ],
        compiler_params=_cparams(("arbitrary", "arbitrary")),
        name="fox_attention",
    )(main, main, main, c, gain)


def _gla_kernel(qe_ref, ke_ref, v_ref, kl_ref, dec_ref, o_ref):
    seq = qe_ref.shape[0]
    cs = GLA_CHUNK
    nc = seq // cs
    width = 2 * GLA_DK

    lane = lax.broadcasted_iota(jnp.int32, (1, width), 1)
    first = lane < GLA_DK
    row = lax.broadcasted_iota(jnp.int32, (2 * cs, cs), 0)
    col = lax.broadcasted_iota(jnp.int32, (2 * cs, cs), 1)
    tril2 = jnp.where(row >= cs, row - cs, row) >= col
    srow = lax.broadcasted_iota(jnp.int32, (2 * GLA_DV, width), 0)
    scol = lax.broadcasted_iota(jnp.int32, (2 * GLA_DV, width), 1)
    same_head = (srow >= GLA_DV) == (scol >= GLA_DK)
    unroll = 8

    def chunks(ci, st):
        r0s = [pl.multiple_of((ci * unroll + u) * cs, cs) for u in range(unroll)]
        qes = [qe_ref[pl.ds(r0, cs), :] for r0 in r0s]
        vs = [v_ref[pl.ds(r0, cs), :] for r0 in r0s]
        atts, upds = [], []
        for u in range(unroll):
            zero = jnp.zeros_like(qes[u])
            q2 = jnp.concatenate([jnp.where(first, qes[u], zero), jnp.where(first, zero, qes[u])], axis=0)
            atts.append(jnp.where(tril2, _dot_nt(q2, ke_ref[pl.ds(r0s[u], cs), :]), 0.0).astype(BF16))
        for u in range(unroll):
            upds.append(lax.dot_general(vs[u], kl_ref[pl.ds(r0s[u], cs), :], (((0,), (0,)), ((), ())),
                                        preferred_element_type=F32))
        ois = [_dot(atts[u], vs[u]) for u in range(unroll)]
        for u in range(unroll):
            o = _dot_nt(qes[u], st.astype(BF16))
            o = o + jnp.concatenate([ois[u][:cs, :GLA_DV], ois[u][cs:, GLA_DV:]], axis=1)
            o_ref[pl.ds(r0s[u], cs), :] = o.astype(BF16)
            st = st * dec_ref[pl.ds(ci * unroll + u, 1), :] + jnp.where(same_head, upds[u], 0.0)
        return st

    lax.fori_loop(0, nc // unroll, chunks, jnp.zeros((2 * GLA_DV, width), F32))


def _gla(main, kl, dec, batch, seq):
    n = main.shape[0]
    pairs = GLA_HEADS // 2
    q_off = 3 * FOX_WIDTH // LANES
    k_off = q_off + GLA_QK // LANES
    pv = 2 * GLA_DV
    v_off = (3 * FOX_WIDTH + 2 * GLA_QK) // pv
    return pl.pallas_call(
        _gla_kernel,
        grid=(batch, pairs),
        in_specs=[
            pl.BlockSpec((seq, LANES), lambda b, p: (b, q_off + p)),
            pl.BlockSpec((seq, LANES), lambda b, p: (b, k_off + p)),
            pl.BlockSpec((seq, pv), lambda b, p: (b, v_off + p)),
            pl.BlockSpec((seq, LANES), lambda b, p: (b, p)),
            pl.BlockSpec((seq // GLA_CHUNK, LANES), lambda b, p: (b, p)),
        ],
        out_specs=pl.BlockSpec((seq, pv), lambda b, p: (b, p)),
        out_shape=jax.ShapeDtypeStruct((n, GLA_V), BF16),
        compiler_params=_cparams(("arbitrary", "arbitrary")),
        name="gla",
    )(main, main, main, kl, dec)


def _post_kernel(fox_ref, gla_ref, gg_ref, h_ref, gg_gain_ref, wout_ref, cg_ref, wxq_ref, k_ref, v_ref, wxo_ref,
                 mg_ref, wr_ref, br_ref, h2_ref, hn_ref, route_ref, cnt_ref, carry_ref):
    tm = h_ref.shape[0]

    @pl.when(pl.program_id(0) == 0)
    def _():
        carry_ref[...] = jnp.zeros_like(carry_ref)

    raw = gla_ref[...].astype(F32)
    normed = []
    for hh in range(GLA_HEADS):
        oh = raw[:, hh * GLA_DV:(hh + 1) * GLA_DV]
        normed.append(oh * lax.rsqrt(jnp.mean(oh * oh, axis=-1, keepdims=True) + EPS))
    g = gg_ref[...].astype(F32)
    gla = (jnp.concatenate(normed, axis=1) * gg_gain_ref[...] * (g * jax.nn.sigmoid(g))).astype(BF16)
    y = _dot(fox_ref[...], wout_ref[0:FOX_WIDTH, :]) + _dot(gla, wout_ref[FOX_WIDTH:, :])
    h1 = h_ref[...] + y
    hn = _rms(h1, cg_ref[...]).astype(BF16)
    q = _dot(hn, wxq_ref[...]).astype(BF16)
    xscale = X_DIM ** -0.5
    heads = []
    for hh in range(X_HEADS):
        sl = slice(hh * X_DIM, (hh + 1) * X_DIM)
        s = _dot_nt(q[:, sl], k_ref[:, sl]) * xscale
        p = jnp.exp(s - jnp.max(s, axis=-1, keepdims=True))
        heads.append(_dot(p.astype(BF16), v_ref[:, sl]) / jnp.sum(p, axis=-1, keepdims=True))
    o = jnp.concatenate(heads, axis=1).astype(BF16)
    h2 = h1 + _dot(o, wxo_ref[...])
    h2_ref[...] = h2
    hn2 = _rms(h2, mg_ref[...])
    hn_ref[...] = _pack_rows(hn2)

    xh = hn2.astype(BF16)
    xl = (hn2 - xh.astype(F32)).astype(BF16)
    both_w = _dot(jnp.concatenate([xh, xl], axis=0), wr_ref[...])
    logits = both_w[:tm, :LANES] + both_w[:tm, LANES:] + both_w[tm:, :LANES] + both_w[tm:, LANES:] + br_ref[...]
    lt = jnp.transpose(logits)[:ROUTE_ROWS, :]
    row = lax.broadcasted_iota(jnp.int32, (ROUTE_ROWS, tm), 0)
    neg = -jnp.inf
    gl = jnp.where(row < N_GROUPS, lt, neg)
    gmax = jnp.max(gl, axis=0, keepdims=True)
    ge = jnp.exp(gl - gmax)
    gprob = ge / jnp.sum(ge, axis=0, keepdims=True)
    pmax = jnp.max(gprob, axis=0, keepdims=True)
    grp = jnp.min(jnp.where(gprob == pmax, row, ROUTE_ROWS), axis=0, keepdims=True)
    in_grp = (row >= N_GROUPS) & (row < N_GROUPS + N_EXPERTS) & (((row - N_GROUPS) // GROUP_SIZE) == grp)
    el = jnp.where(in_grp, lt, neg)
    emax = jnp.max(el, axis=0, keepdims=True)
    ee = jnp.exp(el - emax)
    eprob = ee / jnp.sum(ee, axis=0, keepdims=True)
    p1 = jnp.max(eprob, axis=0, keepdims=True)
    row1 = jnp.min(jnp.where(in_grp & (eprob == p1), row, ROUTE_ROWS), axis=0, keepdims=True)
    rest = jnp.where(in_grp & (row != row1), eprob, -1.0)
    p2 = jnp.max(rest, axis=0, keepdims=True)
    row2 = jnp.min(jnp.where(rest == p2, row, ROUTE_ROWS), axis=0, keepdims=True)
    g1 = pmax * p1 / (p1 + p2)
    g2 = pmax * p2 / (p1 + p2)

    oh1 = row == row1
    oh2 = row == row2
    both = (oh1 | oh2).astype(BF16)
    srow = lax.broadcasted_iota(jnp.int32, (tm, tm), 0)
    scol = lax.broadcasted_iota(jnp.int32, (tm, tm), 1)
    earlier = (srow < scol).astype(BF16)
    carry = carry_ref[...]
    seen = _dot(both, earlier) + jnp.concatenate([carry] * (tm // LANES), axis=1)
    rank1 = jnp.sum(jnp.where(oh1, seen, 0.0), axis=0, keepdims=True)
    rank2 = jnp.sum(jnp.where(oh2, seen, 0.0), axis=0, keepdims=True)
    carry = carry + jnp.sum(both.astype(F32), axis=1, keepdims=True)
    carry_ref[...] = carry
    cnt_ref[...] = carry

    e1 = (row1 - N_GROUPS).astype(F32)
    e2 = (row2 - N_GROUPS).astype(F32)
    zero = jnp.zeros_like(g1)
    route_ref[...] = jnp.concatenate([e1, e2, g1, g2, rank1, rank2, zero, zero], axis=0)


def _post(fox, gla, main, h, gla_gain, wout, cg, wxq, kmem, vmem, wxo, mg, wr, br, seq, mem_len, layer):
    n, d_model = h.shape
    tm = POST_TILE
    per_seq = seq // tm
    const = lambda i: (0, 0)
    pick = lambda i: (layer, 0, 0)
    return pl.pallas_call(
        _post_kernel,
        grid=(n // tm,),
        in_specs=[
            pl.BlockSpec((tm, FOX_WIDTH), lambda i: (i, 0)),
            pl.BlockSpec((tm, GLA_V), lambda i: (i, 0)),
            pl.BlockSpec((tm, GLA_V), lambda i: (i, MAIN_WIDTH // GLA_V - 1)),
            pl.BlockSpec((tm, d_model), lambda i: (i, 0)),
            pl.BlockSpec((None, 1, GLA_V), pick),
            pl.BlockSpec((None, FOX_WIDTH + GLA_V, d_model), pick),
            pl.BlockSpec((None, 1, d_model), pick),
            pl.BlockSpec((None, d_model, X_WIDTH), pick),
            pl.BlockSpec((None, mem_len, X_WIDTH), lambda i: (layer, i // per_seq, 0)),
            pl.BlockSpec((None, mem_len, X_WIDTH), lambda i: (layer, i // per_seq, 0)),
            pl.BlockSpec((None, X_WIDTH, d_model), pick),
            pl.BlockSpec((None, 1, d_model), pick),
            pl.BlockSpec((None, d_model, 2 * LANES), pick),
            pl.BlockSpec((None, 1, LANES), pick),
        ],
        out_specs=[
            pl.BlockSpec((tm, d_model), lambda i: (i, 0)),
            pl.BlockSpec((tm, d_model // 2), lambda i: (i, 0)),
            pl.BlockSpec((ROUTE_WIDTH, tm), lambda i: (0, i)),
            pl.BlockSpec((ROUTE_ROWS, LANES), const),
        ],
        out_shape=[
            jax.ShapeDtypeStruct((n, d_model), F32),
            jax.ShapeDtypeStruct((n, d_model // 2), jnp.uint32),
            jax.ShapeDtypeStruct((ROUTE_WIDTH, n), F32),
            jax.ShapeDtypeStruct((ROUTE_ROWS, LANES), F32),
        ],
        scratch_shapes=[pltpu.VMEM((ROUTE_ROWS, LANES), F32)],
        compiler_params=_cparams(("arbitrary",)),
        name="post_mixer",
    )(fox, gla, main, h, gla_gain, wout, cg, wxq, kmem, vmem, wxo, mg, wr, br)


def _dispatch(dest_kmajor, x, n_rows):
    n, width = x.shape
    window = SC_GATHER_WINDOW
    mesh = plsc.VectorSubcoreMesh(core_axis_name="core", subcore_axis_name="subcore")
    workers = mesh.num_cores * mesh.num_subcores
    per_worker = n // workers
    assert n % (workers * window) == 0

    steps = per_worker // window
    assert steps % 2 == 0
    index_buf = pltpu.VMEM((window,), jnp.int32)
    row_buf = pltpu.VMEM((window, width), x.dtype)
    dma = pltpu.SemaphoreType.DMA

    @functools.partial(
        pl.kernel, out_type=jax.ShapeDtypeStruct((n_rows, width), x.dtype), mesh=mesh,
        scratch_types=[index_buf, index_buf, index_buf, index_buf, row_buf, row_buf, dma, dma, dma])
    def scatter(x_hbm, idx_hbm, out_hbm, idx0_a, idx1_a, idx0_b, idx1_b, rows_a, rows_b, sem_a, sem_b, sem_out):
        worker = lax.axis_index("subcore") * mesh.num_cores + lax.axis_index("core")
        base = worker * per_worker
        slots = ((idx0_a, idx1_a, rows_a, sem_a), (idx0_b, idx1_b, rows_b, sem_b))

        def load(step, slot):
            idx0, idx1, rows, sem = slots[slot]
            off = pl.multiple_of(base + step * window, window)
            pltpu.sync_copy(idx_hbm.at[pl.ds(off, window)], idx0)
            pltpu.sync_copy(idx_hbm.at[pl.ds(n + off, window)], idx1)
            pltpu.async_copy(x_hbm.at[pl.ds(off, window)], rows, sem)

        def store(slot):
            idx0, idx1, rows, sem = slots[slot]
            pltpu.make_async_copy(x_hbm.at[pl.ds(0, window)], rows, sem).wait()
            first = pltpu.async_copy(rows, out_hbm.at[idx0], sem_out)
            second = pltpu.async_copy(rows, out_hbm.at[idx1], sem_out)
            first.wait()
            second.wait()

        load(0, 0)

        @pl.loop(0, steps, step=2)
        def _(step):
            load(step + 1, 1)
            store(0)

            @pl.when(step + 2 < steps)
            def _():
                load(step + 2, 0)

            store(1)

    return scatter(x, dest_kmajor)


def _expert_kernel(be_ref, valid_ref, fresh_ref, x_ref, wg_ref, wu_ref, wd_ref, y_ref, wg_b, wu_b, wd_b):
    del be_ref
    i = pl.program_id(0)
    valid = valid_ref[i]

    @pl.when(fresh_ref[i] > 0)
    def _():
        wg_b[...] = wg_ref[...].astype(BF16)
        wu_b[...] = wu_ref[...].astype(BF16)
        wd_b[...] = wd_ref[...].astype(BF16)

    @pl.when(valid > 0)
    def _():
        row = lax.broadcasted_iota(jnp.int32, x_ref.shape, 0)
        lo, hi = _unpack_rows(jnp.where(row < valid, x_ref[...], jnp.uint32(0)))
        lo = lo.astype(BF16)
        hi = hi.astype(BF16)
        half = lo.shape[1]
        g = _dot(lo, wg_b[:half, :]) + _dot(hi, wg_b[half:, :])
        u = _dot(lo, wu_b[:half, :]) + _dot(hi, wu_b[half:, :])
        a = (g * jax.nn.sigmoid(g) * u).astype(BF16)
        y_ref[...] = _pack_rows(_dot(a, wd_b[...]))

    @pl.when(valid <= 0)
    def _():
        y_ref[...] = jnp.zeros_like(y_ref)


def _experts(block_e, valid, xs, wg, wu, wd, layer):
    r, width = xs.shape
    bm = MOE_BLOCK
    d_model, d_exp = wg.shape[-2:]
    fresh = jnp.concatenate([jnp.ones((1,), jnp.int32), (block_e[1:] != block_e[:-1]).astype(jnp.int32)])
    pick = lambda i, be, va, fr: (layer, be[i], 0, 0)
    return pl.pallas_call(
        _expert_kernel,
        grid_spec=pltpu.PrefetchScalarGridSpec(
            num_scalar_prefetch=3,
            grid=(r // bm,),
            in_specs=[
                pl.BlockSpec((bm, width), lambda i, be, va, fr: (i, 0)),
                pl.BlockSpec((None, None, d_model, d_exp), pick),
                pl.BlockSpec((None, None, d_model, d_exp), pick),
                pl.BlockSpec((None, None, d_exp, d_model), pick),
            ],
            out_specs=pl.BlockSpec((bm, width), lambda i, be, va, fr: (i, 0)),
            scratch_shapes=[
                pltpu.VMEM((d_model, d_exp), BF16),
                pltpu.VMEM((d_model, d_exp), BF16),
                pltpu.VMEM((d_exp, d_model), BF16),
            ],
        ),
        out_shape=jax.ShapeDtypeStruct((r, width), jnp.uint32),
        compiler_params=_cparams(("arbitrary",)),
        name="moe_experts",
    )(block_e, valid, fresh, xs, wg, wu, wd)


def _sc_gather_rows(table, idx):
    m = idx.shape[0]
    width = table.shape[1]
    window = SC_GATHER_WINDOW
    mesh = plsc.VectorSubcoreMesh(core_axis_name="core", subcore_axis_name="subcore")
    workers = mesh.num_cores * mesh.num_subcores
    per_worker = m // workers
    assert m % (workers * window) == 0

    steps = per_worker // window
    assert steps % 2 == 0
    index_buf = pltpu.VMEM((window,), jnp.int32)
    row_buf = pltpu.VMEM((window, width), table.dtype)
    dma = pltpu.SemaphoreType.DMA

    @functools.partial(
        pl.kernel, out_type=jax.ShapeDtypeStruct((m, width), table.dtype), mesh=mesh,
        scratch_types=[index_buf, index_buf, row_buf, row_buf, dma, dma])
    def gather(table_hbm, idx_hbm, out_hbm, idx_a, idx_b, rows_a, rows_b, sem_a, sem_b):
        worker = lax.axis_index("subcore") * mesh.num_cores + lax.axis_index("core")
        base = worker * per_worker
        slots = ((idx_a, rows_a, sem_a), (idx_b, rows_b, sem_b))

        def fetch(step, slot):
            idx, rows, sem = slots[slot]
            off = pl.multiple_of(base + step * window, window)
            pltpu.sync_copy(idx_hbm.at[pl.ds(off, window)], idx)
            pltpu.async_copy(table_hbm.at[idx], rows, sem)

        def flush(step, slot):
            idx, rows, sem = slots[slot]
            off = pl.multiple_of(base + step * window, window)
            pltpu.make_async_copy(table_hbm.at[idx], rows, sem).wait()
            pltpu.sync_copy(rows, out_hbm.at[pl.ds(off, window)])

        fetch(0, 0)

        @pl.loop(0, steps, step=2)
        def _(step):
            fetch(step + 1, 1)
            flush(step, 0)

            @pl.when(step + 2 < steps)
            def _():
                fetch(step + 2, 0)

            flush(step + 1, 1)

    return gather(table, idx)


def _final_kernel(h_ref, gate_ref, gain_ref, y0_ref, y1_ref, o_ref):
    o_ref[...] = _rms(_moe_sum(h_ref[...], gate_ref[...], y0_ref[...], y1_ref[...]), gain_ref[...])


def _final(h, gates, gain, picked):
    n, d_model = h.shape
    tc = MOVE_TILE
    steps = n // tc
    return pl.pallas_call(
        _final_kernel,
        grid=(steps,),
        in_specs=[
            pl.BlockSpec((tc, d_model), lambda i: (i, 0)),
            pl.BlockSpec((tc, 2), lambda i: (i, 0)),
            pl.BlockSpec((1, d_model), lambda i: (0, 0)),
            pl.BlockSpec((tc, d_model // 2), lambda i: (i, 0)),
            pl.BlockSpec((tc, d_model // 2), lambda i: (i + steps, 0)),
        ],
        out_specs=pl.BlockSpec((tc, d_model), lambda i: (i, 0)),
        out_shape=jax.ShapeDtypeStruct((n, d_model), F32),
        compiler_params=_cparams(("arbitrary",)),
        name="moe_final",
    )(h, gates, gain, picked, picked)


def _routing_tables(route, cnt, n_rows):
    bm = MOE_BLOCK
    expert = route[0:2].astype(jnp.int32)
    rank = route[4:6].astype(jnp.int32)
    counts = cnt[N_GROUPS:N_GROUPS + N_EXPERTS, 0].astype(jnp.int32)
    padded = (counts + bm - 1) // bm * bm
    pad_ends = jnp.cumsum(padded)
    pad_starts = pad_ends - padded
    ids = jnp.arange(N_EXPERTS, dtype=jnp.int32)
    start_of = jnp.sum(jnp.where(expert[..., None] == ids, pad_starts, 0), axis=-1)
    dest = (start_of + rank).reshape(-1).astype(jnp.int32)
    block_row = jnp.arange(n_rows // bm, dtype=jnp.int32) * bm
    block_e = jnp.minimum(jnp.sum((pad_ends[None, :] <= block_row[:, None]).astype(jnp.int32), axis=-1),
                          N_EXPERTS - 1)
    row_end = jnp.sum(jnp.where(block_e[:, None] == ids, pad_starts + counts, 0), axis=-1)
    valid = jnp.clip(row_end - block_row, 0, bm).astype(jnp.int32)
    return dest, block_e, valid


def kernel(x, mem, mem_norm, mix_norm, w_in, b_forget, w_alpha_up, b_alpha, fox_out_gain, gla_out_gain, w_out,
           cross_norm, w_xq, w_xk, w_xv, w_xo, moe_norm, w_router_group, b_router_group, w_router_expert,
           b_router_expert, w_expert_gate, w_expert_up, w_expert_down, final_norm):
    batch, seq, d_model = x.shape
    mem_len = mem.shape[1]
    depth = w_in.shape[0]
    n = batch * seq
    assert seq % FOX_TILE == 0 and seq % IN_TILE == 0 and seq % POST_TILE == 0 and seq % GLA_CHUNK == 0
    assert n % MOVE_TILE == 0 and d_model % LANES == 0

    c0 = 3 * FOX_WIDTH
    c1 = c0 + FOX_HEADS
    c2 = c1 + 2 * GLA_QK + 2 * GLA_V
    w_main = jnp.concatenate([w_in[:, :, :c0], w_in[:, :, c1:c2]], axis=-1).astype(BF16)
    pad = LANES - FOX_HEADS - GLA_RANK
    w_small = jnp.concatenate([w_in[:, :, c0:c1], w_in[:, :, c2:], jnp.zeros((depth, d_model, pad), F32)],
                              axis=-1).astype(BF16)
    w_up = jnp.concatenate([jnp.zeros((depth, FOX_HEADS, GLA_QK), F32), w_alpha_up,
                            jnp.zeros((depth, pad, GLA_QK), F32)], axis=1).astype(BF16)
    b_f = jnp.pad(b_forget, ((0, 0), (0, LANES - FOX_HEADS)))[:, None, :]
    b_a = b_alpha[:, None, :]
    w_r = jnp.concatenate([w_router_group, w_router_expert,
                           jnp.zeros((depth, d_model, LANES - N_GROUPS - N_EXPERTS), F32)], axis=-1)
    w_rh = w_r.astype(BF16)
    w_rs = jnp.concatenate([w_rh, (w_r - w_rh.astype(F32)).astype(BF16)], axis=-1)
    b_r = jnp.pad(jnp.concatenate([b_router_group, b_router_expert], axis=-1),
                  ((0, 0), (0, LANES - N_GROUPS - N_EXPERTS)))[:, None, :]
    w_out_b = w_out.astype(BF16)
    w_xq_b = w_xq.astype(BF16)
    w_xo_b = w_xo.astype(BF16)
    mix_g = mix_norm[:, None, :]
    cross_g = cross_norm[:, None, :]
    moe_g = moe_norm[:, None, :]
    gla_g = gla_out_gain[:, None, :]

    kmem, vmem = _mem_kv(mem.reshape(batch * mem_len, d_model), mem_norm[None, :],
                         w_xk.astype(BF16), w_xv.astype(BF16), batch, mem_len)

    n_rows = 2 * n + N_EXPERTS * MOE_BLOCK
    h = x.reshape(n, d_model)
    moe = None
    for l in range(depth):
        if moe is None:
            main, c, kl, dec = _in_proj(h, mix_g, w_main, w_small, w_up, b_f, b_a, l, seq)
        else:
            h, main, c, kl, dec = _in_proj(h, mix_g, w_main, w_small, w_up, b_f, b_a, l, seq, moe)
        fox = _fox_attention(main, c, fox_out_gain[l][None, :], batch, seq)
        gla = _gla(main, kl, dec, batch, seq)
        h2, hn2, route, cnt = _post(fox, gla, main, h, gla_g, w_out_b, cross_g, w_xq_b, kmem, vmem, w_xo_b, moe_g,
                                    w_rs, b_r, seq, mem_len, l)
        dest, block_e, valid = _routing_tables(route, cnt, n_rows)
        xs = _dispatch(dest, hn2, n_rows)
        y = _experts(block_e, valid, xs, w_expert_gate, w_expert_up, w_expert_down, l)
        h, moe = h2, (route[2:4].T, _sc_gather_rows(y, dest))
    return _final(h, moe[0], final_norm[None, :], moe[1]).reshape(batch, seq, d_model)
```

```python
import functools

import jax
import jax.numpy as jnp
from jax import lax
from jax.experimental import pallas as pl
from jax.experimental.pallas import tpu as pltpu
from jax.experimental.pallas import tpu_sc as plsc

F32 = jnp.float32
BF16 = jnp.bfloat16
EPS = 1e-6
LOG2E = 1.4426950408889634

FOX_HEADS = 8
FOX_DIM = 64
FOX_WIDTH = FOX_HEADS * FOX_DIM
GLA_HEADS = 4
GLA_DK = 64
GLA_DV = 128
GLA_QK = GLA_HEADS * GLA_DK
GLA_V = GLA_HEADS * GLA_DV
GLA_RANK = 16
GLA_TAU = 16.0
GLA_CHUNK = 64
X_HEADS = 4
X_DIM = 128
X_WIDTH = X_HEADS * X_DIM
N_GROUPS = 4
GROUP_SIZE = 4
N_EXPERTS = N_GROUPS * GROUP_SIZE
MAIN_WIDTH = 3 * FOX_WIDTH + 2 * GLA_QK + 2 * GLA_V

LANES = 128
ROUTE_WIDTH = 8
ROUTE_ROWS = 32
VMEM_LIMIT = 56 * 1024 * 1024

IN_TILE = 1024
FOX_TILE = 512
FOX_SLAB = 64
POST_TILE = 1024
MOE_BLOCK = 512
MOVE_TILE = 1024
SC_GATHER_WINDOW = 64

def _cparams(sem):
    return pltpu.CompilerParams(dimension_semantics=sem, vmem_limit_bytes=VMEM_LIMIT)


def _rms(x, gain):
    return x * lax.rsqrt(jnp.mean(x * x, axis=-1, keepdims=True) + EPS) * gain


def _log_sigmoid(x):
    return jnp.minimum(x, 0.0) - jnp.log1p(jnp.exp(-jnp.abs(x)))


def _dot(a, b):
    return jnp.dot(a, b, preferred_element_type=F32)


def _dot_nt(a, b):
    return lax.dot_general(a, b, (((1,), (1,)), ((), ())), preferred_element_type=F32)


def _pack_rows(x):
    half = x.shape[1] // 2
    lo = lax.bitcast_convert_type(x[:, :half].astype(BF16).astype(F32), jnp.uint32)
    hi = lax.bitcast_convert_type(x[:, half:].astype(BF16).astype(F32), jnp.uint32)
    return (lo >> 16) | hi


def _unpack_rows(w):
    lo = lax.bitcast_convert_type(w << 16, F32)
    hi = lax.bitcast_convert_type(w & jnp.uint32(0xFFFF0000), F32)
    return lo, hi


def _mem_kv_kernel(mem_ref, gain_ref, wk_ref, wv_ref, k_ref, v_ref):
    mn = _rms(mem_ref[...], gain_ref[...]).astype(BF16)
    for l in range(wk_ref.shape[0]):
        k_ref[l] = _dot(mn, wk_ref[l]).astype(BF16)
        v_ref[l] = _dot(mn, wv_ref[l]).astype(BF16)


def _mem_kv(mem2d, gain, wk, wv, batch, mem_len):
    depth, d_model, width = wk.shape
    out = jax.ShapeDtypeStruct((depth, batch * mem_len, width), BF16)
    return pl.pallas_call(
        _mem_kv_kernel,
        grid=(batch,),
        in_specs=[
            pl.BlockSpec((mem_len, d_model), lambda b: (b, 0)),
            pl.BlockSpec((1, d_model), lambda b: (0, 0)),
            pl.BlockSpec((depth, d_model, width), lambda b: (0, 0, 0)),
            pl.BlockSpec((depth, d_model, width), lambda b: (0, 0, 0)),
        ],
        out_specs=[
            pl.BlockSpec((depth, mem_len, width), lambda b: (0, b, 0)),
            pl.BlockSpec((depth, mem_len, width), lambda b: (0, b, 0)),
        ],
        out_shape=[out, out],
        compiler_params=_cparams(("arbitrary",)),
        name="mem_kv",
    )(mem2d, gain, wk, wv)


def _moe_sum(h, route, y0_packed, y1_packed):
    gate = jnp.transpose(route)
    y0 = jnp.concatenate(_unpack_rows(y0_packed), axis=1)
    y1 = jnp.concatenate(_unpack_rows(y1_packed), axis=1)
    return h + gate[:, 2:3] * y0 + gate[:, 3:4] * y1


def _in_proj_body(h, gain_ref, wmain_ref, wsmall_ref, wup_ref, bf_ref, ba_ref, main_ref, c_ref, kl_ref, dec_ref,
                  carry_ref, tiles_per_seq):
    tm = h.shape[0]
    cs = GLA_CHUNK
    xn = _rms(h, gain_ref[...]).astype(BF16)
    small = _dot(xn, wsmall_ref[...])
    lane = lax.broadcasted_iota(jnp.int32, small.shape, 1)
    c = jnp.where(lane < FOX_HEADS, _log_sigmoid(small + bf_ref[...]), 0.0)
    rowi = lax.broadcasted_iota(jnp.int32, c.shape, 0)
    shift = 1
    while shift < tm:
        c = c + jnp.where(rowi >= shift, pltpu.roll(c, shift, axis=0), 0.0)
        shift *= 2
    starts_sequence = pl.program_id(0) % tiles_per_seq == 0
    c = c + jnp.where(starts_sequence, 0.0, carry_ref[...])
    carry_ref[...] = c[tm - 1:tm, :]
    c_ref[...] = jnp.transpose(c)[:FOX_HEADS, :]
    a = _dot(small.astype(BF16), wup_ref[...]) + ba_ref[...]
    b = _log_sigmoid(a) * (1.0 / GLA_TAU)
    pos = lax.broadcasted_iota(jnp.int32, b.shape, 0) % cs
    shift = 1
    while shift < cs:
        b = b + jnp.where(pos >= shift, pltpu.roll(b, shift, axis=0), 0.0)
        shift *= 2
    dec = jnp.exp(b.reshape(tm // cs, cs, GLA_QK)[:, cs - 1:cs, :])
    q0 = 3 * FOX_WIDTH
    k0 = q0 + GLA_QK
    qk = _dot(xn, wmain_ref[:, q0:k0 + GLA_QK])
    main_ref[:, q0:k0] = (qk[:, :GLA_QK] * jnp.exp(b) * (GLA_DK ** -0.5)).astype(BF16)
    ke = qk[:, GLA_QK:] * jnp.exp(-b)
    main_ref[:, k0:k0 + GLA_QK] = ke.astype(BF16)
    kl_ref[...] = (ke.reshape(tm // cs, cs, GLA_QK) * dec).reshape(tm, GLA_QK).astype(BF16)
    dec_ref[...] = dec.reshape(tm // cs, GLA_QK)
    step = 512
    for lo in list(range(0, q0, step)) + list(range(k0 + GLA_QK, MAIN_WIDTH, step)):
        main_ref[:, lo:lo + step] = _dot(xn, wmain_ref[:, lo:lo + step]).astype(BF16)


def _in_proj_kernel(h_ref, *refs, tiles_per_seq):
    _in_proj_body(h_ref[...], *refs, tiles_per_seq)


def _in_proj_after_moe_kernel(h_ref, gate_ref, y0_ref, y1_ref, gain_ref, wmain_ref, wsmall_ref, wup_ref, bf_ref,
                              ba_ref, hout_ref, main_ref, c_ref, kl_ref, dec_ref, carry_ref, *, tiles_per_seq):
    h = _moe_sum(h_ref[...], gate_ref[...], y0_ref[...], y1_ref[...])
    hout_ref[...] = h
    _in_proj_body(h, gain_ref, wmain_ref, wsmall_ref, wup_ref, bf_ref, ba_ref, main_ref, c_ref, kl_ref, dec_ref,
                  carry_ref, tiles_per_seq)


def _in_proj(h, gain, wmain, wsmall, wup, bf, ba, layer, seq, moe=None):
    n, d_model = h.shape
    tm = IN_TILE
    steps = n // tm
    chunks = tm // GLA_CHUNK
    tiles_per_seq = seq // tm
    pick = lambda i: (layer, 0, 0)
    row_block = lambda width, rows=tm: pl.BlockSpec((rows, width), lambda i: (i, 0))
    weight_specs = [
        pl.BlockSpec((None, 1, d_model), pick),
        pl.BlockSpec((None, d_model, MAIN_WIDTH), pick),
        pl.BlockSpec((None, d_model, LANES), pick),
        pl.BlockSpec((None, LANES, GLA_QK), pick),
        pl.BlockSpec((None, 1, LANES), pick),
        pl.BlockSpec((None, 1, GLA_QK), pick),
    ]
    out_specs = [row_block(MAIN_WIDTH), pl.BlockSpec((FOX_HEADS, tm), lambda i: (0, i)), row_block(GLA_QK),
                 row_block(GLA_QK, chunks)]
    out_shape = [
        jax.ShapeDtypeStruct((n, MAIN_WIDTH), BF16),
        jax.ShapeDtypeStruct((FOX_HEADS, n), F32),
        jax.ShapeDtypeStruct((n, GLA_QK), BF16),
        jax.ShapeDtypeStruct((n // GLA_CHUNK, GLA_QK), F32),
    ]
    weights = (gain, wmain, wsmall, wup, bf, ba)
    carry = [pltpu.VMEM((1, LANES), F32)]
    if moe is None:
        return pl.pallas_call(
            functools.partial(_in_proj_kernel, tiles_per_seq=tiles_per_seq),
            grid=(steps,), in_specs=[row_block(d_model)] + weight_specs,
            out_specs=out_specs, out_shape=out_shape, scratch_shapes=carry,
            compiler_params=_cparams(("arbitrary",)), name="in_proj",
        )(h, *weights)
    gates, picked = moe
    half = d_model // 2
    return pl.pallas_call(
        functools.partial(_in_proj_after_moe_kernel, tiles_per_seq=tiles_per_seq), grid=(steps,),
        in_specs=[row_block(d_model), pl.BlockSpec((ROUTE_WIDTH, tm), lambda i: (0, i)), row_block(half),
                  pl.BlockSpec((tm, half), lambda i: (i + steps, 0))] + weight_specs,
        out_specs=[row_block(d_model)] + out_specs,
        out_shape=[jax.ShapeDtypeStruct((n, d_model), F32)] + out_shape, scratch_shapes=carry,
        compiler_params=_cparams(("arbitrary",)), name="in_proj_after_moe",
    )(h, gates, picked, picked, *weights)


def _fox_kernel(q_ref, k_ref, v_ref, c_ref, gain_ref, o_ref, q2_ref, s_ref, p_ref, alpha_ref, m_ref, l_ref, acc_ref):
    tq = FOX_TILE
    rows = 2 * tq
    slab = FOX_SLAB
    nq = q_ref.shape[0] // tq
    lane = lax.broadcasted_iota(jnp.int32, (1, LANES), 1)
    first = lane < FOX_DIM
    scale = FOX_DIM ** -0.5 * LOG2E
    for qi in range(nq):
        q = q_ref[qi * tq:(qi + 1) * tq, :].astype(F32) * scale
        q2_ref[qi, :tq, :] = jnp.where(first, q, 0.0).astype(BF16)
        q2_ref[qi, tq:, :] = jnp.where(first, 0.0, q).astype(BF16)

    head_row = lax.broadcasted_iota(jnp.int32, (FOX_HEADS, tq), 0)
    pair = pl.program_id(1)

    def scores(qi, j):
        cj = c_ref[:, j * tq:(j + 1) * tq] * LOG2E
        c0 = jnp.sum(jnp.where(head_row == 2 * pair, cj, 0.0), axis=0, keepdims=True)
        c1 = jnp.sum(jnp.where(head_row == 2 * pair + 1, cj, 0.0), axis=0, keepdims=True)
        d = _dot_nt(q2_ref[qi], k_ref[j * tq:(j + 1) * tq, :])
        s_ref[:tq, :] = d[:tq] - c0
        s_ref[tq:, :] = d[tq:] - c1

    def weighted_values(qi, j):
        par = qi % 2
        acc_ref[par] = alpha_ref[par] * acc_ref[par] + _dot(p_ref[...], v_ref[j * tq:(j + 1) * tq, :])

    def softmax(qi, masked):
        par = qi % 2
        for r in range(rows // slab):
            sl = slice(r * slab, (r + 1) * slab)
            s = s_ref[sl, :]
            if masked:
                row = lax.broadcasted_iota(jnp.int32, (slab, tq), 0) + (r * slab) % tq
                col = lax.broadcasted_iota(jnp.int32, (slab, tq), 1)
                s = jnp.where(row >= col, s, -jnp.inf)
            m_old = m_ref[par, sl, :]
            m_new = jnp.maximum(m_old, jnp.max(s, axis=-1, keepdims=True))
            alpha = jnp.exp2(m_old - m_new)
            p = jnp.exp2(s - jnp.concatenate([m_new] * (tq // LANES), axis=1))
            l_ref[par, sl, :] = alpha * l_ref[par, sl, :] + jnp.sum(p, axis=-1, keepdims=True)
            m_ref[par, sl, :] = m_new
            alpha_ref[par, sl, :] = alpha
            p_ref[sl, :] = p.astype(BF16)

    def finalize(qi):
        par = qi % 2
        o2 = acc_ref[par] / l_ref[par]
        o = jnp.where(first, o2[:tq], o2[tq:])
        sq = o * o
        ss0 = jnp.sum(jnp.where(first, sq, 0.0), axis=-1, keepdims=True)
        ss1 = jnp.sum(jnp.where(first, 0.0, sq), axis=-1, keepdims=True)
        ms = jnp.where(first, ss0, ss1) * (1.0 / FOX_DIM)
        o_ref[qi * tq:(qi + 1) * tq, :] = (o * lax.rsqrt(ms + EPS) * gain_ref[...]).astype(BF16)

    steps = [(qi, j) for qi in range(nq) for j in range(qi + 1)]
    scores(*steps[0])
    for t, (qi, j) in enumerate(steps):
        if t > 0:
            weighted_values(*steps[t - 1])
            if steps[t - 1][0] != qi:
                finalize(steps[t - 1][0])
        if j == 0:
            par = qi % 2
            m_ref[par] = jnp.full(m_ref.shape[1:], -jnp.inf, F32)
            l_ref[par] = jnp.zeros(l_ref.shape[1:], F32)
            acc_ref[par] = jnp.zeros(acc_ref.shape[1:], F32)
        softmax(qi, masked=(j == qi))
        if t + 1 < len(steps):
            scores(*steps[t + 1])
    weighted_values(*steps[-1])
    finalize(steps[-1][0])


def _fox_attention(main, c, gain, batch, seq):
    n = main.shape[0]
    tq = FOX_TILE
    nq = seq // tq
    pairs = FOX_HEADS // 2
    k_off = FOX_WIDTH // LANES
    v_off = 2 * FOX_WIDTH // LANES
    stat = pltpu.VMEM((2, 2 * tq, LANES), F32)
    return pl.pallas_call(
        _fox_kernel,
        grid=(batch, pairs),
        in_specs=[
            pl.BlockSpec((seq, LANES), lambda b, p: (b, p)),
            pl.BlockSpec((seq, LANES), lambda b, p: (b, k_off + p)),
            pl.BlockSpec((seq, LANES), lambda b, p: (b, v_off + p)),
            pl.BlockSpec((FOX_HEADS, seq), lambda b, p: (0, b)),
            pl.BlockSpec((1, LANES), lambda b, p: (0, p)),
        ],
        out_specs=pl.BlockSpec((seq, LANES), lambda b, p: (b, p)),
        out_shape=jax.ShapeDtypeStruct((n, FOX_WIDTH), BF16),
        scratch_shapes=[
            pltpu.VMEM((nq, 2 * tq, LANES), BF16),
            pltpu.VMEM((2 * tq, tq), F32),
            pltpu.VMEM((2 * tq, tq), BF16),
            stat, stat, stat, stat,
        ],
        compiler_params=_cparams(("arbitrary", "arbitrary")),
        name="fox_attention",
    )(main, main, main, c, gain)


def _gla_kernel(qe_ref, ke_ref, v_ref, kl_ref, dec_ref, o_ref):
    seq = qe_ref.shape[0]
    cs = GLA_CHUNK
    nc = seq // cs
    width = 2 * GLA_DK

    lane = lax.broadcasted_iota(jnp.int32, (1, width), 1)
    first = lane < GLA_DK
    row = lax.broadcasted_iota(jnp.int32, (2 * cs, cs), 0)
    col = lax.broadcasted_iota(jnp.int32, (2 * cs, cs), 1)
    tril2 = jnp.where(row >= cs, row - cs, row) >= col
    srow = lax.broadcasted_iota(jnp.int32, (2 * GLA_DV, width), 0)
    scol = lax.broadcasted_iota(jnp.int32, (2 * GLA_DV, width), 1)
    same_head = (srow >= GLA_DV) == (scol >= GLA_DK)
    unroll = 8

    def chunks(ci, st):
        r0s = [pl.multiple_of((ci * unroll + u) * cs, cs) for u in range(unroll)]
        qes = [qe_ref[pl.ds(r0, cs), :] for r0 in r0s]
        vs = [v_ref[pl.ds(r0, cs), :] for r0 in r0s]
        atts, upds = [], []
        for u in range(unroll):
            zero = jnp.zeros_like(qes[u])
            q2 = jnp.concatenate([jnp.where(first, qes[u], zero), jnp.where(first, zero, qes[u])], axis=0)
            atts.append(jnp.where(tril2, _dot_nt(q2, ke_ref[pl.ds(r0s[u], cs), :]), 0.0).astype(BF16))
        for u in range(unroll):
            upds.append(lax.dot_general(vs[u], kl_ref[pl.ds(r0s[u], cs), :], (((0,), (0,)), ((), ())),
                                        preferred_element_type=F32))
        ois = [_dot(atts[u], vs[u]) for u in range(unroll)]
        for u in range(unroll):
            o = _dot_nt(qes[u], st.astype(BF16))
            o = o + jnp.concatenate([ois[u][:cs, :GLA_DV], ois[u][cs:, GLA_DV:]], axis=1)
            o_ref[pl.ds(r0s[u], cs), :] = o.astype(BF16)
            st = st * dec_ref[pl.ds(ci * unroll + u, 1), :] + jnp.where(same_head, upds[u], 0.0)
        return st

    lax.fori_loop(0, nc // unroll, chunks, jnp.zeros((2 * GLA_DV, width), F32))


def _gla(main, kl, dec, batch, seq):
    n = main.shape[0]
    pairs = GLA_HEADS // 2
    q_off = 3 * FOX_WIDTH // LANES
    k_off = q_off + GLA_QK // LANES
    pv = 2 * GLA_DV
    v_off = (3 * FOX_WIDTH + 2 * GLA_QK) // pv
    return pl.pallas_call(
        _gla_kernel,
        grid=(batch, pairs),
        in_specs=[
            pl.BlockSpec((seq, LANES), lambda b, p: (b, q_off + p)),
            pl.BlockSpec((seq, LANES), lambda b, p: (b, k_off + p)),
            pl.BlockSpec((seq, pv), lambda b, p: (b, v_off + p)),
            pl.BlockSpec((seq, LANES), lambda b, p: (b, p)),
            pl.BlockSpec((seq // GLA_CHUNK, LANES), lambda b, p: (b, p)),
        ],
        out_specs=pl.BlockSpec((seq, pv), lambda b, p: (b, p)),
        out_shape=jax.ShapeDtypeStruct((n, GLA_V), BF16),
        compiler_params=_cparams(("arbitrary", "arbitrary")),
        name="gla",
    )(main, main, main, kl, dec)


def _post_kernel(fox_ref, gla_ref, gg_ref, h_ref, gg_gain_ref, wout_ref, cg_ref, wxq_ref, k_ref, v_ref, wxo_ref,
                 mg_ref, wr_ref, br_ref, h2_ref, hn_ref, route_ref, cnt_ref, carry_ref):
    tm = h_ref.shape[0]

    @pl.when(pl.program_id(0) == 0)
    def _():
        carry_ref[...] = jnp.zeros_like(carry_ref)

    raw = gla_ref[...].astype(F32)
    normed = []
    for hh in range(GLA_HEADS):
        oh = raw[:, hh * GLA_DV:(hh + 1) * GLA_DV]
        normed.append(oh * lax.rsqrt(jnp.mean(oh * oh, axis=-1, keepdims=True) + EPS))
    g = gg_ref[...].astype(F32)
    gla = (jnp.concatenate(normed, axis=1) * gg_gain_ref[...] * (g * jax.nn.sigmoid(g))).astype(BF16)
    y = _dot(fox_ref[...], wout_ref[0:FOX_WIDTH, :]) + _dot(gla, wout_ref[FOX_WIDTH:, :])
    h1 = h_ref[...] + y
    hn = _rms(h1, cg_ref[...]).astype(BF16)
    q = _dot(hn, wxq_ref[...]).astype(BF16)
    xscale = X_DIM ** -0.5
    heads = []
    for hh in range(X_HEADS):
        sl = slice(hh * X_DIM, (hh + 1) * X_DIM)
        s = _dot_nt(q[:, sl], k_ref[:, sl]) * xscale
        p = jnp.exp(s - jnp.max(s, axis=-1, keepdims=True))
        heads.append(_dot(p.astype(BF16), v_ref[:, sl]) / jnp.sum(p, axis=-1, keepdims=True))
    o = jnp.concatenate(heads, axis=1).astype(BF16)
    h2 = h1 + _dot(o, wxo_ref[...])
    h2_ref[...] = h2
    hn2 = _rms(h2, mg_ref[...])
    hn_ref[...] = _pack_rows(hn2)

    xh = hn2.astype(BF16)
    xl = (hn2 - xh.astype(F32)).astype(BF16)
    both_w = _dot(jnp.concatenate([xh, xl], axis=0), wr_ref[...])
    logits = both_w[:tm, :LANES] + both_w[:tm, LANES:] + both_w[tm:, :LANES] + both_w[tm:, LANES:] + br_ref[...]
    lt = jnp.transpose(logits)[:ROUTE_ROWS, :]
    row = lax.broadcasted_iota(jnp.int32, (ROUTE_ROWS, tm), 0)
    neg = -jnp.inf
    gl = jnp.where(row < N_GROUPS, lt, neg)
    gmax = jnp.max(gl, axis=0, keepdims=True)
    ge = jnp.exp(gl - gmax)
    gprob = ge / jnp.sum(ge, axis=0, keepdims=True)
    pmax = jnp.max(gprob, axis=0, keepdims=True)
    grp = jnp.min(jnp.where(gprob == pmax, row, ROUTE_ROWS), axis=0, keepdims=True)
    in_grp = (row >= N_GROUPS) & (row < N_GROUPS + N_EXPERTS) & (((row - N_GROUPS) // GROUP_SIZE) == grp)
    el = jnp.where(in_grp, lt, neg)
    emax = jnp.max(el, axis=0, keepdims=True)
    ee = jnp.exp(el - emax)
    eprob = ee / jnp.sum(ee, axis=0, keepdims=True)
    p1 = jnp.max(eprob, axis=0, keepdims=True)
    row1 = jnp.min(jnp.where(in_grp & (eprob == p1), row, ROUTE_ROWS), axis=0, keepdims=True)
    rest = jnp.where(in_grp & (row != row1), eprob, -1.0)
    p2 = jnp.max(rest, axis=0, keepdims=True)
    row2 = jnp.min(jnp.where(rest == p2, row, ROUTE_ROWS), axis=0, keepdims=True)
    g1 = pmax * p1 / (p1 + p2)
    g2 = pmax * p2 / (p1 + p2)

    oh1 = row == row1
    oh2 = row == row2
    both = (oh1 | oh2).astype(BF16)
    srow = lax.broadcasted_iota(jnp.int32, (tm, tm), 0)
    scol = lax.broadcasted_iota(jnp.int32, (tm, tm), 1)
    earlier = (srow < scol).astype(BF16)
    carry = carry_ref[...]
    seen = _dot(both, earlier) + jnp.concatenate([carry] * (tm // LANES), axis=1)
    rank1 = jnp.sum(jnp.where(oh1, seen, 0.0), axis=0, keepdims=True)
    rank2 = jnp.sum(jnp.where(oh2, seen, 0.0), axis=0, keepdims=True)
    carry = carry + jnp.sum(both.astype(F32), axis=1, keepdims=True)
    carry_ref[...] = carry
    cnt_ref[...] = carry

    e1 = (row1 - N_GROUPS).astype(F32)
    e2 = (row2 - N_GROUPS).astype(F32)
    zero = jnp.zeros_like(g1)
    route_ref[...] = jnp.concatenate([e1, e2, g1, g2, rank1, rank2, zero, zero], axis=0)


def _post(fox, gla, main, h, gla_gain, wout, cg, wxq, kmem, vmem, wxo, mg, wr, br, seq, mem_len, layer):
    n, d_model = h.shape
    tm = POST_TILE
    per_seq = seq // tm
    const = lambda i: (0, 0)
    pick = lambda i: (layer, 0, 0)
    return pl.pallas_call(
        _post_kernel,
        grid=(n // tm,),
        in_specs=[
            pl.BlockSpec((tm, FOX_WIDTH), lambda i: (i, 0)),
            pl.BlockSpec((tm, GLA_V), lambda i: (i, 0)),
            pl.BlockSpec((tm, GLA_V), lambda i: (i, MAIN_WIDTH // GLA_V - 1)),
            pl.BlockSpec((tm, d_model), lambda i: (i, 0)),
            pl.BlockSpec((None, 1, GLA_V), pick),
            pl.BlockSpec((None, FOX_WIDTH + GLA_V, d_model), pick),
            pl.BlockSpec((None, 1, d_model), pick),
            pl.BlockSpec((None, d_model, X_WIDTH), pick),
            pl.BlockSpec((None, mem_len, X_WIDTH), lambda i: (layer, i // per_seq, 0)),
            pl.BlockSpec((None, mem_len, X_WIDTH), lambda i: (layer, i // per_seq, 0)),
            pl.BlockSpec((None, X_WIDTH, d_model), pick),
            pl.BlockSpec((None, 1, d_model), pick),
            pl.BlockSpec((None, d_model, 2 * LANES), pick),
            pl.BlockSpec((None, 1, LANES), pick),
        ],
        out_specs=[
            pl.BlockSpec((tm, d_model), lambda i: (i, 0)),
            pl.BlockSpec((tm, d_model // 2), lambda i: (i, 0)),
            pl.BlockSpec((ROUTE_WIDTH, tm), lambda i: (0, i)),
            pl.BlockSpec((ROUTE_ROWS, LANES), const),
        ],
        out_shape=[
            jax.ShapeDtypeStruct((n, d_model), F32),
            jax.ShapeDtypeStruct((n, d_model // 2), jnp.uint32),
            jax.ShapeDtypeStruct((ROUTE_WIDTH, n), F32),
            jax.ShapeDtypeStruct((ROUTE_ROWS, LANES), F32),
        ],
        scratch_shapes=[pltpu.VMEM((ROUTE_ROWS, LANES), F32)],
        compiler_params=_cparams(("arbitrary",)),
        name="post_mixer",
    )(fox, gla, main, h, gla_gain, wout, cg, wxq, kmem, vmem, wxo, mg, wr, br)


def _dispatch(dest_kmajor, x, n_rows):
    n, width = x.shape
    window = SC_GATHER_WINDOW
    mesh = plsc.VectorSubcoreMesh(core_axis_name="core", subcore_axis_name="subcore")
    workers = mesh.num_cores * mesh.num_subcores
    per_worker = n // workers
    assert n % (workers * window) == 0

    steps = per_worker // window
    assert steps % 2 == 0
    index_buf = pltpu.VMEM((window,), jnp.int32)
    row_buf = pltpu.VMEM((window, width), x.dtype)
    dma = pltpu.SemaphoreType.DMA

    @functools.partial(
        pl.kernel, out_type=jax.ShapeDtypeStruct((n_rows, width), x.dtype), mesh=mesh,
        scratch_types=[index_buf, index_buf, index_buf, index_buf, row_buf, row_buf, dma, dma, dma])
    def scatter(x_hbm, idx_hbm, out_hbm, idx0_a, idx1_a, idx0_b, idx1_b, rows_a, rows_b, sem_a, sem_b, sem_out):
        worker = lax.axis_index("subcore") * mesh.num_cores + lax.axis_index("core")
        base = worker * per_worker
        slots = ((idx0_a, idx1_a, rows_a, sem_a), (idx0_b, idx1_b, rows_b, sem_b))

        def load(step, slot):
            idx0, idx1, rows, sem = slots[slot]
            off = pl.multiple_of(base + step * window, window)
            pltpu.sync_copy(idx_hbm.at[pl.ds(off, window)], idx0)
            pltpu.sync_copy(idx_hbm.at[pl.ds(n + off, window)], idx1)
            pltpu.async_copy(x_hbm.at[pl.ds(off, window)], rows, sem)

        def store(slot):
            idx0, idx1, rows, sem = slots[slot]
            pltpu.make_async_copy(x_hbm.at[pl.ds(0, window)], rows, sem).wait()
            first = pltpu.async_copy(rows, out_hbm.at[idx0], sem_out)
            second = pltpu.async_copy(rows, out_hbm.at[idx1], sem_out)
            first.wait()
            second.wait()

        load(0, 0)

        @pl.loop(0, steps, step=2)
        def _(step):
            load(step + 1, 1)
            store(0)

            @pl.when(step + 2 < steps)
            def _():
                load(step + 2, 0)

            store(1)

    return scatter(x, dest_kmajor)


def _expert_kernel(be_ref, valid_ref, fresh_ref, x_ref, wg_ref, wu_ref, wd_ref, y_ref, wg_b, wu_b, wd_b):
    del be_ref
    i = pl.program_id(0)
    valid = valid_ref[i]

    @pl.when(fresh_ref[i] > 0)
    def _():
        wg_b[...] = wg_ref[...].astype(BF16)
        wu_b[...] = wu_ref[...].astype(BF16)
        wd_b[...] = wd_ref[...].astype(BF16)

    @pl.when(valid > 0)
    def _():
        row = lax.broadcasted_iota(jnp.int32, x_ref.shape, 0)
        lo, hi = _unpack_rows(jnp.where(row < valid, x_ref[...], jnp.uint32(0)))
        lo = lo.astype(BF16)
        hi = hi.astype(BF16)
        half = lo.shape[1]
        g = _dot(lo, wg_b[:half, :]) + _dot(hi, wg_b[half:, :])
        u = _dot(lo, wu_b[:half, :]) + _dot(hi, wu_b[half:, :])
        a = (g * jax.nn.sigmoid(g) * u).astype(BF16)
        y_ref[...] = _pack_rows(_dot(a, wd_b[...]))

    @pl.when(valid <= 0)
    def _():
        y_ref[...] = jnp.zeros_like(y_ref)


def _experts(block_e, valid, xs, wg, wu, wd, layer):
    r, width = xs.shape
    bm = MOE_BLOCK
    d_model, d_exp = wg.shape[-2:]
    fresh = jnp.concatenate([jnp.ones((1,), jnp.int32), (block_e[1:] != block_e[:-1]).astype(jnp.int32)])
    pick = lambda i, be, va, fr: (layer, be[i], 0, 0)
    return pl.pallas_call(
        _expert_kernel,
        grid_spec=pltpu.PrefetchScalarGridSpec(
            num_scalar_prefetch=3,
            grid=(r // bm,),
            in_specs=[
                pl.BlockSpec((bm, width), lambda i, be, va, fr: (i, 0)),
                pl.BlockSpec((None, None, d_model, d_exp), pick),
                pl.BlockSpec((None, None, d_model, d_exp), pick),
                pl.BlockSpec((None, None, d_exp, d_model), pick),
            ],
            out_specs=pl.BlockSpec((bm, width), lambda i, be, va, fr: (i, 0)),
            scratch_shapes=[
                pltpu.VMEM((d_model, d_exp), BF16),
                pltpu.VMEM((d_model, d_exp), BF16),
                pltpu.VMEM((d_exp, d_model), BF16),
            ],
        ),
        out_shape=jax.ShapeDtypeStruct((r, width), jnp.uint32),
        compiler_params=_cparams(("arbitrary",)),
        name="moe_experts",
    )(block_e, valid, fresh, xs, wg, wu, wd)


def _sc_gather_rows(table, idx):
    m = idx.shape[0]
    width = table.shape[1]
    window = SC_GATHER_WINDOW
    mesh = plsc.VectorSubcoreMesh(core_axis_name="core", subcore_axis_name="subcore")
    workers = mesh.num_cores * mesh.num_subcores
    per_worker = m // workers
    assert m % (workers * window) == 0

    steps = per_worker // window
    assert steps % 2 == 0
    index_buf = pltpu.VMEM((window,), jnp.int32)
    row_buf = pltpu.VMEM((window, width), table.dtype)
    dma = pltpu.SemaphoreType.DMA

    @functools.partial(
        pl.kernel, out_type=jax.ShapeDtypeStruct((m, width), table.dtype), mesh=mesh,
        scratch_types=[index_buf, index_buf, row_buf, row_buf, dma, dma])
    def gather(table_hbm, idx_hbm, out_hbm, idx_a, idx_b, rows_a, rows_b, sem_a, sem_b):
        worker = lax.axis_index("subcore") * mesh.num_cores + lax.axis_index("core")
        base = worker * per_worker
        slots = ((idx_a, rows_a, sem_a), (idx_b, rows_b, sem_b))

        def fetch(step, slot):
            idx, rows, sem = slots[slot]
            off = pl.multiple_of(base + step * window, window)
            pltpu.sync_copy(idx_hbm.at[pl.ds(off, window)], idx)
            pltpu.async_copy(table_hbm.at[idx], rows, sem)

        def flush(step, slot):
            idx, rows, sem = slots[slot]
            off = pl.multiple_of(base + step * window, window)
            pltpu.make_async_copy(table_hbm.at[idx], rows, sem).wait()
            pltpu.sync_copy(rows, out_hbm.at[pl.ds(off, window)])

        fetch(0, 0)

        @pl.loop(0, steps, step=2)
        def _(step):
            fetch(step + 1, 1)
            flush(step, 0)

            @pl.when(step + 2 < steps)
            def _():
                fetch(step + 2, 0)

            flush(step + 1, 1)

    return gather(table, idx)


def _final_kernel(h_ref, gate_ref, gain_ref, y0_ref, y1_ref, o_ref):
    o_ref[...] = _rms(_moe_sum(h_ref[...], gate_ref[...], y0_ref[...], y1_ref[...]), gain_ref[...])


def _final(h, gates, gain, picked):
    n, d_model = h.shape
    tc = MOVE_TILE
    steps = n // tc
    return pl.pallas_call(
        _final_kernel,
        grid=(steps,),
        in_specs=[
            pl.BlockSpec((tc, d_model), lambda i: (i, 0)),
            pl.BlockSpec((ROUTE_WIDTH, tc), lambda i: (0, i)),
            pl.BlockSpec((1, d_model), lambda i: (0, 0)),
            pl.BlockSpec((tc, d_model // 2), lambda i: (i, 0)),
            pl.BlockSpec((tc, d_model // 2), lambda i: (i + steps, 0)),
        ],
        out_specs=pl.BlockSpec((tc, d_model), lambda i: (i, 0)),
        out_shape=jax.ShapeDtypeStruct((n, d_model), F32),
        compiler_params=_cparams(("arbitrary",)),
        name="moe_final",
    )(h, gates, gain, picked, picked)


def _routing_tables(route, cnt, n_rows):
    bm = MOE_BLOCK
    expert = route[0:2].astype(jnp.int32)
    rank = route[4:6].astype(jnp.int32)
    counts = cnt[N_GROUPS:N_GROUPS + N_EXPERTS, 0].astype(jnp.int32)
    padded = (counts + bm - 1) // bm * bm
    pad_ends = jnp.cumsum(padded)
    pad_starts = pad_ends - padded
    ids = jnp.arange(N_EXPERTS, dtype=jnp.int32)
    start_of = jnp.sum(jnp.where(expert[..., None] == ids, pad_starts, 0), axis=-1)
    dest = (start_of + rank).reshape(-1).astype(jnp.int32)
    block_row = jnp.arange(n_rows // bm, dtype=jnp.int32) * bm
    block_e = jnp.minimum(jnp.sum((pad_ends[None, :] <= block_row[:, None]).astype(jnp.int32), axis=-1),
                          N_EXPERTS - 1)
    row_end = jnp.sum(jnp.where(block_e[:, None] == ids, pad_starts + counts, 0), axis=-1)
    valid = jnp.clip(row_end - block_row, 0, bm).astype(jnp.int32)
    return dest, block_e, valid


def kernel(x, mem, mem_norm, mix_norm, w_in, b_forget, w_alpha_up, b_alpha, fox_out_gain, gla_out_gain, w_out,
           cross_norm, w_xq, w_xk, w_xv, w_xo, moe_norm, w_router_group, b_router_group, w_router_expert,
           b_router_expert, w_expert_gate, w_expert_up, w_expert_down, final_norm):
    batch, seq, d_model = x.shape
    mem_len = mem.shape[1]
    depth = w_in.shape[0]
    n = batch * seq
    assert seq % FOX_TILE == 0 and seq % IN_TILE == 0 and seq % POST_TILE == 0 and seq % GLA_CHUNK == 0
    assert n % MOVE_TILE == 0 and d_model % LANES == 0

    c0 = 3 * FOX_WIDTH
    c1 = c0 + FOX_HEADS
    c2 = c1 + 2 * GLA_QK + 2 * GLA_V
    w_main = jnp.concatenate([w_in[:, :, :c0], w_in[:, :, c1:c2]], axis=-1).astype(BF16)
    pad = LANES - FOX_HEADS - GLA_RANK
    w_small = jnp.concatenate([w_in[:, :, c0:c1], w_in[:, :, c2:], jnp.zeros((depth, d_model, pad), F32)],
                              axis=-1).astype(BF16)
    w_up = jnp.concatenate([jnp.zeros((depth, FOX_HEADS, GLA_QK), F32), w_alpha_up,
                            jnp.zeros((depth, pad, GLA_QK), F32)], axis=1).astype(BF16)
    b_f = jnp.pad(b_forget, ((0, 0), (0, LANES - FOX_HEADS)))[:, None, :]
    b_a = b_alpha[:, None, :]
    w_r = jnp.concatenate([w_router_group, w_router_expert,
                           jnp.zeros((depth, d_model, LANES - N_GROUPS - N_EXPERTS), F32)], axis=-1)
    w_rh = w_r.astype(BF16)
    w_rs = jnp.concatenate([w_rh, (w_r - w_rh.astype(F32)).astype(BF16)], axis=-1)
    b_r = jnp.pad(jnp.concatenate([b_router_group, b_router_expert], axis=-1),
                  ((0, 0), (0, LANES - N_GROUPS - N_EXPERTS)))[:, None, :]
    w_out_b = w_out.astype(BF16)
    w_xq_b = w_xq.astype(BF16)
    w_xo_b = w_xo.astype(BF16)
    mix_g = mix_norm[:, None, :]
    cross_g = cross_norm[:, None, :]
    moe_g = moe_norm[:, None, :]
    gla_g = gla_out_gain[:, None, :]

    kmem, vmem = _mem_kv(mem.reshape(batch * mem_len, d_model), mem_norm[None, :],
                         w_xk.astype(BF16), w_xv.astype(BF16), batch, mem_len)

    n_rows = 2 * n + N_EXPERTS * MOE_BLOCK
    h = x.reshape(n, d_model)
    moe = None
    for l in range(depth):
        if moe is None:
            main, c, kl, dec = _in_proj(h, mix_g, w_main, w_small, w_up, b_f, b_a, l, seq)
        else:
            h, main, c, kl, dec = _in_proj(h, mix_g, w_main, w_small, w_up, b_f, b_a, l, seq, moe)
        fox = _fox_attention(main, c, fox_out_gain[l][None, :], batch, seq)
        gla = _gla(main, kl, dec, batch, seq)
        h2, hn2, route, cnt = _post(fox, gla, main, h, gla_g, w_out_b, cross_g, w_xq_b, kmem, vmem, w_xo_b, moe_g,
                                    w_rs, b_r, seq, mem_len, l)
        dest, block_e, valid = _routing_tables(route, cnt, n_rows)
        xs = _dispatch(dest, hn2, n_rows)
        y = _experts(block_e, valid, xs, w_expert_gate, w_expert_up, w_expert_down, l)
        h, moe = h2, (route, _sc_gather_rows(y, dest))
    return _final(h, moe[0], final_norm[None, :], moe[1]).reshape(batch, seq, d_model)
```

```python
import functools

import jax
import jax.numpy as jnp
from jax import lax
from jax.experimental import pallas as pl
from jax.experimental.pallas import tpu as pltpu
from jax.experimental.pallas import tpu_sc as plsc

F32 = jnp.float32
BF16 = jnp.bfloat16
EPS = 1e-6
LOG2E = 1.4426950408889634

FOX_HEADS = 8
FOX_DIM = 64
FOX_WIDTH = FOX_HEADS * FOX_DIM
GLA_HEADS = 4
GLA_DK = 64
GLA_DV = 128
GLA_QK = GLA_HEADS * GLA_DK
GLA_V = GLA_HEADS * GLA_DV
GLA_RANK = 16
GLA_TAU = 16.0
GLA_CHUNK = 64
X_HEADS = 4
X_DIM = 128
X_WIDTH = X_HEADS * X_DIM
N_GROUPS = 4
GROUP_SIZE = 4
N_EXPERTS = N_GROUPS * GROUP_SIZE
MAIN_WIDTH = 3 * FOX_WIDTH + 2 * GLA_QK + 2 * GLA_V

LANES = 128
ROUTE_WIDTH = 8
ROUTE_ROWS = 32
VMEM_LIMIT = 56 * 1024 * 1024

IN_TILE = 1024
FOX_TILE = 512
FOX_SLAB = 64
POST_TILE = 1024
MOE_BLOCK = 512
MOVE_TILE = 1024
SC_GATHER_WINDOW = 64

def _cparams(sem):
    return pltpu.CompilerParams(dimension_semantics=sem, vmem_limit_bytes=VMEM_LIMIT)


def _rms(x, gain):
    return x * lax.rsqrt(jnp.mean(x * x, axis=-1, keepdims=True) + EPS) * gain


def _log_sigmoid(x):
    return jnp.minimum(x, 0.0) - jnp.log1p(jnp.exp(-jnp.abs(x)))


def _dot(a, b):
    return jnp.dot(a, b, preferred_element_type=F32)


def _dot_nt(a, b):
    return lax.dot_general(a, b, (((1,), (1,)), ((), ())), preferred_element_type=F32)


def _pack_rows(x):
    half = x.shape[1] // 2
    lo = lax.bitcast_convert_type(x[:, :half].astype(BF16).astype(F32), jnp.uint32)
    hi = lax.bitcast_convert_type(x[:, half:].astype(BF16).astype(F32), jnp.uint32)
    return (lo >> 16) | hi


def _unpack_rows(w):
    lo = lax.bitcast_convert_type(w << 16, F32)
    hi = lax.bitcast_convert_type(w & jnp.uint32(0xFFFF0000), F32)
    return lo, hi


def _mem_kv_kernel(mem_ref, gain_ref, wk_ref, wv_ref, k_ref, v_ref):
    mn = _rms(mem_ref[...], gain_ref[...]).astype(BF16)
    for l in range(wk_ref.shape[0]):
        k_ref[l] = _dot(mn, wk_ref[l]).astype(BF16)
        v_ref[l] = _dot(mn, wv_ref[l]).astype(BF16)


def _mem_kv(mem2d, gain, wk, wv, batch, mem_len):
    depth, d_model, width = wk.shape
    out = jax.ShapeDtypeStruct((depth, batch * mem_len, width), BF16)
    return pl.pallas_call(
        _mem_kv_kernel,
        grid=(batch,),
        in_specs=[
            pl.BlockSpec((mem_len, d_model), lambda b: (b, 0)),
            pl.BlockSpec((1, d_model), lambda b: (0, 0)),
            pl.BlockSpec((depth, d_model, width), lambda b: (0, 0, 0)),
            pl.BlockSpec((depth, d_model, width), lambda b: (0, 0, 0)),
        ],
        out_specs=[
            pl.BlockSpec((depth, mem_len, width), lambda b: (0, b, 0)),
            pl.BlockSpec((depth, mem_len, width), lambda b: (0, b, 0)),
        ],
        out_shape=[out, out],
        compiler_params=_cparams(("arbitrary",)),
        name="mem_kv",
    )(mem2d, gain, wk, wv)


def _moe_sum(h, route, y0_packed, y1_packed):
    gate = jnp.transpose(route)
    y0 = jnp.concatenate(_unpack_rows(y0_packed), axis=1)
    y1 = jnp.concatenate(_unpack_rows(y1_packed), axis=1)
    return h + gate[:, 2:3] * y0 + gate[:, 3:4] * y1


def _in_proj_body(h, gain_ref, wmain_ref, wsmall_ref, wup_ref, bf_ref, ba_ref, main_ref, c_ref, kl_ref, dec_ref,
                  carry_ref, tiles_per_seq):
    tm = h.shape[0]
    cs = GLA_CHUNK
    xn = _rms(h, gain_ref[...]).astype(BF16)
    small = _dot(xn, wsmall_ref[...])
    lane = lax.broadcasted_iota(jnp.int32, small.shape, 1)
    c = jnp.where(lane < FOX_HEADS, _log_sigmoid(small + bf_ref[...]), 0.0)
    rowi = lax.broadcasted_iota(jnp.int32, c.shape, 0)
    shift = 1
    while shift < tm:
        c = c + jnp.where(rowi >= shift, pltpu.roll(c, shift, axis=0), 0.0)
        shift *= 2
    starts_sequence = pl.program_id(0) % tiles_per_seq == 0
    c = c + jnp.where(starts_sequence, 0.0, carry_ref[...])
    carry_ref[...] = c[tm - 1:tm, :]
    c_ref[...] = jnp.transpose(c)[:FOX_HEADS, :]
    a = _dot(small.astype(BF16), wup_ref[...]) + ba_ref[...]
    b = _log_sigmoid(a) * (1.0 / GLA_TAU)
    pos = lax.broadcasted_iota(jnp.int32, b.shape, 0) % cs
    shift = 1
    while shift < cs:
        b = b + jnp.where(pos >= shift, pltpu.roll(b, shift, axis=0), 0.0)
        shift *= 2
    dec = jnp.exp(b.reshape(tm // cs, cs, GLA_QK)[:, cs - 1:cs, :])
    q0 = 3 * FOX_WIDTH
    k0 = q0 + GLA_QK
    qk = _dot(xn, wmain_ref[:, q0:k0 + GLA_QK])
    main_ref[:, q0:k0] = (qk[:, :GLA_QK] * jnp.exp(b) * (GLA_DK ** -0.5)).astype(BF16)
    ke = qk[:, GLA_QK:] * jnp.exp(-b)
    main_ref[:, k0:k0 + GLA_QK] = ke.astype(BF16)
    kl_ref[...] = (ke.reshape(tm // cs, cs, GLA_QK) * dec).reshape(tm, GLA_QK).astype(BF16)
    dec_ref[...] = dec.reshape(tm // cs, GLA_QK)
    step = 512
    for lo in list(range(0, q0, step)) + list(range(k0 + GLA_QK, MAIN_WIDTH, step)):
        main_ref[:, lo:lo + step] = _dot(xn, wmain_ref[:, lo:lo + step]).astype(BF16)


def _in_proj_kernel(h_ref, *refs, tiles_per_seq):
    _in_proj_body(h_ref[...], *refs, tiles_per_seq)


def _in_proj_after_moe_kernel(h_ref, gate_ref, y0_ref, y1_ref, gain_ref, wmain_ref, wsmall_ref, wup_ref, bf_ref,
                              ba_ref, hout_ref, main_ref, c_ref, kl_ref, dec_ref, carry_ref, *, tiles_per_seq):
    h = _moe_sum(h_ref[...], gate_ref[...], y0_ref[...], y1_ref[...])
    hout_ref[...] = h
    _in_proj_body(h, gain_ref, wmain_ref, wsmall_ref, wup_ref, bf_ref, ba_ref, main_ref, c_ref, kl_ref, dec_ref,
                  carry_ref, tiles_per_seq)


def _in_proj(h, gain, wmain, wsmall, wup, bf, ba, layer, seq, moe=None):
    n, d_model = h.shape
    tm = IN_TILE
    steps = n // tm
    chunks = tm // GLA_CHUNK
    tiles_per_seq = seq // tm
    pick = lambda i: (layer, 0, 0)
    row_block = lambda width, rows=tm: pl.BlockSpec((rows, width), lambda i: (i, 0))
    weight_specs = [
        pl.BlockSpec((None, 1, d_model), pick),
        pl.BlockSpec((None, d_model, MAIN_WIDTH), pick),
        pl.BlockSpec((None, d_model, LANES), pick),
        pl.BlockSpec((None, LANES, GLA_QK), pick),
        pl.BlockSpec((None, 1, LANES), pick),
        pl.BlockSpec((None, 1, GLA_QK), pick),
    ]
    out_specs = [row_block(MAIN_WIDTH), pl.BlockSpec((FOX_HEADS, tm), lambda i: (0, i)), row_block(GLA_QK),
                 row_block(GLA_QK, chunks)]
    out_shape = [
        jax.ShapeDtypeStruct((n, MAIN_WIDTH), BF16),
        jax.ShapeDtypeStruct((FOX_HEADS, n), F32),
        jax.ShapeDtypeStruct((n, GLA_QK), BF16),
        jax.ShapeDtypeStruct((n // GLA_CHUNK, GLA_QK), F32),
    ]
    weights = (gain, wmain, wsmall, wup, bf, ba)
    carry = [pltpu.VMEM((1, LANES), F32)]
    if moe is None:
        return pl.pallas_call(
            functools.partial(_in_proj_kernel, tiles_per_seq=tiles_per_seq),
            grid=(steps,), in_specs=[row_block(d_model)] + weight_specs,
            out_specs=out_specs, out_shape=out_shape, scratch_shapes=carry,
            compiler_params=_cparams(("arbitrary",)), name="in_proj",
        )(h, *weights)
    gates, picked = moe
    half = d_model // 2
    return pl.pallas_call(
        functools.partial(_in_proj_after_moe_kernel, tiles_per_seq=tiles_per_seq), grid=(steps,),
        in_specs=[row_block(d_model), pl.BlockSpec((ROUTE_WIDTH, tm), lambda i: (0, i)), row_block(half),
                  pl.BlockSpec((tm, half), lambda i: (i + steps, 0))] + weight_specs,
        out_specs=[row_block(d_model)] + out_specs,
        out_shape=[jax.ShapeDtypeStruct((n, d_model), F32)] + out_shape, scratch_shapes=carry,
        compiler_params=_cparams(("arbitrary",)), name="in_proj_after_moe",
    )(h, gates, picked, picked, *weights)


def _fox_kernel(q_ref, k_ref, v_ref, c_ref, gain_ref, o_ref, q2_ref, s_ref, p_ref, su_ref, pu_ref, alpha_ref, m_ref,
                l_ref, acc_ref):
    tq = FOX_TILE
    hq = tq // 2
    rows = 2 * tq
    slab = FOX_SLAB
    nq = q_ref.shape[0] // tq
    lane = lax.broadcasted_iota(jnp.int32, (1, LANES), 1)
    first = lane < FOX_DIM
    scale = FOX_DIM ** -0.5 * LOG2E
    for qi in range(nq):
        for part in range(2):
            q = q_ref[qi * tq + part * hq:qi * tq + (part + 1) * hq, :].astype(F32) * scale
            q2_ref[qi, part * tq:part * tq + hq, :] = jnp.where(first, q, 0.0).astype(BF16)
            q2_ref[qi, part * tq + hq:(part + 1) * tq, :] = jnp.where(first, 0.0, q).astype(BF16)

    head_row = lax.broadcasted_iota(jnp.int32, (FOX_HEADS, tq), 0)
    pair = pl.program_id(1)

    def key_bias(j):
        cj = c_ref[:, j * tq:(j + 1) * tq] * LOG2E
        c0 = jnp.sum(jnp.where(head_row == 2 * pair, cj, 0.0), axis=0, keepdims=True)
        c1 = jnp.sum(jnp.where(head_row == 2 * pair + 1, cj, 0.0), axis=0, keepdims=True)
        return c0, c1

    def scores(qi, j):
        c0, c1 = key_bias(j)
        if j < qi:
            d = _dot_nt(q2_ref[qi], k_ref[j * tq:(j + 1) * tq, :])
            for g in range(4):
                s_ref[g * hq:(g + 1) * hq, :] = d[g * hq:(g + 1) * hq] - (c0 if g % 2 == 0 else c1)
        else:
            du = _dot_nt(q2_ref[qi, :tq, :], k_ref[j * tq:j * tq + hq, :])
            su_ref[:hq, :] = du[:hq] - c0[:, :hq]
            su_ref[hq:, :] = du[hq:] - c1[:, :hq]
            dl = _dot_nt(q2_ref[qi, tq:, :], k_ref[j * tq:(j + 1) * tq, :])
            s_ref[tq:tq + hq, :] = dl[:hq] - c0
            s_ref[tq + hq:, :] = dl[hq:] - c1

    def weighted_values(qi, j):
        par = qi % 2
        if j < qi:
            acc_ref[par] = alpha_ref[par] * acc_ref[par] + _dot(p_ref[...], v_ref[j * tq:(j + 1) * tq, :])
        else:
            acc_ref[par, :tq, :] = (alpha_ref[par, :tq, :] * acc_ref[par, :tq, :]
                                    + _dot(pu_ref[...], v_ref[j * tq:j * tq + hq, :]))
            acc_ref[par, tq:, :] = (alpha_ref[par, tq:, :] * acc_ref[par, tq:, :]
                                    + _dot(p_ref[tq:, :], v_ref[j * tq:(j + 1) * tq, :]))

    def softmax_slab(par, sl, s, row0):
        cols = s.shape[1]
        if row0 is not None:
            row = lax.broadcasted_iota(jnp.int32, (slab, cols), 0) + row0
            col = lax.broadcasted_iota(jnp.int32, (slab, cols), 1)
            s = jnp.where(row >= col, s, -jnp.inf)
        m_old = m_ref[par, sl, :]
        m_new = jnp.maximum(m_old, jnp.max(s, axis=-1, keepdims=True))
        alpha = jnp.exp2(m_old - m_new)
        p = jnp.exp2(s - jnp.concatenate([m_new] * (cols // LANES), axis=1))
        l_ref[par, sl, :] = alpha * l_ref[par, sl, :] + jnp.sum(p, axis=-1, keepdims=True)
        m_ref[par, sl, :] = m_new
        alpha_ref[par, sl, :] = alpha
        return p.astype(BF16)

    def softmax(qi, masked):
        par = qi % 2
        for r in range(rows // slab):
            sl = slice(r * slab, (r + 1) * slab)
            within = (r * slab) % hq
            if not masked:
                p_ref[sl, :] = softmax_slab(par, sl, s_ref[sl, :], None)
            elif r * slab < tq:
                pu_ref[sl, :] = softmax_slab(par, sl, su_ref[sl, :], within)
            else:
                p_ref[sl, :] = softmax_slab(par, sl, s_ref[sl, :], hq + within)

    def finalize(qi):
        par = qi % 2
        o2 = acc_ref[par] / l_ref[par]
        o = jnp.concatenate([jnp.where(first, o2[:hq], o2[hq:tq]),
                             jnp.where(first, o2[tq:tq + hq], o2[tq + hq:])], axis=0)
        sq = o * o
        ss0 = jnp.sum(jnp.where(first, sq, 0.0), axis=-1, keepdims=True)
        ss1 = jnp.sum(jnp.where(first, 0.0, sq), axis=-1, keepdims=True)
        ms = jnp.where(first, ss0, ss1) * (1.0 / FOX_DIM)
        o_ref[qi * tq:(qi + 1) * tq, :] = (o * lax.rsqrt(ms + EPS) * gain_ref[...]).astype(BF16)

    steps = [(qi, j) for qi in range(nq) for j in range(qi + 1)]
    scores(*steps[0])
    for t, (qi, j) in enumerate(steps):
        if t > 0:
            weighted_values(*steps[t - 1])
            if steps[t - 1][0] != qi:
                finalize(steps[t - 1][0])
        if j == 0:
            par = qi % 2
            m_ref[par] = jnp.full(m_ref.shape[1:], -jnp.inf, F32)
            l_ref[par] = jnp.zeros(l_ref.shape[1:], F32)
            acc_ref[par] = jnp.zeros(acc_ref.shape[1:], F32)
        softmax(qi, masked=(j == qi))
        if t + 1 < len(steps):
            scores(*steps[t + 1])
    weighted_values(*steps[-1])
    finalize(steps[-1][0])


def _fox_attention(main, c, gain, batch, seq):
    n = main.shape[0]
    tq = FOX_TILE
    nq = seq // tq
    pairs = FOX_HEADS // 2
    k_off = FOX_WIDTH // LANES
    v_off = 2 * FOX_WIDTH // LANES
    stat = pltpu.VMEM((2, 2 * tq, LANES), F32)
    return pl.pallas_call(
        _fox_kernel,
        grid=(batch, pairs),
        in_specs=[
            pl.BlockSpec((seq, LANES), lambda b, p: (b, p)),
            pl.BlockSpec((seq, LANES), lambda b, p: (b, k_off + p)),
            pl.BlockSpec((seq, LANES), lambda b, p: (b, v_off + p)),
            pl.BlockSpec((FOX_HEADS, seq), lambda b, p: (0, b)),
            pl.BlockSpec((1, LANES), lambda b, p: (0, p)),
        ],
        out_specs=pl.BlockSpec((seq, LANES), lambda b, p: (b, p)),
        out_shape=jax.ShapeDtypeStruct((n, FOX_WIDTH), BF16),
        scratch_shapes=[
            pltpu.VMEM((nq, 2 * tq, LANES), BF16),
            pltpu.VMEM((2 * tq, tq), F32),
            pltpu.VMEM((2 * tq, tq), BF16),
            pltpu.VMEM((tq, tq // 2), F32),
            pltpu.VMEM((tq, tq // 2), BF16),
            stat, stat, stat, stat,
        ],
        compiler_params=_cparams(("arbitrary", "arbitrary")),
        name="fox_attention",
    )(main, main, main, c, gain)


def _gla_kernel(qe_ref, ke_ref, v_ref, kl_ref, dec_ref, o_ref):
    seq = qe_ref.shape[0]
    cs = GLA_CHUNK
    nc = seq // cs
    width = 2 * GLA_DK

    lane = lax.broadcasted_iota(jnp.int32, (1, width), 1)
    first = lane < GLA_DK
    row = lax.broadcasted_iota(jnp.int32, (2 * cs, cs), 0)
    col = lax.broadcasted_iota(jnp.int32, (2 * cs, cs), 1)
    tril2 = jnp.where(row >= cs, row - cs, row) >= col
    srow = lax.broadcasted_iota(jnp.int32, (2 * GLA_DV, width), 0)
    scol = lax.broadcasted_iota(jnp.int32, (2 * GLA_DV, width), 1)
    same_head = (srow >= GLA_DV) == (scol >= GLA_DK)
    unroll = 8

    def chunks(ci, st):
        r0s = [pl.multiple_of((ci * unroll + u) * cs, cs) for u in range(unroll)]
        qes = [qe_ref[pl.ds(r0, cs), :] for r0 in r0s]
        vs = [v_ref[pl.ds(r0, cs), :] for r0 in r0s]
        atts, upds = [], []
        for u in range(unroll):
            zero = jnp.zeros_like(qes[u])
            q2 = jnp.concatenate([jnp.where(first, qes[u], zero), jnp.where(first, zero, qes[u])], axis=0)
            atts.append(jnp.where(tril2, _dot_nt(q2, ke_ref[pl.ds(r0s[u], cs), :]), 0.0).astype(BF16))
        for u in range(unroll):
            upds.append(lax.dot_general(vs[u], kl_ref[pl.ds(r0s[u], cs), :], (((0,), (0,)), ((), ())),
                                        preferred_element_type=F32))
        ois = [_dot(atts[u], vs[u]) for u in range(unroll)]
        for u in range(unroll):
            o = _dot_nt(qes[u], st.astype(BF16))
            o = o + jnp.concatenate([ois[u][:cs, :GLA_DV], ois[u][cs:, GLA_DV:]], axis=1)
            o_ref[pl.ds(r0s[u], cs), :] = o.astype(BF16)
            st = st * dec_ref[pl.ds(ci * unroll + u, 1), :] + jnp.where(same_head, upds[u], 0.0)
        return st

    lax.fori_loop(0, nc // unroll, chunks, jnp.zeros((2 * GLA_DV, width), F32))


def _gla(main, kl, dec, batch, seq):
    n = main.shape[0]
    pairs = GLA_HEADS // 2
    q_off = 3 * FOX_WIDTH // LANES
    k_off = q_off + GLA_QK // LANES
    pv = 2 * GLA_DV
    v_off = (3 * FOX_WIDTH + 2 * GLA_QK) // pv
    return pl.pallas_call(
        _gla_kernel,
        grid=(batch, pairs),
        in_specs=[
            pl.BlockSpec((seq, LANES), lambda b, p: (b, q_off + p)),
            pl.BlockSpec((seq, LANES), lambda b, p: (b, k_off + p)),
            pl.BlockSpec((seq, pv), lambda b, p: (b, v_off + p)),
            pl.BlockSpec((seq, LANES), lambda b, p: (b, p)),
            pl.BlockSpec((seq // GLA_CHUNK, LANES), lambda b, p: (b, p)),
        ],
        out_specs=pl.BlockSpec((seq, pv), lambda b, p: (b, p)),
        out_shape=jax.ShapeDtypeStruct((n, GLA_V), BF16),
        compiler_params=_cparams(("arbitrary", "arbitrary")),
        name="gla",
    )(main, main, main, kl, dec)


def _post_kernel(fox_ref, gla_ref, gg_ref, h_ref, gg_gain_ref, wout_ref, cg_ref, wxq_ref, k_ref, v_ref, wxo_ref,
                 mg_ref, wr_ref, br_ref, h2_ref, hn_ref, route_ref, cnt_ref, carry_ref):
    tm = h_ref.shape[0]

    @pl.when(pl.program_id(0) == 0)
    def _():
        carry_ref[...] = jnp.zeros_like(carry_ref)

    raw = gla_ref[...].astype(F32)
    normed = []
    for hh in range(GLA_HEADS):
        oh = raw[:, hh * GLA_DV:(hh + 1) * GLA_DV]
        normed.append(oh * lax.rsqrt(jnp.mean(oh * oh, axis=-1, keepdims=True) + EPS))
    g = gg_ref[...].astype(F32)
    gla = (jnp.concatenate(normed, axis=1) * gg_gain_ref[...] * (g * jax.nn.sigmoid(g))).astype(BF16)
    y = _dot(fox_ref[...], wout_ref[0:FOX_WIDTH, :]) + _dot(gla, wout_ref[FOX_WIDTH:, :])
    h1 = h_ref[...] + y
    hn = _rms(h1, cg_ref[...]).astype(BF16)
    q = _dot(hn, wxq_ref[...]).astype(BF16)
    xscale = X_DIM ** -0.5
    heads = []
    for hh in range(X_HEADS):
        sl = slice(hh * X_DIM, (hh + 1) * X_DIM)
        s = _dot_nt(q[:, sl], k_ref[:, sl]) * xscale
        p = jnp.exp(s - jnp.max(s, axis=-1, keepdims=True))
        heads.append(_dot(p.astype(BF16), v_ref[:, sl]) / jnp.sum(p, axis=-1, keepdims=True))
    o = jnp.concatenate(heads, axis=1).astype(BF16)
    h2 = h1 + _dot(o, wxo_ref[...])
    h2_ref[...] = h2
    hn2 = _rms(h2, mg_ref[...])
    hn_ref[...] = _pack_rows(hn2)

    xh = hn2.astype(BF16)
    xl = (hn2 - xh.astype(F32)).astype(BF16)
    both_w = _dot(jnp.concatenate([xh, xl], axis=0), wr_ref[...])
    logits = both_w[:tm, :LANES] + both_w[:tm, LANES:] + both_w[tm:, :LANES] + both_w[tm:, LANES:] + br_ref[...]
    lt = jnp.transpose(logits)[:ROUTE_ROWS, :]
    row = lax.broadcasted_iota(jnp.int32, (ROUTE_ROWS, tm), 0)
    neg = -jnp.inf
    gl = jnp.where(row < N_GROUPS, lt, neg)
    gmax = jnp.max(gl, axis=0, keepdims=True)
    ge = jnp.exp(gl - gmax)
    gprob = ge / jnp.sum(ge, axis=0, keepdims=True)
    pmax = jnp.max(gprob, axis=0, keepdims=True)
    grp = jnp.min(jnp.where(gprob == pmax, row, ROUTE_ROWS), axis=0, keepdims=True)
    in_grp = (row >= N_GROUPS) & (row < N_GROUPS + N_EXPERTS) & (((row - N_GROUPS) // GROUP_SIZE) == grp)
    el = jnp.where(in_grp, lt, neg)
    emax = jnp.max(el, axis=0, keepdims=True)
    ee = jnp.exp(el - emax)
    eprob = ee / jnp.sum(ee, axis=0, keepdims=True)
    p1 = jnp.max(eprob, axis=0, keepdims=True)
    row1 = jnp.min(jnp.where(in_grp & (eprob == p1), row, ROUTE_ROWS), axis=0, keepdims=True)
    rest = jnp.where(in_grp & (row != row1), eprob, -1.0)
    p2 = jnp.max(rest, axis=0, keepdims=True)
    row2 = jnp.min(jnp.where(rest == p2, row, ROUTE_ROWS), axis=0, keepdims=True)
    g1 = pmax * p1 / (p1 + p2)
    g2 = pmax * p2 / (p1 + p2)

    oh1 = row == row1
    oh2 = row == row2
    both = (oh1 | oh2).astype(BF16)
    srow = lax.broadcasted_iota(jnp.int32, (tm, tm), 0)
    scol = lax.broadcasted_iota(jnp.int32, (tm, tm), 1)
    earlier = (srow < scol).astype(BF16)
    carry = carry_ref[...]
    seen = _dot(both, earlier) + jnp.concatenate([carry] * (tm // LANES), axis=1)
    rank1 = jnp.sum(jnp.where(oh1, seen, 0.0), axis=0, keepdims=True)
    rank2 = jnp.sum(jnp.where(oh2, seen, 0.0), axis=0, keepdims=True)
    carry = carry + jnp.sum(both.astype(F32), axis=1, keepdims=True)
    carry_ref[...] = carry
    cnt_ref[...] = carry

    e1 = (row1 - N_GROUPS).astype(F32)
    e2 = (row2 - N_GROUPS).astype(F32)
    zero = jnp.zeros_like(g1)
    route_ref[...] = jnp.concatenate([e1, e2, g1, g2, rank1, rank2, zero, zero], axis=0)


def _post(fox, gla, main, h, gla_gain, wout, cg, wxq, kmem, vmem, wxo, mg, wr, br, seq, mem_len, layer):
    n, d_model = h.shape
    tm = POST_TILE
    per_seq = seq // tm
    const = lambda i: (0, 0)
    pick = lambda i: (layer, 0, 0)
    return pl.pallas_call(
        _post_kernel,
        grid=(n // tm,),
        in_specs=[
            pl.BlockSpec((tm, FOX_WIDTH), lambda i: (i, 0)),
            pl.BlockSpec((tm, GLA_V), lambda i: (i, 0)),
            pl.BlockSpec((tm, GLA_V), lambda i: (i, MAIN_WIDTH // GLA_V - 1)),
            pl.BlockSpec((tm, d_model), lambda i: (i, 0)),
            pl.BlockSpec((None, 1, GLA_V), pick),
            pl.BlockSpec((None, FOX_WIDTH + GLA_V, d_model), pick),
            pl.BlockSpec((None, 1, d_model), pick),
            pl.BlockSpec((None, d_model, X_WIDTH), pick),
            pl.BlockSpec((None, mem_len, X_WIDTH), lambda i: (layer, i // per_seq, 0)),
            pl.BlockSpec((None, mem_len, X_WIDTH), lambda i: (layer, i // per_seq, 0)),
            pl.BlockSpec((None, X_WIDTH, d_model), pick),
            pl.BlockSpec((None, 1, d_model), pick),
            pl.BlockSpec((None, d_model, 2 * LANES), pick),
            pl.BlockSpec((None, 1, LANES), pick),
        ],
        out_specs=[
            pl.BlockSpec((tm, d_model), lambda i: (i, 0)),
            pl.BlockSpec((tm, d_model // 2), lambda i: (i, 0)),
            pl.BlockSpec((ROUTE_WIDTH, tm), lambda i: (0, i)),
            pl.BlockSpec((ROUTE_ROWS, LANES), const),
        ],
        out_shape=[
            jax.ShapeDtypeStruct((n, d_model), F32),
            jax.ShapeDtypeStruct((n, d_model // 2), jnp.uint32),
            jax.ShapeDtypeStruct((ROUTE_WIDTH, n), F32),
            jax.ShapeDtypeStruct((ROUTE_ROWS, LANES), F32),
        ],
        scratch_shapes=[pltpu.VMEM((ROUTE_ROWS, LANES), F32)],
        compiler_params=_cparams(("arbitrary",)),
        name="post_mixer",
    )(fox, gla, main, h, gla_gain, wout, cg, wxq, kmem, vmem, wxo, mg, wr, br)


def _dispatch(dest_kmajor, x, n_rows):
    n, width = x.shape
    window = SC_GATHER_WINDOW
    mesh = plsc.VectorSubcoreMesh(core_axis_name="core", subcore_axis_name="subcore")
    workers = mesh.num_cores * mesh.num_subcores
    per_worker = n // workers
    assert n % (workers * window) == 0

    steps = per_worker // window
    assert steps % 2 == 0
    index_buf = pltpu.VMEM((window,), jnp.int32)
    row_buf = pltpu.VMEM((window, width), x.dtype)
    dma = pltpu.SemaphoreType.DMA

    @functools.partial(
        pl.kernel, out_type=jax.ShapeDtypeStruct((n_rows, width), x.dtype), mesh=mesh,
        scratch_types=[index_buf, index_buf, index_buf, index_buf, row_buf, row_buf, dma, dma, dma])
    def scatter(x_hbm, idx_hbm, out_hbm, idx0_a, idx1_a, idx0_b, idx1_b, rows_a, rows_b, sem_a, sem_b, sem_out):
        worker = lax.axis_index("subcore") * mesh.num_cores + lax.axis_index("core")
        base = worker * per_worker
        slots = ((idx0_a, idx1_a, rows_a, sem_a), (idx0_b, idx1_b, rows_b, sem_b))

        def load(step, slot):
            idx0, idx1, rows, sem = slots[slot]
            off = pl.multiple_of(base + step * window, window)
            pltpu.sync_copy(idx_hbm.at[pl.ds(off, window)], idx0)
            pltpu.sync_copy(idx_hbm.at[pl.ds(n + off, window)], idx1)
            pltpu.async_copy(x_hbm.at[pl.ds(off, window)], rows, sem)

        def store(slot):
            idx0, idx1, rows, sem = slots[slot]
            pltpu.make_async_copy(x_hbm.at[pl.ds(0, window)], rows, sem).wait()
            first = pltpu.async_copy(rows, out_hbm.at[idx0], sem_out)
            second = pltpu.async_copy(rows, out_hbm.at[idx1], sem_out)
            first.wait()
            second.wait()

        load(0, 0)

        @pl.loop(0, steps, step=2)
        def _(step):
            load(step + 1, 1)
            store(0)

            @pl.when(step + 2 < steps)
            def _():
                load(step + 2, 0)

            store(1)

    return scatter(x, dest_kmajor)


def _expert_kernel(be_ref, valid_ref, fresh_ref, x_ref, wg_ref, wu_ref, wd_ref, y_ref, wg_b, wu_b, wd_b):
    del be_ref
    i = pl.program_id(0)
    valid = valid_ref[i]

    @pl.when(fresh_ref[i] > 0)
    def _():
        wg_b[...] = wg_ref[...].astype(BF16)
        wu_b[...] = wu_ref[...].astype(BF16)
        wd_b[...] = wd_ref[...].astype(BF16)

    @pl.when(valid > 0)
    def _():
        row = lax.broadcasted_iota(jnp.int32, x_ref.shape, 0)
        lo, hi = _unpack_rows(jnp.where(row < valid, x_ref[...], jnp.uint32(0)))
        lo = lo.astype(BF16)
        hi = hi.astype(BF16)
        half = lo.shape[1]
        g = _dot(lo, wg_b[:half, :]) + _dot(hi, wg_b[half:, :])
        u = _dot(lo, wu_b[:half, :]) + _dot(hi, wu_b[half:, :])
        a = (g * jax.nn.sigmoid(g) * u).astype(BF16)
        y_ref[...] = _pack_rows(_dot(a, wd_b[...]))

    @pl.when(valid <= 0)
    def _():
        y_ref[...] = jnp.zeros_like(y_ref)


def _experts(block_e, valid, xs, wg, wu, wd, layer):
    r, width = xs.shape
    bm = MOE_BLOCK
    d_model, d_exp = wg.shape[-2:]
    fresh = jnp.concatenate([jnp.ones((1,), jnp.int32), (block_e[1:] != block_e[:-1]).astype(jnp.int32)])
    pick = lambda i, be, va, fr: (layer, be[i], 0, 0)
    return pl.pallas_call(
        _expert_kernel,
        grid_spec=pltpu.PrefetchScalarGridSpec(
            num_scalar_prefetch=3,
            grid=(r // bm,),
            in_specs=[
                pl.BlockSpec((bm, width), lambda i, be, va, fr: (i, 0)),
                pl.BlockSpec((None, None, d_model, d_exp), pick),
                pl.BlockSpec((None, None, d_model, d_exp), pick),
                pl.BlockSpec((None, None, d_exp, d_model), pick),
            ],
            out_specs=pl.BlockSpec((bm, width), lambda i, be, va, fr: (i, 0)),
            scratch_shapes=[
                pltpu.VMEM((d_model, d_exp), BF16),
                pltpu.VMEM((d_model, d_exp), BF16),
                pltpu.VMEM((d_exp, d_model), BF16),
            ],
        ),
        out_shape=jax.ShapeDtypeStruct((r, width), jnp.uint32),
        compiler_params=_cparams(("arbitrary",)),
        name="moe_experts",
    )(block_e, valid, fresh, xs, wg, wu, wd)


def _sc_gather_rows(table, idx):
    m = idx.shape[0]
    width = table.shape[1]
    window = SC_GATHER_WINDOW
    mesh = plsc.VectorSubcoreMesh(core_axis_name="core", subcore_axis_name="subcore")
    workers = mesh.num_cores * mesh.num_subcores
    per_worker = m // workers
    assert m % (workers * window) == 0

    steps = per_worker // window
    assert steps % 2 == 0
    index_buf = pltpu.VMEM((window,), jnp.int32)
    row_buf = pltpu.VMEM((window, width), table.dtype)
    dma = pltpu.SemaphoreType.DMA

    @functools.partial(
        pl.kernel, out_type=jax.ShapeDtypeStruct((m, width), table.dtype), mesh=mesh,
        scratch_types=[index_buf, index_buf, row_buf, row_buf, dma, dma])
    def gather(table_hbm, idx_hbm, out_hbm, idx_a, idx_b, rows_a, rows_b, sem_a, sem_b):
        worker = lax.axis_index("subcore") * mesh.num_cores + lax.axis_index("core")
        base = worker * per_worker
        slots = ((idx_a, rows_a, sem_a), (idx_b, rows_b, sem_b))

        def fetch(step, slot):
            idx, rows, sem = slots[slot]
            off = pl.multiple_of(base + step * window, window)
            pltpu.sync_copy(idx_hbm.at[pl.ds(off, window)], idx)
            pltpu.async_copy(table_hbm.at[idx], rows, sem)

        def flush(step, slot):
            idx, rows, sem = slots[slot]
            off = pl.multiple_of(base + step * window, window)
            pltpu.make_async_copy(table_hbm.at[idx], rows, sem).wait()
            pltpu.sync_copy(rows, out_hbm.at[pl.ds(off, window)])

        fetch(0, 0)

        @pl.loop(0, steps, step=2)
        def _(step):
            fetch(step + 1, 1)
            flush(step, 0)

            @pl.when(step + 2 < steps)
            def _():
                fetch(step + 2, 0)

            flush(step + 1, 1)

    return gather(table, idx)


def _final_kernel(h_ref, gate_ref, gain_ref, y0_ref, y1_ref, o_ref):
    o_ref[...] = _rms(_moe_sum(h_ref[...], gate_ref[...], y0_ref[...], y1_ref[...]), gain_ref[...])


def _final(h, gates, gain, picked):
    n, d_model = h.shape
    tc = MOVE_TILE
    steps = n // tc
    return pl.pallas_call(
        _final_kernel,
        grid=(steps,),
        in_specs=[
            pl.BlockSpec((tc, d_model), lambda i: (i, 0)),
            pl.BlockSpec((ROUTE_WIDTH, tc), lambda i: (0, i)),
            pl.BlockSpec((1, d_model), lambda i: (0, 0)),
            pl.BlockSpec((tc, d_model // 2), lambda i: (i, 0)),
            pl.BlockSpec((tc, d_model // 2), lambda i: (i + steps, 0)),
        ],
        out_specs=pl.BlockSpec((tc, d_model), lambda i: (i, 0)),
        out_shape=jax.ShapeDtypeStruct((n, d_model), F32),
        compiler_params=_cparams(("arbitrary",)),
        name="moe_final",
    )(h, gates, gain, picked, picked)


def _routing_tables(route, cnt, n_rows):
    bm = MOE_BLOCK
    expert = route[0:2].astype(jnp.int32)
    rank = route[4:6].astype(jnp.int32)
    counts = cnt[N_GROUPS:N_GROUPS + N_EXPERTS, 0].astype(jnp.int32)
    padded = (counts + bm - 1) // bm * bm
    pad_ends = jnp.cumsum(padded)
    pad_starts = pad_ends - padded
    ids = jnp.arange(N_EXPERTS, dtype=jnp.int32)
    start_of = jnp.sum(jnp.where(expert[..., None] == ids, pad_starts, 0), axis=-1)
    dest = (start_of + rank).reshape(-1).astype(jnp.int32)
    block_row = jnp.arange(n_rows // bm, dtype=jnp.int32) * bm
    block_e = jnp.minimum(jnp.sum((pad_ends[None, :] <= block_row[:, None]).astype(jnp.int32), axis=-1),
                          N_EXPERTS - 1)
    row_end = jnp.sum(jnp.where(block_e[:, None] == ids, pad_starts + counts, 0), axis=-1)
    valid = jnp.clip(row_end - block_row, 0, bm).astype(jnp.int32)
    return dest, block_e, valid


def kernel(x, mem, mem_norm, mix_norm, w_in, b_forget, w_alpha_up, b_alpha, fox_out_gain, gla_out_gain, w_out,
           cross_norm, w_xq, w_xk, w_xv, w_xo, moe_norm, w_router_group, b_router_group, w_router_expert,
           b_router_expert, w_expert_gate, w_expert_up, w_expert_down, final_norm):
    batch, seq, d_model = x.shape
    mem_len = mem.shape[1]
    depth = w_in.shape[0]
    n = batch * seq
    assert seq % FOX_TILE == 0 and seq % IN_TILE == 0 and seq % POST_TILE == 0 and seq % GLA_CHUNK == 0
    assert n % MOVE_TILE == 0 and d_model % LANES == 0

    c0 = 3 * FOX_WIDTH
    c1 = c0 + FOX_HEADS
    c2 = c1 + 2 * GLA_QK + 2 * GLA_V
    w_main = jnp.concatenate([w_in[:, :, :c0], w_in[:, :, c1:c2]], axis=-1).astype(BF16)
    pad = LANES - FOX_HEADS - GLA_RANK
    w_small = jnp.concatenate([w_in[:, :, c0:c1], w_in[:, :, c2:], jnp.zeros((depth, d_model, pad), F32)],
                              axis=-1).astype(BF16)
    w_up = jnp.concatenate([jnp.zeros((depth, FOX_HEADS, GLA_QK), F32), w_alpha_up,
                            jnp.zeros((depth, pad, GLA_QK), F32)], axis=1).astype(BF16)
    b_f = jnp.pad(b_forget, ((0, 0), (0, LANES - FOX_HEADS)))[:, None, :]
    b_a = b_alpha[:, None, :]
    w_r = jnp.concatenate([w_router_group, w_router_expert,
                           jnp.zeros((depth, d_model, LANES - N_GROUPS - N_EXPERTS), F32)], axis=-1)
    w_rh = w_r.astype(BF16)
    w_rs = jnp.concatenate([w_rh, (w_r - w_rh.astype(F32)).astype(BF16)], axis=-1)
    b_r = jnp.pad(jnp.concatenate([b_router_group, b_router_expert], axis=-1),
                  ((0, 0), (0, LANES - N_GROUPS - N_EXPERTS)))[:, None, :]
    w_out_b = w_out.astype(BF16)
    w_xq_b = w_xq.astype(BF16)
    w_xo_b = w_xo.astype(BF16)
    mix_g = mix_norm[:, None, :]
    cross_g = cross_norm[:, None, :]
    moe_g = moe_norm[:, None, :]
    gla_g = gla_out_gain[:, None, :]

    kmem, vmem = _mem_kv(mem.reshape(batch * mem_len, d_model), mem_norm[None, :],
                         w_xk.astype(BF16), w_xv.astype(BF16), batch, mem_len)

    n_rows = 2 * n + N_EXPERTS * MOE_BLOCK
    h = x.reshape(n, d_model)
    moe = None
    for l in range(depth):
        if moe is None:
            main, c, kl, dec = _in_proj(h, mix_g, w_main, w_small, w_up, b_f, b_a, l, seq)
        else:
            h, main, c, kl, dec = _in_proj(h, mix_g, w_main, w_small, w_up, b_f, b_a, l, seq, moe)
        fox = _fox_attention(main, c, fox_out_gain[l][None, :], batch, seq)
        gla = _gla(main, kl, dec, batch, seq)
        h2, hn2, route, cnt = _post(fox, gla, main, h, gla_g, w_out_b, cross_g, w_xq_b, kmem, vmem, w_xo_b, moe_g,
                                    w_rs, b_r, seq, mem_len, l)
        dest, block_e, valid = _routing_tables(route, cnt, n_rows)
        xs = _dispatch(dest, hn2, n_rows)
        y = _experts(block_e, valid, xs, w_expert_gate, w_expert_up, w_expert_down, l)
        h, moe = h2, (route, _sc_gather_rows(y, dest))
    return _final(h, moe[0], final_norm[None, :], moe[1]).reshape(batch, seq, d_model)
```

```python
import functools

import jax
import jax.numpy as jnp
from jax import lax
from jax.experimental import pallas as pl
from jax.experimental.pallas import tpu as pltpu
from jax.experimental.pallas import tpu_sc as plsc

F32 = jnp.float32
BF16 = jnp.bfloat16
EPS = 1e-6
LOG2E = 1.4426950408889634

FOX_HEADS = 8
FOX_DIM = 64
FOX_WIDTH = FOX_HEADS * FOX_DIM
GLA_HEADS = 4
GLA_DK = 64
GLA_DV = 128
GLA_QK = GLA_HEADS * GLA_DK
GLA_V = GLA_HEADS * GLA_DV
GLA_RANK = 16
GLA_TAU = 16.0
GLA_CHUNK = 64
X_HEADS = 4
X_DIM = 128
X_WIDTH = X_HEADS * X_DIM
N_GROUPS = 4
GROUP_SIZE = 4
N_EXPERTS = N_GROUPS * GROUP_SIZE
MAIN_WIDTH = 3 * FOX_WIDTH + 2 * GLA_QK + 2 * GLA_V

LANES = 128
ROUTE_WIDTH = 8
ROUTE_ROWS = 32
VMEM_LIMIT = 56 * 1024 * 1024

IN_TILE = 1024
FOX_TILE = 512
FOX_SLAB = 64
POST_TILE = 1024
MOE_BLOCK = 512
MOVE_TILE = 1024
SC_GATHER_WINDOW = 64


def _cparams(sem):
    return pltpu.CompilerParams(dimension_semantics=sem, vmem_limit_bytes=VMEM_LIMIT)


def _rms(x, gain):
    return x * lax.rsqrt(jnp.mean(x * x, axis=-1, keepdims=True) + EPS) * gain


def _log_sigmoid(x):
    return jnp.minimum(x, 0.0) - jnp.log1p(jnp.exp(-jnp.abs(x)))


def _dot(a, b):
    return jnp.dot(a, b, preferred_element_type=F32)


def _dot_nt(a, b):
    return lax.dot_general(a, b, (((1,), (1,)), ((), ())), preferred_element_type=F32)


def _pack_rows(x):
    half = x.shape[1] // 2
    lo = lax.bitcast_convert_type(x[:, :half].astype(BF16).astype(F32), jnp.uint32)
    hi = lax.bitcast_convert_type(x[:, half:].astype(BF16).astype(F32), jnp.uint32)
    return (lo >> 16) | hi


def _unpack_rows(w):
    lo = lax.bitcast_convert_type(w << 16, F32)
    hi = lax.bitcast_convert_type(w & jnp.uint32(0xFFFF0000), F32)
    return lo, hi


def _mem_kv_kernel(mem_ref, gain_ref, wk_ref, wv_ref, k_ref, v_ref):
    mn = _rms(mem_ref[...], gain_ref[...]).astype(BF16)
    for l in range(wk_ref.shape[0]):
        k_ref[l] = _dot(mn, wk_ref[l]).astype(BF16)
        v_ref[l] = _dot(mn, wv_ref[l]).astype(BF16)


def _mem_kv(mem2d, gain, wk, wv, batch, mem_len):
    depth, d_model, width = wk.shape
    out = jax.ShapeDtypeStruct((depth, batch * mem_len, width), BF16)
    return pl.pallas_call(
        _mem_kv_kernel,
        grid=(batch,),
        in_specs=[
            pl.BlockSpec((mem_len, d_model), lambda b: (b, 0)),
            pl.BlockSpec((1, d_model), lambda b: (0, 0)),
            pl.BlockSpec((depth, d_model, width), lambda b: (0, 0, 0)),
            pl.BlockSpec((depth, d_model, width), lambda b: (0, 0, 0)),
        ],
        out_specs=[
            pl.BlockSpec((depth, mem_len, width), lambda b: (0, b, 0)),
            pl.BlockSpec((depth, mem_len, width), lambda b: (0, b, 0)),
        ],
        out_shape=[out, out],
        compiler_params=_cparams(("arbitrary",)),
        name="mem_kv",
    )(mem2d, gain, wk, wv)


def _moe_sum(h, route, y0_packed, y1_packed):
    gate = jnp.transpose(route)
    y0 = jnp.concatenate(_unpack_rows(y0_packed), axis=1)
    y1 = jnp.concatenate(_unpack_rows(y1_packed), axis=1)
    return h + gate[:, 2:3] * y0 + gate[:, 3:4] * y1


def _in_proj_body(h, gain_ref, wmain_ref, wsmall_ref, wup_ref, bf_ref, ba_ref, main_ref, c_ref, kl_ref, dec_ref,
                  carry_ref, tiles_per_seq):
    tm = h.shape[0]
    cs = GLA_CHUNK
    xn = _rms(h, gain_ref[...]).astype(BF16)
    small = _dot(xn, wsmall_ref[...])
    lane = lax.broadcasted_iota(jnp.int32, small.shape, 1)
    c = jnp.transpose(_log_sigmoid(small + bf_ref[...]))[:FOX_HEADS, :]
    pos = lax.broadcasted_iota(jnp.int32, c.shape, 1)
    shift = 1
    while shift < tm:
        c = c + jnp.where(pos >= shift, pltpu.roll(c, shift, axis=1), 0.0)
        shift *= 2
    starts_sequence = pl.program_id(0) % tiles_per_seq == 0
    c = c + jnp.where(starts_sequence, 0.0, carry_ref[:, 0:1])
    carry_ref[...] = jnp.broadcast_to(c[:, tm - 1:tm], carry_ref.shape)
    c_ref[...] = c
    a = _dot(small.astype(BF16), wup_ref[...]) + ba_ref[...]
    b = _log_sigmoid(a) * (1.0 / GLA_TAU)
    pos = lax.broadcasted_iota(jnp.int32, b.shape, 0) % cs
    shift = 1
    while shift < cs:
        b = b + jnp.where(pos >= shift, pltpu.roll(b, shift, axis=0), 0.0)
        shift *= 2
    dec = jnp.exp(b.reshape(tm // cs, cs, GLA_QK)[:, cs - 1:cs, :])
    q0 = 3 * FOX_WIDTH
    k0 = q0 + GLA_QK
    qk = _dot(xn, wmain_ref[:, q0:k0 + GLA_QK])
    main_ref[:, q0:k0] = (qk[:, :GLA_QK] * jnp.exp(b) * (GLA_DK ** -0.5)).astype(BF16)
    ke = qk[:, GLA_QK:] * jnp.exp(-b)
    main_ref[:, k0:k0 + GLA_QK] = ke.astype(BF16)
    kl_ref[...] = (ke.reshape(tm // cs, cs, GLA_QK) * dec).reshape(tm, GLA_QK).astype(BF16)
    dec_ref[...] = dec.reshape(tm // cs, GLA_QK)
    step = 512
    for lo in list(range(0, q0, step)) + list(range(k0 + GLA_QK, MAIN_WIDTH, step)):
        main_ref[:, lo:lo + step] = _dot(xn, wmain_ref[:, lo:lo + step]).astype(BF16)


def _in_proj_kernel(h_ref, *refs, tiles_per_seq):
    _in_proj_body(h_ref[...], *refs, tiles_per_seq)


def _in_proj_after_moe_kernel(h_ref, gate_ref, y0_ref, y1_ref, gain_ref, wmain_ref, wsmall_ref, wup_ref, bf_ref,
                              ba_ref, hout_ref, main_ref, c_ref, kl_ref, dec_ref, carry_ref, *, tiles_per_seq):
    h = _moe_sum(h_ref[...], gate_ref[...], y0_ref[...], y1_ref[...])
    hout_ref[...] = h
    _in_proj_body(h, gain_ref, wmain_ref, wsmall_ref, wup_ref, bf_ref, ba_ref, main_ref, c_ref, kl_ref, dec_ref,
                  carry_ref, tiles_per_seq)


def _in_proj(h, gain, wmain, wsmall, wup, bf, ba, layer, seq, moe=None):
    n, d_model = h.shape
    tm = IN_TILE
    steps = n // tm
    chunks = tm // GLA_CHUNK
    tiles_per_seq = seq // tm
    pick = lambda i: (layer, 0, 0)
    row_block = lambda width, rows=tm: pl.BlockSpec((rows, width), lambda i: (i, 0))
    weight_specs = [
        pl.BlockSpec((None, 1, d_model), pick),
        pl.BlockSpec((None, d_model, MAIN_WIDTH), pick),
        pl.BlockSpec((None, d_model, LANES), pick),
        pl.BlockSpec((None, LANES, GLA_QK), pick),
        pl.BlockSpec((None, 1, LANES), pick),
        pl.BlockSpec((None, 1, GLA_QK), pick),
    ]
    out_specs = [row_block(MAIN_WIDTH), pl.BlockSpec((FOX_HEADS, tm), lambda i: (0, i)), row_block(GLA_QK),
                 row_block(GLA_QK, chunks)]
    out_shape = [
        jax.ShapeDtypeStruct((n, MAIN_WIDTH), BF16),
        jax.ShapeDtypeStruct((FOX_HEADS, n), F32),
        jax.ShapeDtypeStruct((n, GLA_QK), BF16),
        jax.ShapeDtypeStruct((n // GLA_CHUNK, GLA_QK), F32),
    ]
    weights = (gain, wmain, wsmall, wup, bf, ba)
    carry = [pltpu.VMEM((FOX_HEADS, LANES), F32)]
    if moe is None:
        return pl.pallas_call(
            functools.partial(_in_proj_kernel, tiles_per_seq=tiles_per_seq),
            grid=(steps,), in_specs=[row_block(d_model)] + weight_specs,
            out_specs=out_specs, out_shape=out_shape, scratch_shapes=carry,
            compiler_params=_cparams(("arbitrary",)), name="in_proj",
        )(h, *weights)
    route, picked = moe
    half = d_model // 2
    return pl.pallas_call(
        functools.partial(_in_proj_after_moe_kernel, tiles_per_seq=tiles_per_seq), grid=(steps,),
        in_specs=[row_block(d_model), pl.BlockSpec((ROUTE_WIDTH, tm), lambda i: (0, i)), row_block(half),
                  pl.BlockSpec((tm, half), lambda i: (i + steps, 0))] + weight_specs,
        out_specs=[row_block(d_model)] + out_specs,
        out_shape=[jax.ShapeDtypeStruct((n, d_model), F32)] + out_shape, scratch_shapes=carry,
        compiler_params=_cparams(("arbitrary",)), name="in_proj_after_moe",
    )(h, route, picked, picked, *weights)


def _fox_kernel(q_ref, k_ref, v_ref, c_ref, gain_ref, o_ref, q2_ref, s_ref, p_ref, alpha_ref, m_ref, l_ref, acc_ref):
    tq = FOX_TILE
    rows = 2 * tq
    slab = FOX_SLAB
    nq = q_ref.shape[0] // tq
    lane = lax.broadcasted_iota(jnp.int32, (1, LANES), 1)
    first = lane < FOX_DIM
    scale = FOX_DIM ** -0.5 * LOG2E
    for qi in range(nq):
        q = q_ref[qi * tq:(qi + 1) * tq, :].astype(F32) * scale
        q2_ref[qi, :tq, :] = jnp.where(first, q, 0.0).astype(BF16)
        q2_ref[qi, tq:, :] = jnp.where(first, 0.0, q).astype(BF16)

    head_row = lax.broadcasted_iota(jnp.int32, (FOX_HEADS, tq), 0)
    pair = pl.program_id(1)

    def scores(qi, j):
        cj = c_ref[:, j * tq:(j + 1) * tq] * LOG2E
        c0 = jnp.sum(jnp.where(head_row == 2 * pair, cj, 0.0), axis=0, keepdims=True)
        c1 = jnp.sum(jnp.where(head_row == 2 * pair + 1, cj, 0.0), axis=0, keepdims=True)
        d = _dot_nt(q2_ref[qi], k_ref[j * tq:(j + 1) * tq, :])
        s_ref[:tq, :] = d[:tq] - c0
        s_ref[tq:, :] = d[tq:] - c1

    def weighted_values(qi, j):
        par = qi % 2
        acc_ref[par] = alpha_ref[par] * acc_ref[par] + _dot(p_ref[...], v_ref[j * tq:(j + 1) * tq, :])

    def softmax(qi, masked):
        par = qi % 2
        for r in range(rows // slab):
            sl = slice(r * slab, (r + 1) * slab)
            s = s_ref[sl, :]
            if masked:
                row = lax.broadcasted_iota(jnp.int32, (slab, tq), 0) + (r * slab) % tq
                col = lax.broadcasted_iota(jnp.int32, (slab, tq), 1)
                s = jnp.where(row >= col, s, -jnp.inf)
            m_old = m_ref[par, sl, :]
            m_new = jnp.maximum(m_old, jnp.max(s, axis=-1, keepdims=True))
            alpha = jnp.exp2(m_old - m_new)
            p = jnp.exp2(s - jnp.concatenate([m_new] * (tq // LANES), axis=1))
            l_ref[par, sl, :] = alpha * l_ref[par, sl, :] + jnp.sum(p, axis=-1, keepdims=True)
            m_ref[par, sl, :] = m_new
            alpha_ref[par, sl, :] = alpha
            p_ref[sl, :] = p.astype(BF16)

    def finalize(qi):
        par = qi % 2
        o2 = acc_ref[par] / l_ref[par]
        o = jnp.where(first, o2[:tq], o2[tq:])
        sq = o * o
        ss0 = jnp.sum(jnp.where(first, sq, 0.0), axis=-1, keepdims=True)
        ss1 = jnp.sum(jnp.where(first, 0.0, sq), axis=-1, keepdims=True)
        ms = jnp.where(first, ss0, ss1) * (1.0 / FOX_DIM)
        o_ref[qi * tq:(qi + 1) * tq, :] = (o * lax.rsqrt(ms + EPS) * gain_ref[...]).astype(BF16)

    steps = [(qi, j) for qi in range(nq) for j in range(qi + 1)]
    scores(*steps[0])
    for t, (qi, j) in enumerate(steps):
        if t > 0:
            weighted_values(*steps[t - 1])
            if steps[t - 1][0] != qi:
                finalize(steps[t - 1][0])
        if j == 0:
            par = qi % 2
            m_ref[par] = jnp.full(m_ref.shape[1:], -jnp.inf, F32)
            l_ref[par] = jnp.zeros(l_ref.shape[1:], F32)
            acc_ref[par] = jnp.zeros(acc_ref.shape[1:], F32)
        softmax(qi, masked=(j == qi))
        if t + 1 < len(steps):
            scores(*steps[t + 1])
    weighted_values(*steps[-1])
    finalize(steps[-1][0])


def _fox_attention(main, c, gain, batch, seq):
    n = main.shape[0]
    tq = FOX_TILE
    nq = seq // tq
    pairs = FOX_HEADS // 2
    k_off = FOX_WIDTH // LANES
    v_off = 2 * FOX_WIDTH // LANES
    stat = pltpu.VMEM((2, 2 * tq, LANES), F32)
    return pl.pallas_call(
        _fox_kernel,
        grid=(batch, pairs),
        in_specs=[
            pl.BlockSpec((seq, LANES), lambda b, p: (b, p)),
            pl.BlockSpec((seq, LANES), lambda b, p: (b, k_off + p)),
            pl.BlockSpec((seq, LANES), lambda b, p: (b, v_off + p)),
            pl.BlockSpec((FOX_HEADS, seq), lambda b, p: (0, b)),
            pl.BlockSpec((1, LANES), lambda b, p: (0, p)),
        ],
        out_specs=pl.BlockSpec((seq, LANES), lambda b, p: (b, p)),
        out_shape=jax.ShapeDtypeStruct((n, FOX_WIDTH), BF16),
        scratch_shapes=[
            pltpu.VMEM((nq, 2 * tq, LANES), BF16),
            pltpu.VMEM((2 * tq, tq), F32),
            pltpu.VMEM((2 * tq, tq), BF16),
            stat, stat, stat, stat,
        ],
        compiler_params=_cparams(("arbitrary", "arbitrary")),
        name="fox_attention",
    )(main, main, main, c, gain)


def _gla_kernel(qe_ref, ke_ref, v_ref, kl_ref, dec_ref, o_ref):
    seq = qe_ref.shape[0]
    cs = GLA_CHUNK
    nc = seq // cs
    width = 2 * GLA_DK

    lane = lax.broadcasted_iota(jnp.int32, (1, width), 1)
    first = lane < GLA_DK
    row = lax.broadcasted_iota(jnp.int32, (2 * cs, cs), 0)
    col = lax.broadcasted_iota(jnp.int32, (2 * cs, cs), 1)
    tril2 = jnp.where(row >= cs, row - cs, row) >= col
    srow = lax.broadcasted_iota(jnp.int32, (2 * GLA_DV, width), 0)
    scol = lax.broadcasted_iota(jnp.int32, (2 * GLA_DV, width), 1)
    same_head = (srow >= GLA_DV) == (scol >= GLA_DK)
    unroll = 8

    def chunks(ci, st):
        r0s = [pl.multiple_of((ci * unroll + u) * cs, cs) for u in range(unroll)]
        qes = [qe_ref[pl.ds(r0, cs), :] for r0 in r0s]
        vs = [v_ref[pl.ds(r0, cs), :] for r0 in r0s]
        atts, upds = [], []
        for u in range(unroll):
            zero = jnp.zeros_like(qes[u])
            q2 = jnp.concatenate([jnp.where(first, qes[u], zero), jnp.where(first, zero, qes[u])], axis=0)
            atts.append(jnp.where(tril2, _dot_nt(q2, ke_ref[pl.ds(r0s[u], cs), :]), 0.0).astype(BF16))
        for u in range(unroll):
            upds.append(lax.dot_general(vs[u], kl_ref[pl.ds(r0s[u], cs), :], (((0,), (0,)), ((), ())),
                                        preferred_element_type=F32))
        ois = [_dot(atts[u], vs[u]) for u in range(unroll)]
        for u in range(unroll):
            o = _dot_nt(qes[u], st.astype(BF16))
            o = o + jnp.concatenate([ois[u][:cs, :GLA_DV], ois[u][cs:, GLA_DV:]], axis=1)
            o_ref[pl.ds(r0s[u], cs), :] = o.astype(BF16)
            st = st * dec_ref[pl.ds(ci * unroll + u, 1), :] + jnp.where(same_head, upds[u], 0.0)
        return st

    lax.fori_loop(0, nc // unroll, chunks, jnp.zeros((2 * GLA_DV, width), F32))


def _gla(main, kl, dec, batch, seq):
    n = main.shape[0]
    pairs = GLA_HEADS // 2
    q_off = 3 * FOX_WIDTH // LANES
    k_off = q_off + GLA_QK // LANES
    pv = 2 * GLA_DV
    v_off = (3 * FOX_WIDTH + 2 * GLA_QK) // pv
    return pl.pallas_call(
        _gla_kernel,
        grid=(batch, pairs),
        in_specs=[
            pl.BlockSpec((seq, LANES), lambda b, p: (b, q_off + p)),
            pl.BlockSpec((seq, LANES), lambda b, p: (b, k_off + p)),
            pl.BlockSpec((seq, pv), lambda b, p: (b, v_off + p)),
            pl.BlockSpec((seq, LANES), lambda b, p: (b, p)),
            pl.BlockSpec((seq // GLA_CHUNK, LANES), lambda b, p: (b, p)),
        ],
        out_specs=pl.BlockSpec((seq, pv), lambda b, p: (b, p)),
        out_shape=jax.ShapeDtypeStruct((n, GLA_V), BF16),
        compiler_params=_cparams(("arbitrary", "arbitrary")),
        name="gla",
    )(main, main, main, kl, dec)


def _post_kernel(fox_ref, gla_ref, gg_ref, h_ref, gg_gain_ref, wout_ref, cg_ref, wxq_ref, k_ref, v_ref, wxo_ref,
                 mg_ref, wr_ref, br_ref, h2_ref, hn_ref, route_ref, cnt_ref, carry_ref):
    tm = h_ref.shape[0]

    @pl.when(pl.program_id(0) == 0)
    def _():
        carry_ref[...] = jnp.zeros_like(carry_ref)

    raw = gla_ref[...].astype(F32)
    normed = []
    for hh in range(GLA_HEADS):
        oh = raw[:, hh * GLA_DV:(hh + 1) * GLA_DV]
        normed.append(oh * lax.rsqrt(jnp.mean(oh * oh, axis=-1, keepdims=True) + EPS))
    g = gg_ref[...].astype(F32)
    gla = (jnp.concatenate(normed, axis=1) * gg_gain_ref[...] * (g * jax.nn.sigmoid(g))).astype(BF16)
    y = _dot(fox_ref[...], wout_ref[0:FOX_WIDTH, :]) + _dot(gla, wout_ref[FOX_WIDTH:, :])
    h1 = h_ref[...] + y
    hn = _rms(h1, cg_ref[...]).astype(BF16)
    q = _dot(hn, wxq_ref[...]).astype(BF16)
    xscale = X_DIM ** -0.5
    heads = []
    for hh in range(X_HEADS):
        sl = slice(hh * X_DIM, (hh + 1) * X_DIM)
        s = _dot_nt(q[:, sl], k_ref[:, sl]) * xscale
        p = jnp.exp(s - jnp.max(s, axis=-1, keepdims=True))
        heads.append(_dot(p.astype(BF16), v_ref[:, sl]) / jnp.sum(p, axis=-1, keepdims=True))
    o = jnp.concatenate(heads, axis=1).astype(BF16)
    h2 = h1 + _dot(o, wxo_ref[...])
    h2_ref[...] = h2
    hn2 = _rms(h2, mg_ref[...])
    hn_ref[...] = _pack_rows(hn2)

    xh = hn2.astype(BF16)
    xl = (hn2 - xh.astype(F32)).astype(BF16)
    both_w = _dot(jnp.concatenate([xh, xl], axis=0), wr_ref[...])
    logits = both_w[:tm, :LANES] + both_w[:tm, LANES:] + both_w[tm:, :LANES] + both_w[tm:, LANES:] + br_ref[...]
    lt = jnp.transpose(logits)[:ROUTE_ROWS, :]
    row = lax.broadcasted_iota(jnp.int32, (ROUTE_ROWS, tm), 0)
    neg = -jnp.inf
    gl = jnp.where(row < N_GROUPS, lt, neg)
    gmax = jnp.max(gl, axis=0, keepdims=True)
    ge = jnp.exp(gl - gmax)
    gprob = ge / jnp.sum(ge, axis=0, keepdims=True)
    pmax = jnp.max(gprob, axis=0, keepdims=True)
    grp = jnp.min(jnp.where(gprob == pmax, row, ROUTE_ROWS), axis=0, keepdims=True)
    in_grp = (row >= N_GROUPS) & (row < N_GROUPS + N_EXPERTS) & (((row - N_GROUPS) // GROUP_SIZE) == grp)
    el = jnp.where(in_grp, lt, neg)
    emax = jnp.max(el, axis=0, keepdims=True)
    ee = jnp.exp(el - emax)
    eprob = ee / jnp.sum(ee, axis=0, keepdims=True)
    p1 = jnp.max(eprob, axis=0, keepdims=True)
    row1 = jnp.min(jnp.where(in_grp & (eprob == p1), row, ROUTE_ROWS), axis=0, keepdims=True)
    rest = jnp.where(in_grp & (row != row1), eprob, -1.0)
    p2 = jnp.max(rest, axis=0, keepdims=True)
    row2 = jnp.min(jnp.where(rest == p2, row, ROUTE_ROWS), axis=0, keepdims=True)
    g1 = pmax * p1 / (p1 + p2)
    g2 = pmax * p2 / (p1 + p2)

    oh1 = row == row1
    oh2 = row == row2
    both = (oh1 | oh2).astype(BF16)
    srow = lax.broadcasted_iota(jnp.int32, (tm, tm), 0)
    scol = lax.broadcasted_iota(jnp.int32, (tm, tm), 1)
    earlier = (srow < scol).astype(BF16)
    carry = carry_ref[...]
    seen = _dot(both, earlier) + jnp.concatenate([carry] * (tm // LANES), axis=1)
    rank1 = jnp.sum(jnp.where(oh1, seen, 0.0), axis=0, keepdims=True)
    rank2 = jnp.sum(jnp.where(oh2, seen, 0.0), axis=0, keepdims=True)
    carry = carry + jnp.sum(both.astype(F32), axis=1, keepdims=True)
    carry_ref[...] = carry
    cnt_ref[...] = carry

    e1 = (row1 - N_GROUPS).astype(F32)
    e2 = (row2 - N_GROUPS).astype(F32)
    zero = jnp.zeros_like(g1)
    route_ref[...] = jnp.concatenate([e1, e2, g1, g2, rank1, rank2, zero, zero], axis=0)


def _post(fox, gla, main, h, gla_gain, wout, cg, wxq, kmem, vmem, wxo, mg, wr, br, seq, mem_len, layer):
    n, d_model = h.shape
    tm = POST_TILE
    per_seq = seq // tm
    const = lambda i: (0, 0)
    pick = lambda i: (layer, 0, 0)
    return pl.pallas_call(
        _post_kernel,
        grid=(n // tm,),
        in_specs=[
            pl.BlockSpec((tm, FOX_WIDTH), lambda i: (i, 0)),
            pl.BlockSpec((tm, GLA_V), lambda i: (i, 0)),
            pl.BlockSpec((tm, GLA_V), lambda i: (i, MAIN_WIDTH // GLA_V - 1)),
            pl.BlockSpec((tm, d_model), lambda i: (i, 0)),
            pl.BlockSpec((None, 1, GLA_V), pick),
            pl.BlockSpec((None, FOX_WIDTH + GLA_V, d_model), pick),
            pl.BlockSpec((None, 1, d_model), pick),
            pl.BlockSpec((None, d_model, X_WIDTH), pick),
            pl.BlockSpec((None, mem_len, X_WIDTH), lambda i: (layer, i // per_seq, 0)),
            pl.BlockSpec((None, mem_len, X_WIDTH), lambda i: (layer, i // per_seq, 0)),
            pl.BlockSpec((None, X_WIDTH, d_model), pick),
            pl.BlockSpec((None, 1, d_model), pick),
            pl.BlockSpec((None, d_model, 2 * LANES), pick),
            pl.BlockSpec((None, 1, LANES), pick),
        ],
        out_specs=[
            pl.BlockSpec((tm, d_model), lambda i: (i, 0)),
            pl.BlockSpec((tm, d_model // 2), lambda i: (i, 0)),
            pl.BlockSpec((ROUTE_WIDTH, tm), lambda i: (0, i)),
            pl.BlockSpec((ROUTE_ROWS, LANES), const),
        ],
        out_shape=[
            jax.ShapeDtypeStruct((n, d_model), F32),
            jax.ShapeDtypeStruct((n, d_model // 2), jnp.uint32),
            jax.ShapeDtypeStruct((ROUTE_WIDTH, n), F32),
            jax.ShapeDtypeStruct((ROUTE_ROWS, LANES), F32),
        ],
        scratch_shapes=[pltpu.VMEM((ROUTE_ROWS, LANES), F32)],
        compiler_params=_cparams(("arbitrary",)),
        name="post_mixer",
    )(fox, gla, main, h, gla_gain, wout, cg, wxq, kmem, vmem, wxo, mg, wr, br)


def _dispatch(dest_kmajor, x, n_rows):
    n, width = x.shape
    window = SC_GATHER_WINDOW
    mesh = plsc.VectorSubcoreMesh(core_axis_name="core", subcore_axis_name="subcore")
    workers = mesh.num_cores * mesh.num_subcores
    per_worker = n // workers
    assert n % (workers * window) == 0

    steps = per_worker // window
    assert steps % 2 == 0
    index_buf = pltpu.VMEM((window,), jnp.int32)
    row_buf = pltpu.VMEM((window, width), x.dtype)
    dma = pltpu.SemaphoreType.DMA

    @functools.partial(
        pl.kernel, out_type=jax.ShapeDtypeStruct((n_rows, width), x.dtype), mesh=mesh,
        scratch_types=[index_buf, index_buf, index_buf, index_buf, row_buf, row_buf, dma, dma, dma])
    def scatter(x_hbm, idx_hbm, out_hbm, idx0_a, idx1_a, idx0_b, idx1_b, rows_a, rows_b, sem_a, sem_b, sem_out):
        worker = lax.axis_index("subcore") * mesh.num_cores + lax.axis_index("core")
        base = worker * per_worker
        slots = ((idx0_a, idx1_a, rows_a, sem_a), (idx0_b, idx1_b, rows_b, sem_b))

        def load(step, slot):
            idx0, idx1, rows, sem = slots[slot]
            off = pl.multiple_of(base + step * window, window)
            pltpu.sync_copy(idx_hbm.at[pl.ds(off, window)], idx0)
            pltpu.sync_copy(idx_hbm.at[pl.ds(n + off, window)], idx1)
            pltpu.async_copy(x_hbm.at[pl.ds(off, window)], rows, sem)

        def store(slot):
            idx0, idx1, rows, sem = slots[slot]
            pltpu.make_async_copy(x_hbm.at[pl.ds(0, window)], rows, sem).wait()
            first = pltpu.async_copy(rows, out_hbm.at[idx0], sem_out)
            second = pltpu.async_copy(rows, out_hbm.at[idx1], sem_out)
            first.wait()
            second.wait()

        load(0, 0)

        @pl.loop(0, steps, step=2)
        def _(step):
            load(step + 1, 1)
            store(0)

            @pl.when(step + 2 < steps)
            def _():
                load(step + 2, 0)

            store(1)

    return scatter(x, dest_kmajor)


def _expert_kernel(be_ref, valid_ref, fresh_ref, x_ref, wg_ref, wu_ref, wd_ref, y_ref, wg_b, wu_b, wd_b):
    del be_ref
    i = pl.program_id(0)
    valid = valid_ref[i]

    @pl.when(fresh_ref[i] > 0)
    def _():
        wg_b[...] = wg_ref[...].astype(BF16)
        wu_b[...] = wu_ref[...].astype(BF16)
        wd_b[...] = wd_ref[...].astype(BF16)

    @pl.when(valid > 0)
    def _():
        row = lax.broadcasted_iota(jnp.int32, x_ref.shape, 0)
        lo, hi = _unpack_rows(jnp.where(row < valid, x_ref[...], jnp.uint32(0)))
        lo = lo.astype(BF16)
        hi = hi.astype(BF16)
        half = lo.shape[1]
        g = _dot(lo, wg_b[:half, :]) + _dot(hi, wg_b[half:, :])
        u = _dot(lo, wu_b[:half, :]) + _dot(hi, wu_b[half:, :])
        a = (g * jax.nn.sigmoid(g) * u).astype(BF16)
        y_ref[...] = _pack_rows(_dot(a, wd_b[...]))

    @pl.when(valid <= 0)
    def _():
        y_ref[...] = jnp.zeros_like(y_ref)


def _experts(block_e, valid, xs, wg, wu, wd, layer):
    r, width = xs.shape
    bm = MOE_BLOCK
    d_model, d_exp = wg.shape[-2:]
    fresh = jnp.concatenate([jnp.ones((1,), jnp.int32), (block_e[1:] != block_e[:-1]).astype(jnp.int32)])
    pick = lambda i, be, va, fr: (layer, be[i], 0, 0)
    return pl.pallas_call(
        _expert_kernel,
        grid_spec=pltpu.PrefetchScalarGridSpec(
            num_scalar_prefetch=3,
            grid=(r // bm,),
            in_specs=[
                pl.BlockSpec((bm, width), lambda i, be, va, fr: (i, 0)),
                pl.BlockSpec((None, None, d_model, d_exp), pick),
                pl.BlockSpec((None, None, d_model, d_exp), pick),
                pl.BlockSpec((None, None, d_exp, d_model), pick),
            ],
            out_specs=pl.BlockSpec((bm, width), lambda i, be, va, fr: (i, 0)),
            scratch_shapes=[
                pltpu.VMEM((d_model, d_exp), BF16),
                pltpu.VMEM((d_model, d_exp), BF16),
                pltpu.VMEM((d_exp, d_model), BF16),
            ],
        ),
        out_shape=jax.ShapeDtypeStruct((r, width), jnp.uint32),
        compiler_params=_cparams(("arbitrary",)),
        name="moe_experts",
    )(block_e, valid, fresh, xs, wg, wu, wd)


def _sc_gather_rows(table, idx):
    m = idx.shape[0]
    width = table.shape[1]
    window = SC_GATHER_WINDOW
    mesh = plsc.VectorSubcoreMesh(core_axis_name="core", subcore_axis_name="subcore")
    workers = mesh.num_cores * mesh.num_subcores
    per_worker = m // workers
    assert m % (workers * window) == 0

    steps = per_worker // window
    assert steps % 2 == 0
    index_buf = pltpu.VMEM((window,), jnp.int32)
    row_buf = pltpu.VMEM((window, width), table.dtype)
    dma = pltpu.SemaphoreType.DMA

    @functools.partial(
        pl.kernel, out_type=jax.ShapeDtypeStruct((m, width), table.dtype), mesh=mesh,
        scratch_types=[index_buf, index_buf, row_buf, row_buf, dma, dma])
    def gather(table_hbm, idx_hbm, out_hbm, idx_a, idx_b, rows_a, rows_b, sem_a, sem_b):
        worker = lax.axis_index("subcore") * mesh.num_cores + lax.axis_index("core")
        base = worker * per_worker
        slots = ((idx_a, rows_a, sem_a), (idx_b, rows_b, sem_b))

        def fetch(step, slot):
            idx, rows, sem = slots[slot]
            off = pl.multiple_of(base + step * window, window)
            pltpu.sync_copy(idx_hbm.at[pl.ds(off, window)], idx)
            pltpu.async_copy(table_hbm.at[idx], rows, sem)

        def flush(step, slot):
            idx, rows, sem = slots[slot]
            off = pl.multiple_of(base + step * window, window)
            pltpu.make_async_copy(table_hbm.at[idx], rows, sem).wait()
            pltpu.sync_copy(rows, out_hbm.at[pl.ds(off, window)])

        fetch(0, 0)

        @pl.loop(0, steps, step=2)
        def _(step):
            fetch(step + 1, 1)
            flush(step, 0)

            @pl.when(step + 2 < steps)
            def _():
                fetch(step + 2, 0)

            flush(step + 1, 1)

    return gather(table, idx)


def _final_kernel(h_ref, gate_ref, gain_ref, y0_ref, y1_ref, o_ref):
    o_ref[...] = _rms(_moe_sum(h_ref[...], gate_ref[...], y0_ref[...], y1_ref[...]), gain_ref[...])


def _final(h, route, gain, picked):
    n, d_model = h.shape
    tc = MOVE_TILE
    steps = n // tc
    return pl.pallas_call(
        _final_kernel,
        grid=(steps,),
        in_specs=[
            pl.BlockSpec((tc, d_model), lambda i: (i, 0)),
            pl.BlockSpec((ROUTE_WIDTH, tc), lambda i: (0, i)),
            pl.BlockSpec((1, d_model), lambda i: (0, 0)),
            pl.BlockSpec((tc, d_model // 2), lambda i: (i, 0)),
            pl.BlockSpec((tc, d_model // 2), lambda i: (i + steps, 0)),
        ],
        out_specs=pl.BlockSpec((tc, d_model), lambda i: (i, 0)),
        out_shape=jax.ShapeDtypeStruct((n, d_model), F32),
        compiler_params=_cparams(("arbitrary",)),
        name="moe_final",
    )(h, route, gain, picked, picked)


def _routing_tables(route, cnt, n_rows):
    bm = MOE_BLOCK
    expert = route[0:2].astype(jnp.int32)
    rank = route[4:6].astype(jnp.int32)
    counts = cnt[N_GROUPS:N_GROUPS + N_EXPERTS, 0].astype(jnp.int32)
    padded = (counts + bm - 1) // bm * bm
    pad_ends = jnp.cumsum(padded)
    pad_starts = pad_ends - padded
    ids = jnp.arange(N_EXPERTS, dtype=jnp.int32)
    start_of = jnp.sum(jnp.where(expert[..., None] == ids, pad_starts, 0), axis=-1)
    dest = (start_of + rank).reshape(-1).astype(jnp.int32)
    block_row = jnp.arange(n_rows // bm, dtype=jnp.int32) * bm
    block_e = jnp.minimum(jnp.sum((pad_ends[None, :] <= block_row[:, None]).astype(jnp.int32), axis=-1),
                          N_EXPERTS - 1)
    row_end = jnp.sum(jnp.where(block_e[:, None] == ids, pad_starts + counts, 0), axis=-1)
    valid = jnp.clip(row_end - block_row, 0, bm).astype(jnp.int32)
    return dest, block_e, valid


def kernel(x, mem, mem_norm, mix_norm, w_in, b_forget, w_alpha_up, b_alpha, fox_out_gain, gla_out_gain, w_out,
           cross_norm, w_xq, w_xk, w_xv, w_xo, moe_norm, w_router_group, b_router_group, w_router_expert,
           b_router_expert, w_expert_gate, w_expert_up, w_expert_down, final_norm):
    batch, seq, d_model = x.shape
    mem_len = mem.shape[1]
    depth = w_in.shape[0]
    n = batch * seq
    assert seq % FOX_TILE == 0 and seq % IN_TILE == 0 and seq % POST_TILE == 0 and seq % GLA_CHUNK == 0
    assert n % MOVE_TILE == 0 and d_model % LANES == 0

    c0 = 3 * FOX_WIDTH
    c1 = c0 + FOX_HEADS
    c2 = c1 + 2 * GLA_QK + 2 * GLA_V
    w_main = jnp.concatenate([w_in[:, :, :c0], w_in[:, :, c1:c2]], axis=-1).astype(BF16)
    pad = LANES - FOX_HEADS - GLA_RANK
    w_small = jnp.concatenate([w_in[:, :, c0:c1], w_in[:, :, c2:], jnp.zeros((depth, d_model, pad), F32)],
                              axis=-1).astype(BF16)
    w_up = jnp.concatenate([jnp.zeros((depth, FOX_HEADS, GLA_QK), F32), w_alpha_up,
                            jnp.zeros((depth, pad, GLA_QK), F32)], axis=1).astype(BF16)
    b_f = jnp.pad(b_forget, ((0, 0), (0, LANES - FOX_HEADS)))[:, None, :]
    b_a = b_alpha[:, None, :]
    w_r = jnp.concatenate([w_router_group, w_router_expert,
                           jnp.zeros((depth, d_model, LANES - N_GROUPS - N_EXPERTS), F32)], axis=-1)
    w_rh = w_r.astype(BF16)
    w_rs = jnp.concatenate([w_rh, (w_r - w_rh.astype(F32)).astype(BF16)], axis=-1)
    b_r = jnp.pad(jnp.concatenate([b_router_group, b_router_expert], axis=-1),
                  ((0, 0), (0, LANES - N_GROUPS - N_EXPERTS)))[:, None, :]
    w_out_b = w_out.astype(BF16)
    w_xq_b = w_xq.astype(BF16)
    w_xo_b = w_xo.astype(BF16)
    mix_g = mix_norm[:, None, :]
    cross_g = cross_norm[:, None, :]
    moe_g = moe_norm[:, None, :]
    gla_g = gla_out_gain[:, None, :]

    kmem, vmem = _mem_kv(mem.reshape(batch * mem_len, d_model), mem_norm[None, :],
                         w_xk.astype(BF16), w_xv.astype(BF16), batch, mem_len)

    n_rows = 2 * n + N_EXPERTS * MOE_BLOCK
    h = x.reshape(n, d_model)
    moe = None
    for l in range(depth):
        if moe is None:
            main, c, kl, dec = _in_proj(h, mix_g, w_main, w_small, w_up, b_f, b_a, l, seq)
        else:
            h, main, c, kl, dec = _in_proj(h, mix_g, w_main, w_small, w_up, b_f, b_a, l, seq, moe)
        fox = _fox_attention(main, c, fox_out_gain[l][None, :], batch, seq)
        gla = _gla(main, kl, dec, batch, seq)
        h2, hn2, route, cnt = _post(fox, gla, main, h, gla_g, w_out_b, cross_g, w_xq_b, kmem, vmem, w_xo_b, moe_g,
                                    w_rs, b_r, seq, mem_len, l)
        dest, block_e, valid = _routing_tables(route, cnt, n_rows)
        xs = _dispatch(dest, hn2, n_rows)
        y = _experts(block_e, valid, xs, w_expert_gate, w_expert_up, w_expert_down, l)
        h, moe = h2, (route, _sc_gather_rows(y, dest))
    return _final(h, moe[0], final_norm[None, :], moe[1]).reshape(batch, seq, d_model)
```

```python
import functools

import jax
import jax.numpy as jnp
from jax import lax
from jax.experimental import pallas as pl
from jax.experimental.pallas import tpu as pltpu
from jax.experimental.pallas import tpu_sc as plsc

F32 = jnp.float32
BF16 = jnp.bfloat16
EPS = 1e-6
LOG2E = 1.4426950408889634

FOX_HEADS = 8
FOX_DIM = 64
FOX_WIDTH = FOX_HEADS * FOX_DIM
GLA_HEADS = 4
GLA_DK = 64
GLA_DV = 128
GLA_QK = GLA_HEADS * GLA_DK
GLA_V = GLA_HEADS * GLA_DV
GLA_RANK = 16
GLA_TAU = 16.0
GLA_CHUNK = 64
X_HEADS = 4
X_DIM = 128
X_WIDTH = X_HEADS * X_DIM
N_GROUPS = 4
GROUP_SIZE = 4
N_EXPERTS = N_GROUPS * GROUP_SIZE
MAIN_WIDTH = 3 * FOX_WIDTH + 2 * GLA_QK + 2 * GLA_V

LANES = 128
ROUTE_WIDTH = 8
ROUTE_ROWS = 32
VMEM_LIMIT = 56 * 1024 * 1024

IN_TILE = 1024
FOX_TILE = 512
FOX_SLAB = 64
POST_TILE = 1024
MOE_BLOCK = 512
MOVE_TILE = 1024
SC_GATHER_WINDOW = 64


def _cparams(sem):
    return pltpu.CompilerParams(dimension_semantics=sem, vmem_limit_bytes=VMEM_LIMIT)


def _rms(x, gain):
    return x * lax.rsqrt(jnp.mean(x * x, axis=-1, keepdims=True) + EPS) * gain


def _log_sigmoid(x):
    return jnp.minimum(x, 0.0) - jnp.log1p(jnp.exp(-jnp.abs(x)))


def _dot(a, b):
    return jnp.dot(a, b, preferred_element_type=F32)


def _dot_nt(a, b):
    return lax.dot_general(a, b, (((1,), (1,)), ((), ())), preferred_element_type=F32)


def _pack_rows(x):
    half = x.shape[1] // 2
    lo = lax.bitcast_convert_type(x[:, :half].astype(BF16).astype(F32), jnp.uint32)
    hi = lax.bitcast_convert_type(x[:, half:].astype(BF16).astype(F32), jnp.uint32)
    return (lo >> 16) | hi


def _unpack_rows(w):
    lo = lax.bitcast_convert_type(w << 16, F32)
    hi = lax.bitcast_convert_type(w & jnp.uint32(0xFFFF0000), F32)
    return lo, hi


def _mem_kv_kernel(mem_ref, gain_ref, wk_ref, wv_ref, k_ref, v_ref):
    mn = _rms(mem_ref[...], gain_ref[...]).astype(BF16)
    for l in range(wk_ref.shape[0]):
        k_ref[l] = _dot(mn, wk_ref[l]).astype(BF16)
        v_ref[l] = _dot(mn, wv_ref[l]).astype(BF16)


def _mem_kv(mem2d, gain, wk, wv, batch, mem_len):
    depth, d_model, width = wk.shape
    out = jax.ShapeDtypeStruct((depth, batch * mem_len, width), BF16)
    return pl.pallas_call(
        _mem_kv_kernel,
        grid=(batch,),
        in_specs=[
            pl.BlockSpec((mem_len, d_model), lambda b: (b, 0)),
            pl.BlockSpec((1, d_model), lambda b: (0, 0)),
            pl.BlockSpec((depth, d_model, width), lambda b: (0, 0, 0)),
            pl.BlockSpec((depth, d_model, width), lambda b: (0, 0, 0)),
        ],
        out_specs=[
            pl.BlockSpec((depth, mem_len, width), lambda b: (0, b, 0)),
            pl.BlockSpec((depth, mem_len, width), lambda b: (0, b, 0)),
        ],
        out_shape=[out, out],
        compiler_params=_cparams(("arbitrary",)),
        name="mem_kv",
    )(mem2d, gain, wk, wv)


def _moe_sum(h, route, y0_packed, y1_packed):
    gate = jnp.transpose(route)
    y0 = jnp.concatenate(_unpack_rows(y0_packed), axis=1)
    y1 = jnp.concatenate(_unpack_rows(y1_packed), axis=1)
    return h + gate[:, 2:3] * y0 + gate[:, 3:4] * y1


def _in_proj_body(h, gain_ref, wmain_ref, wsmall_ref, wup_ref, bf_ref, ba_ref, main_ref, c_ref, kl_ref, dec_ref,
                  carry_ref, tiles_per_seq):
    tm = h.shape[0]
    cs = GLA_CHUNK
    xn = _rms(h, gain_ref[...]).astype(BF16)
    small = _dot(xn, wsmall_ref[...])
    lane = lax.broadcasted_iota(jnp.int32, small.shape, 1)
    c = jnp.transpose(_log_sigmoid(small + bf_ref[...]))[:FOX_HEADS, :]
    pos = lax.broadcasted_iota(jnp.int32, c.shape, 1)
    shift = 1
    while shift < tm:
        c = c + jnp.where(pos >= shift, pltpu.roll(c, shift, axis=1), 0.0)
        shift *= 2
    starts_sequence = pl.program_id(0) % tiles_per_seq == 0
    c = c + jnp.where(starts_sequence, 0.0, carry_ref[:, 0:1])
    carry_ref[...] = jnp.broadcast_to(c[:, tm - 1:tm], carry_ref.shape)
    c_ref[...] = c
    a = _dot(small.astype(BF16), wup_ref[...]) + ba_ref[...]
    b = _log_sigmoid(a) * (1.0 / GLA_TAU)
    pos = lax.broadcasted_iota(jnp.int32, b.shape, 0) % cs
    shift = 1
    while shift < cs:
        b = b + jnp.where(pos >= shift, pltpu.roll(b, shift, axis=0), 0.0)
        shift *= 2
    dec = jnp.exp(b.reshape(tm // cs, cs, GLA_QK)[:, cs - 1:cs, :])
    q0 = 3 * FOX_WIDTH
    k0 = q0 + GLA_QK
    qk = _dot(xn, wmain_ref[:, q0:k0 + GLA_QK])
    main_ref[:, q0:k0] = (qk[:, :GLA_QK] * jnp.exp(b) * (GLA_DK ** -0.5)).astype(BF16)
    ke = qk[:, GLA_QK:] * jnp.exp(-b)
    main_ref[:, k0:k0 + GLA_QK] = ke.astype(BF16)
    kl_ref[...] = (ke.reshape(tm // cs, cs, GLA_QK) * dec).reshape(tm, GLA_QK).astype(BF16)
    dec_ref[...] = dec.reshape(tm // cs, GLA_QK)
    step = 512
    for lo in list(range(0, q0, step)) + list(range(k0 + GLA_QK, MAIN_WIDTH, step)):
        main_ref[:, lo:lo + step] = _dot(xn, wmain_ref[:, lo:lo + step]).astype(BF16)


def _in_proj_kernel(h_ref, *refs, tiles_per_seq):
    _in_proj_body(h_ref[...], *refs, tiles_per_seq)


def _in_proj_after_moe_kernel(h_ref, gate_ref, y0_ref, y1_ref, gain_ref, wmain_ref, wsmall_ref, wup_ref, bf_ref,
                              ba_ref, hout_ref, main_ref, c_ref, kl_ref, dec_ref, carry_ref, *, tiles_per_seq):
    h = _moe_sum(h_ref[...], gate_ref[...], y0_ref[...], y1_ref[...])
    hout_ref[...] = h
    _in_proj_body(h, gain_ref, wmain_ref, wsmall_ref, wup_ref, bf_ref, ba_ref, main_ref, c_ref, kl_ref, dec_ref,
                  carry_ref, tiles_per_seq)


def _in_proj(h, gain, wmain, wsmall, wup, bf, ba, layer, seq, moe=None):
    n, d_model = h.shape
    tm = IN_TILE
    steps = n // tm
    chunks = tm // GLA_CHUNK
    tiles_per_seq = seq // tm
    pick = lambda i: (layer, 0, 0)
    row_block = lambda width, rows=tm: pl.BlockSpec((rows, width), lambda i: (i, 0))
    weight_specs = [
        pl.BlockSpec((None, 1, d_model), pick),
        pl.BlockSpec((None, d_model, MAIN_WIDTH), pick),
        pl.BlockSpec((None, d_model, LANES), pick),
        pl.BlockSpec((None, LANES, GLA_QK), pick),
        pl.BlockSpec((None, 1, LANES), pick),
        pl.BlockSpec((None, 1, GLA_QK), pick),
    ]
    out_specs = [row_block(MAIN_WIDTH), pl.BlockSpec((FOX_HEADS, tm), lambda i: (0, i)), row_block(GLA_QK),
                 row_block(GLA_QK, chunks)]
    out_shape = [
        jax.ShapeDtypeStruct((n, MAIN_WIDTH), BF16),
        jax.ShapeDtypeStruct((FOX_HEADS, n), F32),
        jax.ShapeDtypeStruct((n, GLA_QK), BF16),
        jax.ShapeDtypeStruct((n // GLA_CHUNK, GLA_QK), F32),
    ]
    weights = (gain, wmain, wsmall, wup, bf, ba)
    carry = [pltpu.VMEM((FOX_HEADS, LANES), F32)]
    if moe is None:
        return pl.pallas_call(
            functools.partial(_in_proj_kernel, tiles_per_seq=tiles_per_seq),
            grid=(steps,), in_specs=[row_block(d_model)] + weight_specs,
            out_specs=out_specs, out_shape=out_shape, scratch_shapes=carry,
            compiler_params=_cparams(("arbitrary",)), name="in_proj",
        )(h, *weights)
    route, picked = moe
    half = d_model // 2
    return pl.pallas_call(
        functools.partial(_in_proj_after_moe_kernel, tiles_per_seq=tiles_per_seq), grid=(steps,),
        in_specs=[row_block(d_model), pl.BlockSpec((ROUTE_WIDTH, tm), lambda i: (0, i)), row_block(half),
                  pl.BlockSpec((tm, half), lambda i: (i + steps, 0))] + weight_specs,
        out_specs=[row_block(d_model)] + out_specs,
        out_shape=[jax.ShapeDtypeStruct((n, d_model), F32)] + out_shape, scratch_shapes=carry,
        compiler_params=_cparams(("arbitrary",)), name="in_proj_after_moe",
    )(h, route, picked, picked, *weights)


def _fox_kernel(q_ref, k_ref, v_ref, c_ref, gain_ref, o_ref, q2_ref, s_ref, p_ref, alpha_ref, m_ref, l_ref, acc_ref):
    tq = FOX_TILE
    rows = 2 * tq
    slab = FOX_SLAB
    nq = q_ref.shape[0] // tq
    lane = lax.broadcasted_iota(jnp.int32, (1, LANES), 1)
    first = lane < FOX_DIM
    scale = FOX_DIM ** -0.5 * LOG2E
    for qi in range(nq):
        q = q_ref[qi * tq:(qi + 1) * tq, :].astype(F32) * scale
        q2_ref[qi, :tq, :] = jnp.where(first, q, 0.0).astype(BF16)
        q2_ref[qi, tq:, :] = jnp.where(first, 0.0, q).astype(BF16)

    head_row = lax.broadcasted_iota(jnp.int32, (FOX_HEADS, tq), 0)
    pair = pl.program_id(1)

    def scores(qi, j):
        cj = c_ref[:, j * tq:(j + 1) * tq] * LOG2E
        c0 = jnp.sum(jnp.where(head_row == 2 * pair, cj, 0.0), axis=0, keepdims=True)
        c1 = jnp.sum(jnp.where(head_row == 2 * pair + 1, cj, 0.0), axis=0, keepdims=True)
        d = _dot_nt(q2_ref[qi], k_ref[j * tq:(j + 1) * tq, :])
        s_ref[:tq, :] = d[:tq] - c0
        s_ref[tq:, :] = d[tq:] - c1

    def weighted_values(qi, j):
        par = qi % 2
        acc_ref[par] = alpha_ref[par] * acc_ref[par] + _dot(p_ref[...], v_ref[j * tq:(j + 1) * tq, :])

    def softmax(qi, masked):
        par = qi % 2
        for r in range(rows // slab):
            sl = slice(r * slab, (r + 1) * slab)
            s = s_ref[sl, :]
            if masked:
                row = lax.broadcasted_iota(jnp.int32, (slab, tq), 0) + (r * slab) % tq
                col = lax.broadcasted_iota(jnp.int32, (slab, tq), 1)
                s = jnp.where(row >= col, s, -jnp.inf)
            m_old = m_ref[par, sl, :]
            m_new = jnp.maximum(m_old, jnp.max(s, axis=-1, keepdims=True))
            alpha = jnp.exp2(m_old - m_new)
            p = jnp.exp2(s - jnp.concatenate([m_new] * (tq // LANES), axis=1))
            l_ref[par, sl, :] = alpha * l_ref[par, sl, :] + jnp.sum(p, axis=-1, keepdims=True)
            m_ref[par, sl, :] = m_new
            alpha_ref[par, sl, :] = alpha
            p_ref[sl, :] = p.astype(BF16)

    def finalize(qi):
        par = qi % 2
        o2 = acc_ref[par] / l_ref[par]
        o = jnp.where(first, o2[:tq], o2[tq:])
        sq = o * o
        ss0 = jnp.sum(jnp.where(first, sq, 0.0), axis=-1, keepdims=True)
        ss1 = jnp.sum(jnp.where(first, 0.0, sq), axis=-1, keepdims=True)
        ms = jnp.where(first, ss0, ss1) * (1.0 / FOX_DIM)
        o_ref[qi * tq:(qi + 1) * tq, :] = (o * lax.rsqrt(ms + EPS) * gain_ref[...]).astype(BF16)

    steps = [(qi, j) for qi in range(nq) for j in range(qi + 1)]
    scores(*steps[0])
    for t, (qi, j) in enumerate(steps):
        if t > 0:
            weighted_values(*steps[t - 1])
            if steps[t - 1][0] != qi:
                finalize(steps[t - 1][0])
        if j == 0:
            par = qi % 2
            m_ref[par] = jnp.full(m_ref.shape[1:], -jnp.inf, F32)
            l_ref[par] = jnp.zeros(l_ref.shape[1:], F32)
            acc_ref[par] = jnp.zeros(acc_ref.shape[1:], F32)
        softmax(qi, masked=(j == qi))
        if t + 1 < len(steps):
            scores(*steps[t + 1])
    weighted_values(*steps[-1])
    finalize(steps[-1][0])


def _fox_attention(main, c, gain, batch, seq):
    n = main.shape[0]
    tq = FOX_TILE
    nq = seq // tq
    pairs = FOX_HEADS // 2
    k_off = FOX_WIDTH // LANES
    v_off = 2 * FOX_WIDTH // LANES
    stat = pltpu.VMEM((2, 2 * tq, LANES), F32)
    return pl.pallas_call(
        _fox_kernel,
        grid=(batch, pairs),
        in_specs=[
            pl.BlockSpec((seq, LANES), lambda b, p: (b, p)),
            pl.BlockSpec((seq, LANES), lambda b, p: (b, k_off + p)),
            pl.BlockSpec((seq, LANES), lambda b, p: (b, v_off + p)),
            pl.BlockSpec((FOX_HEADS, seq), lambda b, p: (0, b)),
            pl.BlockSpec((1, LANES), lambda b, p: (0, p)),
        ],
        out_specs=pl.BlockSpec((seq, LANES), lambda b, p: (b, p)),
        out_shape=jax.ShapeDtypeStruct((n, FOX_WIDTH), BF16),
        scratch_shapes=[
            pltpu.VMEM((nq, 2 * tq, LANES), BF16),
            pltpu.VMEM((2 * tq, tq), F32),
            pltpu.VMEM((2 * tq, tq), BF16),
            stat, stat, stat, stat,
        ],
        compiler_params=_cparams(("arbitrary", "arbitrary")),
        name="fox_attention",
    )(main, main, main, c, gain)


def _gla_kernel(qe_ref, ke_ref, v_ref, kl_ref, dec_ref, o_ref):
    seq = qe_ref.shape[0]
    cs = GLA_CHUNK
    nc = seq // cs
    width = 2 * GLA_DK

    lane = lax.broadcasted_iota(jnp.int32, (1, width), 1)
    first = lane < GLA_DK
    row = lax.broadcasted_iota(jnp.int32, (2 * cs, cs), 0)
    col = lax.broadcasted_iota(jnp.int32, (2 * cs, cs), 1)
    tril2 = jnp.where(row >= cs, row - cs, row) >= col
    srow = lax.broadcasted_iota(jnp.int32, (2 * GLA_DV, width), 0)
    scol = lax.broadcasted_iota(jnp.int32, (2 * GLA_DV, width), 1)
    same_head = (srow >= GLA_DV) == (scol >= GLA_DK)
    unroll = 8

    def chunks(ci, st):
        r0s = [pl.multiple_of((ci * unroll + u) * cs, cs) for u in range(unroll)]
        qes = [qe_ref[pl.ds(r0, cs), :] for r0 in r0s]
        vs = [v_ref[pl.ds(r0, cs), :] for r0 in r0s]
        atts, upds = [], []
        for u in range(unroll):
            zero = jnp.zeros_like(qes[u])
            q2 = jnp.concatenate([jnp.where(first, qes[u], zero), jnp.where(first, zero, qes[u])], axis=0)
            atts.append(jnp.where(tril2, _dot_nt(q2, ke_ref[pl.ds(r0s[u], cs), :]), 0.0).astype(BF16))
        for u in range(unroll):
            upds.append(lax.dot_general(vs[u], kl_ref[pl.ds(r0s[u], cs), :], (((0,), (0,)), ((), ())),
                                        preferred_element_type=F32))
        ois = [_dot(atts[u], vs[u]) for u in range(unroll)]
        for u in range(unroll):
            o = _dot_nt(qes[u], st.astype(BF16))
            o = o + jnp.concatenate([ois[u][:cs, :GLA_DV], ois[u][cs:, GLA_DV:]], axis=1)
            o_ref[pl.ds(r0s[u], cs), :] = o.astype(BF16)
            st = st * dec_ref[pl.ds(ci * unroll + u, 1), :] + jnp.where(same_head, upds[u], 0.0)
        return st

    lax.fori_loop(0, nc // unroll, chunks, jnp.zeros((2 * GLA_DV, width), F32))


def _gla(main, kl, dec, batch, seq):
    n = main.shape[0]
    pairs = GLA_HEADS // 2
    q_off = 3 * FOX_WIDTH // LANES
    k_off = q_off + GLA_QK // LANES
    pv = 2 * GLA_DV
    v_off = (3 * FOX_WIDTH + 2 * GLA_QK) // pv
    return pl.pallas_call(
        _gla_kernel,
        grid=(batch, pairs),
        in_specs=[
            pl.BlockSpec((seq, LANES), lambda b, p: (b, q_off + p)),
            pl.BlockSpec((seq, LANES), lambda b, p: (b, k_off + p)),
            pl.BlockSpec((seq, pv), lambda b, p: (b, v_off + p)),
            pl.BlockSpec((seq, LANES), lambda b, p: (b, p)),
            pl.BlockSpec((seq // GLA_CHUNK, LANES), lambda b, p: (b, p)),
        ],
        out_specs=pl.BlockSpec((seq, pv), lambda b, p: (b, p)),
        out_shape=jax.ShapeDtypeStruct((n, GLA_V), BF16),
        compiler_params=_cparams(("arbitrary", "arbitrary")),
        name="gla",
    )(main, main, main, kl, dec)


def _post_kernel(fox_ref, gla_ref, gg_ref, h_ref, gg_gain_ref, wout_ref, cg_ref, wxq_ref, k_ref, v_ref, wxo_ref,
                 mg_ref, wr_ref, br_ref, h2_ref, hn_ref, route_ref, cnt_ref, carry_ref):
    tm = h_ref.shape[0]

    @pl.when(pl.program_id(0) == 0)
    def _():
        carry_ref[...] = jnp.zeros_like(carry_ref)

    raw = gla_ref[...].astype(F32)
    normed = []
    for hh in range(GLA_HEADS):
        oh = raw[:, hh * GLA_DV:(hh + 1) * GLA_DV]
        normed.append(oh * lax.rsqrt(jnp.mean(oh * oh, axis=-1, keepdims=True) + EPS))
    g = gg_ref[...].astype(F32)
    gla = (jnp.concatenate(normed, axis=1) * gg_gain_ref[...] * (g * jax.nn.sigmoid(g))).astype(BF16)
    y = _dot(fox_ref[...], wout_ref[0:FOX_WIDTH, :]) + _dot(gla, wout_ref[FOX_WIDTH:, :])
    h1 = h_ref[...] + y
    hn = _rms(h1, cg_ref[...]).astype(BF16)
    q = _dot(hn, wxq_ref[...]).astype(BF16)
    xscale = X_DIM ** -0.5
    heads = []
    for hh in range(X_HEADS):
        sl = slice(hh * X_DIM, (hh + 1) * X_DIM)
        s = _dot_nt(q[:, sl], k_ref[:, sl]) * xscale
        p = jnp.exp(s - jnp.max(s, axis=-1, keepdims=True))
        heads.append(_dot(p.astype(BF16), v_ref[:, sl]) / jnp.sum(p, axis=-1, keepdims=True))
    o = jnp.concatenate(heads, axis=1).astype(BF16)
    h2 = h1 + _dot(o, wxo_ref[...])
    h2_ref[...] = h2
    hn2 = _rms(h2, mg_ref[...])
    hn_ref[...] = _pack_rows(hn2)

    xh = hn2.astype(BF16)
    xl = (hn2 - xh.astype(F32)).astype(BF16)
    both_w = _dot(jnp.concatenate([xh, xl], axis=0), wr_ref[...])
    logits = both_w[:tm, :LANES] + both_w[:tm, LANES:] + both_w[tm:, :LANES] + both_w[tm:, LANES:] + br_ref[...]
    lt = jnp.transpose(logits)[:ROUTE_ROWS, :]
    row = lax.broadcasted_iota(jnp.int32, (ROUTE_ROWS, tm), 0)
    neg = -jnp.inf
    gl = jnp.where(row < N_GROUPS, lt, neg)
    gmax = jnp.max(gl, axis=0, keepdims=True)
    ge = jnp.exp(gl - gmax)
    gprob = ge / jnp.sum(ge, axis=0, keepdims=True)
    pmax = jnp.max(gprob, axis=0, keepdims=True)
    grp = jnp.min(jnp.where(gprob == pmax, row, ROUTE_ROWS), axis=0, keepdims=True)
    in_grp = (row >= N_GROUPS) & (row < N_GROUPS + N_EXPERTS) & (((row - N_GROUPS) // GROUP_SIZE) == grp)
    el = jnp.where(in_grp, lt, neg)
    emax = jnp.max(el, axis=0, keepdims=True)
    ee = jnp.exp(el - emax)
    eprob = ee / jnp.sum(ee, axis=0, keepdims=True)
    p1 = jnp.max(eprob, axis=0, keepdims=True)
    row1 = jnp.min(jnp.where(in_grp & (eprob == p1), row, ROUTE_ROWS), axis=0, keepdims=True)
    rest = jnp.where(in_grp & (row != row1), eprob, -1.0)
    p2 = jnp.max(rest, axis=0, keepdims=True)
    row2 = jnp.min(jnp.where(rest == p2, row, ROUTE_ROWS), axis=0, keepdims=True)
    g1 = pmax * p1 / (p1 + p2)
    g2 = pmax * p2 / (p1 + p2)

    oh1 = row == row1
    oh2 = row == row2
    both = (oh1 | oh2).astype(BF16)
    srow = lax.broadcasted_iota(jnp.int32, (tm, tm), 0)
    scol = lax.broadcasted_iota(jnp.int32, (tm, tm), 1)
    earlier = (srow < scol).astype(BF16)
    carry = carry_ref[...]
    seen = _dot(both, earlier) + jnp.concatenate([carry] * (tm // LANES), axis=1)
    rank1 = jnp.sum(jnp.where(oh1, seen, 0.0), axis=0, keepdims=True)
    rank2 = jnp.sum(jnp.where(oh2, seen, 0.0), axis=0, keepdims=True)
    carry = carry + jnp.sum(both.astype(F32), axis=1, keepdims=True)
    carry_ref[...] = carry
    cnt_ref[...] = carry

    e1 = (row1 - N_GROUPS).astype(F32)
    e2 = (row2 - N_GROUPS).astype(F32)
    zero = jnp.zeros_like(g1)
    route_ref[...] = jnp.concatenate([e1, e2, g1, g2, rank1, rank2, zero, zero], axis=0)


def _post(fox, gla, main, h, gla_gain, wout, cg, wxq, kmem, vmem, wxo, mg, wr, br, seq, mem_len, layer):
    n, d_model = h.shape
    tm = POST_TILE
    per_seq = seq // tm
    const = lambda i: (0, 0)
    pick = lambda i: (layer, 0, 0)
    return pl.pallas_call(
        _post_kernel,
        grid=(n // tm,),
        in_specs=[
            pl.BlockSpec((tm, FOX_WIDTH), lambda i: (i, 0)),
            pl.BlockSpec((tm, GLA_V), lambda i: (i, 0)),
            pl.BlockSpec((tm, GLA_V), lambda i: (i, MAIN_WIDTH // GLA_V - 1)),
            pl.BlockSpec((tm, d_model), lambda i: (i, 0)),
            pl.BlockSpec((None, 1, GLA_V), pick),
            pl.BlockSpec((None, FOX_WIDTH + GLA_V, d_model), pick),
            pl.BlockSpec((None, 1, d_model), pick),
            pl.BlockSpec((None, d_model, X_WIDTH), pick),
            pl.BlockSpec((None, mem_len, X_WIDTH), lambda i: (layer, i // per_seq, 0)),
            pl.BlockSpec((None, mem_len, X_WIDTH), lambda i: (layer, i // per_seq, 0)),
            pl.BlockSpec((None, X_WIDTH, d_model), pick),
            pl.BlockSpec((None, 1, d_model), pick),
            pl.BlockSpec((None, d_model, 2 * LANES), pick),
            pl.BlockSpec((None, 1, LANES), pick),
        ],
        out_specs=[
            pl.BlockSpec((tm, d_model), lambda i: (i, 0)),
            pl.BlockSpec((tm, d_model // 2), lambda i: (i, 0)),
            pl.BlockSpec((ROUTE_WIDTH, tm), lambda i: (0, i)),
            pl.BlockSpec((ROUTE_ROWS, LANES), const),
        ],
        out_shape=[
            jax.ShapeDtypeStruct((n, d_model), F32),
            jax.ShapeDtypeStruct((n, d_model // 2), jnp.uint32),
            jax.ShapeDtypeStruct((ROUTE_WIDTH, n), F32),
            jax.ShapeDtypeStruct((ROUTE_ROWS, LANES), F32),
        ],
        scratch_shapes=[pltpu.VMEM((ROUTE_ROWS, LANES), F32)],
        compiler_params=_cparams(("arbitrary",)),
        name="post_mixer",
    )(fox, gla, main, h, gla_gain, wout, cg, wxq, kmem, vmem, wxo, mg, wr, br)


def _dispatch(dest_kmajor, x, n_rows):
    n, width = x.shape
    window = SC_GATHER_WINDOW
    mesh = plsc.VectorSubcoreMesh(core_axis_name="core", subcore_axis_name="subcore")
    workers = mesh.num_cores * mesh.num_subcores
    per_worker = n // workers
    assert n % (workers * window) == 0

    steps = per_worker // window
    assert steps % 2 == 0
    index_buf = pltpu.VMEM((window,), jnp.int32)
    row_buf = pltpu.VMEM((window, width), x.dtype)
    dma = pltpu.SemaphoreType.DMA

    @functools.partial(
        pl.kernel, out_type=jax.ShapeDtypeStruct((n_rows, width), x.dtype), mesh=mesh,
        scratch_types=[index_buf, index_buf, index_buf, index_buf, row_buf, row_buf, dma, dma, dma])
    def scatter(x_hbm, idx_hbm, out_hbm, idx0_a, idx1_a, idx0_b, idx1_b, rows_a, rows_b, sem_a, sem_b, sem_out):
        worker = lax.axis_index("subcore") * mesh.num_cores + lax.axis_index("core")
        base = worker * per_worker
        slots = ((idx0_a, idx1_a, rows_a, sem_a), (idx0_b, idx1_b, rows_b, sem_b))

        def load(step, slot):
            idx0, idx1, rows, sem = slots[slot]
            off = pl.multiple_of(base + step * window, window)
            pltpu.sync_copy(idx_hbm.at[pl.ds(off, window)], idx0)
            pltpu.sync_copy(idx_hbm.at[pl.ds(n + off, window)], idx1)
            pltpu.async_copy(x_hbm.at[pl.ds(off, window)], rows, sem)

        def store(slot):
            idx0, idx1, rows, sem = slots[slot]
            pltpu.make_async_copy(x_hbm.at[pl.ds(0, window)], rows, sem).wait()
            first = pltpu.async_copy(rows, out_hbm.at[idx0], sem_out)
            second = pltpu.async_copy(rows, out_hbm.at[idx1], sem_out)
            first.wait()
            second.wait()

        load(0, 0)

        @pl.loop(0, steps, step=2)
        def _(step):
            load(step + 1, 1)
            store(0)

            @pl.when(step + 2 < steps)
            def _():
                load(step + 2, 0)

            store(1)

    return scatter(x, dest_kmajor)


def _expert_kernel(be_ref, valid_ref, fresh_ref, x_ref, wg_ref, wu_ref, wd_ref, y_ref, wg_b, wu_b, wd_b):
    del be_ref
    i = pl.program_id(0)
    valid = valid_ref[i]

    @pl.when(fresh_ref[i] > 0)
    def _():
        wg_b[...] = wg_ref[...].astype(BF16)
        wu_b[...] = wu_ref[...].astype(BF16)
        wd_b[...] = wd_ref[...].astype(BF16)

    @pl.when(valid > 0)
    def _():
        row = lax.broadcasted_iota(jnp.int32, x_ref.shape, 0)
        lo, hi = _unpack_rows(jnp.where(row < valid, x_ref[...], jnp.uint32(0)))
        lo = lo.astype(BF16)
        hi = hi.astype(BF16)
        half = lo.shape[1]
        g = _dot(lo, wg_b[:half, :]) + _dot(hi, wg_b[half:, :])
        u = _dot(lo, wu_b[:half, :]) + _dot(hi, wu_b[half:, :])
        a = (g * jax.nn.sigmoid(g) * u).astype(BF16)
        y_ref[...] = _pack_rows(_dot(a, wd_b[...]))

    @pl.when(valid <= 0)
    def _():
        y_ref[...] = jnp.zeros_like(y_ref)


def _experts(block_e, valid, xs, wg, wu, wd, layer):
    r, width = xs.shape
    bm = MOE_BLOCK
    d_model, d_exp = wg.shape[-2:]
    fresh = jnp.concatenate([jnp.ones((1,), jnp.int32), (block_e[1:] != block_e[:-1]).astype(jnp.int32)])
    pick = lambda i, be, va, fr: (layer, be[i], 0, 0)
    return pl.pallas_call(
        _expert_kernel,
        grid_spec=pltpu.PrefetchScalarGridSpec(
            num_scalar_prefetch=3,
            grid=(r // bm,),
            in_specs=[
                pl.BlockSpec((bm, width), lambda i, be, va, fr: (i, 0)),
                pl.BlockSpec((None, None, d_model, d_exp), pick),
                pl.BlockSpec((None, None, d_model, d_exp), pick),
                pl.BlockSpec((None, None, d_exp, d_model), pick),
            ],
            out_specs=pl.BlockSpec((bm, width), lambda i, be, va, fr: (i, 0)),
            scratch_shapes=[
                pltpu.VMEM((d_model, d_exp), BF16),
                pltpu.VMEM((d_model, d_exp), BF16),
                pltpu.VMEM((d_exp, d_model), BF16),
            ],
        ),
        out_shape=jax.ShapeDtypeStruct((r, width), jnp.uint32),
        compiler_params=_cparams(("arbitrary",)),
        name="moe_experts",
    )(block_e, valid, fresh, xs, wg, wu, wd)


def _sc_gather_rows(table, idx):
    m = idx.shape[0]
    width = table.shape[1]
    window = SC_GATHER_WINDOW
    mesh = plsc.VectorSubcoreMesh(core_axis_name="core", subcore_axis_name="subcore")
    workers = mesh.num_cores * mesh.num_subcores
    per_worker = m // workers
    assert m % (workers * window) == 0

    steps = per_worker // window
    assert steps % 2 == 0
    index_buf = pltpu.VMEM((window,), jnp.int32)
    row_buf = pltpu.VMEM((window, width), table.dtype)
    dma = pltpu.SemaphoreType.DMA

    @functools.partial(
        pl.kernel, out_type=jax.ShapeDtypeStruct((m, width), table.dtype), mesh=mesh,
        scratch_types=[index_buf, index_buf, row_buf, row_buf, dma, dma])
    def gather(table_hbm, idx_hbm, out_hbm, idx_a, idx_b, rows_a, rows_b, sem_a, sem_b):
        worker = lax.axis_index("subcore") * mesh.num_cores + lax.axis_index("core")
        base = worker * per_worker
        slots = ((idx_a, rows_a, sem_a), (idx_b, rows_b, sem_b))

        def fetch(step, slot):
            idx, rows, sem = slots[slot]
            off = pl.multiple_of(base + step * window, window)
            pltpu.sync_copy(idx_hbm.at[pl.ds(off, window)], idx)
            pltpu.async_copy(table_hbm.at[idx], rows, sem)

        def flush(step, slot):
            idx, rows, sem = slots[slot]
            off = pl.multiple_of(base + step * window, window)
            pltpu.make_async_copy(table_hbm.at[idx], rows, sem).wait()
            pltpu.sync_copy(rows, out_hbm.at[pl.ds(off, window)])

        fetch(0, 0)

        @pl.loop(0, steps, step=2)
        def _(step):
            fetch(step + 1, 1)
            flush(step, 0)

            @pl.when(step + 2 < steps)
            def _():
                fetch(step + 2, 0)

            flush(step + 1, 1)

    return gather(table, idx)


def _final_kernel(h_ref, gate_ref, gain_ref, y0_ref, y1_ref, o_ref):
    o_ref[...] = _rms(_moe_sum(h_ref[...], gate_ref[...], y0_ref[...], y1_ref[...]), gain_ref[...])


def _final(h, route, gain, picked):
    n, d_model = h.shape
    tc = MOVE_TILE
    steps = n // tc
    return pl.pallas_call(
        _final_kernel,
        grid=(steps,),
        in_specs=[
            pl.BlockSpec((tc, d_model), lambda i: (i, 0)),
            pl.BlockSpec((ROUTE_WIDTH, tc), lambda i: (0, i)),
            pl.BlockSpec((1, d_model), lambda i: (0, 0)),
            pl.BlockSpec((tc, d_model // 2), lambda i: (i, 0)),
            pl.BlockSpec((tc, d_model // 2), lambda i: (i + steps, 0)),
        ],
        out_specs=pl.BlockSpec((tc, d_model), lambda i: (i, 0)),
        out_shape=jax.ShapeDtypeStruct((n, d_model), F32),
        compiler_params=_cparams(("arbitrary",)),
        name="moe_final",
    )(h, route, gain, picked, picked)


def _routing_tables(route, cnt, n_rows):
    bm = MOE_BLOCK
    expert = route[0:2].astype(jnp.int32)
    rank = route[4:6].astype(jnp.int32)
    counts = cnt[N_GROUPS:N_GROUPS + N_EXPERTS, 0].astype(jnp.int32)
    padded = (counts + bm - 1) // bm * bm
    pad_ends = jnp.cumsum(padded)
    pad_starts = pad_ends - padded
    ids = jnp.arange(N_EXPERTS, dtype=jnp.int32)
    start_of = jnp.sum(jnp.where(expert[..., None] == ids, pad_starts, 0), axis=-1)
    dest = (start_of + rank).reshape(-1).astype(jnp.int32)
    block_row = jnp.arange(n_rows // bm, dtype=jnp.int32) * bm
    block_e = jnp.minimum(jnp.sum((pad_ends[None, :] <= block_row[:, None]).astype(jnp.int32), axis=-1),
                          N_EXPERTS - 1)
    row_end = jnp.sum(jnp.where(block_e[:, None] == ids, pad_starts + counts, 0), axis=-1)
    valid = jnp.clip(row_end - block_row, 0, bm).astype(jnp.int32)
    return dest, block_e, valid


def kernel(x, mem, mem_norm, mix_norm, w_in, b_forget, w_alpha_up, b_alpha, fox_out_gain, gla_out_gain, w_out,
           cross_norm, w_xq, w_xk, w_xv, w_xo, moe_norm, w_router_group, b_router_group, w_router_expert,
           b_router_expert, w_expert_gate, w_expert_up, w_expert_down, final_norm):
    batch, seq, d_model = x.shape
    mem_len = mem.shape[1]
    depth = w_in.shape[0]
    n = batch * seq
    assert seq % FOX_TILE == 0 and seq % IN_TILE == 0 and seq % POST_TILE == 0 and seq % GLA_CHUNK == 0
    assert n % MOVE_TILE == 0 and d_model % LANES == 0

    c0 = 3 * FOX_WIDTH
    c1 = c0 + FOX_HEADS
    c2 = c1 + 2 * GLA_QK + 2 * GLA_V
    pad = LANES - FOX_HEADS - GLA_RANK
    mem2d = mem.reshape(batch * mem_len, d_model)

    def in_proj_weights(l):
        w = w_in[l:l + 1]
        w_main = jnp.concatenate([w[:, :, :c0], w[:, :, c1:c2]], axis=-1).astype(BF16)
        w_small = jnp.concatenate([w[:, :, c0:c1], w[:, :, c2:], jnp.zeros((1, d_model, pad), F32)],
                                  axis=-1).astype(BF16)
        w_up = jnp.concatenate([jnp.zeros((1, FOX_HEADS, GLA_QK), F32), w_alpha_up[l:l + 1],
                                jnp.zeros((1, pad, GLA_QK), F32)], axis=1).astype(BF16)
        b_f = jnp.pad(b_forget[l:l + 1], ((0, 0), (0, LANES - FOX_HEADS)))[:, None, :]
        return mix_norm[l:l + 1, None, :], w_main, w_small, w_up, b_f, b_alpha[l:l + 1, None, :]

    def post_weights(l):
        w_r = jnp.concatenate([w_router_group[l:l + 1], w_router_expert[l:l + 1],
                               jnp.zeros((1, d_model, LANES - N_GROUPS - N_EXPERTS), F32)], axis=-1)
        w_rh = w_r.astype(BF16)
        w_rs = jnp.concatenate([w_rh, (w_r - w_rh.astype(F32)).astype(BF16)], axis=-1)
        b_r = jnp.pad(jnp.concatenate([b_router_group[l:l + 1], b_router_expert[l:l + 1]], axis=-1),
                      ((0, 0), (0, LANES - N_GROUPS - N_EXPERTS)))[:, None, :]
        kmem, vmem = _mem_kv(mem2d, mem_norm[None, :], w_xk[l:l + 1].astype(BF16), w_xv[l:l + 1].astype(BF16),
                             batch, mem_len)
        return (gla_out_gain[l:l + 1, None, :], w_out[l:l + 1].astype(BF16), cross_norm[l:l + 1, None, :],
                w_xq[l:l + 1].astype(BF16), kmem, vmem, w_xo[l:l + 1].astype(BF16), moe_norm[l:l + 1, None, :],
                w_rs, b_r)

    n_rows = 2 * n + N_EXPERTS * MOE_BLOCK
    h = x.reshape(n, d_model)
    moe = None
    for l in range(depth):
        if moe is None:
            main, c, kl, dec = _in_proj(h, *in_proj_weights(l), 0, seq)
        else:
            h, main, c, kl, dec = _in_proj(h, *in_proj_weights(l), 0, seq, moe)
        fox = _fox_attention(main, c, fox_out_gain[l][None, :], batch, seq)
        gla = _gla(main, kl, dec, batch, seq)
        h2, hn2, route, cnt = _post(fox, gla, main, h, *post_weights(l), seq, mem_len, 0)
        dest, block_e, valid = _routing_tables(route, cnt, n_rows)
        xs = _dispatch(dest, hn2, n_rows)
        y = _experts(block_e, valid, xs, w_expert_gate, w_expert_up, w_expert_down, l)
        h, moe = h2, (route, _sc_gather_rows(y, dest))
    return _final(h, moe[0], final_norm[None, :], moe[1]).reshape(batch, seq, d_model)
```

```python
import functools

import jax
import jax.numpy as jnp
from jax import lax
from jax.experimental import pallas as pl
from jax.experimental.pallas import tpu as pltpu
from jax.experimental.pallas import tpu_sc as plsc

F32 = jnp.float32
BF16 = jnp.bfloat16
EPS = 1e-6
LOG2E = 1.4426950408889634

FOX_HEADS = 8
FOX_DIM = 64
FOX_WIDTH = FOX_HEADS * FOX_DIM
GLA_HEADS = 4
GLA_DK = 64
GLA_DV = 128
GLA_QK = GLA_HEADS * GLA_DK
GLA_V = GLA_HEADS * GLA_DV
GLA_RANK = 16
GLA_TAU = 16.0
GLA_CHUNK = 64
X_HEADS = 4
X_DIM = 128
X_WIDTH = X_HEADS * X_DIM
N_GROUPS = 4
GROUP_SIZE = 4
N_EXPERTS = N_GROUPS * GROUP_SIZE
MAIN_WIDTH = 3 * FOX_WIDTH + 2 * GLA_QK + 2 * GLA_V

LANES = 128
ROUTE_WIDTH = 8
ROUTE_ROWS = 32
VMEM_LIMIT = 56 * 1024 * 1024

IN_TILE = 1024
FOX_TILE = 512
FOX_SLAB = 64
POST_TILE = 1024
MOE_BLOCK = 512
MOVE_TILE = 1024
SC_GATHER_WINDOW = 64


def _cparams(sem):
    return pltpu.CompilerParams(dimension_semantics=sem, vmem_limit_bytes=VMEM_LIMIT)


def _rms(x, gain):
    return x * lax.rsqrt(jnp.mean(x * x, axis=-1, keepdims=True) + EPS) * gain


def _log_sigmoid(x):
    return jnp.minimum(x, 0.0) - jnp.log1p(jnp.exp(-jnp.abs(x)))


def _dot(a, b):
    return jnp.dot(a, b, preferred_element_type=F32)


def _dot_nt(a, b):
    return lax.dot_general(a, b, (((1,), (1,)), ((), ())), preferred_element_type=F32)


def _pack_rows(x):
    half = x.shape[1] // 2
    lo = lax.bitcast_convert_type(x[:, :half].astype(BF16).astype(F32), jnp.uint32)
    hi = lax.bitcast_convert_type(x[:, half:].astype(BF16).astype(F32), jnp.uint32)
    return (lo >> 16) | hi


def _unpack_rows(w):
    lo = lax.bitcast_convert_type(w << 16, F32)
    hi = lax.bitcast_convert_type(w & jnp.uint32(0xFFFF0000), F32)
    return lo, hi


def _mem_kv_kernel(mem_ref, gain_ref, wk_ref, wv_ref, k_ref, v_ref):
    mn = _rms(mem_ref[...], gain_ref[...]).astype(BF16)
    for l in range(wk_ref.shape[0]):
        k_ref[l] = _dot(mn, wk_ref[l]).astype(BF16)
        v_ref[l] = _dot(mn, wv_ref[l]).astype(BF16)


def _mem_kv(mem2d, gain, wk, wv, batch, mem_len):
    depth, d_model, width = wk.shape
    out = jax.ShapeDtypeStruct((depth, batch * mem_len, width), BF16)
    return pl.pallas_call(
        _mem_kv_kernel,
        grid=(batch,),
        in_specs=[
            pl.BlockSpec((mem_len, d_model), lambda b: (b, 0)),
            pl.BlockSpec((1, d_model), lambda b: (0, 0)),
            pl.BlockSpec((depth, d_model, width), lambda b: (0, 0, 0)),
            pl.BlockSpec((depth, d_model, width), lambda b: (0, 0, 0)),
        ],
        out_specs=[
            pl.BlockSpec((depth, mem_len, width), lambda b: (0, b, 0)),
            pl.BlockSpec((depth, mem_len, width), lambda b: (0, b, 0)),
        ],
        out_shape=[out, out],
        compiler_params=_cparams(("arbitrary",)),
        name="mem_kv",
    )(mem2d, gain, wk, wv)


def _moe_sum(h, route, y0_packed, y1_packed):
    gate = jnp.transpose(route)
    y0 = jnp.concatenate(_unpack_rows(y0_packed), axis=1)
    y1 = jnp.concatenate(_unpack_rows(y1_packed), axis=1)
    return h + gate[:, 2:3] * y0 + gate[:, 3:4] * y1


def _in_proj_body(h, gain_ref, wmain_ref, wsmall_ref, wup_ref, bf_ref, ba_ref, main_ref, c_ref, kl_ref, dec_ref,
                  carry_ref, tiles_per_seq):
    tm = h.shape[0]
    cs = GLA_CHUNK
    xn = _rms(h, gain_ref[...]).astype(BF16)
    small = _dot(xn, wsmall_ref[...])
    lane = lax.broadcasted_iota(jnp.int32, small.shape, 1)
    c = jnp.transpose(_log_sigmoid(small + bf_ref[...]))[:FOX_HEADS, :]
    pos = lax.broadcasted_iota(jnp.int32, c.shape, 1)
    shift = 1
    while shift < tm:
        c = c + jnp.where(pos >= shift, pltpu.roll(c, shift, axis=1), 0.0)
        shift *= 2
    starts_sequence = pl.program_id(0) % tiles_per_seq == 0
    c = c + jnp.where(starts_sequence, 0.0, carry_ref[:, 0:1])
    carry_ref[...] = jnp.broadcast_to(c[:, tm - 1:tm], carry_ref.shape)
    c_ref[...] = c
    a = _dot(small.astype(BF16), wup_ref[...]) + ba_ref[...]
    b = _log_sigmoid(a) * (1.0 / GLA_TAU)
    pos = lax.broadcasted_iota(jnp.int32, b.shape, 0) % cs
    shift = 1
    while shift < cs:
        b = b + jnp.where(pos >= shift, pltpu.roll(b, shift, axis=0), 0.0)
        shift *= 2
    dec = jnp.exp(b.reshape(tm // cs, cs, GLA_QK)[:, cs - 1:cs, :])
    q0 = 3 * FOX_WIDTH
    k0 = q0 + GLA_QK
    qk = _dot(xn, wmain_ref[:, q0:k0 + GLA_QK])
    main_ref[:, q0:k0] = (qk[:, :GLA_QK] * jnp.exp(b) * (GLA_DK ** -0.5)).astype(BF16)
    ke = qk[:, GLA_QK:] * jnp.exp(-b)
    main_ref[:, k0:k0 + GLA_QK] = ke.astype(BF16)
    kl_ref[...] = (ke.reshape(tm // cs, cs, GLA_QK) * dec).reshape(tm, GLA_QK).astype(BF16)
    dec_ref[...] = dec.reshape(tm // cs, GLA_QK)
    step = 512
    for lo in list(range(0, q0, step)) + list(range(k0 + GLA_QK, MAIN_WIDTH, step)):
        main_ref[:, lo:lo + step] = _dot(xn, wmain_ref[:, lo:lo + step]).astype(BF16)


def _in_proj_kernel(h_ref, *refs, tiles_per_seq):
    _in_proj_body(h_ref[...], *refs, tiles_per_seq)


def _in_proj_after_moe_kernel(h_ref, gate_ref, y0_ref, y1_ref, gain_ref, wmain_ref, wsmall_ref, wup_ref, bf_ref,
                              ba_ref, hout_ref, main_ref, c_ref, kl_ref, dec_ref, carry_ref, *, tiles_per_seq):
    h = _moe_sum(h_ref[...], gate_ref[...], y0_ref[...], y1_ref[...])
    hout_ref[...] = h
    _in_proj_body(h, gain_ref, wmain_ref, wsmall_ref, wup_ref, bf_ref, ba_ref, main_ref, c_ref, kl_ref, dec_ref,
                  carry_ref, tiles_per_seq)


def _in_proj(h, gain, wmain, wsmall, wup, bf, ba, layer, seq, moe=None):
    n, d_model = h.shape
    tm = IN_TILE
    steps = n // tm
    chunks = tm // GLA_CHUNK
    tiles_per_seq = seq // tm
    pick = lambda i: (layer, 0, 0)
    row_block = lambda width, rows=tm: pl.BlockSpec((rows, width), lambda i: (i, 0))
    weight_specs = [
        pl.BlockSpec((None, 1, d_model), pick),
        pl.BlockSpec((None, d_model, MAIN_WIDTH), pick),
        pl.BlockSpec((None, d_model, LANES), pick),
        pl.BlockSpec((None, LANES, GLA_QK), pick),
        pl.BlockSpec((None, 1, LANES), pick),
        pl.BlockSpec((None, 1, GLA_QK), pick),
    ]
    out_specs = [row_block(MAIN_WIDTH), pl.BlockSpec((FOX_HEADS, tm), lambda i: (0, i)), row_block(GLA_QK),
                 row_block(GLA_QK, chunks)]
    out_shape = [
        jax.ShapeDtypeStruct((n, MAIN_WIDTH), BF16),
        jax.ShapeDtypeStruct((FOX_HEADS, n), F32),
        jax.ShapeDtypeStruct((n, GLA_QK), BF16),
        jax.ShapeDtypeStruct((n // GLA_CHUNK, GLA_QK), F32),
    ]
    weights = (gain, wmain, wsmall, wup, bf, ba)
    carry = [pltpu.VMEM((FOX_HEADS, LANES), F32)]
    if moe is None:
        return pl.pallas_call(
            functools.partial(_in_proj_kernel, tiles_per_seq=tiles_per_seq),
            grid=(steps,), in_specs=[row_block(d_model)] + weight_specs,
            out_specs=out_specs, out_shape=out_shape, scratch_shapes=carry,
            compiler_params=_cparams(("arbitrary",)), name="in_proj",
        )(h, *weights)
    route, picked = moe
    half = d_model // 2
    return pl.pallas_call(
        functools.partial(_in_proj_after_moe_kernel, tiles_per_seq=tiles_per_seq), grid=(steps,),
        in_specs=[row_block(d_model), pl.BlockSpec((ROUTE_WIDTH, tm), lambda i: (0, i)), row_block(half),
                  pl.BlockSpec((tm, half), lambda i: (i + steps, 0))] + weight_specs,
        out_specs=[row_block(d_model)] + out_specs,
        out_shape=[jax.ShapeDtypeStruct((n, d_model), F32)] + out_shape, scratch_shapes=carry,
        compiler_params=_cparams(("arbitrary",)), name="in_proj_after_moe",
    )(h, route, picked, picked, *weights)


def _fox_kernel(q_ref, k_ref, v_ref, c_ref, gain_ref, o_ref, q2_ref, s_ref, p_ref, alpha_ref, m_ref, l_ref, acc_ref):
    tq = FOX_TILE
    rows = 2 * tq
    slab = FOX_SLAB
    nq = q_ref.shape[0] // tq
    lane = lax.broadcasted_iota(jnp.int32, (1, LANES), 1)
    first = lane < FOX_DIM
    scale = FOX_DIM ** -0.5 * LOG2E
    for qi in range(nq):
        q = q_ref[qi * tq:(qi + 1) * tq, :].astype(F32) * scale
        q2_ref[qi, :tq, :] = jnp.where(first, q, 0.0).astype(BF16)
        q2_ref[qi, tq:, :] = jnp.where(first, 0.0, q).astype(BF16)

    head_row = lax.broadcasted_iota(jnp.int32, (FOX_HEADS, tq), 0)
    pair = pl.program_id(1)

    def scores(qi, j):
        cj = c_ref[:, j * tq:(j + 1) * tq] * LOG2E
        c0 = jnp.sum(jnp.where(head_row == 2 * pair, cj, 0.0), axis=0, keepdims=True)
        c1 = jnp.sum(jnp.where(head_row == 2 * pair + 1, cj, 0.0), axis=0, keepdims=True)
        d = _dot_nt(q2_ref[qi], k_ref[j * tq:(j + 1) * tq, :])
        s_ref[:tq, :] = d[:tq] - c0
        s_ref[tq:, :] = d[tq:] - c1

    def weighted_values(qi, j):
        par = qi % 2
        acc_ref[par] = alpha_ref[par] * acc_ref[par] + _dot(p_ref[...], v_ref[j * tq:(j + 1) * tq, :])

    def softmax(qi, masked):
        par = qi % 2
        for r in range(rows // slab):
            sl = slice(r * slab, (r + 1) * slab)
            s = s_ref[sl, :]
            if masked:
                row = lax.broadcasted_iota(jnp.int32, (slab, tq), 0) + (r * slab) % tq
                col = lax.broadcasted_iota(jnp.int32, (slab, tq), 1)
                s = jnp.where(row >= col, s, -jnp.inf)
            m_old = m_ref[par, sl, :]
            m_new = jnp.maximum(m_old, jnp.max(s, axis=-1, keepdims=True))
            alpha = jnp.exp2(m_old - m_new)
            p = jnp.exp2(s - jnp.concatenate([m_new] * (tq // LANES), axis=1))
            l_ref[par, sl, :] = alpha * l_ref[par, sl, :] + jnp.sum(p, axis=-1, keepdims=True)
            m_ref[par, sl, :] = m_new
            alpha_ref[par, sl, :] = alpha
            p_ref[sl, :] = p.astype(BF16)

    def finalize(qi):
        par = qi % 2
        o2 = acc_ref[par] / l_ref[par]
        o = jnp.where(first, o2[:tq], o2[tq:])
        sq = o * o
        ss0 = jnp.sum(jnp.where(first, sq, 0.0), axis=-1, keepdims=True)
        ss1 = jnp.sum(jnp.where(first, 0.0, sq), axis=-1, keepdims=True)
        ms = jnp.where(first, ss0, ss1) * (1.0 / FOX_DIM)
        o_ref[qi * tq:(qi + 1) * tq, :] = (o * lax.rsqrt(ms + EPS) * gain_ref[...]).astype(BF16)

    steps = [(qi, j) for qi in range(nq) for j in range(qi + 1)]
    scores(*steps[0])
    for t, (qi, j) in enumerate(steps):
        if t > 0:
            weighted_values(*steps[t - 1])
            if steps[t - 1][0] != qi:
                finalize(steps[t - 1][0])
        if j == 0:
            par = qi % 2
            m_ref[par] = jnp.full(m_ref.shape[1:], -jnp.inf, F32)
            l_ref[par] = jnp.zeros(l_ref.shape[1:], F32)
            acc_ref[par] = jnp.zeros(acc_ref.shape[1:], F32)
        softmax(qi, masked=(j == qi))
        if t + 1 < len(steps):
            scores(*steps[t + 1])
    weighted_values(*steps[-1])
    finalize(steps[-1][0])


def _fox_attention(main, c, gain, batch, seq):
    n = main.shape[0]
    tq = FOX_TILE
    nq = seq // tq
    pairs = FOX_HEADS // 2
    k_off = FOX_WIDTH // LANES
    v_off = 2 * FOX_WIDTH // LANES
    stat = pltpu.VMEM((2, 2 * tq, LANES), F32)
    return pl.pallas_call(
        _fox_kernel,
        grid=(batch, pairs),
        in_specs=[
            pl.BlockSpec((seq, LANES), lambda b, p: (b, p)),
            pl.BlockSpec((seq, LANES), lambda b, p: (b, k_off + p)),
            pl.BlockSpec((seq, LANES), lambda b, p: (b, v_off + p)),
            pl.BlockSpec((FOX_HEADS, seq), lambda b, p: (0, b)),
            pl.BlockSpec((1, LANES), lambda b, p: (0, p)),
        ],
        out_specs=pl.BlockSpec((seq, LANES), lambda b, p: (b, p)),
        out_shape=jax.ShapeDtypeStruct((n, FOX_WIDTH), BF16),
        scratch_shapes=[
            pltpu.VMEM((nq, 2 * tq, LANES), BF16),
            pltpu.VMEM((2 * tq, tq), F32),
            pltpu.VMEM((2 * tq, tq), BF16),
            stat, stat, stat, stat,
        ],
        compiler_params=_cparams(("arbitrary", "arbitrary")),
        name="fox_attention",
    )(main, main, main, c, gain)


def _gla_kernel(qe_ref, ke_ref, v_ref, kl_ref, dec_ref, o_ref):
    seq = qe_ref.shape[0]
    cs = GLA_CHUNK
    nc = seq // cs
    width = 2 * GLA_DK

    lane = lax.broadcasted_iota(jnp.int32, (1, width), 1)
    first = lane < GLA_DK
    row = lax.broadcasted_iota(jnp.int32, (2 * cs, cs), 0)
    col = lax.broadcasted_iota(jnp.int32, (2 * cs, cs), 1)
    tril2 = jnp.where(row >= cs, row - cs, row) >= col
    srow = lax.broadcasted_iota(jnp.int32, (2 * GLA_DV, width), 0)
    scol = lax.broadcasted_iota(jnp.int32, (2 * GLA_DV, width), 1)
    same_head = (srow >= GLA_DV) == (scol >= GLA_DK)
    unroll = 8

    def chunks(ci, st):
        r0s = [pl.multiple_of((ci * unroll + u) * cs, cs) for u in range(unroll)]
        qes = [qe_ref[pl.ds(r0, cs), :] for r0 in r0s]
        vs = [v_ref[pl.ds(r0, cs), :] for r0 in r0s]
        atts, upds = [], []
        for u in range(unroll):
            zero = jnp.zeros_like(qes[u])
            q2 = jnp.concatenate([jnp.where(first, qes[u], zero), jnp.where(first, zero, qes[u])], axis=0)
            atts.append(jnp.where(tril2, _dot_nt(q2, ke_ref[pl.ds(r0s[u], cs), :]), 0.0).astype(BF16))
        for u in range(unroll):
            upds.append(lax.dot_general(vs[u], kl_ref[pl.ds(r0s[u], cs), :], (((0,), (0,)), ((), ())),
                                        preferred_element_type=F32))
        ois = [_dot(atts[u], vs[u]) for u in range(unroll)]
        for u in range(unroll):
            o = _dot_nt(qes[u], st.astype(BF16))
            o = o + jnp.concatenate([ois[u][:cs, :GLA_DV], ois[u][cs:, GLA_DV:]], axis=1)
            o_ref[pl.ds(r0s[u], cs), :] = o.astype(BF16)
            st = st * dec_ref[pl.ds(ci * unroll + u, 1), :] + jnp.where(same_head, upds[u], 0.0)
        return st

    lax.fori_loop(0, nc // unroll, chunks, jnp.zeros((2 * GLA_DV, width), F32))


def _gla(main, kl, dec, batch, seq):
    n = main.shape[0]
    pairs = GLA_HEADS // 2
    q_off = 3 * FOX_WIDTH // LANES
    k_off = q_off + GLA_QK // LANES
    pv = 2 * GLA_DV
    v_off = (3 * FOX_WIDTH + 2 * GLA_QK) // pv
    return pl.pallas_call(
        _gla_kernel,
        grid=(batch, pairs),
        in_specs=[
            pl.BlockSpec((seq, LANES), lambda b, p: (b, q_off + p)),
            pl.BlockSpec((seq, LANES), lambda b, p: (b, k_off + p)),
            pl.BlockSpec((seq, pv), lambda b, p: (b, v_off + p)),
            pl.BlockSpec((seq, LANES), lambda b, p: (b, p)),
            pl.BlockSpec((seq // GLA_CHUNK, LANES), lambda b, p: (b, p)),
        ],
        out_specs=pl.BlockSpec((seq, pv), lambda b, p: (b, p)),
        out_shape=jax.ShapeDtypeStruct((n, GLA_V), BF16),
        compiler_params=_cparams(("arbitrary", "arbitrary")),
        name="gla",
    )(main, main, main, kl, dec)


def _post_kernel(fox_ref, gla_ref, gg_ref, h_ref, gg_gain_ref, wout_ref, cg_ref, wxq_ref, k_ref, v_ref, wxo_ref,
                 mg_ref, wr_ref, br_ref, h2_ref, hn_ref, route_ref, cnt_ref, carry_ref):
    tm = h_ref.shape[0]

    @pl.when(pl.program_id(0) == 0)
    def _():
        carry_ref[...] = jnp.zeros_like(carry_ref)

    raw = gla_ref[...].astype(F32)
    normed = []
    for hh in range(GLA_HEADS):
        oh = raw[:, hh * GLA_DV:(hh + 1) * GLA_DV]
        normed.append(oh * lax.rsqrt(jnp.mean(oh * oh, axis=-1, keepdims=True) + EPS))
    g = gg_ref[...].astype(F32)
    gla = (jnp.concatenate(normed, axis=1) * gg_gain_ref[...] * (g * jax.nn.sigmoid(g))).astype(BF16)
    y = _dot(fox_ref[...], wout_ref[0:FOX_WIDTH, :]) + _dot(gla, wout_ref[FOX_WIDTH:, :])
    h1 = h_ref[...] + y
    hn = _rms(h1, cg_ref[...]).astype(BF16)
    q = _dot(hn, wxq_ref[...]).astype(BF16)
    xscale = X_DIM ** -0.5
    heads = []
    for hh in range(X_HEADS):
        sl = slice(hh * X_DIM, (hh + 1) * X_DIM)
        s = _dot_nt(q[:, sl], k_ref[:, sl]) * xscale
        p = jnp.exp(s - jnp.max(s, axis=-1, keepdims=True))
        heads.append(_dot(p.astype(BF16), v_ref[:, sl]) / jnp.sum(p, axis=-1, keepdims=True))
    o = jnp.concatenate(heads, axis=1).astype(BF16)
    h2 = h1 + _dot(o, wxo_ref[...])
    h2_ref[...] = h2
    hn2 = _rms(h2, mg_ref[...])
    hn_ref[...] = _pack_rows(hn2)

    xh = hn2.astype(BF16)
    xl = (hn2 - xh.astype(F32)).astype(BF16)
    both_w = _dot(jnp.concatenate([xh, xl], axis=0), wr_ref[...])
    logits = both_w[:tm, :LANES] + both_w[:tm, LANES:] + both_w[tm:, :LANES] + both_w[tm:, LANES:] + br_ref[...]
    lt = jnp.transpose(logits)[:ROUTE_ROWS, :]
    row = lax.broadcasted_iota(jnp.int32, (ROUTE_ROWS, tm), 0)
    neg = -jnp.inf
    gl = jnp.where(row < N_GROUPS, lt, neg)
    gmax = jnp.max(gl, axis=0, keepdims=True)
    ge = jnp.exp(gl - gmax)
    gprob = ge / jnp.sum(ge, axis=0, keepdims=True)
    pmax = jnp.max(gprob, axis=0, keepdims=True)
    grp = jnp.min(jnp.where(gprob == pmax, row, ROUTE_ROWS), axis=0, keepdims=True)
    in_grp = (row >= N_GROUPS) & (row < N_GROUPS + N_EXPERTS) & (((row - N_GROUPS) // GROUP_SIZE) == grp)
    el = jnp.where(in_grp, lt, neg)
    emax = jnp.max(el, axis=0, keepdims=True)
    ee = jnp.exp(el - emax)
    eprob = ee / jnp.sum(ee, axis=0, keepdims=True)
    p1 = jnp.max(eprob, axis=0, keepdims=True)
    row1 = jnp.min(jnp.where(in_grp & (eprob == p1), row, ROUTE_ROWS), axis=0, keepdims=True)
    rest = jnp.where(in_grp & (row != row1), eprob, -1.0)
    p2 = jnp.max(rest, axis=0, keepdims=True)
    row2 = jnp.min(jnp.where(rest == p2, row, ROUTE_ROWS), axis=0, keepdims=True)
    g1 = pmax * p1 / (p1 + p2)
    g2 = pmax * p2 / (p1 + p2)

    oh1 = row == row1
    oh2 = row == row2
    both = (oh1 | oh2).astype(BF16)
    srow = lax.broadcasted_iota(jnp.int32, (tm, tm), 0)
    scol = lax.broadcasted_iota(jnp.int32, (tm, tm), 1)
    earlier = (srow < scol).astype(BF16)
    carry = carry_ref[...]
    seen = _dot(both, earlier) + jnp.concatenate([carry] * (tm // LANES), axis=1)
    rank1 = jnp.sum(jnp.where(oh1, seen, 0.0), axis=0, keepdims=True)
    rank2 = jnp.sum(jnp.where(oh2, seen, 0.0), axis=0, keepdims=True)
    carry = carry + jnp.sum(both.astype(F32), axis=1, keepdims=True)
    carry_ref[...] = carry
    cnt_ref[...] = carry

    e1 = (row1 - N_GROUPS).astype(F32)
    e2 = (row2 - N_GROUPS).astype(F32)
    zero = jnp.zeros_like(g1)
    route_ref[...] = jnp.concatenate([e1, e2, g1, g2, rank1, rank2, zero, zero], axis=0)


def _post(fox, gla, main, h, gla_gain, wout, cg, wxq, kmem, vmem, wxo, mg, wr, br, seq, mem_len, layer):
    n, d_model = h.shape
    tm = POST_TILE
    per_seq = seq // tm
    const = lambda i: (0, 0)
    pick = lambda i: (layer, 0, 0)
    return pl.pallas_call(
        _post_kernel,
        grid=(n // tm,),
        in_specs=[
            pl.BlockSpec((tm, FOX_WIDTH), lambda i: (i, 0)),
            pl.BlockSpec((tm, GLA_V), lambda i: (i, 0)),
            pl.BlockSpec((tm, GLA_V), lambda i: (i, MAIN_WIDTH // GLA_V - 1)),
            pl.BlockSpec((tm, d_model), lambda i: (i, 0)),
            pl.BlockSpec((None, 1, GLA_V), pick),
            pl.BlockSpec((None, FOX_WIDTH + GLA_V, d_model), pick),
            pl.BlockSpec((None, 1, d_model), pick),
            pl.BlockSpec((None, d_model, X_WIDTH), pick),
            pl.BlockSpec((None, mem_len, X_WIDTH), lambda i: (layer, i // per_seq, 0)),
            pl.BlockSpec((None, mem_len, X_WIDTH), lambda i: (layer, i // per_seq, 0)),
            pl.BlockSpec((None, X_WIDTH, d_model), pick),
            pl.BlockSpec((None, 1, d_model), pick),
            pl.BlockSpec((None, d_model, 2 * LANES), pick),
            pl.BlockSpec((None, 1, LANES), pick),
        ],
        out_specs=[
            pl.BlockSpec((tm, d_model), lambda i: (i, 0)),
            pl.BlockSpec((tm, d_model // 2), lambda i: (i, 0)),
            pl.BlockSpec((ROUTE_WIDTH, tm), lambda i: (0, i)),
            pl.BlockSpec((ROUTE_ROWS, LANES), const),
        ],
        out_shape=[
            jax.ShapeDtypeStruct((n, d_model), F32),
            jax.ShapeDtypeStruct((n, d_model // 2), jnp.uint32),
            jax.ShapeDtypeStruct((ROUTE_WIDTH, n), F32),
            jax.ShapeDtypeStruct((ROUTE_ROWS, LANES), F32),
        ],
        scratch_shapes=[pltpu.VMEM((ROUTE_ROWS, LANES), F32)],
        compiler_params=_cparams(("arbitrary",)),
        name="post_mixer",
    )(fox, gla, main, h, gla_gain, wout, cg, wxq, kmem, vmem, wxo, mg, wr, br)


def _dispatch(dest_kmajor, x, n_rows):
    n, width = x.shape
    window = SC_GATHER_WINDOW
    mesh = plsc.VectorSubcoreMesh(core_axis_name="core", subcore_axis_name="subcore")
    workers = mesh.num_cores * mesh.num_subcores
    per_worker = n // workers
    assert n % (workers * window) == 0

    steps = per_worker // window
    assert steps % 2 == 0
    index_buf = pltpu.VMEM((window,), jnp.int32)
    row_buf = pltpu.VMEM((window, width), x.dtype)
    dma = pltpu.SemaphoreType.DMA

    @functools.partial(
        pl.kernel, out_type=jax.ShapeDtypeStruct((n_rows, width), x.dtype), mesh=mesh,
        scratch_types=[index_buf, index_buf, index_buf, index_buf, row_buf, row_buf, dma, dma, dma])
    def scatter(x_hbm, idx_hbm, out_hbm, idx0_a, idx1_a, idx0_b, idx1_b, rows_a, rows_b, sem_a, sem_b, sem_out):
        worker = lax.axis_index("subcore") * mesh.num_cores + lax.axis_index("core")
        base = worker * per_worker
        slots = ((idx0_a, idx1_a, rows_a, sem_a), (idx0_b, idx1_b, rows_b, sem_b))

        def load(step, slot):
            idx0, idx1, rows, sem = slots[slot]
            off = pl.multiple_of(base + step * window, window)
            pltpu.sync_copy(idx_hbm.at[pl.ds(off, window)], idx0)
            pltpu.sync_copy(idx_hbm.at[pl.ds(n + off, window)], idx1)
            pltpu.async_copy(x_hbm.at[pl.ds(off, window)], rows, sem)

        def store(slot):
            idx0, idx1, rows, sem = slots[slot]
            pltpu.make_async_copy(x_hbm.at[pl.ds(0, window)], rows, sem).wait()
            first = pltpu.async_copy(rows, out_hbm.at[idx0], sem_out)
            second = pltpu.async_copy(rows, out_hbm.at[idx1], sem_out)
            first.wait()
            second.wait()

        load(0, 0)

        @pl.loop(0, steps, step=2)
        def _(step):
            load(step + 1, 1)
            store(0)

            @pl.when(step + 2 < steps)
            def _():
                load(step + 2, 0)

            store(1)

    return scatter(x, dest_kmajor)


def _expert_kernel(first_ref, count_ref, valid_ref, x_hbm, wg_ref, wu_ref, wd_ref, y_hbm,
                   x_buf, y_buf, wg_b, wu_b, wd_b, sem_in, sem_out):
    e = pl.program_id(0)
    bm = x_buf.shape[1]
    first = first_ref[e]
    count = count_ref[e]
    wg_b[...] = wg_ref[...].astype(BF16)
    wu_b[...] = wu_ref[...].astype(BF16)
    wd_b[...] = wd_ref[...].astype(BF16)

    def rows_of(block):
        return pl.ds(pl.multiple_of(block * bm, bm), bm)

    def load(j, slot):
        return pltpu.make_async_copy(x_hbm.at[rows_of(first + j)], x_buf.at[slot], sem_in.at[slot])

    def store(block, slot):
        return pltpu.make_async_copy(y_buf.at[slot], y_hbm.at[rows_of(block)], sem_out.at[slot])

    @pl.when(count > 0)
    def _():
        load(0, 0).start()

    def block_step(j, carry):
        slot = j % 2
        load(j, slot).wait()

        @pl.when(j + 1 < count)
        def _():
            load(j + 1, 1 - slot).start()

        @pl.when(j >= 2)
        def _():
            store(first + j - 2, slot).wait()

        row = lax.broadcasted_iota(jnp.int32, (bm, x_buf.shape[2]), 0)
        lo, hi = _unpack_rows(jnp.where(row < valid_ref[first + j], x_buf[slot], jnp.uint32(0)))
        lo = lo.astype(BF16)
        hi = hi.astype(BF16)
        half = lo.shape[1]
        g = _dot(lo, wg_b[:half, :]) + _dot(hi, wg_b[half:, :])
        u = _dot(lo, wu_b[:half, :]) + _dot(hi, wu_b[half:, :])
        a = (g * jax.nn.sigmoid(g) * u).astype(BF16)
        y_buf[slot] = _pack_rows(_dot(a, wd_b[...]))
        store(first + j, slot).start()
        return carry

    lax.fori_loop(0, count, block_step, 0)

    @pl.when(count >= 2)
    def _():
        store(first + count - 2, count % 2).wait()

    @pl.when(count >= 1)
    def _():
        store(first + count - 1, (count - 1) % 2).wait()

    @pl.when(e == pl.num_programs(0) - 1)
    def _():
        y_buf[0] = jnp.zeros(y_buf.shape[1:], y_buf.dtype)

        def fill(block, carry):
            copy = store(block, 0)
            copy.start()
            copy.wait()
            return carry

        lax.fori_loop(first + count, y_hbm.shape[0] // bm, fill, 0)


def _experts(first_block, block_count, valid, xs, wg, wu, wd, layer):
    r, width = xs.shape
    bm = MOE_BLOCK
    n_experts, d_model, d_exp = wg.shape[-3:]
    pick = lambda e, fb, bc, va: (layer, e, 0, 0)
    return pl.pallas_call(
        _expert_kernel,
        grid_spec=pltpu.PrefetchScalarGridSpec(
            num_scalar_prefetch=3,
            grid=(n_experts,),
            in_specs=[
                pl.BlockSpec(memory_space=pl.ANY),
                pl.BlockSpec((None, None, d_model, d_exp), pick),
                pl.BlockSpec((None, None, d_model, d_exp), pick),
                pl.BlockSpec((None, None, d_exp, d_model), pick),
            ],
            out_specs=pl.BlockSpec(memory_space=pl.ANY),
            scratch_shapes=[
                pltpu.VMEM((2, bm, width), xs.dtype),
                pltpu.VMEM((2, bm, width), xs.dtype),
                pltpu.VMEM((d_model, d_exp), BF16),
                pltpu.VMEM((d_model, d_exp), BF16),
                pltpu.VMEM((d_exp, d_model), BF16),
                pltpu.SemaphoreType.DMA((2,)),
                pltpu.SemaphoreType.DMA((2,)),
            ],
        ),
        out_shape=jax.ShapeDtypeStruct((r, width), jnp.uint32),
        compiler_params=_cparams(("arbitrary",)),
        name="moe_experts",
    )(first_block, block_count, valid, xs, wg, wu, wd)


def _sc_gather_rows(table, idx):
    m = idx.shape[0]
    width = table.shape[1]
    window = SC_GATHER_WINDOW
    mesh = plsc.VectorSubcoreMesh(core_axis_name="core", subcore_axis_name="subcore")
    workers = mesh.num_cores * mesh.num_subcores
    per_worker = m // workers
    assert m % (workers * window) == 0

    steps = per_worker // window
    assert steps % 2 == 0
    index_buf = pltpu.VMEM((window,), jnp.int32)
    row_buf = pltpu.VMEM((window, width), table.dtype)
    dma = pltpu.SemaphoreType.DMA

    @functools.partial(
        pl.kernel, out_type=jax.ShapeDtypeStruct((m, width), table.dtype), mesh=mesh,
        scratch_types=[index_buf, index_buf, row_buf, row_buf, dma, dma])
    def gather(table_hbm, idx_hbm, out_hbm, idx_a, idx_b, rows_a, rows_b, sem_a, sem_b):
        worker = lax.axis_index("subcore") * mesh.num_cores + lax.axis_index("core")
        base = worker * per_worker
        slots = ((idx_a, rows_a, sem_a), (idx_b, rows_b, sem_b))

        def fetch(step, slot):
            idx, rows, sem = slots[slot]
            off = pl.multiple_of(base + step * window, window)
            pltpu.sync_copy(idx_hbm.at[pl.ds(off, window)], idx)
            pltpu.async_copy(table_hbm.at[idx], rows, sem)

        def flush(step, slot):
            idx, rows, sem = slots[slot]
            off = pl.multiple_of(base + step * window, window)
            pltpu.make_async_copy(table_hbm.at[idx], rows, sem).wait()
            pltpu.sync_copy(rows, out_hbm.at[pl.ds(off, window)])

        fetch(0, 0)

        @pl.loop(0, steps, step=2)
        def _(step):
            fetch(step + 1, 1)
            flush(step, 0)

            @pl.when(step + 2 < steps)
            def _():
                fetch(step + 2, 0)

            flush(step + 1, 1)

    return gather(table, idx)


def _final_kernel(h_ref, gate_ref, gain_ref, y0_ref, y1_ref, o_ref):
    o_ref[...] = _rms(_moe_sum(h_ref[...], gate_ref[...], y0_ref[...], y1_ref[...]), gain_ref[...])


def _final(h, route, gain, picked):
    n, d_model = h.shape
    tc = MOVE_TILE
    steps = n // tc
    return pl.pallas_call(
        _final_kernel,
        grid=(steps,),
        in_specs=[
            pl.BlockSpec((tc, d_model), lambda i: (i, 0)),
            pl.BlockSpec((ROUTE_WIDTH, tc), lambda i: (0, i)),
            pl.BlockSpec((1, d_model), lambda i: (0, 0)),
            pl.BlockSpec((tc, d_model // 2), lambda i: (i, 0)),
            pl.BlockSpec((tc, d_model // 2), lambda i: (i + steps, 0)),
        ],
        out_specs=pl.BlockSpec((tc, d_model), lambda i: (i, 0)),
        out_shape=jax.ShapeDtypeStruct((n, d_model), F32),
        compiler_params=_cparams(("arbitrary",)),
        name="moe_final",
    )(h, route, gain, picked, picked)


def _routing_tables(route, cnt, n_rows):
    bm = MOE_BLOCK
    expert = route[0:2].astype(jnp.int32)
    rank = route[4:6].astype(jnp.int32)
    counts = cnt[N_GROUPS:N_GROUPS + N_EXPERTS, 0].astype(jnp.int32)
    padded = (counts + bm - 1) // bm * bm
    pad_ends = jnp.cumsum(padded)
    pad_starts = pad_ends - padded
    ids = jnp.arange(N_EXPERTS, dtype=jnp.int32)
    start_of = jnp.sum(jnp.where(expert[..., None] == ids, pad_starts, 0), axis=-1)
    dest = (start_of + rank).reshape(-1).astype(jnp.int32)
    block_row = jnp.arange(n_rows // bm, dtype=jnp.int32) * bm
    block_e = jnp.minimum(jnp.sum((pad_ends[None, :] <= block_row[:, None]).astype(jnp.int32), axis=-1),
                          N_EXPERTS - 1)
    row_end = jnp.sum(jnp.where(block_e[:, None] == ids, pad_starts + counts, 0), axis=-1)
    valid = jnp.clip(row_end - block_row, 0, bm).astype(jnp.int32)
    return dest, (pad_starts // bm).astype(jnp.int32), (padded // bm).astype(jnp.int32), valid


def kernel(x, mem, mem_norm, mix_norm, w_in, b_forget, w_alpha_up, b_alpha, fox_out_gain, gla_out_gain, w_out,
           cross_norm, w_xq, w_xk, w_xv, w_xo, moe_norm, w_router_group, b_router_group, w_router_expert,
           b_router_expert, w_expert_gate, w_expert_up, w_expert_down, final_norm):
    batch, seq, d_model = x.shape
    mem_len = mem.shape[1]
    depth = w_in.shape[0]
    n = batch * seq
    assert seq % FOX_TILE == 0 and seq % IN_TILE == 0 and seq % POST_TILE == 0 and seq % GLA_CHUNK == 0
    assert n % MOVE_TILE == 0 and d_model % LANES == 0

    c0 = 3 * FOX_WIDTH
    c1 = c0 + FOX_HEADS
    c2 = c1 + 2 * GLA_QK + 2 * GLA_V
    w_main = jnp.concatenate([w_in[:, :, :c0], w_in[:, :, c1:c2]], axis=-1).astype(BF16)
    pad = LANES - FOX_HEADS - GLA_RANK
    w_small = jnp.concatenate([w_in[:, :, c0:c1], w_in[:, :, c2:], jnp.zeros((depth, d_model, pad), F32)],
                              axis=-1).astype(BF16)
    w_up = jnp.concatenate([jnp.zeros((depth, FOX_HEADS, GLA_QK), F32), w_alpha_up,
                            jnp.zeros((depth, pad, GLA_QK), F32)], axis=1).astype(BF16)
    b_f = jnp.pad(b_forget, ((0, 0), (0, LANES - FOX_HEADS)))[:, None, :]
    b_a = b_alpha[:, None, :]
    w_r = jnp.concatenate([w_router_group, w_router_expert,
                           jnp.zeros((depth, d_model, LANES - N_GROUPS - N_EXPERTS), F32)], axis=-1)
    w_rh = w_r.astype(BF16)
    w_rs = jnp.concatenate([w_rh, (w_r - w_rh.astype(F32)).astype(BF16)], axis=-1)
    b_r = jnp.pad(jnp.concatenate([b_router_group, b_router_expert], axis=-1),
                  ((0, 0), (0, LANES - N_GROUPS - N_EXPERTS)))[:, None, :]
    w_out_b = w_out.astype(BF16)
    w_xq_b = w_xq.astype(BF16)
    w_xo_b = w_xo.astype(BF16)
    mix_g = mix_norm[:, None, :]
    cross_g = cross_norm[:, None, :]
    moe_g = moe_norm[:, None, :]
    gla_g = gla_out_gain[:, None, :]

    kmem, vmem = _mem_kv(mem.reshape(batch * mem_len, d_model), mem_norm[None, :],
                         w_xk.astype(BF16), w_xv.astype(BF16), batch, mem_len)

    n_rows = 2 * n + N_EXPERTS * MOE_BLOCK
    h = x.reshape(n, d_model)
    moe = None
    for l in range(depth):
        if moe is None:
            main, c, kl, dec = _in_proj(h, mix_g, w_main, w_small, w_up, b_f, b_a, l, seq)
        else:
            h, main, c, kl, dec = _in_proj(h, mix_g, w_main, w_small, w_up, b_f, b_a, l, seq, moe)
        fox = _fox_attention(main, c, fox_out_gain[l][None, :], batch, seq)
        gla = _gla(main, kl, dec, batch, seq)
        h2, hn2, route, cnt = _post(fox, gla, main, h, gla_g, w_out_b, cross_g, w_xq_b, kmem, vmem, w_xo_b, moe_g,
                                    w_rs, b_r, seq, mem_len, l)
        dest, first_block, block_count, valid = _routing_tables(route, cnt, n_rows)
        xs = _dispatch(dest, hn2, n_rows)
        y = _experts(first_block, block_count, valid, xs, w_expert_gate, w_expert_up, w_expert_down, l)
        h, moe = h2, (route, _sc_gather_rows(y, dest))
    return _final(h, moe[0], final_norm[None, :], moe[1]).reshape(batch, seq, d_model)
```

```python
import functools

import jax
import jax.numpy as jnp
from jax import lax
from jax.experimental import pallas as pl
from jax.experimental.pallas import tpu as pltpu
from jax.experimental.pallas import tpu_sc as plsc

F32 = jnp.float32
BF16 = jnp.bfloat16
EPS = 1e-6
LOG2E = 1.4426950408889634

FOX_HEADS = 8
FOX_DIM = 64
FOX_WIDTH = FOX_HEADS * FOX_DIM
GLA_HEADS = 4
GLA_DK = 64
GLA_DV = 128
GLA_QK = GLA_HEADS * GLA_DK
GLA_V = GLA_HEADS * GLA_DV
GLA_RANK = 16
GLA_TAU = 16.0
GLA_CHUNK = 64
X_HEADS = 4
X_DIM = 128
X_WIDTH = X_HEADS * X_DIM
N_GROUPS = 4
GROUP_SIZE = 4
N_EXPERTS = N_GROUPS * GROUP_SIZE
MAIN_WIDTH = 3 * FOX_WIDTH + 2 * GLA_QK + 2 * GLA_V

LANES = 128
ROUTE_WIDTH = 8
ROUTE_ROWS = 32
VMEM_LIMIT = 56 * 1024 * 1024

IN_TILE = 1024
FOX_TILE = 512
FOX_SLAB = 64
POST_TILE = 1024
MOE_BLOCK = 512
MOVE_TILE = 1024
SC_GATHER_WINDOW = 64


def _cparams(sem):
    return pltpu.CompilerParams(dimension_semantics=sem, vmem_limit_bytes=VMEM_LIMIT)


def _rms(x, gain):
    return x * lax.rsqrt(jnp.mean(x * x, axis=-1, keepdims=True) + EPS) * gain


def _log_sigmoid(x):
    return jnp.minimum(x, 0.0) - jnp.log1p(jnp.exp(-jnp.abs(x)))


def _dot(a, b):
    return jnp.dot(a, b, preferred_element_type=F32)


def _dot_nt(a, b):
    return lax.dot_general(a, b, (((1,), (1,)), ((), ())), preferred_element_type=F32)


def _pack_rows(x):
    half = x.shape[1] // 2
    lo = lax.bitcast_convert_type(x[:, :half].astype(BF16).astype(F32), jnp.uint32)
    hi = lax.bitcast_convert_type(x[:, half:].astype(BF16).astype(F32), jnp.uint32)
    return (lo >> 16) | hi


def _unpack_rows(w):
    lo = lax.bitcast_convert_type(w << 16, F32)
    hi = lax.bitcast_convert_type(w & jnp.uint32(0xFFFF0000), F32)
    return lo, hi


def _mem_kv_kernel(mem_ref, gain_ref, wk_ref, wv_ref, k_ref, v_ref):
    mn = _rms(mem_ref[...], gain_ref[...]).astype(BF16)
    for l in range(wk_ref.shape[0]):
        k_ref[l] = _dot(mn, wk_ref[l]).astype(BF16)
        v_ref[l] = _dot(mn, wv_ref[l]).astype(BF16)


def _mem_kv(mem2d, gain, wk, wv, batch, mem_len):
    depth, d_model, width = wk.shape
    out = jax.ShapeDtypeStruct((depth, batch * mem_len, width), BF16)
    return pl.pallas_call(
        _mem_kv_kernel,
        grid=(batch,),
        in_specs=[
            pl.BlockSpec((mem_len, d_model), lambda b: (b, 0)),
            pl.BlockSpec((1, d_model), lambda b: (0, 0)),
            pl.BlockSpec((depth, d_model, width), lambda b: (0, 0, 0)),
            pl.BlockSpec((depth, d_model, width), lambda b: (0, 0, 0)),
        ],
        out_specs=[
            pl.BlockSpec((depth, mem_len, width), lambda b: (0, b, 0)),
            pl.BlockSpec((depth, mem_len, width), lambda b: (0, b, 0)),
        ],
        out_shape=[out, out],
        compiler_params=_cparams(("arbitrary",)),
        name="mem_kv",
    )(mem2d, gain, wk, wv)


def _moe_sum(h, route, y0_packed, y1_packed):
    gate = jnp.transpose(route)
    y0 = jnp.concatenate(_unpack_rows(y0_packed), axis=1)
    y1 = jnp.concatenate(_unpack_rows(y1_packed), axis=1)
    return h + gate[:, 2:3] * y0 + gate[:, 3:4] * y1


def _in_proj_body(h, gain_ref, wmain_ref, wsmall_ref, wup_ref, bf_ref, ba_ref, main_ref, c_ref, kl_ref, dec_ref,
                  carry_ref, tiles_per_seq):
    tm = h.shape[0]
    cs = GLA_CHUNK
    xn = _rms(h, gain_ref[...]).astype(BF16)
    small = _dot(xn, wsmall_ref[...])
    lane = lax.broadcasted_iota(jnp.int32, small.shape, 1)
    c = jnp.transpose(_log_sigmoid(small + bf_ref[...]))[:FOX_HEADS, :]
    pos = lax.broadcasted_iota(jnp.int32, c.shape, 1)
    shift = 1
    while shift < tm:
        c = c + jnp.where(pos >= shift, pltpu.roll(c, shift, axis=1), 0.0)
        shift *= 2
    starts_sequence = pl.program_id(0) % tiles_per_seq == 0
    c = c + jnp.where(starts_sequence, 0.0, carry_ref[:, 0:1])
    carry_ref[...] = jnp.broadcast_to(c[:, tm - 1:tm], carry_ref.shape)
    c_ref[...] = c
    a = _dot(small.astype(BF16), wup_ref[...]) + ba_ref[...]
    b = _log_sigmoid(a) * (1.0 / GLA_TAU)
    pos = lax.broadcasted_iota(jnp.int32, b.shape, 0) % cs
    shift = 1
    while shift < cs:
        b = b + jnp.where(pos >= shift, pltpu.roll(b, shift, axis=0), 0.0)
        shift *= 2
    dec = jnp.exp(b.reshape(tm // cs, cs, GLA_QK)[:, cs - 1:cs, :])
    q0 = 3 * FOX_WIDTH
    k0 = q0 + GLA_QK
    qk = _dot(xn, wmain_ref[:, q0:k0 + GLA_QK])
    main_ref[:, q0:k0] = (qk[:, :GLA_QK] * jnp.exp(b) * (GLA_DK ** -0.5)).astype(BF16)
    ke = qk[:, GLA_QK:] * jnp.exp(-b)
    main_ref[:, k0:k0 + GLA_QK] = ke.astype(BF16)
    kl_ref[...] = (ke.reshape(tm // cs, cs, GLA_QK) * dec).reshape(tm, GLA_QK).astype(BF16)
    dec_ref[...] = dec.reshape(tm // cs, GLA_QK)
    step = 512
    for lo in list(range(0, q0, step)) + list(range(k0 + GLA_QK, MAIN_WIDTH, step)):
        main_ref[:, lo:lo + step] = _dot(xn, wmain_ref[:, lo:lo + step]).astype(BF16)


def _in_proj_kernel(h_ref, *refs, tiles_per_seq):
    _in_proj_body(h_ref[...], *refs, tiles_per_seq)


def _in_proj_after_moe_kernel(h_ref, gate_ref, y0_ref, y1_ref, gain_ref, wmain_ref, wsmall_ref, wup_ref, bf_ref,
                              ba_ref, hout_ref, main_ref, c_ref, kl_ref, dec_ref, carry_ref, *, tiles_per_seq):
    h = _moe_sum(h_ref[...], gate_ref[...], y0_ref[...], y1_ref[...])
    hout_ref[...] = h
    _in_proj_body(h, gain_ref, wmain_ref, wsmall_ref, wup_ref, bf_ref, ba_ref, main_ref, c_ref, kl_ref, dec_ref,
                  carry_ref, tiles_per_seq)


def _in_proj(h, gain, wmain, wsmall, wup, bf, ba, layer, seq, moe=None):
    n, d_model = h.shape
    tm = IN_TILE
    steps = n // tm
    chunks = tm // GLA_CHUNK
    tiles_per_seq = seq // tm
    pick = lambda i: (layer, 0, 0)
    row_block = lambda width, rows=tm: pl.BlockSpec((rows, width), lambda i: (i, 0))
    weight_specs = [
        pl.BlockSpec((None, 1, d_model), pick),
        pl.BlockSpec((None, d_model, MAIN_WIDTH), pick),
        pl.BlockSpec((None, d_model, LANES), pick),
        pl.BlockSpec((None, LANES, GLA_QK), pick),
        pl.BlockSpec((None, 1, LANES), pick),
        pl.BlockSpec((None, 1, GLA_QK), pick),
    ]
    out_specs = [row_block(MAIN_WIDTH), pl.BlockSpec((FOX_HEADS, tm), lambda i: (0, i)), row_block(GLA_QK),
                 row_block(GLA_QK, chunks)]
    out_shape = [
        jax.ShapeDtypeStruct((n, MAIN_WIDTH), BF16),
        jax.ShapeDtypeStruct((FOX_HEADS, n), F32),
        jax.ShapeDtypeStruct((n, GLA_QK), BF16),
        jax.ShapeDtypeStruct((n // GLA_CHUNK, GLA_QK), F32),
    ]
    weights = (gain, wmain, wsmall, wup, bf, ba)
    carry = [pltpu.VMEM((FOX_HEADS, LANES), F32)]
    if moe is None:
        return pl.pallas_call(
            functools.partial(_in_proj_kernel, tiles_per_seq=tiles_per_seq),
            grid=(steps,), in_specs=[row_block(d_model)] + weight_specs,
            out_specs=out_specs, out_shape=out_shape, scratch_shapes=carry,
            compiler_params=_cparams(("arbitrary",)), name="in_proj",
        )(h, *weights)
    route, picked = moe
    half = d_model // 2
    return pl.pallas_call(
        functools.partial(_in_proj_after_moe_kernel, tiles_per_seq=tiles_per_seq), grid=(steps,),
        in_specs=[row_block(d_model), pl.BlockSpec((ROUTE_WIDTH, tm), lambda i: (0, i)), row_block(half),
                  pl.BlockSpec((tm, half), lambda i: (i + steps, 0))] + weight_specs,
        out_specs=[row_block(d_model)] + out_specs,
        out_shape=[jax.ShapeDtypeStruct((n, d_model), F32)] + out_shape, scratch_shapes=carry,
        compiler_params=_cparams(("arbitrary",)), name="in_proj_after_moe",
    )(h, route, picked, picked, *weights)


def _mixer_kernel(q_ref, k_ref, v_ref, c_ref, gain_ref, gq_ref, gk_ref, gv_ref, gkl_ref, gdec_ref, o_ref, go_ref,
                  q2_ref, s_ref, p_ref, alpha_ref, m_ref, l_ref, acc_ref, st_ref):
    tq = FOX_TILE
    rows = 2 * tq
    slab = FOX_SLAB
    nq = q_ref.shape[0] // tq
    lane = lax.broadcasted_iota(jnp.int32, (1, LANES), 1)
    first = lane < FOX_DIM
    scale = FOX_DIM ** -0.5 * LOG2E
    for qi in range(nq):
        q = q_ref[qi * tq:(qi + 1) * tq, :].astype(F32) * scale
        q2_ref[qi, :tq, :] = jnp.where(first, q, 0.0).astype(BF16)
        q2_ref[qi, tq:, :] = jnp.where(first, 0.0, q).astype(BF16)

    head_row = lax.broadcasted_iota(jnp.int32, (FOX_HEADS, tq), 0)
    pair = pl.program_id(1)

    def scores(qi, j):
        cj = c_ref[:, j * tq:(j + 1) * tq] * LOG2E
        c0 = jnp.sum(jnp.where(head_row == 2 * pair, cj, 0.0), axis=0, keepdims=True)
        c1 = jnp.sum(jnp.where(head_row == 2 * pair + 1, cj, 0.0), axis=0, keepdims=True)
        d = _dot_nt(q2_ref[qi], k_ref[j * tq:(j + 1) * tq, :])
        s_ref[:tq, :] = d[:tq] - c0
        s_ref[tq:, :] = d[tq:] - c1

    def weighted_values(qi, j):
        par = qi % 2
        acc_ref[par] = alpha_ref[par] * acc_ref[par] + _dot(p_ref[...], v_ref[j * tq:(j + 1) * tq, :])

    def softmax(qi, masked):
        par = qi % 2
        for r in range(rows // slab):
            sl = slice(r * slab, (r + 1) * slab)
            s = s_ref[sl, :]
            if masked:
                row = lax.broadcasted_iota(jnp.int32, (slab, tq), 0) + (r * slab) % tq
                col = lax.broadcasted_iota(jnp.int32, (slab, tq), 1)
                s = jnp.where(row >= col, s, -jnp.inf)
            m_old = m_ref[par, sl, :]
            m_new = jnp.maximum(m_old, jnp.max(s, axis=-1, keepdims=True))
            alpha = jnp.exp2(m_old - m_new)
            p = jnp.exp2(s - jnp.concatenate([m_new] * (tq // LANES), axis=1))
            l_ref[par, sl, :] = alpha * l_ref[par, sl, :] + jnp.sum(p, axis=-1, keepdims=True)
            m_ref[par, sl, :] = m_new
            alpha_ref[par, sl, :] = alpha
            p_ref[sl, :] = p.astype(BF16)

    def finalize(qi):
        par = qi % 2
        o2 = acc_ref[par] / l_ref[par]
        o = jnp.where(first, o2[:tq], o2[tq:])
        sq = o * o
        ss0 = jnp.sum(jnp.where(first, sq, 0.0), axis=-1, keepdims=True)
        ss1 = jnp.sum(jnp.where(first, 0.0, sq), axis=-1, keepdims=True)
        ms = jnp.where(first, ss0, ss1) * (1.0 / FOX_DIM)
        o_ref[qi * tq:(qi + 1) * tq, :] = (o * lax.rsqrt(ms + EPS) * gain_ref[...]).astype(BF16)

    gla_refs = (gq_ref, gk_ref, gv_ref, gkl_ref, gdec_ref, go_ref, st_ref)
    gla_chunks = q_ref.shape[0] // GLA_CHUNK // 2
    gla_groups = 2
    per_group = gla_chunks // gla_groups
    gla_base = (pair % 2) * gla_chunks

    steps = [(qi, j) for qi in range(nq) for j in range(qi + 1)]
    gla_after = [g * len(steps) // gla_groups for g in range(gla_groups)]
    scores(*steps[0])
    for t, (qi, j) in enumerate(steps):
        if t > 0:
            weighted_values(*steps[t - 1])
            if steps[t - 1][0] != qi:
                finalize(steps[t - 1][0])
        if j == 0:
            par = qi % 2
            m_ref[par] = jnp.full(m_ref.shape[1:], -jnp.inf, F32)
            l_ref[par] = jnp.zeros(l_ref.shape[1:], F32)
            acc_ref[par] = jnp.zeros(acc_ref.shape[1:], F32)
        softmax(qi, masked=(j == qi))
        if t + 1 < len(steps):
            scores(*steps[t + 1])
        for g in range(gla_groups):
            if gla_after[g] == t:
                fresh = (pair % 2 == 0) if g == 0 else None
                _gla_group(*gla_refs, gla_base + g * per_group, per_group, fresh)
    weighted_values(*steps[-1])
    finalize(steps[-1][0])


def _mixer(main, c, gain, kl, dec, batch, seq):
    n = main.shape[0]
    tq = FOX_TILE
    nq = seq // tq
    pairs = FOX_HEADS // 2
    assert pairs == 2 * (GLA_HEADS // 2) and (seq // GLA_CHUNK) % 8 == 0
    k_off = FOX_WIDTH // LANES
    v_off = 2 * FOX_WIDTH // LANES
    gq_off = 3 * FOX_WIDTH // LANES
    gk_off = gq_off + GLA_QK // LANES
    pv = 2 * GLA_DV
    gv_off = (3 * FOX_WIDTH + 2 * GLA_QK) // pv
    stat = pltpu.VMEM((2, 2 * tq, LANES), F32)
    return pl.pallas_call(
        _mixer_kernel,
        grid=(batch, pairs),
        in_specs=[
            pl.BlockSpec((seq, LANES), lambda b, p: (b, p)),
            pl.BlockSpec((seq, LANES), lambda b, p: (b, k_off + p)),
            pl.BlockSpec((seq, LANES), lambda b, p: (b, v_off + p)),
            pl.BlockSpec((FOX_HEADS, seq), lambda b, p: (0, b)),
            pl.BlockSpec((1, LANES), lambda b, p: (0, p)),
            pl.BlockSpec((seq, LANES), lambda b, p: (b, gq_off + p // 2)),
            pl.BlockSpec((seq, LANES), lambda b, p: (b, gk_off + p // 2)),
            pl.BlockSpec((seq, pv), lambda b, p: (b, gv_off + p // 2)),
            pl.BlockSpec((seq, LANES), lambda b, p: (b, p // 2)),
            pl.BlockSpec((seq // GLA_CHUNK, LANES), lambda b, p: (b, p // 2)),
        ],
        out_specs=[
            pl.BlockSpec((seq, LANES), lambda b, p: (b, p)),
            pl.BlockSpec((seq, pv), lambda b, p: (b, p // 2)),
        ],
        out_shape=[
            jax.ShapeDtypeStruct((n, FOX_WIDTH), BF16),
            jax.ShapeDtypeStruct((n, GLA_V), BF16),
        ],
        scratch_shapes=[
            pltpu.VMEM((nq, 2 * tq, LANES), BF16),
            pltpu.VMEM((2 * tq, tq), F32),
            pltpu.VMEM((2 * tq, tq), BF16),
            stat, stat, stat, stat,
            pltpu.VMEM((2 * GLA_DV, 2 * GLA_DK), F32),
        ],
        compiler_params=_cparams(("arbitrary", "arbitrary")),
        name="mixer",
    )(main, main, main, c, gain, main, main, main, kl, dec)


def _gla_group(qe_ref, ke_ref, v_ref, kl_ref, dec_ref, o_ref, st_ref, chunk0, count, fresh):
    cs = GLA_CHUNK
    width = 2 * GLA_DK
    lane = lax.broadcasted_iota(jnp.int32, (1, width), 1)
    first = lane < GLA_DK
    row = lax.broadcasted_iota(jnp.int32, (2 * cs, cs), 0)
    col = lax.broadcasted_iota(jnp.int32, (2 * cs, cs), 1)
    tril2 = jnp.where(row >= cs, row - cs, row) >= col
    srow = lax.broadcasted_iota(jnp.int32, (2 * GLA_DV, width), 0)
    scol = lax.broadcasted_iota(jnp.int32, (2 * GLA_DV, width), 1)
    same_head = (srow >= GLA_DV) == (scol >= GLA_DK)

    st = st_ref[...]
    if fresh is not None:
        st = jnp.where(fresh, 0.0, st)
    r0s = [pl.multiple_of((chunk0 + u) * cs, cs) for u in range(count)]
    qes = [qe_ref[pl.ds(r0, cs), :] for r0 in r0s]
    vs = [v_ref[pl.ds(r0, cs), :] for r0 in r0s]
    atts, upds = [], []
    for u in range(count):
        zero = jnp.zeros_like(qes[u])
        q2 = jnp.concatenate([jnp.where(first, qes[u], zero), jnp.where(first, zero, qes[u])], axis=0)
        atts.append(jnp.where(tril2, _dot_nt(q2, ke_ref[pl.ds(r0s[u], cs), :]), 0.0).astype(BF16))
    for u in range(count):
        upds.append(lax.dot_general(vs[u], kl_ref[pl.ds(r0s[u], cs), :], (((0,), (0,)), ((), ())),
                                    preferred_element_type=F32))
    ois = [_dot(atts[u], vs[u]) for u in range(count)]
    for u in range(count):
        o = _dot_nt(qes[u], st.astype(BF16))
        o = o + jnp.concatenate([ois[u][:cs, :GLA_DV], ois[u][cs:, GLA_DV:]], axis=1)
        o_ref[pl.ds(r0s[u], cs), :] = o.astype(BF16)
        st = st * dec_ref[pl.ds(chunk0 + u, 1), :] + jnp.where(same_head, upds[u], 0.0)
    st_ref[...] = st


def _post_kernel(fox_ref, gla_ref, gg_ref, h_ref, gg_gain_ref, wout_ref, cg_ref, wxq_ref, k_ref, v_ref, wxo_ref,
                 mg_ref, wr_ref, br_ref, h2_ref, hn_ref, route_ref, cnt_ref, carry_ref):
    tm = h_ref.shape[0]

    @pl.when(pl.program_id(0) == 0)
    def _():
        carry_ref[...] = jnp.zeros_like(carry_ref)

    raw = gla_ref[...].astype(F32)
    normed = []
    for hh in range(GLA_HEADS):
        oh = raw[:, hh * GLA_DV:(hh + 1) * GLA_DV]
        normed.append(oh * lax.rsqrt(jnp.mean(oh * oh, axis=-1, keepdims=True) + EPS))
    g = gg_ref[...].astype(F32)
    gla = (jnp.concatenate(normed, axis=1) * gg_gain_ref[...] * (g * jax.nn.sigmoid(g))).astype(BF16)
    y = _dot(fox_ref[...], wout_ref[0:FOX_WIDTH, :]) + _dot(gla, wout_ref[FOX_WIDTH:, :])
    h1 = h_ref[...] + y
    hn = _rms(h1, cg_ref[...]).astype(BF16)
    q = _dot(hn, wxq_ref[...]).astype(BF16)
    xscale = X_DIM ** -0.5
    heads = []
    for hh in range(X_HEADS):
        sl = slice(hh * X_DIM, (hh + 1) * X_DIM)
        s = _dot_nt(q[:, sl], k_ref[:, sl]) * xscale
        p = jnp.exp(s - jnp.max(s, axis=-1, keepdims=True))
        heads.append(_dot(p.astype(BF16), v_ref[:, sl]) / jnp.sum(p, axis=-1, keepdims=True))
    o = jnp.concatenate(heads, axis=1).astype(BF16)
    h2 = h1 + _dot(o, wxo_ref[...])
    h2_ref[...] = h2
    hn2 = _rms(h2, mg_ref[...])
    hn_ref[...] = _pack_rows(hn2)

    xh = hn2.astype(BF16)
    xl = (hn2 - xh.astype(F32)).astype(BF16)
    both_w = _dot(jnp.concatenate([xh, xl], axis=0), wr_ref[...])
    logits = both_w[:tm, :LANES] + both_w[:tm, LANES:] + both_w[tm:, :LANES] + both_w[tm:, LANES:] + br_ref[...]
    lt = jnp.transpose(logits)[:ROUTE_ROWS, :]
    row = lax.broadcasted_iota(jnp.int32, (ROUTE_ROWS, tm), 0)
    neg = -jnp.inf
    gl = jnp.where(row < N_GROUPS, lt, neg)
    gmax = jnp.max(gl, axis=0, keepdims=True)
    ge = jnp.exp(gl - gmax)
    gprob = ge / jnp.sum(ge, axis=0, keepdims=True)
    pmax = jnp.max(gprob, axis=0, keepdims=True)
    grp = jnp.min(jnp.where(gprob == pmax, row, ROUTE_ROWS), axis=0, keepdims=True)
    in_grp = (row >= N_GROUPS) & (row < N_GROUPS + N_EXPERTS) & (((row - N_GROUPS) // GROUP_SIZE) == grp)
    el = jnp.where(in_grp, lt, neg)
    emax = jnp.max(el, axis=0, keepdims=True)
    ee = jnp.exp(el - emax)
    eprob = ee / jnp.sum(ee, axis=0, keepdims=True)
    p1 = jnp.max(eprob, axis=0, keepdims=True)
    row1 = jnp.min(jnp.where(in_grp & (eprob == p1), row, ROUTE_ROWS), axis=0, keepdims=True)
    rest = jnp.where(in_grp & (row != row1), eprob, -1.0)
    p2 = jnp.max(rest, axis=0, keepdims=True)
    row2 = jnp.min(jnp.where(rest == p2, row, ROUTE_ROWS), axis=0, keepdims=True)
    g1 = pmax * p1 / (p1 + p2)
    g2 = pmax * p2 / (p1 + p2)

    oh1 = row == row1
    oh2 = row == row2
    both = (oh1 | oh2).astype(BF16)
    srow = lax.broadcasted_iota(jnp.int32, (tm, tm), 0)
    scol = lax.broadcasted_iota(jnp.int32, (tm, tm), 1)
    earlier = (srow < scol).astype(BF16)
    carry = carry_ref[...]
    seen = _dot(both, earlier) + jnp.concatenate([carry] * (tm // LANES), axis=1)
    rank1 = jnp.sum(jnp.where(oh1, seen, 0.0), axis=0, keepdims=True)
    rank2 = jnp.sum(jnp.where(oh2, seen, 0.0), axis=0, keepdims=True)
    carry = carry + jnp.sum(both.astype(F32), axis=1, keepdims=True)
    carry_ref[...] = carry
    cnt_ref[...] = carry

    e1 = (row1 - N_GROUPS).astype(F32)
    e2 = (row2 - N_GROUPS).astype(F32)
    zero = jnp.zeros_like(g1)
    route_ref[...] = jnp.concatenate([e1, e2, g1, g2, rank1, rank2, zero, zero], axis=0)


def _post(fox, gla, main, h, gla_gain, wout, cg, wxq, kmem, vmem, wxo, mg, wr, br, seq, mem_len, layer):
    n, d_model = h.shape
    tm = POST_TILE
    per_seq = seq // tm
    const = lambda i: (0, 0)
    pick = lambda i: (layer, 0, 0)
    return pl.pallas_call(
        _post_kernel,
        grid=(n // tm,),
        in_specs=[
            pl.BlockSpec((tm, FOX_WIDTH), lambda i: (i, 0)),
            pl.BlockSpec((tm, GLA_V), lambda i: (i, 0)),
            pl.BlockSpec((tm, GLA_V), lambda i: (i, MAIN_WIDTH // GLA_V - 1)),
            pl.BlockSpec((tm, d_model), lambda i: (i, 0)),
            pl.BlockSpec((None, 1, GLA_V), pick),
            pl.BlockSpec((None, FOX_WIDTH + GLA_V, d_model), pick),
            pl.BlockSpec((None, 1, d_model), pick),
            pl.BlockSpec((None, d_model, X_WIDTH), pick),
            pl.BlockSpec((None, mem_len, X_WIDTH), lambda i: (layer, i // per_seq, 0)),
            pl.BlockSpec((None, mem_len, X_WIDTH), lambda i: (layer, i // per_seq, 0)),
            pl.BlockSpec((None, X_WIDTH, d_model), pick),
            pl.BlockSpec((None, 1, d_model), pick),
            pl.BlockSpec((None, d_model, 2 * LANES), pick),
            pl.BlockSpec((None, 1, LANES), pick),
        ],
        out_specs=[
            pl.BlockSpec((tm, d_model), lambda i: (i, 0)),
            pl.BlockSpec((tm, d_model // 2), lambda i: (i, 0)),
            pl.BlockSpec((ROUTE_WIDTH, tm), lambda i: (0, i)),
            pl.BlockSpec((ROUTE_ROWS, LANES), const),
        ],
        out_shape=[
            jax.ShapeDtypeStruct((n, d_model), F32),
            jax.ShapeDtypeStruct((n, d_model // 2), jnp.uint32),
            jax.ShapeDtypeStruct((ROUTE_WIDTH, n), F32),
            jax.ShapeDtypeStruct((ROUTE_ROWS, LANES), F32),
        ],
        scratch_shapes=[pltpu.VMEM((ROUTE_ROWS, LANES), F32)],
        compiler_params=_cparams(("arbitrary",)),
        name="post_mixer",
    )(fox, gla, main, h, gla_gain, wout, cg, wxq, kmem, vmem, wxo, mg, wr, br)


def _dispatch(dest_kmajor, x, n_rows):
    n, width = x.shape
    window = SC_GATHER_WINDOW
    mesh = plsc.VectorSubcoreMesh(core_axis_name="core", subcore_axis_name="subcore")
    workers = mesh.num_cores * mesh.num_subcores
    per_worker = n // workers
    assert n % (workers * window) == 0

    steps = per_worker // window
    assert steps % 2 == 0
    index_buf = pltpu.VMEM((window,), jnp.int32)
    row_buf = pltpu.VMEM((window, width), x.dtype)
    dma = pltpu.SemaphoreType.DMA

    @functools.partial(
        pl.kernel, out_type=jax.ShapeDtypeStruct((n_rows, width), x.dtype), mesh=mesh,
        scratch_types=[index_buf, index_buf, index_buf, index_buf, row_buf, row_buf, dma, dma, dma])
    def scatter(x_hbm, idx_hbm, out_hbm, idx0_a, idx1_a, idx0_b, idx1_b, rows_a, rows_b, sem_a, sem_b, sem_out):
        worker = lax.axis_index("subcore") * mesh.num_cores + lax.axis_index("core")
        base = worker * per_worker
        slots = ((idx0_a, idx1_a, rows_a, sem_a), (idx0_b, idx1_b, rows_b, sem_b))

        def load(step, slot):
            idx0, idx1, rows, sem = slots[slot]
            off = pl.multiple_of(base + step * window, window)
            pltpu.sync_copy(idx_hbm.at[pl.ds(off, window)], idx0)
            pltpu.sync_copy(idx_hbm.at[pl.ds(n + off, window)], idx1)
            pltpu.async_copy(x_hbm.at[pl.ds(off, window)], rows, sem)

        def store(slot):
            idx0, idx1, rows, sem = slots[slot]
            pltpu.make_async_copy(x_hbm.at[pl.ds(0, window)], rows, sem).wait()
            first = pltpu.async_copy(rows, out_hbm.at[idx0], sem_out)
            second = pltpu.async_copy(rows, out_hbm.at[idx1], sem_out)
            first.wait()
            second.wait()

        load(0, 0)

        @pl.loop(0, steps, step=2)
        def _(step):
            load(step + 1, 1)
            store(0)

            @pl.when(step + 2 < steps)
            def _():
                load(step + 2, 0)

            store(1)

    return scatter(x, dest_kmajor)


def _expert_kernel(be_ref, valid_ref, fresh_ref, x_ref, wg_ref, wu_ref, wd_ref, y_ref, wg_b, wu_b, wd_b):
    del be_ref
    i = pl.program_id(0)
    valid = valid_ref[i]

    @pl.when(fresh_ref[i] > 0)
    def _():
        wg_b[...] = wg_ref[...].astype(BF16)
        wu_b[...] = wu_ref[...].astype(BF16)
        wd_b[...] = wd_ref[...].astype(BF16)

    @pl.when(valid > 0)
    def _():
        row = lax.broadcasted_iota(jnp.int32, x_ref.shape, 0)
        lo, hi = _unpack_rows(jnp.where(row < valid, x_ref[...], jnp.uint32(0)))
        lo = lo.astype(BF16)
        hi = hi.astype(BF16)
        half = lo.shape[1]
        g = _dot(lo, wg_b[:half, :]) + _dot(hi, wg_b[half:, :])
        u = _dot(lo, wu_b[:half, :]) + _dot(hi, wu_b[half:, :])
        a = (g * jax.nn.sigmoid(g) * u).astype(BF16)
        y_ref[...] = _pack_rows(_dot(a, wd_b[...]))

    @pl.when(valid <= 0)
    def _():
        y_ref[...] = jnp.zeros_like(y_ref)


def _experts(block_e, valid, xs, wg, wu, wd, layer):
    r, width = xs.shape
    bm = MOE_BLOCK
    d_model, d_exp = wg.shape[-2:]
    fresh = jnp.concatenate([jnp.ones((1,), jnp.int32), (block_e[1:] != block_e[:-1]).astype(jnp.int32)])
    pick = lambda i, be, va, fr: (layer, be[i], 0, 0)
    return pl.pallas_call(
        _expert_kernel,
        grid_spec=pltpu.PrefetchScalarGridSpec(
            num_scalar_prefetch=3,
            grid=(r // bm,),
            in_specs=[
                pl.BlockSpec((bm, width), lambda i, be, va, fr: (i, 0)),
                pl.BlockSpec((None, None, d_model, d_exp), pick),
                pl.BlockSpec((None, None, d_model, d_exp), pick),
                pl.BlockSpec((None, None, d_exp, d_model), pick),
            ],
            out_specs=pl.BlockSpec((bm, width), lambda i, be, va, fr: (i, 0)),
            scratch_shapes=[
                pltpu.VMEM((d_model, d_exp), BF16),
                pltpu.VMEM((d_model, d_exp), BF16),
                pltpu.VMEM((d_exp, d_model), BF16),
            ],
        ),
        out_shape=jax.ShapeDtypeStruct((r, width), jnp.uint32),
        compiler_params=_cparams(("arbitrary",)),
        name="moe_experts",
    )(block_e, valid, fresh, xs, wg, wu, wd)


def _sc_gather_rows(table, idx):
    m = idx.shape[0]
    width = table.shape[1]
    window = SC_GATHER_WINDOW
    mesh = plsc.VectorSubcoreMesh(core_axis_name="core", subcore_axis_name="subcore")
    workers = mesh.num_cores * mesh.num_subcores
    per_worker = m // workers
    assert m % (workers * window) == 0

    steps = per_worker // window
    assert steps % 2 == 0
    index_buf = pltpu.VMEM((window,), jnp.int32)
    row_buf = pltpu.VMEM((window, width), table.dtype)
    dma = pltpu.SemaphoreType.DMA

    @functools.partial(
        pl.kernel, out_type=jax.ShapeDtypeStruct((m, width), table.dtype), mesh=mesh,
        scratch_types=[index_buf, index_buf, row_buf, row_buf, dma, dma])
    def gather(table_hbm, idx_hbm, out_hbm, idx_a, idx_b, rows_a, rows_b, sem_a, sem_b):
        worker = lax.axis_index("subcore") * mesh.num_cores + lax.axis_index("core")
        base = worker * per_worker
        slots = ((idx_a, rows_a, sem_a), (idx_b, rows_b, sem_b))

        def fetch(step, slot):
            idx, rows, sem = slots[slot]
            off = pl.multiple_of(base + step * window, window)
            pltpu.sync_copy(idx_hbm.at[pl.ds(off, window)], idx)
            pltpu.async_copy(table_hbm.at[idx], rows, sem)

        def flush(step, slot):
            idx, rows, sem = slots[slot]
            off = pl.multiple_of(base + step * window, window)
            pltpu.make_async_copy(table_hbm.at[idx], rows, sem).wait()
            pltpu.sync_copy(rows, out_hbm.at[pl.ds(off, window)])

        fetch(0, 0)

        @pl.loop(0, steps, step=2)
        def _(step):
            fetch(step + 1, 1)
            flush(step, 0)

            @pl.when(step + 2 < steps)
            def _():
                fetch(step + 2, 0)

            flush(step + 1, 1)

    return gather(table, idx)


def _final_kernel(h_ref, gate_ref, gain_ref, y0_ref, y1_ref, o_ref):
    o_ref[...] = _rms(_moe_sum(h_ref[...], gate_ref[...], y0_ref[...], y1_ref[...]), gain_ref[...])


def _final(h, route, gain, picked):
    n, d_model = h.shape
    tc = MOVE_TILE
    steps = n // tc
    return pl.pallas_call(
        _final_kernel,
        grid=(steps,),
        in_specs=[
            pl.BlockSpec((tc, d_model), lambda i: (i, 0)),
            pl.BlockSpec((ROUTE_WIDTH, tc), lambda i: (0, i)),
            pl.BlockSpec((1, d_model), lambda i: (0, 0)),
            pl.BlockSpec((tc, d_model // 2), lambda i: (i, 0)),
            pl.BlockSpec((tc, d_model // 2), lambda i: (i + steps, 0)),
        ],
        out_specs=pl.BlockSpec((tc, d_model), lambda i: (i, 0)),
        out_shape=jax.ShapeDtypeStruct((n, d_model), F32),
        compiler_params=_cparams(("arbitrary",)),
        name="moe_final",
    )(h, route, gain, picked, picked)


def _routing_tables(route, cnt, n_rows):
    bm = MOE_BLOCK
    expert = route[0:2].astype(jnp.int32)
    rank = route[4:6].astype(jnp.int32)
    counts = cnt[N_GROUPS:N_GROUPS + N_EXPERTS, 0].astype(jnp.int32)
    padded = (counts + bm - 1) // bm * bm
    pad_ends = jnp.cumsum(padded)
    pad_starts = pad_ends - padded
    ids = jnp.arange(N_EXPERTS, dtype=jnp.int32)
    start_of = jnp.sum(jnp.where(expert[..., None] == ids, pad_starts, 0), axis=-1)
    dest = (start_of + rank).reshape(-1).astype(jnp.int32)
    block_row = jnp.arange(n_rows // bm, dtype=jnp.int32) * bm
    block_e = jnp.minimum(jnp.sum((pad_ends[None, :] <= block_row[:, None]).astype(jnp.int32), axis=-1),
                          N_EXPERTS - 1)
    row_end = jnp.sum(jnp.where(block_e[:, None] == ids, pad_starts + counts, 0), axis=-1)
    valid = jnp.clip(row_end - block_row, 0, bm).astype(jnp.int32)
    return dest, block_e, valid


def kernel(x, mem, mem_norm, mix_norm, w_in, b_forget, w_alpha_up, b_alpha, fox_out_gain, gla_out_gain, w_out,
           cross_norm, w_xq, w_xk, w_xv, w_xo, moe_norm, w_router_group, b_router_group, w_router_expert,
           b_router_expert, w_expert_gate, w_expert_up, w_expert_down, final_norm):
    batch, seq, d_model = x.shape
    mem_len = mem.shape[1]
    depth = w_in.shape[0]
    n = batch * seq
    assert seq % FOX_TILE == 0 and seq % IN_TILE == 0 and seq % POST_TILE == 0 and seq % GLA_CHUNK == 0
    assert n % MOVE_TILE == 0 and d_model % LANES == 0

    c0 = 3 * FOX_WIDTH
    c1 = c0 + FOX_HEADS
    c2 = c1 + 2 * GLA_QK + 2 * GLA_V
    w_main = jnp.concatenate([w_in[:, :, :c0], w_in[:, :, c1:c2]], axis=-1).astype(BF16)
    pad = LANES - FOX_HEADS - GLA_RANK
    w_small = jnp.concatenate([w_in[:, :, c0:c1], w_in[:, :, c2:], jnp.zeros((depth, d_model, pad), F32)],
                              axis=-1).astype(BF16)
    w_up = jnp.concatenate([jnp.zeros((depth, FOX_HEADS, GLA_QK), F32), w_alpha_up,
                            jnp.zeros((depth, pad, GLA_QK), F32)], axis=1).astype(BF16)
    b_f = jnp.pad(b_forget, ((0, 0), (0, LANES - FOX_HEADS)))[:, None, :]
    b_a = b_alpha[:, None, :]
    w_r = jnp.concatenate([w_router_group, w_router_expert,
                           jnp.zeros((depth, d_model, LANES - N_GROUPS - N_EXPERTS), F32)], axis=-1)
    w_rh = w_r.astype(BF16)
    w_rs = jnp.concatenate([w_rh, (w_r - w_rh.astype(F32)).astype(BF16)], axis=-1)
    b_r = jnp.pad(jnp.concatenate([b_router_group, b_router_expert], axis=-1),
                  ((0, 0), (0, LANES - N_GROUPS - N_EXPERTS)))[:, None, :]
    w_out_b = w_out.astype(BF16)
    w_xq_b = w_xq.astype(BF16)
    w_xo_b = w_xo.astype(BF16)
    mix_g = mix_norm[:, None, :]
    cross_g = cross_norm[:, None, :]
    moe_g = moe_norm[:, None, :]
    gla_g = gla_out_gain[:, None, :]

    kmem, vmem = _mem_kv(mem.reshape(batch * mem_len, d_model), mem_norm[None, :],
                         w_xk.astype(BF16), w_xv.astype(BF16), batch, mem_len)

    n_rows = 2 * n + N_EXPERTS * MOE_BLOCK
    h = x.reshape(n, d_model)
    moe = None
    for l in range(depth):
        if moe is None:
            main, c, kl, dec = _in_proj(h, mix_g, w_main, w_small, w_up, b_f, b_a, l, seq)
        else:
            h, main, c, kl, dec = _in_proj(h, mix_g, w_main, w_small, w_up, b_f, b_a, l, seq, moe)
        fox, gla = _mixer(main, c, fox_out_gain[l][None, :], kl, dec, batch, seq)
        h2, hn2, route, cnt = _post(fox, gla, main, h, gla_g, w_out_b, cross_g, w_xq_b, kmem, vmem, w_xo_b, moe_g,
                                    w_rs, b_r, seq, mem_len, l)
        dest, block_e, valid = _routing_tables(route, cnt, n_rows)
        xs = _dispatch(dest, hn2, n_rows)
        y = _experts(block_e, valid, xs, w_expert_gate, w_expert_up, w_expert_down, l)
        h, moe = h2, (route, _sc_gather_rows(y, dest))
    return _final(h, moe[0], final_norm[None, :], moe[1]).reshape(batch, seq, d_model)
```

```python
import functools

import jax
import jax.numpy as jnp
from jax import lax
from jax.experimental import pallas as pl
from jax.experimental.pallas import tpu as pltpu
from jax.experimental.pallas import tpu_sc as plsc

F32 = jnp.float32
BF16 = jnp.bfloat16
EPS = 1e-6
LOG2E = 1.4426950408889634

FOX_HEADS = 8
FOX_DIM = 64
FOX_WIDTH = FOX_HEADS * FOX_DIM
GLA_HEADS = 4
GLA_DK = 64
GLA_DV = 128
GLA_QK = GLA_HEADS * GLA_DK
GLA_V = GLA_HEADS * GLA_DV
GLA_RANK = 16
GLA_TAU = 16.0
GLA_CHUNK = 64
X_HEADS = 4
X_DIM = 128
X_WIDTH = X_HEADS * X_DIM
N_GROUPS = 4
GROUP_SIZE = 4
N_EXPERTS = N_GROUPS * GROUP_SIZE
MAIN_WIDTH = 3 * FOX_WIDTH + 2 * GLA_QK + 2 * GLA_V

LANES = 128
ROUTE_WIDTH = 8
ROUTE_ROWS = 32
VMEM_LIMIT = 56 * 1024 * 1024

IN_TILE = 1024
FOX_TILE = 512
FOX_SLAB = 64
POST_TILE = 1024
MOE_BLOCK = 512
MOVE_TILE = 1024
SC_GATHER_WINDOW = 64


def _cparams(sem):
    return pltpu.CompilerParams(dimension_semantics=sem, vmem_limit_bytes=VMEM_LIMIT)


def _rms(x, gain):
    return x * lax.rsqrt(jnp.mean(x * x, axis=-1, keepdims=True) + EPS) * gain


def _log_sigmoid(x):
    return jnp.minimum(x, 0.0) - jnp.log1p(jnp.exp(-jnp.abs(x)))


def _dot(a, b):
    return jnp.dot(a, b, preferred_element_type=F32)


def _dot_nt(a, b):
    return lax.dot_general(a, b, (((1,), (1,)), ((), ())), preferred_element_type=F32)


def _pack_rows(x):
    half = x.shape[1] // 2
    lo = lax.bitcast_convert_type(x[:, :half].astype(BF16).astype(F32), jnp.uint32)
    hi = lax.bitcast_convert_type(x[:, half:].astype(BF16).astype(F32), jnp.uint32)
    return (lo >> 16) | hi


def _unpack_rows(w):
    lo = lax.bitcast_convert_type(w << 16, F32)
    hi = lax.bitcast_convert_type(w & jnp.uint32(0xFFFF0000), F32)
    return lo, hi


def _mem_kv_kernel(mem_ref, gain_ref, wk_ref, wv_ref, k_ref, v_ref):
    mn = _rms(mem_ref[...], gain_ref[...]).astype(BF16)
    for l in range(wk_ref.shape[0]):
        k_ref[l] = _dot(mn, wk_ref[l]).astype(BF16)
        v_ref[l] = _dot(mn, wv_ref[l]).astype(BF16)


def _mem_kv(mem2d, gain, wk, wv, batch, mem_len):
    depth, d_model, width = wk.shape
    out = jax.ShapeDtypeStruct((depth, batch * mem_len, width), BF16)
    return pl.pallas_call(
        _mem_kv_kernel,
        grid=(batch,),
        in_specs=[
            pl.BlockSpec((mem_len, d_model), lambda b: (b, 0)),
            pl.BlockSpec((1, d_model), lambda b: (0, 0)),
            pl.BlockSpec((depth, d_model, width), lambda b: (0, 0, 0)),
            pl.BlockSpec((depth, d_model, width), lambda b: (0, 0, 0)),
        ],
        out_specs=[
            pl.BlockSpec((depth, mem_len, width), lambda b: (0, b, 0)),
            pl.BlockSpec((depth, mem_len, width), lambda b: (0, b, 0)),
        ],
        out_shape=[out, out],
        compiler_params=_cparams(("arbitrary",)),
        name="mem_kv",
    )(mem2d, gain, wk, wv)


def _moe_sum(h, route, y0_packed, y1_packed):
    gate = jnp.transpose(route)
    y0 = jnp.concatenate(_unpack_rows(y0_packed), axis=1)
    y1 = jnp.concatenate(_unpack_rows(y1_packed), axis=1)
    return h + gate[:, 2:3] * y0 + gate[:, 3:4] * y1


def _in_proj_body(h, gain_ref, wmain_ref, wsmall_ref, wup_ref, bf_ref, ba_ref, main_ref, c_ref, kl_ref, dec_ref,
                  carry_ref, tiles_per_seq):
    tm = h.shape[0]
    cs = GLA_CHUNK
    xn = _rms(h, gain_ref[...]).astype(BF16)
    small = _dot(xn, wsmall_ref[...])
    lane = lax.broadcasted_iota(jnp.int32, small.shape, 1)
    c = jnp.transpose(_log_sigmoid(small + bf_ref[...]))[:FOX_HEADS, :]
    pos = lax.broadcasted_iota(jnp.int32, c.shape, 1)
    shift = 1
    while shift < tm:
        c = c + jnp.where(pos >= shift, pltpu.roll(c, shift, axis=1), 0.0)
        shift *= 2
    starts_sequence = pl.program_id(0) % tiles_per_seq == 0
    c = c + jnp.where(starts_sequence, 0.0, carry_ref[:, 0:1])
    carry_ref[...] = jnp.broadcast_to(c[:, tm - 1:tm], carry_ref.shape)
    c_ref[...] = c
    a = _dot(small.astype(BF16), wup_ref[...]) + ba_ref[...]
    b = _log_sigmoid(a) * (1.0 / GLA_TAU)
    pos = lax.broadcasted_iota(jnp.int32, b.shape, 0) % cs
    shift = 1
    while shift < cs:
        b = b + jnp.where(pos >= shift, pltpu.roll(b, shift, axis=0), 0.0)
        shift *= 2
    dec = jnp.exp(b.reshape(tm // cs, cs, GLA_QK)[:, cs - 1:cs, :])
    q0 = 3 * FOX_WIDTH
    k0 = q0 + GLA_QK
    qk = _dot(xn, wmain_ref[:, q0:k0 + GLA_QK])
    main_ref[:, q0:k0] = (qk[:, :GLA_QK] * jnp.exp(b) * (GLA_DK ** -0.5)).astype(BF16)
    ke = qk[:, GLA_QK:] * jnp.exp(-b)
    main_ref[:, k0:k0 + GLA_QK] = ke.astype(BF16)
    kl_ref[...] = (ke.reshape(tm // cs, cs, GLA_QK) * dec).reshape(tm, GLA_QK).astype(BF16)
    dec_ref[...] = dec.reshape(tm // cs, GLA_QK)
    step = 512
    for lo in list(range(0, q0, step)) + list(range(k0 + GLA_QK, MAIN_WIDTH, step)):
        main_ref[:, lo:lo + step] = _dot(xn, wmain_ref[:, lo:lo + step]).astype(BF16)


def _in_proj_kernel(h_ref, *refs, tiles_per_seq):
    _in_proj_body(h_ref[...], *refs, tiles_per_seq)


def _in_proj_after_moe_kernel(h_ref, gate_ref, y0_ref, y1_ref, gain_ref, wmain_ref, wsmall_ref, wup_ref, bf_ref,
                              ba_ref, hout_ref, main_ref, c_ref, kl_ref, dec_ref, carry_ref, *, tiles_per_seq):
    h = _moe_sum(h_ref[...], gate_ref[...], y0_ref[...], y1_ref[...])
    hout_ref[...] = h
    _in_proj_body(h, gain_ref, wmain_ref, wsmall_ref, wup_ref, bf_ref, ba_ref, main_ref, c_ref, kl_ref, dec_ref,
                  carry_ref, tiles_per_seq)


def _in_proj(h, gain, wmain, wsmall, wup, bf, ba, layer, seq, moe=None):
    n, d_model = h.shape
    tm = IN_TILE
    steps = n // tm
    chunks = tm // GLA_CHUNK
    tiles_per_seq = seq // tm
    pick = lambda i: (layer, 0, 0)
    row_block = lambda width, rows=tm: pl.BlockSpec((rows, width), lambda i: (i, 0))
    weight_specs = [
        pl.BlockSpec((None, 1, d_model), pick),
        pl.BlockSpec((None, d_model, MAIN_WIDTH), pick),
        pl.BlockSpec((None, d_model, LANES), pick),
        pl.BlockSpec((None, LANES, GLA_QK), pick),
        pl.BlockSpec((None, 1, LANES), pick),
        pl.BlockSpec((None, 1, GLA_QK), pick),
    ]
    out_specs = [row_block(MAIN_WIDTH), pl.BlockSpec((FOX_HEADS, tm), lambda i: (0, i)), row_block(GLA_QK),
                 row_block(GLA_QK, chunks)]
    out_shape = [
        jax.ShapeDtypeStruct((n, MAIN_WIDTH), BF16),
        jax.ShapeDtypeStruct((FOX_HEADS, n), F32),
        jax.ShapeDtypeStruct((n, GLA_QK), BF16),
        jax.ShapeDtypeStruct((n // GLA_CHUNK, GLA_QK), F32),
    ]
    weights = (gain, wmain, wsmall, wup, bf, ba)
    carry = [pltpu.VMEM((FOX_HEADS, LANES), F32)]
    if moe is None:
        return pl.pallas_call(
            functools.partial(_in_proj_kernel, tiles_per_seq=tiles_per_seq),
            grid=(steps,), in_specs=[row_block(d_model)] + weight_specs,
            out_specs=out_specs, out_shape=out_shape, scratch_shapes=carry,
            compiler_params=_cparams(("arbitrary",)), name="in_proj",
        )(h, *weights)
    route, picked = moe
    half = d_model // 2
    return pl.pallas_call(
        functools.partial(_in_proj_after_moe_kernel, tiles_per_seq=tiles_per_seq), grid=(steps,),
        in_specs=[row_block(d_model), pl.BlockSpec((ROUTE_WIDTH, tm), lambda i: (0, i)), row_block(half),
                  pl.BlockSpec((tm, half), lambda i: (i + steps, 0))] + weight_specs,
        out_specs=[row_block(d_model)] + out_specs,
        out_shape=[jax.ShapeDtypeStruct((n, d_model), F32)] + out_shape, scratch_shapes=carry,
        compiler_params=_cparams(("arbitrary",)), name="in_proj_after_moe",
    )(h, route, picked, picked, *weights)


def _mixer_kernel(q_ref, k_ref, v_ref, c_ref, gain_ref, gq_ref, gk_ref, gv_ref, gkl_ref, gdec_ref, o_ref, go_ref,
                  q2_ref, s_ref, p_ref, alpha_ref, m_ref, l_ref, acc_ref, st_ref):
    tq = FOX_TILE
    rows = 2 * tq
    slab = FOX_SLAB
    nq = q_ref.shape[0] // tq
    lane = lax.broadcasted_iota(jnp.int32, (1, LANES), 1)
    first = lane < FOX_DIM
    scale = FOX_DIM ** -0.5 * LOG2E
    for qi in range(nq):
        q = q_ref[qi * tq:(qi + 1) * tq, :].astype(F32) * scale
        q2_ref[qi, :tq, :] = jnp.where(first, q, 0.0).astype(BF16)
        q2_ref[qi, tq:, :] = jnp.where(first, 0.0, q).astype(BF16)

    head_row = lax.broadcasted_iota(jnp.int32, (FOX_HEADS, tq), 0)
    pair = pl.program_id(1)

    def scores(qi, j):
        cj = c_ref[:, j * tq:(j + 1) * tq] * LOG2E
        c0 = jnp.sum(jnp.where(head_row == 2 * pair, cj, 0.0), axis=0, keepdims=True)
        c1 = jnp.sum(jnp.where(head_row == 2 * pair + 1, cj, 0.0), axis=0, keepdims=True)
        d = _dot_nt(q2_ref[qi], k_ref[j * tq:(j + 1) * tq, :])
        s_ref[:tq, :] = d[:tq] - c0
        s_ref[tq:, :] = d[tq:] - c1

    def weighted_values(qi, j):
        par = qi % 2
        acc_ref[par] = alpha_ref[par] * acc_ref[par] + _dot(p_ref[...], v_ref[j * tq:(j + 1) * tq, :])

    def softmax(qi, masked):
        par = qi % 2
        for r in range(rows // slab):
            sl = slice(r * slab, (r + 1) * slab)
            s = s_ref[sl, :]
            if masked:
                row = lax.broadcasted_iota(jnp.int32, (slab, tq), 0) + (r * slab) % tq
                col = lax.broadcasted_iota(jnp.int32, (slab, tq), 1)
                s = jnp.where(row >= col, s, -jnp.inf)
            m_old = m_ref[par, sl, :]
            m_new = jnp.maximum(m_old, jnp.max(s, axis=-1, keepdims=True))
            alpha = jnp.exp2(m_old - m_new)
            p = jnp.exp2(s - jnp.concatenate([m_new] * (tq // LANES), axis=1))
            l_ref[par, sl, :] = alpha * l_ref[par, sl, :] + jnp.sum(p, axis=-1, keepdims=True)
            m_ref[par, sl, :] = m_new
            alpha_ref[par, sl, :] = alpha
            p_ref[sl, :] = p.astype(BF16)

    def finalize(qi):
        par = qi % 2
        o2 = acc_ref[par] / l_ref[par]
        o = jnp.where(first, o2[:tq], o2[tq:])
        sq = o * o
        ss0 = jnp.sum(jnp.where(first, sq, 0.0), axis=-1, keepdims=True)
        ss1 = jnp.sum(jnp.where(first, 0.0, sq), axis=-1, keepdims=True)
        ms = jnp.where(first, ss0, ss1) * (1.0 / FOX_DIM)
        o_ref[qi * tq:(qi + 1) * tq, :] = (o * lax.rsqrt(ms + EPS) * gain_ref[...]).astype(BF16)

    gla_refs = (gq_ref, gk_ref, gv_ref, gkl_ref, gdec_ref, go_ref, st_ref)
    gla_chunks = q_ref.shape[0] // GLA_CHUNK // 2
    gla_groups = 2
    per_group = gla_chunks // gla_groups
    gla_base = (pair % 2) * gla_chunks

    steps = [(qi, j) for qi in range(nq) for j in range(qi + 1)]
    gla_after = [g * len(steps) // gla_groups for g in range(gla_groups)]
    scores(*steps[0])
    for t, (qi, j) in enumerate(steps):
        if t > 0:
            weighted_values(*steps[t - 1])
            if steps[t - 1][0] != qi:
                finalize(steps[t - 1][0])
        if j == 0:
            par = qi % 2
            m_ref[par] = jnp.full(m_ref.shape[1:], -jnp.inf, F32)
            l_ref[par] = jnp.zeros(l_ref.shape[1:], F32)
            acc_ref[par] = jnp.zeros(acc_ref.shape[1:], F32)
        softmax(qi, masked=(j == qi))
        for g in range(gla_groups):
            if gla_after[g] == t:
                fresh = (pair % 2 == 0) if g == 0 else None
                _gla_group(*gla_refs, gla_base + g * per_group, per_group, fresh)
        if t + 1 < len(steps):
            scores(*steps[t + 1])
    weighted_values(*steps[-1])
    finalize(steps[-1][0])


def _mixer(main, c, gain, kl, dec, batch, seq):
    n = main.shape[0]
    tq = FOX_TILE
    nq = seq // tq
    pairs = FOX_HEADS // 2
    assert pairs == 2 * (GLA_HEADS // 2) and (seq // GLA_CHUNK) % 8 == 0
    k_off = FOX_WIDTH // LANES
    v_off = 2 * FOX_WIDTH // LANES
    gq_off = 3 * FOX_WIDTH // LANES
    gk_off = gq_off + GLA_QK // LANES
    pv = 2 * GLA_DV
    gv_off = (3 * FOX_WIDTH + 2 * GLA_QK) // pv
    stat = pltpu.VMEM((2, 2 * tq, LANES), F32)
    return pl.pallas_call(
        _mixer_kernel,
        grid=(batch, pairs),
        in_specs=[
            pl.BlockSpec((seq, LANES), lambda b, p: (b, p)),
            pl.BlockSpec((seq, LANES), lambda b, p: (b, k_off + p)),
            pl.BlockSpec((seq, LANES), lambda b, p: (b, v_off + p)),
            pl.BlockSpec((FOX_HEADS, seq), lambda b, p: (0, b)),
            pl.BlockSpec((1, LANES), lambda b, p: (0, p)),
            pl.BlockSpec((seq, LANES), lambda b, p: (b, gq_off + p // 2)),
            pl.BlockSpec((seq, LANES), lambda b, p: (b, gk_off + p // 2)),
            pl.BlockSpec((seq, pv), lambda b, p: (b, gv_off + p // 2)),
            pl.BlockSpec((seq, LANES), lambda b, p: (b, p // 2)),
            pl.BlockSpec((seq // GLA_CHUNK, LANES), lambda b, p: (b, p // 2)),
        ],
        out_specs=[
            pl.BlockSpec((seq, LANES), lambda b, p: (b, p)),
            pl.BlockSpec((seq, pv), lambda b, p: (b, p // 2)),
        ],
        out_shape=[
            jax.ShapeDtypeStruct((n, FOX_WIDTH), BF16),
            jax.ShapeDtypeStruct((n, GLA_V), BF16),
        ],
        scratch_shapes=[
            pltpu.VMEM((nq, 2 * tq, LANES), BF16),
            pltpu.VMEM((2 * tq, tq), F32),
            pltpu.VMEM((2 * tq, tq), BF16),
            stat, stat, stat, stat,
            pltpu.VMEM((2 * GLA_DV, 2 * GLA_DK), F32),
        ],
        compiler_params=_cparams(("arbitrary", "arbitrary")),
        name="mixer",
    )(main, main, main, c, gain, main, main, main, kl, dec)


def _gla_group(qe_ref, ke_ref, v_ref, kl_ref, dec_ref, o_ref, st_ref, chunk0, count, fresh):
    cs = GLA_CHUNK
    width = 2 * GLA_DK
    lane = lax.broadcasted_iota(jnp.int32, (1, width), 1)
    first = lane < GLA_DK
    row = lax.broadcasted_iota(jnp.int32, (2 * cs, cs), 0)
    col = lax.broadcasted_iota(jnp.int32, (2 * cs, cs), 1)
    tril2 = jnp.where(row >= cs, row - cs, row) >= col
    srow = lax.broadcasted_iota(jnp.int32, (2 * GLA_DV, width), 0)
    scol = lax.broadcasted_iota(jnp.int32, (2 * GLA_DV, width), 1)
    same_head = (srow >= GLA_DV) == (scol >= GLA_DK)

    st = st_ref[...]
    if fresh is not None:
        st = jnp.where(fresh, 0.0, st)
    r0s = [pl.multiple_of((chunk0 + u) * cs, cs) for u in range(count)]
    qes = [qe_ref[pl.ds(r0, cs), :] for r0 in r0s]
    vs = [v_ref[pl.ds(r0, cs), :] for r0 in r0s]
    atts, upds = [], []
    for u in range(count):
        zero = jnp.zeros_like(qes[u])
        q2 = jnp.concatenate([jnp.where(first, qes[u], zero), jnp.where(first, zero, qes[u])], axis=0)
        atts.append(jnp.where(tril2, _dot_nt(q2, ke_ref[pl.ds(r0s[u], cs), :]), 0.0).astype(BF16))
    for u in range(count):
        upds.append(lax.dot_general(vs[u], kl_ref[pl.ds(r0s[u], cs), :], (((0,), (0,)), ((), ())),
                                    preferred_element_type=F32))
    ois = [_dot(atts[u], vs[u]) for u in range(count)]
    for u in range(count):
        o = _dot_nt(qes[u], st.astype(BF16))
        o = o + jnp.concatenate([ois[u][:cs, :GLA_DV], ois[u][cs:, GLA_DV:]], axis=1)
        o_ref[pl.ds(r0s[u], cs), :] = o.astype(BF16)
        st = st * dec_ref[pl.ds(chunk0 + u, 1), :] + jnp.where(same_head, upds[u], 0.0)
    st_ref[...] = st


def _post_kernel(fox_ref, gla_ref, gg_ref, h_ref, gg_gain_ref, wout_ref, cg_ref, wxq_ref, k_ref, v_ref, wxo_ref,
                 mg_ref, wr_ref, br_ref, h2_ref, hn_ref, route_ref, cnt_ref, carry_ref):
    tm = h_ref.shape[0]

    @pl.when(pl.program_id(0) == 0)
    def _():
        carry_ref[...] = jnp.zeros_like(carry_ref)

    raw = gla_ref[...].astype(F32)
    normed = []
    for hh in range(GLA_HEADS):
        oh = raw[:, hh * GLA_DV:(hh + 1) * GLA_DV]
        normed.append(oh * lax.rsqrt(jnp.mean(oh * oh, axis=-1, keepdims=True) + EPS))
    g = gg_ref[...].astype(F32)
    gla = (jnp.concatenate(normed, axis=1) * gg_gain_ref[...] * (g * jax.nn.sigmoid(g))).astype(BF16)
    y = _dot(fox_ref[...], wout_ref[0:FOX_WIDTH, :]) + _dot(gla, wout_ref[FOX_WIDTH:, :])
    h1 = h_ref[...] + y
    hn = _rms(h1, cg_ref[...]).astype(BF16)
    q = _dot(hn, wxq_ref[...]).astype(BF16)
    xscale = X_DIM ** -0.5
    heads = []
    for hh in range(X_HEADS):
        sl = slice(hh * X_DIM, (hh + 1) * X_DIM)
        s = _dot_nt(q[:, sl], k_ref[:, sl]) * xscale
        p = jnp.exp(s - jnp.max(s, axis=-1, keepdims=True))
        heads.append(_dot(p.astype(BF16), v_ref[:, sl]) / jnp.sum(p, axis=-1, keepdims=True))
    o = jnp.concatenate(heads, axis=1).astype(BF16)
    h2 = h1 + _dot(o, wxo_ref[...])
    h2_ref[...] = h2
    hn2 = _rms(h2, mg_ref[...])
    hn_ref[...] = _pack_rows(hn2)

    xh = hn2.astype(BF16)
    xl = (hn2 - xh.astype(F32)).astype(BF16)
    both_w = _dot(jnp.concatenate([xh, xl], axis=0), wr_ref[...])
    logits = both_w[:tm, :LANES] + both_w[:tm, LANES:] + both_w[tm:, :LANES] + both_w[tm:, LANES:] + br_ref[...]
    lt = jnp.transpose(logits)[:ROUTE_ROWS, :]
    row = lax.broadcasted_iota(jnp.int32, (ROUTE_ROWS, tm), 0)
    neg = -jnp.inf
    gl = jnp.where(row < N_GROUPS, lt, neg)
    gmax = jnp.max(gl, axis=0, keepdims=True)
    ge = jnp.exp(gl - gmax)
    gprob = ge / jnp.sum(ge, axis=0, keepdims=True)
    pmax = jnp.max(gprob, axis=0, keepdims=True)
    grp = jnp.min(jnp.where(gprob == pmax, row, ROUTE_ROWS), axis=0, keepdims=True)
    in_grp = (row >= N_GROUPS) & (row < N_GROUPS + N_EXPERTS) & (((row - N_GROUPS) // GROUP_SIZE) == grp)
    el = jnp.where(in_grp, lt, neg)
    emax = jnp.max(el, axis=0, keepdims=True)
    ee = jnp.exp(el - emax)
    eprob = ee / jnp.sum(ee, axis=0, keepdims=True)
    p1 = jnp.max(eprob, axis=0, keepdims=True)
    row1 = jnp.min(jnp.where(in_grp & (eprob == p1), row, ROUTE_ROWS), axis=0, keepdims=True)
    rest = jnp.where(in_grp & (row != row1), eprob, -1.0)
    p2 = jnp.max(rest, axis=0, keepdims=True)
    row2 = jnp.min(jnp.where(rest == p2, row, ROUTE_ROWS), axis=0, keepdims=True)
    g1 = pmax * p1 / (p1 + p2)
    g2 = pmax * p2 / (p1 + p2)

    oh1 = row == row1
    oh2 = row == row2
    both = (oh1 | oh2).astype(BF16)
    srow = lax.broadcasted_iota(jnp.int32, (tm, tm), 0)
    scol = lax.broadcasted_iota(jnp.int32, (tm, tm), 1)
    earlier = (srow < scol).astype(BF16)
    carry = carry_ref[...]
    seen = _dot(both, earlier) + jnp.concatenate([carry] * (tm // LANES), axis=1)
    rank1 = jnp.sum(jnp.where(oh1, seen, 0.0), axis=0, keepdims=True)
    rank2 = jnp.sum(jnp.where(oh2, seen, 0.0), axis=0, keepdims=True)
    carry = carry + jnp.sum(both.astype(F32), axis=1, keepdims=True)
    carry_ref[...] = carry
    cnt_ref[...] = carry

    e1 = (row1 - N_GROUPS).astype(F32)
    e2 = (row2 - N_GROUPS).astype(F32)
    zero = jnp.zeros_like(g1)
    route_ref[...] = jnp.concatenate([e1, e2, g1, g2, rank1, rank2, zero, zero], axis=0)


def _post(fox, gla, main, h, gla_gain, wout, cg, wxq, kmem, vmem, wxo, mg, wr, br, seq, mem_len, layer):
    n, d_model = h.shape
    tm = POST_TILE
    per_seq = seq // tm
    const = lambda i: (0, 0)
    pick = lambda i: (layer, 0, 0)
    return pl.pallas_call(
        _post_kernel,
        grid=(n // tm,),
        in_specs=[
            pl.BlockSpec((tm, FOX_WIDTH), lambda i: (i, 0)),
            pl.BlockSpec((tm, GLA_V), lambda i: (i, 0)),
            pl.BlockSpec((tm, GLA_V), lambda i: (i, MAIN_WIDTH // GLA_V - 1)),
            pl.BlockSpec((tm, d_model), lambda i: (i, 0)),
            pl.BlockSpec((None, 1, GLA_V), pick),
            pl.BlockSpec((None, FOX_WIDTH + GLA_V, d_model), pick),
            pl.BlockSpec((None, 1, d_model), pick),
            pl.BlockSpec((None, d_model, X_WIDTH), pick),
            pl.BlockSpec((None, mem_len, X_WIDTH), lambda i: (layer, i // per_seq, 0)),
            pl.BlockSpec((None, mem_len, X_WIDTH), lambda i: (layer, i // per_seq, 0)),
            pl.BlockSpec((None, X_WIDTH, d_model), pick),
            pl.BlockSpec((None, 1, d_model), pick),
            pl.BlockSpec((None, d_model, 2 * LANES), pick),
            pl.BlockSpec((None, 1, LANES), pick),
        ],
        out_specs=[
            pl.BlockSpec((tm, d_model), lambda i: (i, 0)),
            pl.BlockSpec((tm, d_model // 2), lambda i: (i, 0)),
            pl.BlockSpec((ROUTE_WIDTH, tm), lambda i: (0, i)),
            pl.BlockSpec((ROUTE_ROWS, LANES), const),
        ],
        out_shape=[
            jax.ShapeDtypeStruct((n, d_model), F32),
            jax.ShapeDtypeStruct((n, d_model // 2), jnp.uint32),
            jax.ShapeDtypeStruct((ROUTE_WIDTH, n), F32),
            jax.ShapeDtypeStruct((ROUTE_ROWS, LANES), F32),
        ],
        scratch_shapes=[pltpu.VMEM((ROUTE_ROWS, LANES), F32)],
        compiler_params=_cparams(("arbitrary",)),
        name="post_mixer",
    )(fox, gla, main, h, gla_gain, wout, cg, wxq, kmem, vmem, wxo, mg, wr, br)


def _dispatch(dest_kmajor, x, n_rows):
    n, width = x.shape
    window = SC_GATHER_WINDOW
    mesh = plsc.VectorSubcoreMesh(core_axis_name="core", subcore_axis_name="subcore")
    workers = mesh.num_cores * mesh.num_subcores
    per_worker = n // workers
    assert n % (workers * window) == 0

    steps = per_worker // window
    assert steps % 2 == 0
    index_buf = pltpu.VMEM((window,), jnp.int32)
    row_buf = pltpu.VMEM((window, width), x.dtype)
    dma = pltpu.SemaphoreType.DMA

    @functools.partial(
        pl.kernel, out_type=jax.ShapeDtypeStruct((n_rows, width), x.dtype), mesh=mesh,
        scratch_types=[index_buf, index_buf, index_buf, index_buf, row_buf, row_buf, dma, dma, dma])
    def scatter(x_hbm, idx_hbm, out_hbm, idx0_a, idx1_a, idx0_b, idx1_b, rows_a, rows_b, sem_a, sem_b, sem_out):
        worker = lax.axis_index("subcore") * mesh.num_cores + lax.axis_index("core")
        base = worker * per_worker
        slots = ((idx0_a, idx1_a, rows_a, sem_a), (idx0_b, idx1_b, rows_b, sem_b))

        def load(step, slot):
            idx0, idx1, rows, sem = slots[slot]
            off = pl.multiple_of(base + step * window, window)
            pltpu.sync_copy(idx_hbm.at[pl.ds(off, window)], idx0)
            pltpu.sync_copy(idx_hbm.at[pl.ds(n + off, window)], idx1)
            pltpu.async_copy(x_hbm.at[pl.ds(off, window)], rows, sem)

        def store(slot):
            idx0, idx1, rows, sem = slots[slot]
            pltpu.make_async_copy(x_hbm.at[pl.ds(0, window)], rows, sem).wait()
            first = pltpu.async_copy(rows, out_hbm.at[idx0], sem_out)
            second = pltpu.async_copy(rows, out_hbm.at[idx1], sem_out)
            first.wait()
            second.wait()

        load(0, 0)

        @pl.loop(0, steps, step=2)
        def _(step):
            load(step + 1, 1)
            store(0)

            @pl.when(step + 2 < steps)
            def _():
                load(step + 2, 0)

            store(1)

    return scatter(x, dest_kmajor)


def _expert_kernel(be_ref, valid_ref, fresh_ref, x_ref, wg_ref, wu_ref, wd_ref, y_ref, wg_b, wu_b, wd_b):
    del be_ref
    i = pl.program_id(0)
    valid = valid_ref[i]

    @pl.when(fresh_ref[i] > 0)
    def _():
        wg_b[...] = wg_ref[...].astype(BF16)
        wu_b[...] = wu_ref[...].astype(BF16)
        wd_b[...] = wd_ref[...].astype(BF16)

    @pl.when(valid > 0)
    def _():
        row = lax.broadcasted_iota(jnp.int32, x_ref.shape, 0)
        lo, hi = _unpack_rows(jnp.where(row < valid, x_ref[...], jnp.uint32(0)))
        lo = lo.astype(BF16)
        hi = hi.astype(BF16)
        half = lo.shape[1]
        g = _dot(lo, wg_b[:half, :]) + _dot(hi, wg_b[half:, :])
        u = _dot(lo, wu_b[:half, :]) + _dot(hi, wu_b[half:, :])
        a = (g * jax.nn.sigmoid(g) * u).astype(BF16)
        y_ref[...] = _pack_rows(_dot(a, wd_b[...]))

    @pl.when(valid <= 0)
    def _():
        y_ref[...] = jnp.zeros_like(y_ref)


def _experts(block_e, valid, xs, wg, wu, wd, layer):
    r, width = xs.shape
    bm = MOE_BLOCK
    d_model, d_exp = wg.shape[-2:]
    fresh = jnp.concatenate([jnp.ones((1,), jnp.int32), (block_e[1:] != block_e[:-1]).astype(jnp.int32)])
    pick = lambda i, be, va, fr: (layer, be[i], 0, 0)
    return pl.pallas_call(
        _expert_kernel,
        grid_spec=pltpu.PrefetchScalarGridSpec(
            num_scalar_prefetch=3,
            grid=(r // bm,),
            in_specs=[
                pl.BlockSpec((bm, width), lambda i, be, va, fr: (i, 0)),
                pl.BlockSpec((None, None, d_model, d_exp), pick),
                pl.BlockSpec((None, None, d_model, d_exp), pick),
                pl.BlockSpec((None, None, d_exp, d_model), pick),
            ],
            out_specs=pl.BlockSpec((bm, width), lambda i, be, va, fr: (i, 0)),
            scratch_shapes=[
                pltpu.VMEM((d_model, d_exp), BF16),
                pltpu.VMEM((d_model, d_exp), BF16),
                pltpu.VMEM((d_exp, d_model), BF16),
            ],
        ),
        out_shape=jax.ShapeDtypeStruct((r, width), jnp.uint32),
        compiler_params=_cparams(("arbitrary",)),
        name="moe_experts",
    )(block_e, valid, fresh, xs, wg, wu, wd)


def _sc_gather_rows(table, idx):
    m = idx.shape[0]
    width = table.shape[1]
    window = SC_GATHER_WINDOW
    mesh = plsc.VectorSubcoreMesh(core_axis_name="core", subcore_axis_name="subcore")
    workers = mesh.num_cores * mesh.num_subcores
    per_worker = m // workers
    assert m % (workers * window) == 0

    steps = per_worker // window
    assert steps % 2 == 0
    index_buf = pltpu.VMEM((window,), jnp.int32)
    row_buf = pltpu.VMEM((window, width), table.dtype)
    dma = pltpu.SemaphoreType.DMA

    @functools.partial(
        pl.kernel, out_type=jax.ShapeDtypeStruct((m, width), table.dtype), mesh=mesh,
        scratch_types=[index_buf, index_buf, row_buf, row_buf, dma, dma])
    def gather(table_hbm, idx_hbm, out_hbm, idx_a, idx_b, rows_a, rows_b, sem_a, sem_b):
        worker = lax.axis_index("subcore") * mesh.num_cores + lax.axis_index("core")
        base = worker * per_worker
        slots = ((idx_a, rows_a, sem_a), (idx_b, rows_b, sem_b))

        def fetch(step, slot):
            idx, rows, sem = slots[slot]
            off = pl.multiple_of(base + step * window, window)
            pltpu.sync_copy(idx_hbm.at[pl.ds(off, window)], idx)
            pltpu.async_copy(table_hbm.at[idx], rows, sem)

        def flush(step, slot):
            idx, rows, sem = slots[slot]
            off = pl.multiple_of(base + step * window, window)
            pltpu.make_async_copy(table_hbm.at[idx], rows, sem).wait()
            pltpu.sync_copy(rows, out_hbm.at[pl.ds(off, window)])

        fetch(0, 0)

        @pl.loop(0, steps, step=2)
        def _(step):
            fetch(step + 1, 1)
            flush(step, 0)

            @pl.when(step + 2 < steps)
            def _():
                fetch(step + 2, 0)

            flush(step + 1, 1)

    return gather(table, idx)


def _final_kernel(h_ref, gate_ref, gain_ref, y0_ref, y1_ref, o_ref):
    o_ref[...] = _rms(_moe_sum(h_ref[...], gate_ref[...], y0_ref[...], y1_ref[...]), gain_ref[...])


def _final(h, route, gain, picked):
    n, d_model = h.shape
    tc = MOVE_TILE
    steps = n // tc
    return pl.pallas_call(
        _final_kernel,
        grid=(steps,),
        in_specs=[
            pl.BlockSpec((tc, d_model), lambda i: (i, 0)),
            pl.BlockSpec((ROUTE_WIDTH, tc), lambda i: (0, i)),
            pl.BlockSpec((1, d_model), lambda i: (0, 0)),
            pl.BlockSpec((tc, d_model // 2), lambda i: (i, 0)),
            pl.BlockSpec((tc, d_model // 2), lambda i: (i + steps, 0)),
        ],
        out_specs=pl.BlockSpec((tc, d_model), lambda i: (i, 0)),
        out_shape=jax.ShapeDtypeStruct((n, d_model), F32),
        compiler_params=_cparams(("arbitrary",)),
        name="moe_final",
    )(h, route, gain, picked, picked)


def _routing_tables(route, cnt, n_rows):
    bm = MOE_BLOCK
    expert = route[0:2].astype(jnp.int32)
    rank = route[4:6].astype(jnp.int32)
    counts = cnt[N_GROUPS:N_GROUPS + N_EXPERTS, 0].astype(jnp.int32)
    padded = (counts + bm - 1) // bm * bm
    pad_ends = jnp.cumsum(padded)
    pad_starts = pad_ends - padded
    ids = jnp.arange(N_EXPERTS, dtype=jnp.int32)
    start_of = jnp.sum(jnp.where(expert[..., None] == ids, pad_starts, 0), axis=-1)
    dest = (start_of + rank).reshape(-1).astype(jnp.int32)
    block_row = jnp.arange(n_rows // bm, dtype=jnp.int32) * bm
    block_e = jnp.minimum(jnp.sum((pad_ends[None, :] <= block_row[:, None]).astype(jnp.int32), axis=-1),
                          N_EXPERTS - 1)
    row_end = jnp.sum(jnp.where(block_e[:, None] == ids, pad_starts + counts, 0), axis=-1)
    valid = jnp.clip(row_end - block_row, 0, bm).astype(jnp.int32)
    return dest, block_e, valid


def kernel(x, mem, mem_norm, mix_norm, w_in, b_forget, w_alpha_up, b_alpha, fox_out_gain, gla_out_gain, w_out,
           cross_norm, w_xq, w_xk, w_xv, w_xo, moe_norm, w_router_group, b_router_group, w_router_expert,
           b_router_expert, w_expert_gate, w_expert_up, w_expert_down, final_norm):
    batch, seq, d_model = x.shape
    mem_len = mem.shape[1]
    depth = w_in.shape[0]
    n = batch * seq
    assert seq % FOX_TILE == 0 and seq % IN_TILE == 0 and seq % POST_TILE == 0 and seq % GLA_CHUNK == 0
    assert n % MOVE_TILE == 0 and d_model % LANES == 0

    c0 = 3 * FOX_WIDTH
    c1 = c0 + FOX_HEADS
    c2 = c1 + 2 * GLA_QK + 2 * GLA_V
    w_main = jnp.concatenate([w_in[:, :, :c0], w_in[:, :, c1:c2]], axis=-1).astype(BF16)
    pad = LANES - FOX_HEADS - GLA_RANK
    w_small = jnp.concatenate([w_in[:, :, c0:c1], w_in[:, :, c2:], jnp.zeros((depth, d_model, pad), F32)],
                              axis=-1).astype(BF16)
    w_up = jnp.concatenate([jnp.zeros((depth, FOX_HEADS, GLA_QK), F32), w_alpha_up,
                            jnp.zeros((depth, pad, GLA_QK), F32)], axis=1).astype(BF16)
    b_f = jnp.pad(b_forget, ((0, 0), (0, LANES - FOX_HEADS)))[:, None, :]
    b_a = b_alpha[:, None, :]
    w_r = jnp.concatenate([w_router_group, w_router_expert,
                           jnp.zeros((depth, d_model, LANES - N_GROUPS - N_EXPERTS), F32)], axis=-1)
    w_rh = w_r.astype(BF16)
    w_rs = jnp.concatenate([w_rh, (w_r - w_rh.astype(F32)).astype(BF16)], axis=-1)
    b_r = jnp.pad(jnp.concatenate([b_router_group, b_router_expert], axis=-1),
                  ((0, 0), (0, LANES - N_GROUPS - N_EXPERTS)))[:, None, :]
    w_out_b = w_out.astype(BF16)
    w_xq_b = w_xq.astype(BF16)
    w_xo_b = w_xo.astype(BF16)
    mix_g = mix_norm[:, None, :]
    cross_g = cross_norm[:, None, :]
    moe_g = moe_norm[:, None, :]
    gla_g = gla_out_gain[:, None, :]

    kmem, vmem = _mem_kv(mem.reshape(batch * mem_len, d_model), mem_norm[None, :],
                         w_xk.astype(BF16), w_xv.astype(BF16), batch, mem_len)

    n_rows = 2 * n + N_EXPERTS * MOE_BLOCK
    h = x.reshape(n, d_model)
    moe = None
    for l in range(depth):
        if moe is None:
            main, c, kl, dec = _in_proj(h, mix_g, w_main, w_small, w_up, b_f, b_a, l, seq)
        else:
            h, main, c, kl, dec = _in_proj(h, mix_g, w_main, w_small, w_up, b_f, b_a, l, seq, moe)
        fox, gla = _mixer(main, c, fox_out_gain[l][None, :], kl, dec, batch, seq)
        h2, hn2, route, cnt = _post(fox, gla, main, h, gla_g, w_out_b, cross_g, w_xq_b, kmem, vmem, w_xo_b, moe_g,
                                    w_rs, b_r, seq, mem_len, l)
        dest, block_e, valid = _routing_tables(route, cnt, n_rows)
        xs = _dispatch(dest, hn2, n_rows)
        y = _experts(block_e, valid, xs, w_expert_gate, w_expert_up, w_expert_down, l)
        h, moe = h2, (route, _sc_gather_rows(y, dest))
    return _final(h, moe[0], final_norm[None, :], moe[1]).reshape(batch, seq, d_model)
```
